```python
import math
import jax, jax.numpy as jnp
from jax import lax
import numpy as np

D_MODEL = 1024
BATCH = 8
SEQ = 2048
DEPTH = 1
DEC_BATCH = 128
DEC_SEQ = 8
PAST_LEN = 16384
PAGE_SIZE = 128

RET_HEADS = 4
RET_DK = 128
RET_DV = 256
RET_QK = RET_HEADS * RET_DK
RET_V = RET_HEADS * RET_DV
ROPE_BASE = 10000.0
SSM_INNER = 2 * D_MODEL
SSM_HEADDIM = 64
SSM_HEADS = SSM_INNER // SSM_HEADDIM
SSM_GROUPS = 4
SSM_DSTATE = 128
CONV_WIDTH = 4
CONV_DIM = SSM_INNER + 2 * SSM_GROUPS * SSM_DSTATE
DT_MIN = 0.001
DT_MAX = 0.1
CHUNK = 128
N_EXPERTS = 32
TOP_K = 4
D_FF = D_MODEL
SWIGLU_LIMIT = 7.0
SWIGLU_ALPHA = 1.702
MOE_BLOCK = 128
EPS = 1e-6
IN_PROJ_DIM = 2 * RET_QK + 2 * RET_V + SSM_INNER + CONV_DIM + SSM_HEADS + 2 * D_MODEL

kernel_name = 'retnet_mamba2_gptoss_hybrid_step'


def rmsnorm(x, w):
    xf = x.astype(jnp.float32)
    y = xf * lax.rsqrt(jnp.mean(xf * xf, axis=-1, keepdims=True) + EPS)
    return (y * w.astype(jnp.float32)).astype(x.dtype)


def in_proj_splits():
    sizes = [RET_QK, RET_QK, RET_V, RET_V, SSM_INNER, CONV_DIM, SSM_HEADS, D_MODEL, D_MODEL]
    return [int(i) for i in np.cumsum(sizes)[:-1]]


def rotary(t, pos):
    half = RET_DK // 2
    inv_freq = 1.0 / (ROPE_BASE ** jnp.linspace(0.0, 1.0, half, dtype=jnp.float32))
    ang = pos.astype(jnp.float32)[:, None] * inv_freq[None, :]
    cos = jnp.cos(ang)[None, :, None, :]
    sin = jnp.sin(ang)[None, :, None, :]
    t1, t2 = t[..., :half], t[..., half:]
    return jnp.concatenate([t1 * cos - t2 * sin, t1 * sin + t2 * cos], axis=-1)


def retention_log_decay():
    return jnp.log(1.0 - 2.0 ** (-5.0 - jnp.arange(RET_HEADS, dtype=jnp.float32)))


def retention_chunked(q, k, v, s0):
    bsz, L = q.shape[:2]
    c = math.gcd(L, CHUNK)
    n = L // c
    lg = retention_log_decay()
    idx = jnp.arange(c, dtype=jnp.float32)
    diff = idx[:, None] - idx[None, :]
    decay_mask = jnp.where(diff >= 0, jnp.exp(lg[:, None, None] * jnp.maximum(diff, 0.0)), 0.0)
    q_decay = jnp.exp(lg[None, :] * (idx[:, None] + 1.0))[None, :, :, None]
    k_decay = jnp.exp(lg[None, :] * (c - 1.0 - idx[:, None]))[None, :, :, None]
    chunk_decay = jnp.exp(lg * c)[None, :, None, None]

    def to_chunks(t):
        return t.reshape(bsz, n, c, *t.shape[2:]).swapaxes(0, 1)

    def step(s, inp):
        qc, kc, vc = inp
        scores = jnp.einsum('bihd,bjhd->bhij', qc, kc) * decay_mask[None]
        o = jnp.einsum('bhij,bjhv->bihv', scores, vc)
        o = o + jnp.einsum('bihd,bhdv->bihv', qc * q_decay, s)
        s = s * chunk_decay + jnp.einsum('bjhd,bjhv->bhdv', kc * k_decay, vc)
        return s, o

    s, o = lax.scan(step, s0, (to_chunks(q), to_chunks(k), to_chunks(v)))
    return o.swapaxes(0, 1).reshape(bsz, L, RET_HEADS, RET_DV), s


def causal_conv(xbc, buf, w, b):
    L = xbc.shape[1]
    full = jnp.concatenate([buf, xbc], axis=1)
    out = b
    for i in range(CONV_WIDTH):
        out = out + full[:, i:i + L] * w[i]
    return jax.nn.silu(out), full[:, full.shape[1] - (CONV_WIDTH - 1):]


def ssd_chunked(xs, dt, bm, cm, a, h0):
    bsz, L = xs.shape[:2]
    c = math.gcd(L, CHUNK)
    n = L // c
    hpg = SSM_HEADS // SSM_GROUPS
    xg = xs.reshape(bsz, L, SSM_GROUPS, hpg, SSM_HEADDIM)
    dtg = dt.reshape(bsz, L, SSM_GROUPS, hpg)
    ag = a.reshape(SSM_GROUPS, hpg)
    causal = jnp.tril(jnp.ones((c, c), dtype=bool))[None, :, :, None, None]

    def to_chunks(t):
        return t.reshape(bsz, n, c, *t.shape[2:]).swapaxes(0, 1)

    def step(h, inp):
        xc, dtc, bc, cc = inp
        cum = jnp.cumsum(dtc * ag, axis=1)
        seg = cum[:, :, None] - cum[:, None, :]
        decay = jnp.exp(jnp.where(causal, seg, -jnp.inf))
        cb = jnp.einsum('bign,bjgn->bijg', cc, bc)
        wts = cb[..., None] * decay * dtc[:, None]
        y = jnp.einsum('bijgh,bjghp->bighp', wts, xc)
        y = y + jnp.einsum('bign,bghpn->bighp', cc, h) * jnp.exp(cum)[..., None]
        to_end = jnp.exp(cum[:, -1:] - cum) * dtc
        h = h * jnp.exp(cum[:, -1])[..., None, None] + jnp.einsum('bjgn,bjgh,bjghp->bghpn', bc, to_end, xc)
        return h, y

    hg0 = h0.reshape(bsz, SSM_GROUPS, hpg, SSM_HEADDIM, SSM_DSTATE)
    h, y = lax.scan(step, hg0, (to_chunks(xg), to_chunks(dtg), to_chunks(bm), to_chunks(cm)))
    y = y.swapaxes(0, 1).reshape(bsz, L, SSM_HEADS, SSM_HEADDIM)
    return y, h.reshape(bsz, SSM_HEADS, SSM_HEADDIM, SSM_DSTATE)


def mixer_block(x, pos, s_ret, s_ssm, s_conv, norm_mix, w_in, ret_norm, w_out_ret, conv_w, conv_b,
                dt_bias, a_log, d_skip, ssm_norm, w_out_ssm, w_out):
    f32 = jnp.float32
    bsz, L, _ = x.shape
    h = rmsnorm(x, norm_mix)
    proj = h @ w_in
    q, k, v, g_ret, z, xbc, dt_raw, g_a, g_b = jnp.split(proj, in_proj_splits(), axis=-1)
    q = rotary(q.astype(f32).reshape(bsz, L, RET_HEADS, RET_DK), pos)
    k = rotary(k.astype(f32).reshape(bsz, L, RET_HEADS, RET_DK), pos) * (RET_DK ** -0.5)
    v = v.astype(f32).reshape(bsz, L, RET_HEADS, RET_DV)
    o_ret, s_ret_new = retention_chunked(q, k, v, s_ret.astype(f32))
    o_ret = rmsnorm(o_ret, ret_norm).reshape(bsz, L, RET_V).astype(x.dtype)
    branch_a = (jax.nn.silu(g_ret) * o_ret) @ w_out_ret
    xbc_act, s_conv_new = causal_conv(xbc, s_conv.astype(x.dtype), conv_w, conv_b)
    xs, b_ssm, c_ssm = jnp.split(xbc_act.astype(f32), [SSM_INNER, SSM_INNER + SSM_GROUPS * SSM_DSTATE], axis=-1)
    dt = jax.nn.softplus(dt_raw.astype(f32) + dt_bias.astype(f32))
    a = -jnp.exp(a_log.astype(f32))
    xs_h = xs.reshape(bsz, L, SSM_HEADS, SSM_HEADDIM)
    y, s_ssm_new = ssd_chunked(xs_h, dt, b_ssm.reshape(bsz, L, SSM_GROUPS, SSM_DSTATE),
                               c_ssm.reshape(bsz, L, SSM_GROUPS, SSM_DSTATE), a, s_ssm.astype(f32))
    y = y + xs_h * d_skip.astype(f32)[:, None]
    y = y.reshape(bsz, L, SSM_INNER) * jax.nn.silu(z.astype(f32))
    y = rmsnorm(y.reshape(bsz, L, SSM_GROUPS, SSM_INNER // SSM_GROUPS),
                ssm_norm.reshape(SSM_GROUPS, SSM_INNER // SSM_GROUPS)).reshape(bsz, L, SSM_INNER).astype(x.dtype)
    branch_b = y @ w_out_ssm
    merged = jax.nn.sigmoid(g_a) * branch_a + jax.nn.sigmoid(g_b) * branch_b
    return merged @ w_out, s_ret_new, s_ssm_new, s_conv_new


def moe_ffn(h, w_router, b_router, w_gate_up, b_gate_up, w_down, b_down):
    bsz, L, d = h.shape
    x = h.reshape(-1, d)
    t = x.shape[0]
    logits = (x @ w_router + b_router).astype(jnp.float32)
    top_val, top_idx = lax.top_k(logits, TOP_K)
    gates = jax.nn.softmax(top_val, axis=-1).astype(x.dtype)
    n_assign = t * TOP_K
    e_flat = top_idx.reshape(-1).astype(jnp.int32)
    order = jnp.argsort(e_flat)
    e_sorted = e_flat[order]
    counts = jnp.bincount(e_flat, length=N_EXPERTS).astype(jnp.int32)
    starts = jnp.cumsum(counts) - counts
    padded = (counts + MOE_BLOCK - 1) // MOE_BLOCK * MOE_BLOCK
    pends = jnp.cumsum(padded)
    pstarts = pends - padded
    dest = pstarts[e_sorted] + jnp.arange(n_assign, dtype=jnp.int32) - starts[e_sorted]
    n_blocks = -(-(n_assign + N_EXPERTS * (MOE_BLOCK - 1)) // MOE_BLOCK)
    p = n_blocks * MOE_BLOCK
    tok_buf = jnp.full((p,), t, dtype=jnp.int32).at[dest].set((order // TOP_K).astype(jnp.int32))
    gate_buf = jnp.zeros((p,), x.dtype).at[dest].set(gates.reshape(-1)[order])
    block_start = jnp.arange(n_blocks, dtype=jnp.int32) * MOE_BLOCK
    block_expert = jnp.minimum(jnp.searchsorted(pends, block_start, side='right'), N_EXPERTS - 1)
    x_pad = jnp.concatenate([x, jnp.zeros((1, d), x.dtype)], axis=0)
    xb = x_pad[tok_buf].reshape(n_blocks, MOE_BLOCK, d)

    def expert_block(args):
        xblk, e = args
        gu = xblk @ w_gate_up[e] + b_gate_up[e]
        glu, lin = gu[:, :D_FF], gu[:, D_FF:]
        glu = jnp.minimum(glu, SWIGLU_LIMIT)
        lin = jnp.clip(lin, -SWIGLU_LIMIT, SWIGLU_LIMIT)
        act = glu * jax.nn.sigmoid(SWIGLU_ALPHA * glu) * (lin + 1.0)
        return act @ w_down[e] + b_down[e]

    yb = lax.map(expert_block, (xb, block_expert)).reshape(p, d)
    y = jnp.zeros((t + 1, d), x.dtype).at[tok_buf].add(yb * gate_buf[:, None])[:t]
    return y.reshape(bsz, L, d)


def setup_inputs(seed: int = 0) -> dict:
    key = jax.random.key(seed)
    ks = jax.random.split(key, 26)
    f32 = jnp.float32

    def nrm(k, shape, scale):
        return jax.random.normal(k, shape, f32) * scale

    x_prompt = nrm(ks[0], (BATCH, SEQ, D_MODEL), 1.0)
    x_sample = nrm(ks[1], (DEC_BATCH, DEC_SEQ, D_MODEL), 1.0)
    state_ret = nrm(ks[2], (DEPTH, DEC_BATCH, RET_HEADS, RET_DK, RET_DV), 0.5)
    state_ssm = nrm(ks[3], (DEPTH, DEC_BATCH, SSM_HEADS, SSM_HEADDIM, SSM_DSTATE), 0.1)
    state_conv = nrm(ks[4], (DEPTH, DEC_BATCH, CONV_WIDTH - 1, CONV_DIM), 1.0)
    norm_mix = 1.0 + nrm(ks[5], (DEPTH, D_MODEL), 0.02)
    w_in = nrm(ks[6], (DEPTH, D_MODEL, IN_PROJ_DIM), D_MODEL ** -0.5)
    ret_norm = 1.0 + nrm(ks[7], (DEPTH, RET_HEADS, RET_DV), 0.02)
    w_out_ret = nrm(ks[8], (DEPTH, RET_V, D_MODEL), RET_V ** -0.5)
    conv_w = nrm(ks[9], (DEPTH, CONV_WIDTH, CONV_DIM), CONV_WIDTH ** -0.5)
    conv_b = nrm(ks[10], (DEPTH, CONV_DIM), 0.02)
    dt0 = jnp.exp(jax.random.uniform(ks[11], (DEPTH, SSM_HEADS), f32, math.log(DT_MIN), math.log(DT_MAX)))
    dt_bias = dt0 + jnp.log(-jnp.expm1(-dt0))
    a_log = jnp.log(jax.random.uniform(ks[12], (DEPTH, SSM_HEADS), f32, 1.0, 16.0))
    d_skip = 1.0 + nrm(ks[13], (DEPTH, SSM_HEADS), 0.02)
    ssm_norm = 1.0 + nrm(ks[14], (DEPTH, SSM_INNER), 0.02)
    w_out_ssm = nrm(ks[15], (DEPTH, SSM_INNER, D_MODEL), SSM_INNER ** -0.5)
    w_out = nrm(ks[16], (DEPTH, D_MODEL, D_MODEL), D_MODEL ** -0.5)
    norm_ffn = 1.0 + nrm(ks[17], (DEPTH, D_MODEL), 0.02)
    w_router = nrm(ks[18], (DEPTH, D_MODEL, N_EXPERTS), D_MODEL ** -0.5)
    b_router = nrm(ks[19], (DEPTH, N_EXPERTS), 0.01)
    w_gate_up = nrm(ks[20], (DEPTH, N_EXPERTS, D_MODEL, 2 * D_FF), D_MODEL ** -0.5)
    b_gate_up = nrm(ks[21], (DEPTH, N_EXPERTS, 2 * D_FF), 0.01)
    w_down = nrm(ks[22], (DEPTH, N_EXPERTS, D_FF, D_MODEL), D_FF ** -0.5)
    b_down = nrm(ks[23], (DEPTH, N_EXPERTS, D_MODEL), 0.01)
    norm_final = 1.0 + nrm(ks[24], (D_MODEL,), 0.02)
    return {'x_prompt': x_prompt, 'x_sample': x_sample,
            'state_ret': state_ret, 'state_ssm': state_ssm, 'state_conv': state_conv,
            'norm_mix': norm_mix, 'w_in': w_in, 'ret_norm': ret_norm, 'w_out_ret': w_out_ret,
            'conv_w': conv_w, 'conv_b': conv_b, 'dt_bias': dt_bias, 'a_log': a_log, 'd_skip': d_skip,
            'ssm_norm': ssm_norm, 'w_out_ssm': w_out_ssm, 'w_out': w_out, 'norm_ffn': norm_ffn,
            'w_router': w_router, 'b_router': b_router, 'w_gate_up': w_gate_up, 'b_gate_up': b_gate_up,
            'w_down': w_down, 'b_down': b_down, 'norm_final': norm_final}


def reference(x_prompt, x_sample, state_ret, state_ssm, state_conv, norm_mix, w_in, ret_norm, w_out_ret,
              conv_w, conv_b, dt_bias, a_log, d_skip, ssm_norm, w_out_ssm, w_out, norm_ffn,
              w_router, b_router, w_gate_up, b_gate_up, w_down, b_down, norm_final):
    def trunk(x, pos, ret_states, ssm_states, conv_states):
        new_ret, new_ssm, new_conv = [], [], []
        for l in range(DEPTH):
            mix, sr, ss, sc = mixer_block(x, pos, ret_states[l], ssm_states[l], conv_states[l],
                                          norm_mix[l], w_in[l], ret_norm[l], w_out_ret[l], conv_w[l], conv_b[l],
                                          dt_bias[l], a_log[l], d_skip[l], ssm_norm[l], w_out_ssm[l], w_out[l])
            x = x + mix
            x = x + moe_ffn(rmsnorm(x, norm_ffn[l]), w_router[l], b_router[l], w_gate_up[l], b_gate_up[l],
                            w_down[l], b_down[l])
            new_ret.append(sr)
            new_ssm.append(ss)
            new_conv.append(sc)
        return rmsnorm(x, norm_final), jnp.stack(new_ret), jnp.stack(new_ssm), jnp.stack(new_conv)

    bp = x_prompt.shape[0]
    zeros_ret = jnp.zeros((DEPTH, bp, RET_HEADS, RET_DK, RET_DV), jnp.float32)
    zeros_ssm = jnp.zeros((DEPTH, bp, SSM_HEADS, SSM_HEADDIM, SSM_DSTATE), jnp.float32)
    zeros_conv = jnp.zeros((DEPTH, bp, CONV_WIDTH - 1, CONV_DIM), x_prompt.dtype)
    pos_prompt = jnp.arange(x_prompt.shape[1], dtype=jnp.int32)
    pos_sample = PAST_LEN + jnp.arange(x_sample.shape[1], dtype=jnp.int32)
    y_prompt, ret_p, ssm_p, conv_p = trunk(x_prompt, pos_prompt, zeros_ret, zeros_ssm, zeros_conv)
    y_sample, ret_s, ssm_s, conv_s = trunk(x_sample, pos_sample, state_ret, state_ssm, state_conv)
    return (y_prompt, y_sample, ret_p, ret_s, ssm_p, ssm_s, conv_p, conv_s)
```

```python
import functools
import math

import jax
import jax.numpy as jnp
from jax import lax
from jax.experimental import pallas as pl
from jax.experimental.pallas import tpu as pltpu

F32 = jnp.float32
BF16 = jnp.bfloat16

D_MODEL = 1024
PAST_LEN = 16384
RET_HEADS = 4
RET_DK = 128
RET_DV = 256
RET_QK = RET_HEADS * RET_DK
RET_V = RET_HEADS * RET_DV
ROPE_BASE = 10000.0
SSM_INNER = 2 * D_MODEL
SSM_HEADDIM = 64
SSM_HEADS = SSM_INNER // SSM_HEADDIM
SSM_GROUPS = 4
SSM_DSTATE = 128
HEADS_PER_GROUP = SSM_HEADS // SSM_GROUPS
GROUP_INNER = SSM_INNER // SSM_GROUPS
CONV_WIDTH = 4
CONV_DIM = SSM_INNER + 2 * SSM_GROUPS * SSM_DSTATE
CHUNK = 128
N_EXPERTS = 32
TOP_K = 4
D_FF = D_MODEL
SWIGLU_LIMIT = 7.0
SWIGLU_ALPHA = 1.702
EPS = 1e-6

LANES = 128
SUBLANES = 8
ROWS = 128
SAMPLE_PER_TILE = 16
EXPERT_BLOCK = 256
VMEM_LIMIT = 56 * 1024 * 1024


class _Cols:
    Z = 0
    XS = Z + SSM_INNER
    V = XS + SSM_INNER
    GRET = V + RET_V
    GA = GRET + RET_V
    GB = GA + D_MODEL
    Q = GB + D_MODEL
    K = Q + RET_QK
    B = K + RET_QK
    C = B + SSM_GROUPS * SSM_DSTATE
    DT = C + SSM_GROUPS * SSM_DSTATE
    TOTAL = DT + SSM_GROUPS * LANES


def _pick(n, target):
    best = None
    for c in range(LANES, target + 1, LANES):
        if n % c == 0:
            best = c
    assert best is not None, (n, target)
    return best


def _params(sem, **kw):
    return pltpu.CompilerParams(dimension_semantics=sem, vmem_limit_bytes=VMEM_LIMIT, **kw)


def _dot(a, b):
    return jnp.dot(a, b, preferred_element_type=F32)


def _dot_nt(a, b):
    return lax.dot_general(a, b, (((1,), (1,)), ((), ())), preferred_element_type=F32)


def _dot_tn(a, b):
    return lax.dot_general(a, b, (((0,), (0,)), ((), ())), preferred_element_type=F32)


def _silu(x):
    return x * jax.nn.sigmoid(x)


def _inproj_kernel(x_ref, nw_ref, w_ref, o_ref, h_ref):
    @pl.when(pl.program_id(1) == 0)
    def _():
        x = x_ref[...]
        ms = jnp.mean(x * x, axis=-1, keepdims=True)
        h_ref[...] = (x * lax.rsqrt(ms + EPS) * nw_ref[...]).astype(BF16)

    o_ref[...] = _dot(h_ref[...], w_ref[...])


def _inproj(x, norm_w, w_perm):
    t = x.shape[0]
    tm = _pick(t, 1024)
    tn = _pick(_Cols.TOTAL, 1536)
    return pl.pallas_call(
        _inproj_kernel,
        out_shape=jax.ShapeDtypeStruct((t, _Cols.TOTAL), F32),
        grid=(t // tm, _Cols.TOTAL // tn),
        in_specs=[
            pl.BlockSpec((tm, D_MODEL), lambda i, j: (i, 0)),
            pl.BlockSpec((1, D_MODEL), lambda i, j: (0, 0)),
            pl.BlockSpec((D_MODEL, tn), lambda i, j: (0, j)),
        ],
        out_specs=pl.BlockSpec((tm, tn), lambda i, j: (i, j)),
        scratch_shapes=[pltpu.VMEM((tm, D_MODEL), BF16)],
        compiler_params=_params(("parallel", "arbitrary")),
        name="in_proj",
    )(x, norm_w, w_perm)


def _ret_log_decay():
    return jnp.log(1.0 - 2.0 ** (-5.0 - jnp.arange(RET_HEADS, dtype=F32)))


def _ret_tables(c):
    lg = _ret_log_decay()
    r = jnp.arange(ROWS)
    t = (r % c).astype(F32)
    seg = r // c
    diff = t[:, None] - t[None, :]
    ok = (seg[:, None] == seg[None, :]) & (diff >= 0)
    dm = jnp.where(ok[None], jnp.exp(lg[:, None, None] * jnp.maximum(diff, 0.0)[None]), 0.0)
    qd = jnp.exp(lg[:, None] * (t[None, :] + 1.0))
    kd = jnp.exp(lg[:, None] * (c - 1.0 - t[None, :]))
    cd = jnp.exp(lg * c)
    qd = jnp.broadcast_to(qd[:, :, None], (RET_HEADS, ROWS, LANES))
    kd = jnp.broadcast_to(kd[:, :, None], (RET_HEADS, ROWS, LANES))
    cd = jnp.broadcast_to(cd[:, None, None], (RET_HEADS, 1, RET_DV))
    return dm, qd, kd, cd


def _rope_tables(pos):
    half = RET_DK // 2
    inv_freq = 1.0 / (ROPE_BASE ** jnp.linspace(0.0, 1.0, half, dtype=F32))
    ang = pos.astype(F32)[:, None] * inv_freq[None, :]
    cos, sin = jnp.cos(ang), jnp.sin(ang)
    return jnp.concatenate([cos, cos], -1), jnp.concatenate([-sin, sin], -1)


def _block_mask(dtype):
    r = jnp.arange(ROWS) // (ROWS // SAMPLE_PER_TILE)
    b = jnp.arange(SAMPLE_PER_TILE * LANES) // LANES
    return (r[:, None] == b[None, :]).astype(dtype)


def _ret_common(q_ref, k_ref, v_ref, g_ref, cos_ref, sin_ref, dm_ref, qd_ref, kd_ref, nw_ref):
    cosf, sinf = cos_ref[...], sin_ref[...]
    q, k = q_ref[...], k_ref[...]
    qr = q * cosf + pltpu.roll(q, RET_DK // 2, 1) * sinf
    kr = (k * cosf + pltpu.roll(k, RET_DK // 2, 1) * sinf) * (RET_DK ** -0.5)
    vb = v_ref[...].astype(BF16)
    scores = _dot_nt(qr.astype(BF16), kr.astype(BF16)) * dm_ref[0]
    o_intra = _dot(scores.astype(BF16), vb)
    q_state = (qr * qd_ref[0]).astype(BF16)
    k_state = kr * kd_ref[0]

    def finish(o):
        ms = jnp.mean(o * o, axis=-1, keepdims=True)
        on = o * lax.rsqrt(ms + EPS) * nw_ref[0]
        return (_silu(g_ref[...]) * on).astype(BF16)

    return o_intra, q_state, k_state, vb, finish


def _ret_prompt_kernel(q_ref, k_ref, v_ref, g_ref, cos_ref, sin_ref, dm_ref, qd_ref, kd_ref, cd_ref, nw_ref,
                       a_ref, s_ref):
    @pl.when(pl.program_id(2) == 0)
    def _():
        s_ref[...] = jnp.zeros_like(s_ref)

    o_intra, q_state, k_state, vb, finish = _ret_common(
        q_ref, k_ref, v_ref, g_ref, cos_ref, sin_ref, dm_ref, qd_ref, kd_ref, nw_ref)
    s = s_ref[0, 0]
    a_ref[...] = finish(o_intra + _dot(q_state, s.astype(BF16)))
    s_ref[0, 0] = s * cd_ref[0] + _dot_tn(k_state.astype(BF16), vb)


def _ret_sample_kernel(q_ref, k_ref, v_ref, g_ref, cos_ref, sin_ref, dm_ref, qd_ref, kd_ref, cd_ref, nw_ref,
                       bm_ref, bmt_ref, s0_ref, a_ref, s_ref):
    o_intra, q_state, k_state, vb, finish = _ret_common(
        q_ref, k_ref, v_ref, g_ref, cos_ref, sin_ref, dm_ref, qd_ref, kd_ref, nw_ref)
    nb = SAMPLE_PER_TILE
    s0 = s0_ref[:, 0]
    q_bd = jnp.concatenate([q_state] * nb, axis=1) * bm_ref[...]
    a_ref[...] = finish(o_intra + _dot(q_bd, s0.reshape(nb * RET_DK, RET_DV).astype(BF16)))
    k_t = k_state.T.astype(BF16)
    k_bd_t = jnp.concatenate([k_t] * nb, axis=0) * bmt_ref[...]
    ds = _dot(k_bd_t, vb).reshape(nb, RET_DK, RET_DV)
    s_ref[:, 0] = s0 * cd_ref[0] + ds


def _ret_specs(row_block):
    def at(width, off):
        return lambda *ids: (row_block(*ids)[0], off // width + row_block(*ids)[1])

    def head(*shape):
        return pl.BlockSpec((1,) + shape, lambda *ids: (row_block(*ids)[1],) + (0,) * len(shape))

    return [
        pl.BlockSpec((ROWS, RET_DK), at(RET_DK, _Cols.Q)),
        pl.BlockSpec((ROWS, RET_DK), at(RET_DK, _Cols.K)),
        pl.BlockSpec((ROWS, RET_DV), at(RET_DV, _Cols.V)),
        pl.BlockSpec((ROWS, RET_DV), at(RET_DV, _Cols.GRET)),
    ], head


def _retention_prompt(proj, bsz, seq, ret_norm):
    nch = seq // ROWS
    dm, qd, kd, cd = _ret_tables(CHUNK)
    cosf, sinf = _rope_tables(jnp.arange(seq, dtype=jnp.int32))
    rb = lambda b, h, c: (b * nch + c, h)
    specs, head = _ret_specs(rb)
    specs += [
        pl.BlockSpec((ROWS, RET_DK), lambda b, h, c: (c, 0)),
        pl.BlockSpec((ROWS, RET_DK), lambda b, h, c: (c, 0)),
        head(ROWS, ROWS), head(ROWS, LANES), head(ROWS, LANES), head(1, RET_DV), head(1, RET_DV),
    ]
    return pl.pallas_call(
        _ret_prompt_kernel,
        out_shape=(jax.ShapeDtypeStruct((bsz * seq, RET_V), BF16),
                   jax.ShapeDtypeStruct((bsz, RET_HEADS, RET_DK, RET_DV), F32)),
        grid=(bsz, RET_HEADS, nch),
        in_specs=specs,
        out_specs=(pl.BlockSpec((ROWS, RET_DV), lambda b, h, c: (b * nch + c, h)),
                   pl.BlockSpec((1, 1, RET_DK, RET_DV), lambda b, h, c: (b, h, 0, 0))),
        compiler_params=_params(("parallel", "parallel", "arbitrary")),
        name="retention_prompt",
    )(proj, proj, proj, proj, cosf, sinf, dm, qd, kd, cd, ret_norm.reshape(RET_HEADS, 1, RET_DV))


def _retention_sample(proj, row0, bsz, seq, state, ret_norm):
    ntile = bsz // SAMPLE_PER_TILE
    rb0 = row0 // ROWS
    dm, qd, kd, cd = _ret_tables(seq)
    pos = PAST_LEN + jnp.arange(seq, dtype=jnp.int32)
    cosf, sinf = _rope_tables(jnp.tile(pos, SAMPLE_PER_TILE))
    rb = lambda t, h: (rb0 + t, h)
    specs, head = _ret_specs(rb)
    full = lambda *shape: pl.BlockSpec(shape, lambda t, h: (0,) * len(shape))
    specs += [
        full(ROWS, RET_DK), full(ROWS, RET_DK),
        head(ROWS, ROWS), head(ROWS, LANES), head(ROWS, LANES), head(1, RET_DV), head(1, RET_DV),
        full(ROWS, SAMPLE_PER_TILE * LANES), full(SAMPLE_PER_TILE * LANES, ROWS),
        pl.BlockSpec((SAMPLE_PER_TILE, 1, RET_DK, RET_DV), lambda t, h: (t, h, 0, 0)),
    ]
    bm = _block_mask(BF16)
    return pl.pallas_call(
        _ret_sample_kernel,
        out_shape=(jax.ShapeDtypeStruct((bsz * seq, RET_V), BF16),
                   jax.ShapeDtypeStruct(state.shape, F32)),
        grid=(ntile, RET_HEADS),
        in_specs=specs,
        out_specs=(pl.BlockSpec((ROWS, RET_DV), lambda t, h: (t, h)),
                   pl.BlockSpec((SAMPLE_PER_TILE, 1, RET_DK, RET_DV), lambda t, h: (t, h, 0, 0))),
        compiler_params=_params(("parallel", "parallel")),
        name="retention_sample",
    )(proj, proj, proj, proj, cosf, sinf, dm, qd, kd, cd, ret_norm.reshape(RET_HEADS, 1, RET_DV),
      bm, bm.T, state)


def _softplus(x):
    return jnp.maximum(x, 0.0) + jnp.log1p(jnp.exp(-jnp.abs(x)))


def _conv_piece(x, pred, w_ref, b_ref):
    width = x.shape[1]
    x3 = x.reshape(ROWS // SUBLANES, SUBLANES, width)
    t8 = lax.broadcasted_iota(jnp.int32, x3.shape, 1)
    acc = b_ref[...].reshape(1, 1, width)
    for i in range(CONV_WIDTH):
        s = CONV_WIDTH - 1 - i
        tap = x3 if s == 0 else pltpu.roll(jnp.where(t8 >= SUBLANES - s, pred, x3), s, 1)
        acc = acc + tap * w_ref[pl.ds(i, 1), :].reshape(1, 1, width)
    return _silu(acc).reshape(ROWS, width)


def _ssd_core(c, z_ref, dt_ref, dtb_ref, alog_ref, dsk_ref, nrm_ref, ltri_ref, xs, bm, cm, y_ref, state_io):
    dt = _softplus(dt_ref[...] + dtb_ref[...])
    d_a = dt * (-jnp.exp(alog_ref[...]))
    hi = d_a.astype(BF16)
    r1 = d_a - hi.astype(F32)
    mid = r1.astype(BF16)
    lo = (r1 - mid.astype(F32)).astype(BF16)
    ltri = ltri_ref[...]
    cum = _dot(ltri, hi) + _dot(ltri, mid) + _dot(ltri, lo)
    cum3 = cum.reshape(ROWS // c, c, LANES)
    c_last = jnp.broadcast_to(cum3[:, c - 1:c, :], cum3.shape).reshape(ROWS, LANES)
    to_end = jnp.exp(c_last - cum) * dt
    e_cum = jnp.exp(cum)
    cum_t, dt_t = cum.T, dt.T

    b_b, c_b = bm.astype(BF16), cm.astype(BF16)
    cb = _dot_nt(c_b, b_b)
    ri = lax.broadcasted_iota(jnp.int32, (ROWS, ROWS), 0)
    ci = lax.broadcasted_iota(jnp.int32, (ROWS, ROWS), 1)
    shift = c.bit_length() - 1
    causal = (ci <= ri) & (jnp.right_shift(ri, shift) == jnp.right_shift(ci, shift))
    lo_half = lax.broadcasted_iota(jnp.int32, (ROWS, LANES), 1) < SSM_HEADDIM

    def col(a, h):
        return jnp.broadcast_to(a[:, h:h + 1], (ROWS, LANES))

    def weights(h):
        seg = col(cum, h) - jnp.broadcast_to(cum_t[h:h + 1, :], (ROWS, ROWS))
        decay = jnp.exp(jnp.where(causal, seg, -jnp.inf))
        return (cb * decay * jnp.broadcast_to(dt_t[h:h + 1, :], (ROWS, ROWS))).astype(BF16)

    pieces = []
    for m in range(HEADS_PER_GROUP // 2):
        ha, hb = 2 * m, 2 * m + 1
        xp = xs[:, m * LANES:(m + 1) * LANES]
        w2 = jnp.concatenate([weights(ha), weights(hb)], axis=1)
        x2 = jnp.concatenate([jnp.where(lo_half, xp, 0.0), jnp.where(lo_half, 0.0, xp)], axis=0).astype(BF16)
        y = _dot(w2, x2)
        xw = (xp * jnp.where(lo_half, col(to_end, ha), col(to_end, hb))).astype(BF16)
        y_state = state_io(m, c_b, xw, b_b, e_cum)
        y = y + y_state * jnp.where(lo_half, col(e_cum, ha), col(e_cum, hb))
        pieces.append(y + xp * dsk_ref[:, m * LANES:(m + 1) * LANES])
    yg = jnp.concatenate(pieces, axis=1) * _silu(z_ref[...])
    ms = jnp.mean(yg * yg, axis=-1, keepdims=True)
    y_ref[...] = (yg * lax.rsqrt(ms + EPS) * nrm_ref[...]).astype(BF16)


def _pair_decay(e_cum, row, m):
    top = lax.broadcasted_iota(jnp.int32, (2 * SSM_HEADDIM, SSM_DSTATE), 0) < SSM_HEADDIM
    ea = jnp.broadcast_to(e_cum[row:row + 1, 2 * m:2 * m + 1], top.shape)
    eb = jnp.broadcast_to(e_cum[row:row + 1, 2 * m + 1:2 * m + 2], top.shape)
    return jnp.where(top, ea, eb)


def _ssd_prompt_kernel(z_ref, xs_ref, b_ref, c_ref, dt_ref, cwx_ref, cbx_ref, cwb_ref, cbb_ref, cwc_ref, cbc_ref,
                       dtb_ref, alog_ref, dsk_ref, nrm_ref, ltri_ref, y_ref, hs_ref, px_ref, pb_ref, pc_ref):
    @pl.when(pl.program_id(2) == 0)
    def _():
        hs_ref[...] = jnp.zeros_like(hs_ref)
        px_ref[...] = jnp.zeros_like(px_ref)
        pb_ref[...] = jnp.zeros_like(pb_ref)
        pc_ref[...] = jnp.zeros_like(pc_ref)

    def conv(x_ref, p_ref, w_ref, bias_ref):
        x = x_ref[...]
        x3 = x.reshape(ROWS // SUBLANES, SUBLANES, x.shape[1])
        pred = jnp.concatenate([p_ref[...][None], x3[:-1]], axis=0)
        out = _conv_piece(x, pred, w_ref, bias_ref)
        p_ref[...] = x3[-1]
        return out

    xs = conv(xs_ref, px_ref, cwx_ref, cbx_ref)
    bm = conv(b_ref, pb_ref, cwb_ref, cbb_ref)
    cm = conv(c_ref, pc_ref, cwc_ref, cbc_ref)

    def state_io(m, c_b, xw, b_b, e_cum):
        rows = pl.ds(m * 2 * SSM_HEADDIM, 2 * SSM_HEADDIM)
        h = hs_ref[0, rows, :]
        hs_ref[0, rows, :] = h * _pair_decay(e_cum, ROWS - 1, m) + _dot_tn(xw, b_b)
        return _dot_nt(c_b, h.astype(BF16))

    _ssd_core(CHUNK, z_ref, dt_ref, dtb_ref, alog_ref, dsk_ref, nrm_ref, ltri_ref, xs, bm, cm, y_ref, state_io)


def _ssd_sample_kernel(seq, z_ref, xs_ref, b_ref, c_ref, dt_ref, cwx_ref, cbx_ref, cwb_ref, cbb_ref, cwc_ref,
                       cbc_ref, dtb_ref, alog_ref, dsk_ref, nrm_ref, ltri_ref, bmask_ref, px_ref, pb_ref, pc_ref,
                       h0_ref, y_ref, hs_ref):
    nb = SAMPLE_PER_TILE
    xs = _conv_piece(xs_ref[...], px_ref[...], cwx_ref, cbx_ref)
    bm = _conv_piece(b_ref[...], pb_ref[...], cwb_ref, cbb_ref)
    cm = _conv_piece(c_ref[...], pc_ref[...], cwc_ref, cbc_ref)
    bmask = bmask_ref[...]

    def state_io(m, c_b, xw, b_b, e_cum):
        rows = pl.ds(m * 2 * SSM_HEADDIM, 2 * SSM_HEADDIM)
        c_bd = jnp.concatenate([c_b] * nb, axis=1) * bmask
        b_bd = jnp.concatenate([b_b] * nb, axis=1) * bmask
        hs = [h0_ref[i, rows, :] for i in range(nb)]
        h_cat = jnp.concatenate(hs, axis=1).astype(BF16)
        dh = _dot_tn(xw, b_bd)
        for i in range(nb):
            decay = _pair_decay(e_cum, i * seq + seq - 1, m)
            hs_ref[i, rows, :] = hs[i] * decay + dh[:, i * SSM_DSTATE:(i + 1) * SSM_DSTATE]
        return _dot_nt(c_bd, h_cat)

    _ssd_core(seq, z_ref, dt_ref, dtb_ref, alog_ref, dsk_ref, nrm_ref, ltri_ref, xs, bm, cm, y_ref, state_io)


def _ssd_tables(c):
    r = jnp.arange(ROWS)
    ltri = ((r[:, None] >= r[None, :]) & ((r[:, None] // c) == (r[None, :] // c))).astype(BF16)
    return ltri


def _ssd_specs(row_block):
    def at(width, off):
        return lambda *ids: (row_block(*ids)[0], off // width + row_block(*ids)[1])

    def grp(rows, width):
        return pl.BlockSpec((rows, width), lambda *ids: (0, row_block(*ids)[1]))

    n = SSM_DSTATE
    return [
        pl.BlockSpec((ROWS, GROUP_INNER), at(GROUP_INNER, _Cols.Z)),
        pl.BlockSpec((ROWS, GROUP_INNER), at(GROUP_INNER, _Cols.XS)),
        pl.BlockSpec((ROWS, n), at(n, _Cols.B)),
        pl.BlockSpec((ROWS, n), at(n, _Cols.C)),
        pl.BlockSpec((ROWS, LANES), at(LANES, _Cols.DT)),
        grp(CONV_WIDTH, GROUP_INNER), grp(1, GROUP_INNER),
        grp(CONV_WIDTH, n), grp(1, n), grp(CONV_WIDTH, n), grp(1, n),
        grp(1, LANES), grp(1, LANES), grp(1, GROUP_INNER), grp(1, GROUP_INNER),
    ]


def _ssd_params(conv_w, conv_b, dt_bias, a_log, d_skip, ssm_norm):
    gn = SSM_GROUPS * SSM_DSTATE
    cwx, cwb, cwc = conv_w[:, :SSM_INNER], conv_w[:, SSM_INNER:SSM_INNER + gn], conv_w[:, SSM_INNER + gn:]
    cb = conv_b.reshape(1, CONV_DIM)
    cbx, cbb, cbc = cb[:, :SSM_INNER], cb[:, SSM_INNER:SSM_INNER + gn], cb[:, SSM_INNER + gn:]

    def per_group_lanes(v):
        v = v.reshape(SSM_GROUPS, HEADS_PER_GROUP)
        return jnp.pad(v, ((0, 0), (0, LANES - HEADS_PER_GROUP))).reshape(1, SSM_GROUPS * LANES)

    dsk = jnp.repeat(d_skip, SSM_HEADDIM).reshape(1, SSM_INNER)
    return (cwx, cbx, cwb, cbb, cwc, cbc, per_group_lanes(dt_bias), per_group_lanes(a_log), dsk,
            ssm_norm.reshape(1, SSM_INNER))


def _ssd_prompt(proj, bsz, seq, params):
    nch = seq // ROWS
    specs = _ssd_specs(lambda b, g, c: (b * nch + c, g))
    specs.append(pl.BlockSpec((ROWS, ROWS), lambda b, g, c: (0, 0)))
    gh = HEADS_PER_GROUP * SSM_HEADDIM
    return pl.pallas_call(
        _ssd_prompt_kernel,
        out_shape=(jax.ShapeDtypeStruct((bsz * seq, SSM_INNER), BF16),
                   jax.ShapeDtypeStruct((bsz, SSM_HEADS * SSM_HEADDIM, SSM_DSTATE), F32)),
        grid=(bsz, SSM_GROUPS, nch),
        in_specs=specs,
        out_specs=(pl.BlockSpec((ROWS, GROUP_INNER), lambda b, g, c: (b * nch + c, g)),
                   pl.BlockSpec((1, gh, SSM_DSTATE), lambda b, g, c: (b, g, 0))),
        scratch_shapes=[pltpu.VMEM((SUBLANES, GROUP_INNER), F32), pltpu.VMEM((SUBLANES, SSM_DSTATE), F32),
                        pltpu.VMEM((SUBLANES, SSM_DSTATE), F32)],
        compiler_params=_params(("parallel", "parallel", "arbitrary")),
        name="ssd_prompt",
    )(proj, proj, proj, proj, proj, *params, _ssd_tables(CHUNK))


def _ssd_sample(proj, row0, bsz, seq, params, state, conv_state):
    ntile = bsz // SAMPLE_PER_TILE
    rb0 = row0 // ROWS
    gn = SSM_GROUPS * SSM_DSTATE
    specs = _ssd_specs(lambda t, g: (rb0 + t, g))
    gh = HEADS_PER_GROUP * SSM_HEADDIM
    cs = jnp.pad(conv_state, ((0, 0), (SUBLANES - (CONV_WIDTH - 1), 0), (0, 0)))
    csx, csb, csc = cs[:, :, :SSM_INNER], cs[:, :, SSM_INNER:SSM_INNER + gn], cs[:, :, SSM_INNER + gn:]
    pred = lambda width: pl.BlockSpec((SAMPLE_PER_TILE, SUBLANES, width), lambda t, g: (t, 0, g))
    state_spec = pl.BlockSpec((SAMPLE_PER_TILE, gh, SSM_DSTATE), lambda t, g: (t, g, 0))
    specs += [
        pl.BlockSpec((ROWS, ROWS), lambda t, g: (0, 0)),
        pl.BlockSpec((ROWS, SAMPLE_PER_TILE * LANES), lambda t, g: (0, 0)),
        pred(GROUP_INNER), pred(SSM_DSTATE), pred(SSM_DSTATE),
        state_spec,
    ]
    state2 = state.reshape(bsz, SSM_HEADS * SSM_HEADDIM, SSM_DSTATE)
    return pl.pallas_call(
        functools.partial(_ssd_sample_kernel, seq),
        out_shape=(jax.ShapeDtypeStruct((bsz * seq, SSM_INNER), BF16), jax.ShapeDtypeStruct(state2.shape, F32)),
        grid=(ntile, SSM_GROUPS),
        in_specs=specs,
        out_specs=(pl.BlockSpec((ROWS, GROUP_INNER), lambda t, g: (t, g)), state_spec),
        compiler_params=_params(("parallel", "parallel")),
        name="ssd_sample",
    )(proj, proj, proj, proj, proj, *params, _ssd_tables(seq), _block_mask(BF16), csx, csb, csc, state2)


def _post_kernel(n_prompt_tiles, ap_ref, as_ref, yp_ref, ys_ref, ga_ref, gb_ref, x_ref, wr_ref, ws_ref, wo_ref,
                 nw_ref, wrt_ref, brt_ref, x1_ref, h2_ref, lg_ref):
    is_prompt = pl.program_id(0) < n_prompt_tiles
    branch_a = _dot(jnp.where(is_prompt, ap_ref[...], as_ref[...]), wr_ref[...])
    branch_b = _dot(jnp.where(is_prompt, yp_ref[...], ys_ref[...]), ws_ref[...])
    merged = jax.nn.sigmoid(ga_ref[...]) * branch_a + jax.nn.sigmoid(gb_ref[...]) * branch_b
    x1 = x_ref[...] + _dot(merged.astype(BF16), wo_ref[...])
    x1_ref[...] = x1
    ms = jnp.mean(x1 * x1, axis=-1, keepdims=True)
    h2 = x1 * lax.rsqrt(ms + EPS) * nw_ref[...]
    h2_ref[...] = h2
    lg_ref[...] = _dot(h2.astype(BF16), wrt_ref[...]) + brt_ref[...]


def _post(a_p, a_s, y_p, y_s, proj, x, w_ret, w_ssm, w_out, norm_ffn, w_router, b_router):
    t = x.shape[0]
    tm = _pick(math.gcd(a_p.shape[0], a_s.shape[0]), 512)
    npt = a_p.shape[0] // tm
    row = lambda width, blk: pl.BlockSpec((tm, width), lambda i: (i, blk))
    prow = lambda width: pl.BlockSpec((tm, width), lambda i: (jnp.minimum(i, npt - 1), 0))
    srow = lambda width: pl.BlockSpec((tm, width), lambda i: (jnp.maximum(i - npt, 0), 0))
    full = lambda *shape: pl.BlockSpec(shape, lambda i: (0,) * len(shape))
    return pl.pallas_call(
        functools.partial(_post_kernel, npt),
        out_shape=(jax.ShapeDtypeStruct((t, D_MODEL), F32), jax.ShapeDtypeStruct((t, D_MODEL), F32),
                   jax.ShapeDtypeStruct((t, LANES), F32)),
        grid=(t // tm,),
        in_specs=[prow(RET_V), srow(RET_V), prow(SSM_INNER), srow(SSM_INNER), row(D_MODEL, _Cols.GA // D_MODEL),
                  row(D_MODEL, _Cols.GB // D_MODEL), row(D_MODEL, 0),
                  full(RET_V, D_MODEL), full(SSM_INNER, D_MODEL), full(D_MODEL, D_MODEL), full(1, D_MODEL),
                  full(D_MODEL, LANES), full(1, LANES)],
        out_specs=(row(D_MODEL, 0), row(D_MODEL, 0), row(LANES, 0)),
        compiler_params=_params(("parallel",)),
        name="post_mixer",
    )(a_p, a_s, y_p, y_s, proj, proj, x, w_ret, w_ssm, w_out, norm_ffn, w_router, b_router)


def _route_kernel(lg_ref, lstrict_ref, meta_ref, gate_ref, cnt_ref):
    @pl.when(pl.program_id(0) == 0)
    def _():
        cnt_ref[...] = jnp.zeros_like(cnt_ref)

    tm = lg_ref.shape[0]
    lane = lax.broadcasted_iota(jnp.int32, (tm, LANES), 1)
    lane_f = lane.astype(F32)
    cur = jnp.where(lane < N_EXPERTS, lg_ref[...], -jnp.inf)
    vals, hots = [], []
    for _ in range(TOP_K):
        m = jnp.max(cur, axis=1, keepdims=True)
        idx = jnp.min(jnp.where(cur == m, lane_f, float(LANES)), axis=1, keepdims=True)
        hot = lane_f == idx
        vals.append(m)
        hots.append(hot)
        cur = jnp.where(hot, -jnp.inf, cur)
    exps = [jnp.exp(v - vals[0]) for v in vals]
    denom = exps[0] + exps[1] + exps[2] + exps[3]
    sel = hots[0] | hots[1] | hots[2] | hots[3]
    self32 = sel.astype(F32)
    before = _dot(lstrict_ref[...], self32.astype(BF16)) + cnt_ref[...]
    cnt_ref[...] = cnt_ref[...] + jnp.sum(self32, axis=0, keepdims=True)
    meta = jnp.zeros((tm, LANES), F32)
    gates = jnp.zeros((tm, LANES), F32)
    for k in range(TOP_K):
        e_k = jnp.sum(jnp.where(hots[k], lane_f, 0.0), axis=1, keepdims=True)
        p_k = jnp.sum(jnp.where(hots[k], before, 0.0), axis=1, keepdims=True)
        meta = jnp.where(lane == k, e_k, meta)
        meta = jnp.where(lane == SUBLANES + k, p_k, meta)
        gates = jnp.where(lane == k, exps[k] / denom, gates)
    gate_ref[...] = gates
    meta_ref[...] = meta.T[:2 * SUBLANES, :].astype(jnp.int32)


def _route(logits):
    t = logits.shape[0]
    tm = _pick(t, 512)
    r = jnp.arange(tm)
    lstrict = (r[:, None] > r[None, :]).astype(BF16)
    return pl.pallas_call(
        _route_kernel,
        out_shape=(jax.ShapeDtypeStruct((2 * SUBLANES, t), jnp.int32),
                   jax.ShapeDtypeStruct((t, LANES), F32),
                   jax.ShapeDtypeStruct((1, LANES), F32)),
        grid=(t // tm,),
        in_specs=[pl.BlockSpec((tm, LANES), lambda i: (i, 0)), pl.BlockSpec((tm, tm), lambda i: (0, 0))],
        out_specs=(pl.BlockSpec((2 * SUBLANES, tm), lambda i: (0, i)),
                   pl.BlockSpec((tm, LANES), lambda i: (i, 0)),
                   pl.BlockSpec((1, LANES), lambda i: (0, 0))),
        compiler_params=_params(("arbitrary",)),
        name="route",
    )(logits, lstrict)


def _row_copy(src_ref, src_row, dst_ref, dst_row, sem):
    return pltpu.make_async_copy(src_ref.at[pl.ds(src_row, 1), :], dst_ref.at[pl.ds(dst_row, 1), :], sem)


def _dispatch_kernel(pstart_ref, meta_ref, h2_ref, xs_in_ref, xs_ref, sem):
    del xs_in_ref
    tm = h2_ref.shape[0]

    def copy(t, k):
        dest = pstart_ref[meta_ref[k, t]] + meta_ref[SUBLANES + k, t]
        return _row_copy(h2_ref, t, xs_ref, dest, sem)

    def issue(t, carry):
        for k in range(TOP_K):
            copy(t, k).start()
        return carry

    def drain(t, carry):
        for k in range(TOP_K):
            copy(t, k).wait()
        return carry

    lax.fori_loop(0, tm, issue, 0)
    lax.fori_loop(0, tm, drain, 0)


def _dispatch(pstart, meta, h2, n_rows):
    t = h2.shape[0]
    tm = _pick(t, 256)
    xs0 = jnp.zeros((n_rows, D_MODEL), F32)
    return pl.pallas_call(
        _dispatch_kernel,
        out_shape=jax.ShapeDtypeStruct((n_rows, D_MODEL), F32),
        grid_spec=pltpu.PrefetchScalarGridSpec(
            num_scalar_prefetch=1,
            grid=(t // tm,),
            in_specs=[pl.BlockSpec((2 * SUBLANES, tm), lambda i, ps: (0, i), memory_space=pltpu.SMEM),
                      pl.BlockSpec((tm, D_MODEL), lambda i, ps: (i, 0)),
                      pl.BlockSpec(memory_space=pl.ANY)],
            out_specs=pl.BlockSpec(memory_space=pl.ANY),
            scratch_shapes=[pltpu.SemaphoreType.DMA],
        ),
        input_output_aliases={3: 0},
        compiler_params=_params(("arbitrary",)),
        name="dispatch",
    )(pstart, meta, h2, xs0)


def _combine_kernel(pstart_ref, meta_ref, gate_ref, x1_ref, nw_ref, yb_ref, o_ref, buf_ref, sem):
    tm = x1_ref.shape[0]

    def copy(t, k):
        src = pstart_ref[meta_ref[k, t]] + meta_ref[SUBLANES + k, t]
        return _row_copy(yb_ref, src, buf_ref.at[k], t, sem)

    def issue(t, carry):
        for k in range(TOP_K):
            copy(t, k).start()
        return carry

    def drain(t, carry):
        for k in range(TOP_K):
            copy(t, k).wait()
        return carry

    lax.fori_loop(0, tm, issue, 0)
    lax.fori_loop(0, tm, drain, 0)
    gates = gate_ref[...]
    moe = buf_ref[0] * gates[:, 0:1]
    for k in range(1, TOP_K):
        moe = moe + buf_ref[k] * gates[:, k:k + 1]
    x2 = x1_ref[...] + moe
    ms = jnp.mean(x2 * x2, axis=-1, keepdims=True)
    o_ref[...] = x2 * lax.rsqrt(ms + EPS) * nw_ref[...]


def _combine(pstart, meta, gates, x1, norm_final, yb):
    t = x1.shape[0]
    tm = _pick(t, 256)
    return pl.pallas_call(
        _combine_kernel,
        out_shape=jax.ShapeDtypeStruct((t, D_MODEL), F32),
        grid_spec=pltpu.PrefetchScalarGridSpec(
            num_scalar_prefetch=1,
            grid=(t // tm,),
            in_specs=[pl.BlockSpec((2 * SUBLANES, tm), lambda i, ps: (0, i), memory_space=pltpu.SMEM),
                      pl.BlockSpec((tm, LANES), lambda i, ps: (i, 0)),
                      pl.BlockSpec((tm, D_MODEL), lambda i, ps: (i, 0)),
                      pl.BlockSpec((1, D_MODEL), lambda i, ps: (0, 0)),
                      pl.BlockSpec(memory_space=pl.ANY)],
            out_specs=pl.BlockSpec((tm, D_MODEL), lambda i, ps: (i, 0)),
            scratch_shapes=[pltpu.VMEM((TOP_K, tm, D_MODEL), F32), pltpu.SemaphoreType.DMA],
        ),
        compiler_params=_params(("arbitrary",)),
        name="combine",
    )(pstart, meta, gates, x1, norm_final, yb)


def _expert_kernel(be_ref, nv_ref, xs_ref, wgu_ref, bgu_ref, wd_ref, bd_ref, yb_ref):
    i = pl.program_id(0)

    @pl.when(i < nv_ref[0])
    def _():
        gu = _dot(xs_ref[...].astype(BF16), wgu_ref[0]) + bgu_ref[0]
        glu = jnp.minimum(gu[:, :D_FF], SWIGLU_LIMIT)
        lin = jnp.clip(gu[:, D_FF:], -SWIGLU_LIMIT, SWIGLU_LIMIT)
        act = glu * jax.nn.sigmoid(SWIGLU_ALPHA * glu) * (lin + 1.0)
        yb_ref[...] = _dot(act.astype(BF16), wd_ref[0]) + bd_ref[0]

    @pl.when(i >= nv_ref[0])
    def _():
        yb_ref[...] = jnp.zeros_like(yb_ref)


def _experts(block_expert, n_valid, xs, w_gu, b_gu, w_d, b_d):
    n_rows = xs.shape[0]
    nb = n_rows // EXPERT_BLOCK
    return pl.pallas_call(
        _expert_kernel,
        out_shape=jax.ShapeDtypeStruct((n_rows, D_MODEL), F32),
        grid_spec=pltpu.PrefetchScalarGridSpec(
            num_scalar_prefetch=2,
            grid=(nb,),
            in_specs=[pl.BlockSpec((EXPERT_BLOCK, D_MODEL), lambda i, be, nv: (i, 0)),
                      pl.BlockSpec((1, D_MODEL, 2 * D_FF), lambda i, be, nv: (be[i], 0, 0)),
                      pl.BlockSpec((1, 1, 2 * D_FF), lambda i, be, nv: (be[i], 0, 0)),
                      pl.BlockSpec((1, D_FF, D_MODEL), lambda i, be, nv: (be[i], 0, 0)),
                      pl.BlockSpec((1, 1, D_MODEL), lambda i, be, nv: (be[i], 0, 0))],
            out_specs=pl.BlockSpec((EXPERT_BLOCK, D_MODEL), lambda i, be, nv: (i, 0)),
        ),
        compiler_params=_params(("arbitrary",)),
        name="experts",
    )(block_expert, n_valid, xs, w_gu, b_gu, w_d, b_d)


def _expert_layout(counts, n_blocks):
    counts = counts.astype(jnp.int32)
    padded = (counts + EXPERT_BLOCK - 1) // EXPERT_BLOCK * EXPERT_BLOCK
    pends = jnp.cumsum(padded)
    pstart = pends - padded
    block_start = jnp.arange(n_blocks, dtype=jnp.int32) * EXPERT_BLOCK
    block_expert = jnp.minimum(jnp.searchsorted(pends, block_start, side='right'), N_EXPERTS - 1)
    n_valid = (pends[-1:] // EXPERT_BLOCK).astype(jnp.int32)
    return pstart.astype(jnp.int32), block_expert.astype(jnp.int32), n_valid


def _permute_w_in(w_in):
    sizes = [RET_QK, RET_QK, RET_V, RET_V, SSM_INNER, CONV_DIM, SSM_HEADS, D_MODEL, D_MODEL]
    offs = [0]
    for s in sizes:
        offs.append(offs[-1] + s)
    q, k, v, g_ret, z, xbc, dt, g_a, g_b = [w_in[:, offs[i]:offs[i + 1]] for i in range(len(sizes))]
    gn = SSM_GROUPS * SSM_DSTATE
    xs, bm, cm = xbc[:, :SSM_INNER], xbc[:, SSM_INNER:SSM_INNER + gn], xbc[:, SSM_INNER + gn:]
    dt4 = jnp.pad(dt.reshape(D_MODEL, SSM_GROUPS, HEADS_PER_GROUP),
                  ((0, 0), (0, 0), (0, LANES - HEADS_PER_GROUP))).reshape(D_MODEL, SSM_GROUPS * LANES)
    return jnp.concatenate([z, xs, v, g_ret, g_a, g_b, q, k, bm, cm, dt4], axis=1).astype(BF16)


def _conv_state_out(proj, row0, bsz, seq):
    rows = proj[row0:row0 + bsz * seq].reshape(bsz, seq, _Cols.TOTAL)[:, seq - (CONV_WIDTH - 1):]
    gn = SSM_GROUPS * SSM_DSTATE
    return jnp.concatenate([rows[..., _Cols.XS:_Cols.XS + SSM_INNER], rows[..., _Cols.B:_Cols.B + gn],
                            rows[..., _Cols.C:_Cols.C + gn]], axis=-1)


def _forward(x_prompt, x_sample, state_ret, state_ssm, state_conv, norm_mix, w_in, ret_norm, w_out_ret,
             conv_w, conv_b, dt_bias, a_log, d_skip, ssm_norm, w_out_ssm, w_out, norm_ffn,
             w_router, b_router, w_gate_up, b_gate_up, w_down, b_down, norm_final):
    bp, lp, _ = x_prompt.shape
    bs, ls, _ = x_sample.shape
    assert lp % ROWS == 0 and bs % SAMPLE_PER_TILE == 0 and ls * SAMPLE_PER_TILE == ROWS
    tp, ts = bp * lp, bs * ls
    t_all = tp + ts
    x = jnp.concatenate([x_prompt.reshape(tp, D_MODEL), x_sample.reshape(ts, D_MODEL)], axis=0)

    proj = _inproj(x, norm_mix[0].reshape(1, D_MODEL), _permute_w_in(w_in[0]))

    a_p, ret_p = _retention_prompt(proj, bp, lp, ret_norm[0])
    a_s, ret_s = _retention_sample(proj, tp, bs, ls, state_ret[0], ret_norm[0])

    sp = _ssd_params(conv_w[0], conv_b[0], dt_bias[0], a_log[0], d_skip[0], ssm_norm[0])
    y_p, ssm_p = _ssd_prompt(proj, bp, lp, sp)
    y_s, ssm_s = _ssd_sample(proj, tp, bs, ls, sp, state_ssm[0], state_conv[0])

    w_router_pad = jnp.pad(w_router[0], ((0, 0), (0, LANES - N_EXPERTS))).astype(BF16)
    b_router_pad = jnp.pad(b_router[0], (0, LANES - N_EXPERTS)).reshape(1, LANES)
    x1, h2, logits = _post(a_p, a_s, y_p, y_s, proj, x, w_out_ret[0].astype(BF16), w_out_ssm[0].astype(BF16),
                           w_out[0].astype(BF16), norm_ffn[0].reshape(1, D_MODEL), w_router_pad, b_router_pad)

    meta, gates, counts = _route(logits)
    n_blocks = -(-(t_all * TOP_K + N_EXPERTS * (EXPERT_BLOCK - 1)) // EXPERT_BLOCK)
    pstart, block_expert, n_valid = _expert_layout(counts[0, :N_EXPERTS], n_blocks)
    xs = _dispatch(pstart, meta, h2, n_blocks * EXPERT_BLOCK)
    yb = _experts(block_expert, n_valid, xs, w_gate_up[0].astype(BF16),
                  b_gate_up[0].reshape(N_EXPERTS, 1, 2 * D_FF), w_down[0].astype(BF16),
                  b_down[0].reshape(N_EXPERTS, 1, D_MODEL))
    y = _combine(pstart, meta, gates, x1, norm_final.reshape(1, D_MODEL), yb)

    shape_s = (1, bs, SSM_HEADS, SSM_HEADDIM, SSM_DSTATE)
    shape_p = (1, bp, SSM_HEADS, SSM_HEADDIM, SSM_DSTATE)
    return (y[:tp].reshape(bp, lp, D_MODEL), y[tp:].reshape(bs, ls, D_MODEL),
            ret_p[None], ret_s[None], ssm_p.reshape(shape_p), ssm_s.reshape(shape_s),
            _conv_state_out(proj, 0, bp, lp)[None], _conv_state_out(proj, tp, bs, ls)[None])


def kernel(x_prompt, x_sample, state_ret, state_ssm, state_conv, norm_mix, w_in, ret_norm, w_out_ret, conv_w, conv_b, dt_bias, a_log, d_skip, ssm_norm, w_out_ssm, w_out, norm_ffn, w_router, b_router, w_gate_up, b_gate_up, w_down, b_down, norm_final):
    return _forward(x_prompt, x_sample, state_ret, state_ssm, state_conv, norm_mix, w_in, ret_norm, w_out_ret,
                    conv_w, conv_b, dt_bias, a_log, d_skip, ssm_norm, w_out_ssm, w_out, norm_ffn,
                    w_router, b_router, w_gate_up, b_gate_up, w_down, b_down, norm_final)
```

```python
import functools
import math

import jax
import jax.numpy as jnp
from jax import lax
from jax.experimental import pallas as pl
from jax.experimental.pallas import tpu as pltpu

F32 = jnp.float32
BF16 = jnp.bfloat16

D_MODEL = 1024
PAST_LEN = 16384
RET_HEADS = 4
RET_DK = 128
RET_DV = 256
RET_QK = RET_HEADS * RET_DK
RET_V = RET_HEADS * RET_DV
ROPE_BASE = 10000.0
SSM_INNER = 2 * D_MODEL
SSM_HEADDIM = 64
SSM_HEADS = SSM_INNER // SSM_HEADDIM
SSM_GROUPS = 4
SSM_DSTATE = 128
HEADS_PER_GROUP = SSM_HEADS // SSM_GROUPS
GROUP_INNER = SSM_INNER // SSM_GROUPS
CONV_WIDTH = 4
CONV_DIM = SSM_INNER + 2 * SSM_GROUPS * SSM_DSTATE
CHUNK = 128
N_EXPERTS = 32
TOP_K = 4
D_FF = D_MODEL
SWIGLU_LIMIT = 7.0
SWIGLU_ALPHA = 1.702
EPS = 1e-6

LANES = 128
SUBLANES = 8
ROWS = 128
SAMPLE_PER_TILE = 16
EXPERT_BLOCK = 256
ROW_TILES = D_MODEL // LANES
ISSUE_UNROLL = 4
VMEM_LIMIT = 56 * 1024 * 1024


class _Cols:
    Z = 0
    XS = Z + SSM_INNER
    V = XS + SSM_INNER
    GRET = V + RET_V
    GA = GRET + RET_V
    GB = GA + D_MODEL
    Q = GB + D_MODEL
    K = Q + RET_QK
    B = K + RET_QK
    C = B + SSM_GROUPS * SSM_DSTATE
    DT = C + SSM_GROUPS * SSM_DSTATE
    TOTAL = DT + SSM_GROUPS * LANES


def _pick(n, target):
    best = None
    for c in range(LANES, target + 1, LANES):
        if n % c == 0:
            best = c
    assert best is not None, (n, target)
    return best


def _params(sem, **kw):
    return pltpu.CompilerParams(dimension_semantics=sem, vmem_limit_bytes=VMEM_LIMIT, **kw)


def _dot(a, b):
    return jnp.dot(a, b, preferred_element_type=F32)


def _dot_nt(a, b):
    return lax.dot_general(a, b, (((1,), (1,)), ((), ())), preferred_element_type=F32)


def _dot_tn(a, b):
    return lax.dot_general(a, b, (((0,), (0,)), ((), ())), preferred_element_type=F32)


def _silu(x):
    return x * jax.nn.sigmoid(x)


def _store_row_tiles(ref, value, *lead):
    n = value.shape[0]
    for s in range(ROW_TILES):
        ref[(*lead, pl.ds(s, n, stride=ROW_TILES), slice(None))] = value[:, s * LANES:(s + 1) * LANES]


def _load_row_tiles(ref, n, *lead):
    return jnp.concatenate([ref[(*lead, pl.ds(s, n, stride=ROW_TILES), slice(None))] for s in range(ROW_TILES)],
                           axis=1)


def _inproj_kernel(x_ref, nw_ref, w_ref, o_ref, h_ref):
    @pl.when(pl.program_id(1) == 0)
    def _():
        x = x_ref[...]
        ms = jnp.mean(x * x, axis=-1, keepdims=True)
        h_ref[...] = (x * lax.rsqrt(ms + EPS) * nw_ref[...]).astype(BF16)

    o_ref[...] = _dot(h_ref[...], w_ref[...])


def _inproj(x, norm_w, w_perm):
    t = x.shape[0]
    tm = _pick(t, 1024)
    tn = _pick(_Cols.TOTAL, 1536)
    return pl.pallas_call(
        _inproj_kernel,
        out_shape=jax.ShapeDtypeStruct((t, _Cols.TOTAL), F32),
        grid=(t // tm, _Cols.TOTAL // tn),
        in_specs=[
            pl.BlockSpec((tm, D_MODEL), lambda i, j: (i, 0)),
            pl.BlockSpec((1, D_MODEL), lambda i, j: (0, 0)),
            pl.BlockSpec((D_MODEL, tn), lambda i, j: (0, j)),
        ],
        out_specs=pl.BlockSpec((tm, tn), lambda i, j: (i, j)),
        scratch_shapes=[pltpu.VMEM((tm, D_MODEL), BF16)],
        compiler_params=_params(("parallel", "arbitrary")),
        name="in_proj",
    )(x, norm_w, w_perm)


def _ret_log_decay():
    return jnp.log(1.0 - 2.0 ** (-5.0 - jnp.arange(RET_HEADS, dtype=F32)))


def _ret_tables(c):
    lg = _ret_log_decay()
    r = jnp.arange(ROWS)
    t = (r % c).astype(F32)
    seg = r // c
    diff = t[:, None] - t[None, :]
    ok = (seg[:, None] == seg[None, :]) & (diff >= 0)
    dm = jnp.where(ok[None], jnp.exp(lg[:, None, None] * jnp.maximum(diff, 0.0)[None]), 0.0)
    qd = jnp.exp(lg[:, None] * (t[None, :] + 1.0))
    kd = jnp.exp(lg[:, None] * (c - 1.0 - t[None, :]))
    cd = jnp.exp(lg * c)
    qd = jnp.broadcast_to(qd[:, :, None], (RET_HEADS, ROWS, LANES))
    kd = jnp.broadcast_to(kd[:, :, None], (RET_HEADS, ROWS, LANES))
    cd = jnp.broadcast_to(cd[:, None, None], (RET_HEADS, 1, RET_DV))
    return dm, qd, kd, cd


def _rope_tables(pos):
    half = RET_DK // 2
    inv_freq = 1.0 / (ROPE_BASE ** jnp.linspace(0.0, 1.0, half, dtype=F32))
    ang = pos.astype(F32)[:, None] * inv_freq[None, :]
    cos, sin = jnp.cos(ang), jnp.sin(ang)
    return jnp.concatenate([cos, cos], -1), jnp.concatenate([-sin, sin], -1)


def _block_mask(dtype):
    r = jnp.arange(ROWS) // (ROWS // SAMPLE_PER_TILE)
    b = jnp.arange(SAMPLE_PER_TILE * LANES) // LANES
    return (r[:, None] == b[None, :]).astype(dtype)


def _ret_common(q_ref, k_ref, v_ref, g_ref, cos_ref, sin_ref, dm_ref, qd_ref, kd_ref, nw_ref):
    cosf, sinf = cos_ref[...], sin_ref[...]
    q, k = q_ref[...], k_ref[...]
    qr = q * cosf + pltpu.roll(q, RET_DK // 2, 1) * sinf
    kr = (k * cosf + pltpu.roll(k, RET_DK // 2, 1) * sinf) * (RET_DK ** -0.5)
    vb = v_ref[...].astype(BF16)
    scores = _dot_nt(qr.astype(BF16), kr.astype(BF16)) * dm_ref[0]
    o_intra = _dot(scores.astype(BF16), vb)
    q_state = (qr * qd_ref[0]).astype(BF16)
    k_state = kr * kd_ref[0]

    def finish(o):
        ms = jnp.mean(o * o, axis=-1, keepdims=True)
        on = o * lax.rsqrt(ms + EPS) * nw_ref[0]
        return (_silu(g_ref[...]) * on).astype(BF16)

    return o_intra, q_state, k_state, vb, finish


def _ret_prompt_kernel(q_ref, k_ref, v_ref, g_ref, cos_ref, sin_ref, dm_ref, qd_ref, kd_ref, cd_ref, nw_ref,
                       a_ref, s_ref):
    @pl.when(pl.program_id(2) == 0)
    def _():
        s_ref[...] = jnp.zeros_like(s_ref)

    o_intra, q_state, k_state, vb, finish = _ret_common(
        q_ref, k_ref, v_ref, g_ref, cos_ref, sin_ref, dm_ref, qd_ref, kd_ref, nw_ref)
    s = s_ref[0, 0]
    a_ref[...] = finish(o_intra + _dot(q_state, s.astype(BF16)))
    s_ref[0, 0] = s * cd_ref[0] + _dot_tn(k_state.astype(BF16), vb)


def _ret_sample_kernel(q_ref, k_ref, v_ref, g_ref, cos_ref, sin_ref, dm_ref, qd_ref, kd_ref, cd_ref, nw_ref,
                       bm_ref, bmt_ref, s0_ref, a_ref, s_ref):
    o_intra, q_state, k_state, vb, finish = _ret_common(
        q_ref, k_ref, v_ref, g_ref, cos_ref, sin_ref, dm_ref, qd_ref, kd_ref, nw_ref)
    nb = SAMPLE_PER_TILE
    s0 = s0_ref[:, 0]
    q_bd = jnp.concatenate([q_state] * nb, axis=1) * bm_ref[...]
    a_ref[...] = finish(o_intra + _dot(q_bd, s0.reshape(nb * RET_DK, RET_DV).astype(BF16)))
    k_t = k_state.T.astype(BF16)
    k_bd_t = jnp.concatenate([k_t] * nb, axis=0) * bmt_ref[...]
    ds = _dot(k_bd_t, vb).reshape(nb, RET_DK, RET_DV)
    s_ref[:, 0] = s0 * cd_ref[0] + ds


def _ret_specs(row_block):
    def at(width, off):
        return lambda *ids: (row_block(*ids)[0], off // width + row_block(*ids)[1])

    def head(*shape):
        return pl.BlockSpec((1,) + shape, lambda *ids: (row_block(*ids)[1],) + (0,) * len(shape))

    return [
        pl.BlockSpec((ROWS, RET_DK), at(RET_DK, _Cols.Q)),
        pl.BlockSpec((ROWS, RET_DK), at(RET_DK, _Cols.K)),
        pl.BlockSpec((ROWS, RET_DV), at(RET_DV, _Cols.V)),
        pl.BlockSpec((ROWS, RET_DV), at(RET_DV, _Cols.GRET)),
    ], head


def _retention_prompt(proj, bsz, seq, ret_norm):
    nch = seq // ROWS
    dm, qd, kd, cd = _ret_tables(CHUNK)
    cosf, sinf = _rope_tables(jnp.arange(seq, dtype=jnp.int32))
    rb = lambda b, h, c: (b * nch + c, h)
    specs, head = _ret_specs(rb)
    specs += [
        pl.BlockSpec((ROWS, RET_DK), lambda b, h, c: (c, 0)),
        pl.BlockSpec((ROWS, RET_DK), lambda b, h, c: (c, 0)),
        head(ROWS, ROWS), head(ROWS, LANES), head(ROWS, LANES), head(1, RET_DV), head(1, RET_DV),
    ]
    return pl.pallas_call(
        _ret_prompt_kernel,
        out_shape=(jax.ShapeDtypeStruct((bsz * seq, RET_V), BF16),
                   jax.ShapeDtypeStruct((bsz, RET_HEADS, RET_DK, RET_DV), F32)),
        grid=(bsz, RET_HEADS, nch),
        in_specs=specs,
        out_specs=(pl.BlockSpec((ROWS, RET_DV), lambda b, h, c: (b * nch + c, h)),
                   pl.BlockSpec((1, 1, RET_DK, RET_DV), lambda b, h, c: (b, h, 0, 0))),
        compiler_params=_params(("parallel", "parallel", "arbitrary")),
        name="retention_prompt",
    )(proj, proj, proj, proj, cosf, sinf, dm, qd, kd, cd, ret_norm.reshape(RET_HEADS, 1, RET_DV))


def _retention_sample(proj, row0, bsz, seq, state, ret_norm):
    ntile = bsz // SAMPLE_PER_TILE
    rb0 = row0 // ROWS
    dm, qd, kd, cd = _ret_tables(seq)
    pos = PAST_LEN + jnp.arange(seq, dtype=jnp.int32)
    cosf, sinf = _rope_tables(jnp.tile(pos, SAMPLE_PER_TILE))
    rb = lambda t, h: (rb0 + t, h)
    specs, head = _ret_specs(rb)
    full = lambda *shape: pl.BlockSpec(shape, lambda t, h: (0,) * len(shape))
    specs += [
        full(ROWS, RET_DK), full(ROWS, RET_DK),
        head(ROWS, ROWS), head(ROWS, LANES), head(ROWS, LANES), head(1, RET_DV), head(1, RET_DV),
        full(ROWS, SAMPLE_PER_TILE * LANES), full(SAMPLE_PER_TILE * LANES, ROWS),
        pl.BlockSpec((SAMPLE_PER_TILE, 1, RET_DK, RET_DV), lambda t, h: (t, h, 0, 0)),
    ]
    bm = _block_mask(BF16)
    return pl.pallas_call(
        _ret_sample_kernel,
        out_shape=(jax.ShapeDtypeStruct((bsz * seq, RET_V), BF16),
                   jax.ShapeDtypeStruct(state.shape, F32)),
        grid=(ntile, RET_HEADS),
        in_specs=specs,
        out_specs=(pl.BlockSpec((ROWS, RET_DV), lambda t, h: (t, h)),
                   pl.BlockSpec((SAMPLE_PER_TILE, 1, RET_DK, RET_DV), lambda t, h: (t, h, 0, 0))),
        compiler_params=_params(("parallel", "parallel")),
        name="retention_sample",
    )(proj, proj, proj, proj, cosf, sinf, dm, qd, kd, cd, ret_norm.reshape(RET_HEADS, 1, RET_DV),
      bm, bm.T, state)


def _softplus(x):
    return jnp.maximum(x, 0.0) + jnp.log1p(jnp.exp(-jnp.abs(x)))


def _conv_piece(x, pred, w_ref, b_ref):
    width = x.shape[1]
    x3 = x.reshape(ROWS // SUBLANES, SUBLANES, width)
    t8 = lax.broadcasted_iota(jnp.int32, x3.shape, 1)
    acc = b_ref[...].reshape(1, 1, width)
    for i in range(CONV_WIDTH):
        s = CONV_WIDTH - 1 - i
        tap = x3 if s == 0 else pltpu.roll(jnp.where(t8 >= SUBLANES - s, pred, x3), s, 1)
        acc = acc + tap * w_ref[pl.ds(i, 1), :].reshape(1, 1, width)
    return _silu(acc).reshape(ROWS, width)


def _ssd_core(c, z_ref, dt_ref, dtb_ref, alog_ref, dsk_ref, nrm_ref, ltri_ref, xs, bm, cm, y_ref, state_io):
    dt = _softplus(dt_ref[...] + dtb_ref[...])
    d_a = dt * (-jnp.exp(alog_ref[...]))
    hi = d_a.astype(BF16)
    r1 = d_a - hi.astype(F32)
    mid = r1.astype(BF16)
    lo = (r1 - mid.astype(F32)).astype(BF16)
    ltri = ltri_ref[...]
    cum = _dot(ltri, hi) + _dot(ltri, mid) + _dot(ltri, lo)
    cum3 = cum.reshape(ROWS // c, c, LANES)
    c_last = jnp.broadcast_to(cum3[:, c - 1:c, :], cum3.shape).reshape(ROWS, LANES)
    to_end = jnp.exp(c_last - cum) * dt
    e_cum = jnp.exp(cum)
    cum_t, dt_t = cum.T, dt.T

    b_b, c_b = bm.astype(BF16), cm.astype(BF16)
    cb = _dot_nt(c_b, b_b)
    ri = lax.broadcasted_iota(jnp.int32, (ROWS, ROWS), 0)
    ci = lax.broadcasted_iota(jnp.int32, (ROWS, ROWS), 1)
    shift = c.bit_length() - 1
    causal = (ci <= ri) & (jnp.right_shift(ri, shift) == jnp.right_shift(ci, shift))
    lo_half = lax.broadcasted_iota(jnp.int32, (ROWS, LANES), 1) < SSM_HEADDIM

    def col(a, h):
        return jnp.broadcast_to(a[:, h:h + 1], (ROWS, LANES))

    def weights(h):
        seg = col(cum, h) - jnp.broadcast_to(cum_t[h:h + 1, :], (ROWS, ROWS))
        decay = jnp.exp(jnp.where(causal, seg, -jnp.inf))
        return (cb * decay * jnp.broadcast_to(dt_t[h:h + 1, :], (ROWS, ROWS))).astype(BF16)

    pieces = []
    for m in range(HEADS_PER_GROUP // 2):
        ha, hb = 2 * m, 2 * m + 1
        xp = xs[:, m * LANES:(m + 1) * LANES]
        w2 = jnp.concatenate([weights(ha), weights(hb)], axis=1)
        x2 = jnp.concatenate([jnp.where(lo_half, xp, 0.0), jnp.where(lo_half, 0.0, xp)], axis=0).astype(BF16)
        y = _dot(w2, x2)
        xw = (xp * jnp.where(lo_half, col(to_end, ha), col(to_end, hb))).astype(BF16)
        y_state = state_io(m, c_b, xw, b_b, e_cum)
        y = y + y_state * jnp.where(lo_half, col(e_cum, ha), col(e_cum, hb))
        pieces.append(y + xp * dsk_ref[:, m * LANES:(m + 1) * LANES])
    yg = jnp.concatenate(pieces, axis=1) * _silu(z_ref[...])
    ms = jnp.mean(yg * yg, axis=-1, keepdims=True)
    y_ref[...] = (yg * lax.rsqrt(ms + EPS) * nrm_ref[...]).astype(BF16)


def _pair_decay(e_cum, row, m):
    top = lax.broadcasted_iota(jnp.int32, (2 * SSM_HEADDIM, SSM_DSTATE), 0) < SSM_HEADDIM
    ea = jnp.broadcast_to(e_cum[row:row + 1, 2 * m:2 * m + 1], top.shape)
    eb = jnp.broadcast_to(e_cum[row:row + 1, 2 * m + 1:2 * m + 2], top.shape)
    return jnp.where(top, ea, eb)


def _ssd_prompt_kernel(z_ref, xs_ref, b_ref, c_ref, dt_ref, cwx_ref, cbx_ref, cwb_ref, cbb_ref, cwc_ref, cbc_ref,
                       dtb_ref, alog_ref, dsk_ref, nrm_ref, ltri_ref, y_ref, hs_ref, px_ref, pb_ref, pc_ref):
    @pl.when(pl.program_id(2) == 0)
    def _():
        hs_ref[...] = jnp.zeros_like(hs_ref)
        px_ref[...] = jnp.zeros_like(px_ref)
        pb_ref[...] = jnp.zeros_like(pb_ref)
        pc_ref[...] = jnp.zeros_like(pc_ref)

    def conv(x_ref, p_ref, w_ref, bias_ref):
        x = x_ref[...]
        x3 = x.reshape(ROWS // SUBLANES, SUBLANES, x.shape[1])
        pred = jnp.concatenate([p_ref[...], x3[:-1]], axis=0)
        out = _conv_piece(x, pred, w_ref, bias_ref)
        p_ref[0] = x3[-1]
        return out

    xs = conv(xs_ref, px_ref, cwx_ref, cbx_ref)
    bm = conv(b_ref, pb_ref, cwb_ref, cbb_ref)
    cm = conv(c_ref, pc_ref, cwc_ref, cbc_ref)

    def state_io(m, c_b, xw, b_b, e_cum):
        rows = pl.ds(m * 2 * SSM_HEADDIM, 2 * SSM_HEADDIM)
        h = hs_ref[0, rows, :]
        hs_ref[0, rows, :] = h * _pair_decay(e_cum, ROWS - 1, m) + _dot_tn(xw, b_b)
        return _dot_nt(c_b, h.astype(BF16))

    _ssd_core(CHUNK, z_ref, dt_ref, dtb_ref, alog_ref, dsk_ref, nrm_ref, ltri_ref, xs, bm, cm, y_ref, state_io)


def _ssd_sample_kernel(seq, z_ref, xs_ref, b_ref, c_ref, dt_ref, cwx_ref, cbx_ref, cwb_ref, cbb_ref, cwc_ref,
                       cbc_ref, dtb_ref, alog_ref, dsk_ref, nrm_ref, ltri_ref, bmask_ref, px_ref, pb_ref, pc_ref,
                       h0_ref, y_ref, hs_ref, csx_ref, csb_ref, csc_ref):
    nb = SAMPLE_PER_TILE

    def conv(x_ref, p_ref, w_ref, bias_ref, cs_ref):
        x = x_ref[...]
        cs_ref[...] = x.reshape(nb, seq, x.shape[1])
        return _conv_piece(x, p_ref[...], w_ref, bias_ref)

    xs = conv(xs_ref, px_ref, cwx_ref, cbx_ref, csx_ref)
    bm = conv(b_ref, pb_ref, cwb_ref, cbb_ref, csb_ref)
    cm = conv(c_ref, pc_ref, cwc_ref, cbc_ref, csc_ref)
    bmask = bmask_ref[...]

    def state_io(m, c_b, xw, b_b, e_cum):
        rows = pl.ds(m * 2 * SSM_HEADDIM, 2 * SSM_HEADDIM)
        c_bd = jnp.concatenate([c_b] * nb, axis=1) * bmask
        b_bd = jnp.concatenate([b_b] * nb, axis=1) * bmask
        hs = [h0_ref[i, rows, :] for i in range(nb)]
        h_cat = jnp.concatenate(hs, axis=1).astype(BF16)
        dh = _dot_tn(xw, b_bd)
        for i in range(nb):
            decay = _pair_decay(e_cum, i * seq + seq - 1, m)
            hs_ref[i, rows, :] = hs[i] * decay + dh[:, i * SSM_DSTATE:(i + 1) * SSM_DSTATE]
        return _dot_nt(c_bd, h_cat)

    _ssd_core(seq, z_ref, dt_ref, dtb_ref, alog_ref, dsk_ref, nrm_ref, ltri_ref, xs, bm, cm, y_ref, state_io)


def _ssd_tables(c):
    r = jnp.arange(ROWS)
    ltri = ((r[:, None] >= r[None, :]) & ((r[:, None] // c) == (r[None, :] // c))).astype(BF16)
    return ltri


def _ssd_specs(row_block):
    def at(width, off):
        return lambda *ids: (row_block(*ids)[0], off // width + row_block(*ids)[1])

    def grp(rows, width):
        return pl.BlockSpec((rows, width), lambda *ids: (0, row_block(*ids)[1]))

    n = SSM_DSTATE
    return [
        pl.BlockSpec((ROWS, GROUP_INNER), at(GROUP_INNER, _Cols.Z)),
        pl.BlockSpec((ROWS, GROUP_INNER), at(GROUP_INNER, _Cols.XS)),
        pl.BlockSpec((ROWS, n), at(n, _Cols.B)),
        pl.BlockSpec((ROWS, n), at(n, _Cols.C)),
        pl.BlockSpec((ROWS, LANES), at(LANES, _Cols.DT)),
        grp(CONV_WIDTH, GROUP_INNER), grp(1, GROUP_INNER),
        grp(CONV_WIDTH, n), grp(1, n), grp(CONV_WIDTH, n), grp(1, n),
        grp(1, LANES), grp(1, LANES), grp(1, GROUP_INNER), grp(1, GROUP_INNER),
    ]


def _ssd_params(conv_w, conv_b, dt_bias, a_log, d_skip, ssm_norm):
    gn = SSM_GROUPS * SSM_DSTATE
    cwx, cwb, cwc = conv_w[:, :SSM_INNER], conv_w[:, SSM_INNER:SSM_INNER + gn], conv_w[:, SSM_INNER + gn:]
    cb = conv_b.reshape(1, CONV_DIM)
    cbx, cbb, cbc = cb[:, :SSM_INNER], cb[:, SSM_INNER:SSM_INNER + gn], cb[:, SSM_INNER + gn:]

    def per_group_lanes(v):
        v = v.reshape(SSM_GROUPS, HEADS_PER_GROUP)
        return jnp.pad(v, ((0, 0), (0, LANES - HEADS_PER_GROUP))).reshape(1, SSM_GROUPS * LANES)

    dsk = jnp.repeat(d_skip, SSM_HEADDIM).reshape(1, SSM_INNER)
    return (cwx, cbx, cwb, cbb, cwc, cbc, per_group_lanes(dt_bias), per_group_lanes(a_log), dsk,
            ssm_norm.reshape(1, SSM_INNER))


def _conv_tail(csx, csb, csc):
    keep = SUBLANES - (CONV_WIDTH - 1)
    return jnp.concatenate([csx[:, keep:], csb[:, keep:], csc[:, keep:]], axis=-1)


def _ssd_prompt(proj, bsz, seq, params):
    nch = seq // ROWS
    specs = _ssd_specs(lambda b, g, c: (b * nch + c, g))
    specs.append(pl.BlockSpec((ROWS, ROWS), lambda b, g, c: (0, 0)))
    gh = HEADS_PER_GROUP * SSM_HEADDIM
    gn = SSM_GROUPS * SSM_DSTATE
    tail = lambda width: pl.BlockSpec((1, SUBLANES, width), lambda b, g, c: (b, 0, g))
    y, hs, csx, csb, csc = pl.pallas_call(
        _ssd_prompt_kernel,
        out_shape=(jax.ShapeDtypeStruct((bsz * seq, SSM_INNER), BF16),
                   jax.ShapeDtypeStruct((bsz, SSM_HEADS * SSM_HEADDIM, SSM_DSTATE), F32),
                   jax.ShapeDtypeStruct((bsz, SUBLANES, SSM_INNER), F32),
                   jax.ShapeDtypeStruct((bsz, SUBLANES, gn), F32),
                   jax.ShapeDtypeStruct((bsz, SUBLANES, gn), F32)),
        grid=(bsz, SSM_GROUPS, nch),
        in_specs=specs,
        out_specs=(pl.BlockSpec((ROWS, GROUP_INNER), lambda b, g, c: (b * nch + c, g)),
                   pl.BlockSpec((1, gh, SSM_DSTATE), lambda b, g, c: (b, g, 0)),
                   tail(GROUP_INNER), tail(SSM_DSTATE), tail(SSM_DSTATE)),
        compiler_params=_params(("parallel", "parallel", "arbitrary")),
        name="ssd_prompt",
    )(proj, proj, proj, proj, proj, *params, _ssd_tables(CHUNK))
    return y, hs, _conv_tail(csx, csb, csc)


def _ssd_sample(proj, row0, bsz, seq, params, state, conv_state):
    ntile = bsz // SAMPLE_PER_TILE
    rb0 = row0 // ROWS
    gn = SSM_GROUPS * SSM_DSTATE
    specs = _ssd_specs(lambda t, g: (rb0 + t, g))
    gh = HEADS_PER_GROUP * SSM_HEADDIM
    cs = jnp.pad(conv_state, ((0, 0), (SUBLANES - (CONV_WIDTH - 1), 0), (0, 0)))
    csx, csb, csc = cs[:, :, :SSM_INNER], cs[:, :, SSM_INNER:SSM_INNER + gn], cs[:, :, SSM_INNER + gn:]
    pred = lambda width: pl.BlockSpec((SAMPLE_PER_TILE, SUBLANES, width), lambda t, g: (t, 0, g))
    state_spec = pl.BlockSpec((SAMPLE_PER_TILE, gh, SSM_DSTATE), lambda t, g: (t, g, 0))
    specs += [
        pl.BlockSpec((ROWS, ROWS), lambda t, g: (0, 0)),
        pl.BlockSpec((ROWS, SAMPLE_PER_TILE * LANES), lambda t, g: (0, 0)),
        pred(GROUP_INNER), pred(SSM_DSTATE), pred(SSM_DSTATE),
        state_spec,
    ]
    state2 = state.reshape(bsz, SSM_HEADS * SSM_HEADDIM, SSM_DSTATE)
    assert seq == SUBLANES
    y, hs, csx, csb, csc = pl.pallas_call(
        functools.partial(_ssd_sample_kernel, seq),
        out_shape=(jax.ShapeDtypeStruct((bsz * seq, SSM_INNER), BF16), jax.ShapeDtypeStruct(state2.shape, F32),
                   jax.ShapeDtypeStruct((bsz, seq, SSM_INNER), F32),
                   jax.ShapeDtypeStruct((bsz, seq, gn), F32),
                   jax.ShapeDtypeStruct((bsz, seq, gn), F32)),
        grid=(ntile, SSM_GROUPS),
        in_specs=specs,
        out_specs=(pl.BlockSpec((ROWS, GROUP_INNER), lambda t, g: (t, g)), state_spec,
                   pred(GROUP_INNER), pred(SSM_DSTATE), pred(SSM_DSTATE)),
        compiler_params=_params(("parallel", "parallel")),
        name="ssd_sample",
    )(proj, proj, proj, proj, proj, *params, _ssd_tables(seq), _block_mask(BF16), csx, csb, csc, state2)
    return y, hs, _conv_tail(csx, csb, csc)


def _post_kernel(n_prompt_tiles, ap_ref, as_ref, yp_ref, ys_ref, ga_ref, gb_ref, x_ref, wr_ref, ws_ref, wo_ref,
                 nw_ref, wrt_ref, brt_ref, x1_ref, h2_ref, lg_ref):
    is_prompt = pl.program_id(0) < n_prompt_tiles
    branch_a = _dot(jnp.where(is_prompt, ap_ref[...], as_ref[...]), wr_ref[...])
    branch_b = _dot(jnp.where(is_prompt, yp_ref[...], ys_ref[...]), ws_ref[...])
    merged = jax.nn.sigmoid(ga_ref[...]) * branch_a + jax.nn.sigmoid(gb_ref[...]) * branch_b
    x1 = x_ref[...] + _dot(merged.astype(BF16), wo_ref[...])
    x1_ref[...] = x1
    ms = jnp.mean(x1 * x1, axis=-1, keepdims=True)
    h2 = x1 * lax.rsqrt(ms + EPS) * nw_ref[...]
    _store_row_tiles(h2_ref, h2)
    lg_ref[...] = _dot(h2.astype(BF16), wrt_ref[...]) + brt_ref[...]


def _post(a_p, a_s, y_p, y_s, proj, x, w_ret, w_ssm, w_out, norm_ffn, w_router, b_router):
    t = x.shape[0]
    tm = _pick(math.gcd(a_p.shape[0], a_s.shape[0]), 512)
    npt = a_p.shape[0] // tm
    row = lambda width, blk: pl.BlockSpec((tm, width), lambda i: (i, blk))
    prow = lambda width: pl.BlockSpec((tm, width), lambda i: (jnp.minimum(i, npt - 1), 0))
    srow = lambda width: pl.BlockSpec((tm, width), lambda i: (jnp.maximum(i - npt, 0), 0))
    full = lambda *shape: pl.BlockSpec(shape, lambda i: (0,) * len(shape))
    return pl.pallas_call(
        functools.partial(_post_kernel, npt),
        out_shape=(jax.ShapeDtypeStruct((t, D_MODEL), F32), jax.ShapeDtypeStruct((t * ROW_TILES, LANES), F32),
                   jax.ShapeDtypeStruct((t, LANES), F32)),
        grid=(t // tm,),
        in_specs=[prow(RET_V), srow(RET_V), prow(SSM_INNER), srow(SSM_INNER), row(D_MODEL, _Cols.GA // D_MODEL),
                  row(D_MODEL, _Cols.GB // D_MODEL), row(D_MODEL, 0),
                  full(RET_V, D_MODEL), full(SSM_INNER, D_MODEL), full(D_MODEL, D_MODEL), full(1, D_MODEL),
                  full(D_MODEL, LANES), full(1, LANES)],
        out_specs=(row(D_MODEL, 0), pl.BlockSpec((tm * ROW_TILES, LANES), lambda i: (i, 0)), row(LANES, 0)),
        compiler_params=_params(("parallel",)),
        name="post_mixer",
    )(a_p, a_s, y_p, y_s, proj, proj, x, w_ret, w_ssm, w_out, norm_ffn, w_router, b_router)


def _route_kernel(lg_ref, lstrict_ref, meta_ref, gate_ref, cnt_ref):
    @pl.when(pl.program_id(0) == 0)
    def _():
        cnt_ref[...] = jnp.zeros_like(cnt_ref)

    tm = lg_ref.shape[0]
    lane = lax.broadcasted_iota(jnp.int32, (tm, LANES), 1)
    lane_f = lane.astype(F32)
    cur = jnp.where(lane < N_EXPERTS, lg_ref[...], -jnp.inf)
    vals, hots = [], []
    for _ in range(TOP_K):
        m = jnp.max(cur, axis=1, keepdims=True)
        idx = jnp.min(jnp.where(cur == m, lane_f, float(LANES)), axis=1, keepdims=True)
        hot = lane_f == idx
        vals.append(m)
        hots.append(hot)
        cur = jnp.where(hot, -jnp.inf, cur)
    exps = [jnp.exp(v - vals[0]) for v in vals]
    denom = exps[0] + exps[1] + exps[2] + exps[3]
    sel = hots[0] | hots[1] | hots[2] | hots[3]
    self32 = sel.astype(F32)
    before = _dot(lstrict_ref[...], self32.astype(BF16)) + cnt_ref[...]
    cnt_ref[...] = cnt_ref[...] + jnp.sum(self32, axis=0, keepdims=True)
    meta = jnp.zeros((tm, LANES), F32)
    gates = jnp.zeros((tm, LANES), F32)
    for k in range(TOP_K):
        e_k = jnp.sum(jnp.where(hots[k], lane_f, 0.0), axis=1, keepdims=True)
        p_k = jnp.sum(jnp.where(hots[k], before, 0.0), axis=1, keepdims=True)
        meta = jnp.where(lane == k, e_k, meta)
        meta = jnp.where(lane == SUBLANES + k, p_k, meta)
        gates = jnp.where(lane == k, exps[k] / denom, gates)
    gate_ref[...] = gates
    meta_ref[...] = meta.T[:2 * SUBLANES, :].astype(jnp.int32)


def _route(logits):
    t = logits.shape[0]
    tm = _pick(t, 512)
    r = jnp.arange(tm)
    lstrict = (r[:, None] > r[None, :]).astype(BF16)
    return pl.pallas_call(
        _route_kernel,
        out_shape=(jax.ShapeDtypeStruct((2 * SUBLANES, t), jnp.int32),
                   jax.ShapeDtypeStruct((t, LANES), F32),
                   jax.ShapeDtypeStruct((1, LANES), F32)),
        grid=(t // tm,),
        in_specs=[pl.BlockSpec((tm, LANES), lambda i: (i, 0)), pl.BlockSpec((tm, tm), lambda i: (0, 0))],
        out_specs=(pl.BlockSpec((2 * SUBLANES, tm), lambda i: (0, i)),
                   pl.BlockSpec((tm, LANES), lambda i: (i, 0)),
                   pl.BlockSpec((1, LANES), lambda i: (0, 0))),
        compiler_params=_params(("arbitrary",)),
        name="route",
    )(logits, lstrict)


def _row_copy(src_ref, src_row, dst_ref, dst_row, sem):
    src = pl.ds(pl.multiple_of(src_row * ROW_TILES, ROW_TILES), ROW_TILES)
    dst = pl.ds(pl.multiple_of(dst_row * ROW_TILES, ROW_TILES), ROW_TILES)
    return pltpu.make_async_copy(src_ref.at[src, :], dst_ref.at[dst, :], sem)


def _dispatch_kernel(pstart_ref, pend_ref, dest_ref, h2_ref, xs_ref, zero_ref, sem):
    tm = h2_ref.shape[0] // ROW_TILES

    @pl.when(pl.program_id(0) == 0)
    def _():
        zero_ref[...] = jnp.zeros_like(zero_ref)

        def fill(e):
            n = EXPERT_BLOCK * ROW_TILES
            start = pl.multiple_of((pend_ref[e] - EXPERT_BLOCK) * ROW_TILES, n)
            return pltpu.make_async_copy(zero_ref, xs_ref.at[pl.ds(start, n), :], sem)

        def fill_start(e, carry):
            @pl.when(pend_ref[e] > pstart_ref[e])
            def _():
                fill(e).start()
            return carry

        def fill_wait(e, carry):
            @pl.when(pend_ref[e] > pstart_ref[e])
            def _():
                fill(e).wait()
            return carry

        lax.fori_loop(0, N_EXPERTS, fill_start, 0)
        lax.fori_loop(0, N_EXPERTS, fill_wait, 0)

        def tail(j):
            n = EXPERT_BLOCK * ROW_TILES
            return pltpu.make_async_copy(zero_ref, xs_ref.at[pl.ds(pl.multiple_of(j * n, n), n), :], sem)

        used = pend_ref[N_EXPERTS - 1] // EXPERT_BLOCK
        total = xs_ref.shape[0] // (EXPERT_BLOCK * ROW_TILES)
        lax.fori_loop(used, total, lambda j, c: (tail(j).start(), c)[1], 0)
        lax.fori_loop(used, total, lambda j, c: (tail(j).wait(), c)[1], 0)

    def copy(t, k):
        return _row_copy(h2_ref, t, xs_ref, dest_ref[k, t], sem)

    def issue(t, carry):
        for k in range(TOP_K):
            copy(t, k).start()
        return carry

    def drain(t, carry):
        for k in range(TOP_K):
            copy(t, k).wait()
        return carry

    lax.fori_loop(0, tm, issue, 0, unroll=ISSUE_UNROLL)
    lax.fori_loop(0, tm, drain, 0, unroll=ISSUE_UNROLL)


def _dispatch(pstart, pends, dest, h2, n_rows):
    t = h2.shape[0] // ROW_TILES
    tm = _pick(t, 256)
    return pl.pallas_call(
        _dispatch_kernel,
        out_shape=jax.ShapeDtypeStruct((n_rows * ROW_TILES, LANES), F32),
        grid_spec=pltpu.PrefetchScalarGridSpec(
            num_scalar_prefetch=2,
            grid=(t // tm,),
            in_specs=[pl.BlockSpec((SUBLANES, tm), lambda i, ps, pe: (0, i), memory_space=pltpu.SMEM),
                      pl.BlockSpec((tm * ROW_TILES, LANES), lambda i, ps, pe: (i, 0))],
            out_specs=pl.BlockSpec(memory_space=pl.ANY),
            scratch_shapes=[pltpu.VMEM((EXPERT_BLOCK * ROW_TILES, LANES), F32), pltpu.SemaphoreType.DMA],
        ),
        compiler_params=_params(("arbitrary",)),
        name="dispatch",
    )(pstart, pends, dest, h2)


def _combine_kernel(dest_ref, gate_ref, x1_ref, nw_ref, yb_ref, o_ref, buf_ref, sem):
    tm = x1_ref.shape[0]

    def copy(t, k):
        return _row_copy(yb_ref, dest_ref[k, t], buf_ref.at[k], t, sem)

    def issue(t, carry):
        for k in range(TOP_K):
            copy(t, k).start()
        return carry

    def drain(t, carry):
        for k in range(TOP_K):
            copy(t, k).wait()
        return carry

    lax.fori_loop(0, tm, issue, 0, unroll=ISSUE_UNROLL)
    lax.fori_loop(0, tm, drain, 0, unroll=ISSUE_UNROLL)
    gates = gate_ref[...]
    moe = _load_row_tiles(buf_ref, tm, 0) * gates[:, 0:1]
    for k in range(1, TOP_K):
        moe = moe + _load_row_tiles(buf_ref, tm, k) * gates[:, k:k + 1]
    x2 = x1_ref[...] + moe
    ms = jnp.mean(x2 * x2, axis=-1, keepdims=True)
    o_ref[...] = x2 * lax.rsqrt(ms + EPS) * nw_ref[...]


def _combine(dest, gates, x1, norm_final, yb):
    t = x1.shape[0]
    tm = _pick(t, 256)
    return pl.pallas_call(
        _combine_kernel,
        out_shape=jax.ShapeDtypeStruct((t, D_MODEL), F32),
        grid=(t // tm,),
        in_specs=[pl.BlockSpec((SUBLANES, tm), lambda i: (0, i), memory_space=pltpu.SMEM),
                  pl.BlockSpec((tm, LANES), lambda i: (i, 0)),
                  pl.BlockSpec((tm, D_MODEL), lambda i: (i, 0)),
                  pl.BlockSpec((1, D_MODEL), lambda i: (0, 0)),
                  pl.BlockSpec(memory_space=pl.ANY)],
        out_specs=pl.BlockSpec((tm, D_MODEL), lambda i: (i, 0)),
        scratch_shapes=[pltpu.VMEM((TOP_K, tm * ROW_TILES, LANES), F32), pltpu.SemaphoreType.DMA],
        compiler_params=_params(("arbitrary",)),
        name="combine",
    )(dest, gates, x1, norm_final, yb)


def _expert_kernel(be_ref, nv_ref, xs_ref, wgu_ref, bgu_ref, wd_ref, bd_ref, yb_ref, wgu_b, wd_b):
    i = pl.program_id(0)

    @pl.when(jnp.logical_or(i == 0, be_ref[i] != be_ref[jnp.maximum(i - 1, 0)]))
    def _():
        wgu_b[...] = wgu_ref[0].astype(BF16)
        wd_b[...] = wd_ref[0].astype(BF16)

    @pl.when(i < nv_ref[0])
    def _():
        gu = _dot(_load_row_tiles(xs_ref, EXPERT_BLOCK).astype(BF16), wgu_b[...]) + bgu_ref[0]
        glu = jnp.minimum(gu[:, :D_FF], SWIGLU_LIMIT)
        lin = jnp.clip(gu[:, D_FF:], -SWIGLU_LIMIT, SWIGLU_LIMIT)
        act = glu * jax.nn.sigmoid(SWIGLU_ALPHA * glu) * (lin + 1.0)
        _store_row_tiles(yb_ref, _dot(act.astype(BF16), wd_b[...]) + bd_ref[0])

    @pl.when(i >= nv_ref[0])
    def _():
        yb_ref[...] = jnp.zeros_like(yb_ref)


def _experts(block_expert, n_valid, xs, w_gu, b_gu, w_d, b_d):
    rows = EXPERT_BLOCK * ROW_TILES
    nb = xs.shape[0] // rows
    return pl.pallas_call(
        _expert_kernel,
        out_shape=jax.ShapeDtypeStruct(xs.shape, F32),
        grid_spec=pltpu.PrefetchScalarGridSpec(
            num_scalar_prefetch=2,
            grid=(nb,),
            in_specs=[pl.BlockSpec((rows, LANES), lambda i, be, nv: (jnp.minimum(i, nv[0] - 1), 0)),
                      pl.BlockSpec((1, D_MODEL, 2 * D_FF), lambda i, be, nv: (be[i], 0, 0)),
                      pl.BlockSpec((1, 1, 2 * D_FF), lambda i, be, nv: (be[i], 0, 0)),
                      pl.BlockSpec((1, D_FF, D_MODEL), lambda i, be, nv: (be[i], 0, 0)),
                      pl.BlockSpec((1, 1, D_MODEL), lambda i, be, nv: (be[i], 0, 0))],
            out_specs=pl.BlockSpec((rows, LANES), lambda i, be, nv: (i, 0)),
            scratch_shapes=[pltpu.VMEM((D_MODEL, 2 * D_FF), BF16), pltpu.VMEM((D_FF, D_MODEL), BF16)],
        ),
        compiler_params=_params(("arbitrary",)),
        name="experts",
    )(block_expert, n_valid, xs, w_gu, b_gu, w_d, b_d)


def _expert_layout(counts, n_blocks):
    counts = counts.astype(jnp.int32)
    padded = (counts + EXPERT_BLOCK - 1) // EXPERT_BLOCK * EXPERT_BLOCK
    pends = jnp.cumsum(padded)
    pstart = pends - padded
    block_start = jnp.arange(n_blocks, dtype=jnp.int32) * EXPERT_BLOCK
    block_expert = jnp.minimum(jnp.sum(pends[None, :] <= block_start[:, None], axis=1), N_EXPERTS - 1)
    n_valid = (pends[-1:] // EXPERT_BLOCK).astype(jnp.int32)
    return pstart.astype(jnp.int32), pends.astype(jnp.int32), block_expert.astype(jnp.int32), n_valid


def _permute_w_in(w_in):
    sizes = [RET_QK, RET_QK, RET_V, RET_V, SSM_INNER, CONV_DIM, SSM_HEADS, D_MODEL, D_MODEL]
    offs = [0]
    for s in sizes:
        offs.append(offs[-1] + s)
    q, k, v, g_ret, z, xbc, dt, g_a, g_b = [w_in[:, offs[i]:offs[i + 1]] for i in range(len(sizes))]
    gn = SSM_GROUPS * SSM_DSTATE
    xs, bm, cm = xbc[:, :SSM_INNER], xbc[:, SSM_INNER:SSM_INNER + gn], xbc[:, SSM_INNER + gn:]
    dt4 = jnp.pad(dt.reshape(D_MODEL, SSM_GROUPS, HEADS_PER_GROUP),
                  ((0, 0), (0, 0), (0, LANES - HEADS_PER_GROUP))).reshape(D_MODEL, SSM_GROUPS * LANES)
    return jnp.concatenate([z, xs, v, g_ret, g_a, g_b, q, k, bm, cm, dt4], axis=1).astype(BF16)


def _forward(x_prompt, x_sample, state_ret, state_ssm, state_conv, norm_mix, w_in, ret_norm, w_out_ret,
             conv_w, conv_b, dt_bias, a_log, d_skip, ssm_norm, w_out_ssm, w_out, norm_ffn,
             w_router, b_router, w_gate_up, b_gate_up, w_down, b_down, norm_final):
    bp, lp, _ = x_prompt.shape
    bs, ls, _ = x_sample.shape
    assert lp % ROWS == 0 and bs % SAMPLE_PER_TILE == 0 and ls * SAMPLE_PER_TILE == ROWS
    tp, ts = bp * lp, bs * ls
    t_all = tp + ts
    x = jnp.concatenate([x_prompt.reshape(tp, D_MODEL), x_sample.reshape(ts, D_MODEL)], axis=0)

    proj = _inproj(x, norm_mix[0].reshape(1, D_MODEL), _permute_w_in(w_in[0]))

    a_p, ret_p = _retention_prompt(proj, bp, lp, ret_norm[0])
    a_s, ret_s = _retention_sample(proj, tp, bs, ls, state_ret[0], ret_norm[0])

    sp = _ssd_params(conv_w[0], conv_b[0], dt_bias[0], a_log[0], d_skip[0], ssm_norm[0])
    y_p, ssm_p, conv_p = _ssd_prompt(proj, bp, lp, sp)
    y_s, ssm_s, conv_s = _ssd_sample(proj, tp, bs, ls, sp, state_ssm[0], state_conv[0])

    w_router_pad = jnp.pad(w_router[0], ((0, 0), (0, LANES - N_EXPERTS))).astype(BF16)
    b_router_pad = jnp.pad(b_router[0], (0, LANES - N_EXPERTS)).reshape(1, LANES)
    x1, h2, logits = _post(a_p, a_s, y_p, y_s, proj, x, w_out_ret[0].astype(BF16), w_out_ssm[0].astype(BF16),
                           w_out[0].astype(BF16), norm_ffn[0].reshape(1, D_MODEL), w_router_pad, b_router_pad)

    meta, gates, counts = _route(logits)
    n_blocks = -(-(t_all * TOP_K + N_EXPERTS * (EXPERT_BLOCK - 1)) // EXPERT_BLOCK)
    pstart, pends, block_expert, n_valid = _expert_layout(counts[0, :N_EXPERTS], n_blocks)
    dest = jnp.pad(pstart[meta[:TOP_K]] + meta[SUBLANES:SUBLANES + TOP_K], ((0, SUBLANES - TOP_K), (0, 0)))
    xs = _dispatch(pstart, pends, dest, h2, n_blocks * EXPERT_BLOCK)
    yb = _experts(block_expert, n_valid, xs, w_gate_up[0], b_gate_up[0].reshape(N_EXPERTS, 1, 2 * D_FF),
                  w_down[0], b_down[0].reshape(N_EXPERTS, 1, D_MODEL))
    y = _combine(dest, gates, x1, norm_final.reshape(1, D_MODEL), yb)

    shape_s = (1, bs, SSM_HEADS, SSM_HEADDIM, SSM_DSTATE)
    shape_p = (1, bp, SSM_HEADS, SSM_HEADDIM, SSM_DSTATE)
    return (y[:tp].reshape(bp, lp, D_MODEL), y[tp:].reshape(bs, ls, D_MODEL),
            ret_p[None], ret_s[None], ssm_p.reshape(shape_p), ssm_s.reshape(shape_s),
            conv_p[None], conv_s[None])


def kernel(x_prompt, x_sample, state_ret, state_ssm, state_conv, norm_mix, w_in, ret_norm, w_out_ret, conv_w, conv_b, dt_bias, a_log, d_skip, ssm_norm, w_out_ssm, w_out, norm_ffn, w_router, b_router, w_gate_up, b_gate_up, w_down, b_down, norm_final):
    return _forward(x_prompt, x_sample, state_ret, state_ssm, state_conv, norm_mix, w_in, ret_norm, w_out_ret,
                    conv_w, conv_b, dt_bias, a_log, d_skip, ssm_norm, w_out_ssm, w_out, norm_ffn,
                    w_router, b_router, w_gate_up, b_gate_up, w_down, b_down, norm_final)
```

```python
import functools
import math

import jax
import jax.numpy as jnp
from jax import lax
from jax.experimental import pallas as pl
from jax.experimental.pallas import tpu as pltpu

F32 = jnp.float32
BF16 = jnp.bfloat16

D_MODEL = 1024
PAST_LEN = 16384
RET_HEADS = 4
RET_DK = 128
RET_DV = 256
RET_QK = RET_HEADS * RET_DK
RET_V = RET_HEADS * RET_DV
ROPE_BASE = 10000.0
SSM_INNER = 2 * D_MODEL
SSM_HEADDIM = 64
SSM_HEADS = SSM_INNER // SSM_HEADDIM
SSM_GROUPS = 4
SSM_DSTATE = 128
HEADS_PER_GROUP = SSM_HEADS // SSM_GROUPS
GROUP_INNER = SSM_INNER // SSM_GROUPS
CONV_WIDTH = 4
CONV_DIM = SSM_INNER + 2 * SSM_GROUPS * SSM_DSTATE
CHUNK = 128
N_EXPERTS = 32
TOP_K = 4
D_FF = D_MODEL
SWIGLU_LIMIT = 7.0
SWIGLU_ALPHA = 1.702
EPS = 1e-6

LANES = 128
SUBLANES = 8
ROWS = 128
SAMPLE_PER_TILE = 16
EXPERT_BLOCK = 512
ROW_TILES = D_MODEL // LANES
ISSUE_UNROLL = 4
VMEM_LIMIT = 56 * 1024 * 1024


class _Cols:
    Z = 0
    XS = Z + SSM_INNER
    V = XS + SSM_INNER
    GRET = V + RET_V
    GA = GRET + RET_V
    GB = GA + D_MODEL
    Q = GB + D_MODEL
    K = Q + RET_QK
    B = K + RET_QK
    C = B + SSM_GROUPS * SSM_DSTATE
    DT = C + SSM_GROUPS * SSM_DSTATE
    TOTAL = DT + SSM_GROUPS * LANES


def _pick(n, target):
    best = None
    for c in range(LANES, target + 1, LANES):
        if n % c == 0:
            best = c
    assert best is not None, (n, target)
    return best


def _params(sem, **kw):
    return pltpu.CompilerParams(dimension_semantics=sem, vmem_limit_bytes=VMEM_LIMIT, **kw)


def _dot(a, b):
    return jnp.dot(a, b, preferred_element_type=F32)


def _dot_nt(a, b):
    return lax.dot_general(a, b, (((1,), (1,)), ((), ())), preferred_element_type=F32)


def _dot_tn(a, b):
    return lax.dot_general(a, b, (((0,), (0,)), ((), ())), preferred_element_type=F32)


def _silu(x):
    return x * jax.nn.sigmoid(x)


def _store_row_tiles(ref, value, *lead):
    n = value.shape[0]
    for s in range(ROW_TILES):
        ref[(*lead, pl.ds(s, n, stride=ROW_TILES), slice(None))] = value[:, s * LANES:(s + 1) * LANES]


def _load_row_tiles(ref, n, *lead):
    return jnp.concatenate([ref[(*lead, pl.ds(s, n, stride=ROW_TILES), slice(None))] for s in range(ROW_TILES)],
                           axis=1)


def _inproj_kernel(n_prompt_tiles, xp_ref, xs_ref, nw_ref, w_ref, o_ref, h_ref):
    @pl.when(pl.program_id(1) == 0)
    def _():
        x = jnp.where(pl.program_id(0) < n_prompt_tiles, xp_ref[...], xs_ref[...])
        ms = jnp.mean(x * x, axis=-1, keepdims=True)
        h_ref[...] = (x * lax.rsqrt(ms + EPS) * nw_ref[...]).astype(BF16)

    o_ref[...] = _dot(h_ref[...], w_ref[...])


def _split_rows(tm, npt, width):
    prompt = pl.BlockSpec((tm, width), lambda i, *_: (jnp.minimum(i, npt - 1), 0))
    sample = pl.BlockSpec((tm, width), lambda i, *_: (jnp.maximum(i - npt, 0), 0))
    return prompt, sample


def _inproj(xp, xs, norm_w, w_perm):
    t = xp.shape[0] + xs.shape[0]
    tm = _pick(math.gcd(xp.shape[0], xs.shape[0]), 1024)
    npt = xp.shape[0] // tm
    tn = _pick(_Cols.TOTAL, 1536)
    return pl.pallas_call(
        functools.partial(_inproj_kernel, npt),
        out_shape=jax.ShapeDtypeStruct((t, _Cols.TOTAL), F32),
        grid=(t // tm, _Cols.TOTAL // tn),
        in_specs=[
            *_split_rows(tm, npt, D_MODEL),
            pl.BlockSpec((1, D_MODEL), lambda i, j: (0, 0)),
            pl.BlockSpec((D_MODEL, tn), lambda i, j: (0, j)),
        ],
        out_specs=pl.BlockSpec((tm, tn), lambda i, j: (i, j)),
        scratch_shapes=[pltpu.VMEM((tm, D_MODEL), BF16)],
        compiler_params=_params(("parallel", "arbitrary")),
        name="in_proj",
    )(xp, xs, norm_w, w_perm)


def _ret_log_decay():
    return jnp.log(1.0 - 2.0 ** (-5.0 - jnp.arange(RET_HEADS, dtype=F32)))


def _ret_tables(c):
    lg = _ret_log_decay()
    r = jnp.arange(ROWS)
    t = (r % c).astype(F32)
    seg = r // c
    diff = t[:, None] - t[None, :]
    ok = (seg[:, None] == seg[None, :]) & (diff >= 0)
    dm = jnp.where(ok[None], jnp.exp(lg[:, None, None] * jnp.maximum(diff, 0.0)[None]), 0.0)
    qd = jnp.exp(lg[:, None] * (t[None, :] + 1.0))
    kd = jnp.exp(lg[:, None] * (c - 1.0 - t[None, :]))
    cd = jnp.exp(lg * c)
    qd = jnp.broadcast_to(qd[:, :, None], (RET_HEADS, ROWS, LANES))
    kd = jnp.broadcast_to(kd[:, :, None], (RET_HEADS, ROWS, LANES))
    cd = jnp.broadcast_to(cd[:, None, None], (RET_HEADS, 1, RET_DV))
    return dm, qd, kd, cd


def _rope_tables(pos):
    half = RET_DK // 2
    inv_freq = 1.0 / (ROPE_BASE ** jnp.linspace(0.0, 1.0, half, dtype=F32))
    ang = pos.astype(F32)[:, None] * inv_freq[None, :]
    cos, sin = jnp.cos(ang), jnp.sin(ang)
    return jnp.concatenate([cos, cos], -1), jnp.concatenate([-sin, sin], -1)


def _block_mask(dtype):
    r = jnp.arange(ROWS) // (ROWS // SAMPLE_PER_TILE)
    b = jnp.arange(SAMPLE_PER_TILE * LANES) // LANES
    return (r[:, None] == b[None, :]).astype(dtype)


def _ret_common(q, k, v, g, cosf, sinf, dm, qd, kd, nw):
    qr = q * cosf + pltpu.roll(q, RET_DK // 2, 1) * sinf
    kr = (k * cosf + pltpu.roll(k, RET_DK // 2, 1) * sinf) * (RET_DK ** -0.5)
    vb = v.astype(BF16)
    scores = _dot_nt(qr.astype(BF16), kr.astype(BF16)) * dm
    o_intra = _dot(scores.astype(BF16), vb)
    q_state = (qr * qd).astype(BF16)
    k_state = kr * kd

    def finish(o):
        ms = jnp.mean(o * o, axis=-1, keepdims=True)
        on = o * lax.rsqrt(ms + EPS) * nw
        return (_silu(g) * on).astype(BF16)

    return o_intra, q_state, k_state, vb, finish


def _ret_prompt_kernel(q_ref, k_ref, v_ref, g_ref, cos_ref, sin_ref, dm_ref, qd_ref, kd_ref, cd_ref, nw_ref,
                       a_ref, s_ref):
    @pl.when(pl.program_id(1) == 0)
    def _():
        s_ref[...] = jnp.zeros_like(s_ref)

    cosf, sinf = cos_ref[...], sin_ref[...]
    for h in range(RET_HEADS):
        dk = slice(h * RET_DK, (h + 1) * RET_DK)
        dv = slice(h * RET_DV, (h + 1) * RET_DV)
        o_intra, q_state, k_state, vb, finish = _ret_common(
            q_ref[:, dk], k_ref[:, dk], v_ref[:, dv], g_ref[:, dv], cosf, sinf, dm_ref[h], qd_ref[h], kd_ref[h],
            nw_ref[h])
        s = s_ref[0, h]
        a_ref[:, dv] = finish(o_intra + _dot(q_state, s.astype(BF16)))
        s_ref[0, h] = s * cd_ref[h] + _dot_tn(k_state.astype(BF16), vb)


def _ret_sample_kernel(q_ref, k_ref, v_ref, g_ref, cos_ref, sin_ref, dm_ref, qd_ref, kd_ref, cd_ref, nw_ref,
                       bm_ref, bmt_ref, s0_ref, a_ref, s_ref):
    o_intra, q_state, k_state, vb, finish = _ret_common(
        q_ref[...], k_ref[...], v_ref[...], g_ref[...], cos_ref[...], sin_ref[...], dm_ref[0], qd_ref[0], kd_ref[0],
        nw_ref[0])
    nb = SAMPLE_PER_TILE
    s0 = s0_ref[:, 0]
    q_bd = jnp.concatenate([q_state] * nb, axis=1) * bm_ref[...]
    a_ref[...] = finish(o_intra + _dot(q_bd, s0.reshape(nb * RET_DK, RET_DV).astype(BF16)))
    k_t = k_state.T.astype(BF16)
    k_bd_t = jnp.concatenate([k_t] * nb, axis=0) * bmt_ref[...]
    ds = _dot(k_bd_t, vb).reshape(nb, RET_DK, RET_DV)
    s_ref[:, 0] = s0 * cd_ref[0] + ds


def _ret_specs(row_block):
    def at(width, off):
        return lambda *ids: (row_block(*ids)[0], off // width + row_block(*ids)[1])

    def head(*shape):
        return pl.BlockSpec((1,) + shape, lambda *ids: (row_block(*ids)[1],) + (0,) * len(shape))

    return [
        pl.BlockSpec((ROWS, RET_DK), at(RET_DK, _Cols.Q)),
        pl.BlockSpec((ROWS, RET_DK), at(RET_DK, _Cols.K)),
        pl.BlockSpec((ROWS, RET_DV), at(RET_DV, _Cols.V)),
        pl.BlockSpec((ROWS, RET_DV), at(RET_DV, _Cols.GRET)),
    ], head


def _retention_prompt(proj, bsz, seq, ret_norm):
    nch = seq // ROWS
    dm, qd, kd, cd = _ret_tables(CHUNK)
    cosf, sinf = _rope_tables(jnp.arange(seq, dtype=jnp.int32))
    row = lambda width, off: pl.BlockSpec((ROWS, width), lambda b, c: (b * nch + c, off // width))
    full = lambda *shape: pl.BlockSpec(shape, lambda b, c: (0,) * len(shape))
    specs = [
        row(RET_QK, _Cols.Q), row(RET_QK, _Cols.K), row(RET_V, _Cols.V), row(RET_V, _Cols.GRET),
        pl.BlockSpec((ROWS, RET_DK), lambda b, c: (c, 0)),
        pl.BlockSpec((ROWS, RET_DK), lambda b, c: (c, 0)),
        full(RET_HEADS, ROWS, ROWS), full(RET_HEADS, ROWS, LANES), full(RET_HEADS, ROWS, LANES),
        full(RET_HEADS, 1, RET_DV), full(RET_HEADS, 1, RET_DV),
    ]
    return pl.pallas_call(
        _ret_prompt_kernel,
        out_shape=(jax.ShapeDtypeStruct((bsz * seq, RET_V), BF16),
                   jax.ShapeDtypeStruct((bsz, RET_HEADS, RET_DK, RET_DV), F32)),
        grid=(bsz, nch),
        in_specs=specs,
        out_specs=(pl.BlockSpec((ROWS, RET_V), lambda b, c: (b * nch + c, 0)),
                   pl.BlockSpec((1, RET_HEADS, RET_DK, RET_DV), lambda b, c: (b, 0, 0, 0))),
        compiler_params=_params(("parallel", "arbitrary")),
        name="retention_prompt",
    )(proj, proj, proj, proj, cosf, sinf, dm, qd, kd, cd, ret_norm.reshape(RET_HEADS, 1, RET_DV))


def _retention_sample(proj, row0, bsz, seq, state, ret_norm):
    ntile = bsz // SAMPLE_PER_TILE
    rb0 = row0 // ROWS
    dm, qd, kd, cd = _ret_tables(seq)
    pos = PAST_LEN + jnp.arange(seq, dtype=jnp.int32)
    cosf, sinf = _rope_tables(jnp.tile(pos, SAMPLE_PER_TILE))
    rb = lambda t, h: (rb0 + t, h)
    specs, head = _ret_specs(rb)
    full = lambda *shape: pl.BlockSpec(shape, lambda t, h: (0,) * len(shape))
    specs += [
        full(ROWS, RET_DK), full(ROWS, RET_DK),
        head(ROWS, ROWS), head(ROWS, LANES), head(ROWS, LANES), head(1, RET_DV), head(1, RET_DV),
        full(ROWS, SAMPLE_PER_TILE * LANES), full(SAMPLE_PER_TILE * LANES, ROWS),
        pl.BlockSpec((SAMPLE_PER_TILE, 1, RET_DK, RET_DV), lambda t, h: (t, h, 0, 0)),
    ]
    bm = _block_mask(BF16)
    return pl.pallas_call(
        _ret_sample_kernel,
        out_shape=(jax.ShapeDtypeStruct((bsz * seq, RET_V), BF16),
                   jax.ShapeDtypeStruct(state.shape, F32)),
        grid=(ntile, RET_HEADS),
        in_specs=specs,
        out_specs=(pl.BlockSpec((ROWS, RET_DV), lambda t, h: (t, h)),
                   pl.BlockSpec((SAMPLE_PER_TILE, 1, RET_DK, RET_DV), lambda t, h: (t, h, 0, 0))),
        compiler_params=_params(("parallel", "parallel")),
        name="retention_sample",
    )(proj, proj, proj, proj, cosf, sinf, dm, qd, kd, cd, ret_norm.reshape(RET_HEADS, 1, RET_DV),
      bm, bm.T, state)


def _softplus(x):
    return jnp.maximum(x, 0.0) + jnp.log1p(jnp.exp(-jnp.abs(x)))


def _conv_piece(x, pred, w_ref, b_ref):
    width = x.shape[1]
    x3 = x.reshape(ROWS // SUBLANES, SUBLANES, width)
    t8 = lax.broadcasted_iota(jnp.int32, x3.shape, 1)
    acc = b_ref[...].reshape(1, 1, width)
    for i in range(CONV_WIDTH):
        s = CONV_WIDTH - 1 - i
        tap = x3 if s == 0 else pltpu.roll(jnp.where(t8 >= SUBLANES - s, pred, x3), s, 1)
        acc = acc + tap * w_ref[pl.ds(i, 1), :].reshape(1, 1, width)
    return _silu(acc).reshape(ROWS, width)


def _ssd_core(c, z_ref, dt_ref, dtb_ref, alog_ref, dsk_ref, nrm_ref, ltri_ref, xs, bm, cm, y_ref, state_io):
    dt = _softplus(dt_ref[...] + dtb_ref[...])
    d_a = dt * (-jnp.exp(alog_ref[...]))
    hi = d_a.astype(BF16)
    r1 = d_a - hi.astype(F32)
    mid = r1.astype(BF16)
    lo = (r1 - mid.astype(F32)).astype(BF16)
    ltri = ltri_ref[...]
    cum = _dot(ltri, hi) + _dot(ltri, mid) + _dot(ltri, lo)
    cum3 = cum.reshape(ROWS // c, c, LANES)
    c_last = jnp.broadcast_to(cum3[:, c - 1:c, :], cum3.shape).reshape(ROWS, LANES)
    to_end = jnp.exp(c_last - cum) * dt
    e_cum = jnp.exp(cum)
    cum_t, dt_t = cum.T, dt.T

    b_b, c_b = bm.astype(BF16), cm.astype(BF16)
    cb = _dot_nt(c_b, b_b)
    ri = lax.broadcasted_iota(jnp.int32, (ROWS, ROWS), 0)
    ci = lax.broadcasted_iota(jnp.int32, (ROWS, ROWS), 1)
    shift = c.bit_length() - 1
    causal = (ci <= ri) & (jnp.right_shift(ri, shift) == jnp.right_shift(ci, shift))
    lo_half = lax.broadcasted_iota(jnp.int32, (ROWS, LANES), 1) < SSM_HEADDIM

    def col(a, h):
        return jnp.broadcast_to(a[:, h:h + 1], (ROWS, LANES))

    def weights(h):
        seg = col(cum, h) - jnp.broadcast_to(cum_t[h:h + 1, :], (ROWS, ROWS))
        decay = jnp.exp(jnp.where(causal, seg, -jnp.inf))
        return (cb * decay * jnp.broadcast_to(dt_t[h:h + 1, :], (ROWS, ROWS))).astype(BF16)

    pieces = []
    for m in range(HEADS_PER_GROUP // 2):
        ha, hb = 2 * m, 2 * m + 1
        xp = xs[:, m * LANES:(m + 1) * LANES]
        w2 = jnp.concatenate([weights(ha), weights(hb)], axis=1)
        x2 = jnp.concatenate([jnp.where(lo_half, xp, 0.0), jnp.where(lo_half, 0.0, xp)], axis=0).astype(BF16)
        y = _dot(w2, x2)
        xw = (xp * jnp.where(lo_half, col(to_end, ha), col(to_end, hb))).astype(BF16)
        y_state = state_io(m, c_b, xw, b_b, e_cum)
        y = y + y_state * jnp.where(lo_half, col(e_cum, ha), col(e_cum, hb))
        pieces.append(y + xp * dsk_ref[:, m * LANES:(m + 1) * LANES])
    yg = jnp.concatenate(pieces, axis=1) * _silu(z_ref[...])
    ms = jnp.mean(yg * yg, axis=-1, keepdims=True)
    y_ref[...] = (yg * lax.rsqrt(ms + EPS) * nrm_ref[...]).astype(BF16)


def _pair_decay(e_cum, row, m):
    top = lax.broadcasted_iota(jnp.int32, (2 * SSM_HEADDIM, SSM_DSTATE), 0) < SSM_HEADDIM
    ea = jnp.broadcast_to(e_cum[row:row + 1, 2 * m:2 * m + 1], top.shape)
    eb = jnp.broadcast_to(e_cum[row:row + 1, 2 * m + 1:2 * m + 2], top.shape)
    return jnp.where(top, ea, eb)


def _ssd_prompt_kernel(z_ref, xs_ref, b_ref, c_ref, dt_ref, cwx_ref, cbx_ref, cwb_ref, cbb_ref, cwc_ref, cbc_ref,
                       dtb_ref, alog_ref, dsk_ref, nrm_ref, ltri_ref, y_ref, hs_ref, px_ref, pb_ref, pc_ref):
    @pl.when(pl.program_id(2) == 0)
    def _():
        hs_ref[...] = jnp.zeros_like(hs_ref)
        px_ref[...] = jnp.zeros_like(px_ref)
        pb_ref[...] = jnp.zeros_like(pb_ref)
        pc_ref[...] = jnp.zeros_like(pc_ref)

    def conv(x_ref, p_ref, w_ref, bias_ref):
        x = x_ref[...]
        x3 = x.reshape(ROWS // SUBLANES, SUBLANES, x.shape[1])
        pred = jnp.concatenate([p_ref[...], x3[:-1]], axis=0)
        out = _conv_piece(x, pred, w_ref, bias_ref)
        p_ref[0] = x3[-1]
        return out

    xs = conv(xs_ref, px_ref, cwx_ref, cbx_ref)
    bm = conv(b_ref, pb_ref, cwb_ref, cbb_ref)
    cm = conv(c_ref, pc_ref, cwc_ref, cbc_ref)

    def state_io(m, c_b, xw, b_b, e_cum):
        rows = pl.ds(m * 2 * SSM_HEADDIM, 2 * SSM_HEADDIM)
        h = hs_ref[0, rows, :]
        hs_ref[0, rows, :] = h * _pair_decay(e_cum, ROWS - 1, m) + _dot_tn(xw, b_b)
        return _dot_nt(c_b, h.astype(BF16))

    _ssd_core(CHUNK, z_ref, dt_ref, dtb_ref, alog_ref, dsk_ref, nrm_ref, ltri_ref, xs, bm, cm, y_ref, state_io)


def _ssd_sample_kernel(seq, z_ref, xs_ref, b_ref, c_ref, dt_ref, cwx_ref, cbx_ref, cwb_ref, cbb_ref, cwc_ref,
                       cbc_ref, dtb_ref, alog_ref, dsk_ref, nrm_ref, ltri_ref, bmask_ref, px_ref, pb_ref, pc_ref,
                       h0_ref, y_ref, hs_ref, csx_ref, csb_ref, csc_ref):
    nb = SAMPLE_PER_TILE

    def conv(x_ref, p_ref, w_ref, bias_ref, cs_ref):
        x = x_ref[...]
        cs_ref[...] = x.reshape(nb, seq, x.shape[1])
        return _conv_piece(x, p_ref[...], w_ref, bias_ref)

    xs = conv(xs_ref, px_ref, cwx_ref, cbx_ref, csx_ref)
    bm = conv(b_ref, pb_ref, cwb_ref, cbb_ref, csb_ref)
    cm = conv(c_ref, pc_ref, cwc_ref, cbc_ref, csc_ref)
    bmask = bmask_ref[...]

    def state_io(m, c_b, xw, b_b, e_cum):
        rows = pl.ds(m * 2 * SSM_HEADDIM, 2 * SSM_HEADDIM)
        c_bd = jnp.concatenate([c_b] * nb, axis=1) * bmask
        b_bd = jnp.concatenate([b_b] * nb, axis=1) * bmask
        hs = [h0_ref[i, rows, :] for i in range(nb)]
        h_cat = jnp.concatenate(hs, axis=1).astype(BF16)
        dh = _dot_tn(xw, b_bd)
        for i in range(nb):
            decay = _pair_decay(e_cum, i * seq + seq - 1, m)
            hs_ref[i, rows, :] = hs[i] * decay + dh[:, i * SSM_DSTATE:(i + 1) * SSM_DSTATE]
        return _dot_nt(c_bd, h_cat)

    _ssd_core(seq, z_ref, dt_ref, dtb_ref, alog_ref, dsk_ref, nrm_ref, ltri_ref, xs, bm, cm, y_ref, state_io)


def _ssd_tables(c):
    r = jnp.arange(ROWS)
    ltri = ((r[:, None] >= r[None, :]) & ((r[:, None] // c) == (r[None, :] // c))).astype(BF16)
    return ltri


def _ssd_specs(row_block):
    def at(width, off):
        return lambda *ids: (row_block(*ids)[0], off // width + row_block(*ids)[1])

    def grp(rows, width):
        return pl.BlockSpec((rows, width), lambda *ids: (0, row_block(*ids)[1]))

    n = SSM_DSTATE
    return [
        pl.BlockSpec((ROWS, GROUP_INNER), at(GROUP_INNER, _Cols.Z)),
        pl.BlockSpec((ROWS, GROUP_INNER), at(GROUP_INNER, _Cols.XS)),
        pl.BlockSpec((ROWS, n), at(n, _Cols.B)),
        pl.BlockSpec((ROWS, n), at(n, _Cols.C)),
        pl.BlockSpec((ROWS, LANES), at(LANES, _Cols.DT)),
        grp(CONV_WIDTH, GROUP_INNER), grp(1, GROUP_INNER),
        grp(CONV_WIDTH, n), grp(1, n), grp(CONV_WIDTH, n), grp(1, n),
        grp(1, LANES), grp(1, LANES), grp(1, GROUP_INNER), grp(1, GROUP_INNER),
    ]


def _ssd_params(conv_w, conv_b, dt_bias, a_log, d_skip, ssm_norm):
    gn = SSM_GROUPS * SSM_DSTATE
    cwx, cwb, cwc = conv_w[:, :SSM_INNER], conv_w[:, SSM_INNER:SSM_INNER + gn], conv_w[:, SSM_INNER + gn:]
    cb = conv_b.reshape(1, CONV_DIM)
    cbx, cbb, cbc = cb[:, :SSM_INNER], cb[:, SSM_INNER:SSM_INNER + gn], cb[:, SSM_INNER + gn:]

    def per_group_lanes(v):
        v = v.reshape(SSM_GROUPS, HEADS_PER_GROUP)
        return jnp.pad(v, ((0, 0), (0, LANES - HEADS_PER_GROUP))).reshape(1, SSM_GROUPS * LANES)

    dsk = jnp.repeat(d_skip, SSM_HEADDIM).reshape(1, SSM_INNER)
    return (cwx, cbx, cwb, cbb, cwc, cbc, per_group_lanes(dt_bias), per_group_lanes(a_log), dsk,
            ssm_norm.reshape(1, SSM_INNER))


def _conv_tail(csx, csb, csc):
    keep = SUBLANES - (CONV_WIDTH - 1)
    return jnp.concatenate([csx[:, keep:], csb[:, keep:], csc[:, keep:]], axis=-1)


def _ssd_prompt(proj, bsz, seq, params):
    nch = seq // ROWS
    specs = _ssd_specs(lambda b, g, c: (b * nch + c, g))
    specs.append(pl.BlockSpec((ROWS, ROWS), lambda b, g, c: (0, 0)))
    gh = HEADS_PER_GROUP * SSM_HEADDIM
    gn = SSM_GROUPS * SSM_DSTATE
    tail = lambda width: pl.BlockSpec((1, SUBLANES, width), lambda b, g, c: (b, 0, g))
    y, hs, csx, csb, csc = pl.pallas_call(
        _ssd_prompt_kernel,
        out_shape=(jax.ShapeDtypeStruct((bsz * seq, SSM_INNER), BF16),
                   jax.ShapeDtypeStruct((bsz, SSM_HEADS * SSM_HEADDIM, SSM_DSTATE), F32),
                   jax.ShapeDtypeStruct((bsz, SUBLANES, SSM_INNER), F32),
                   jax.ShapeDtypeStruct((bsz, SUBLANES, gn), F32),
                   jax.ShapeDtypeStruct((bsz, SUBLANES, gn), F32)),
        grid=(bsz, SSM_GROUPS, nch),
        in_specs=specs,
        out_specs=(pl.BlockSpec((ROWS, GROUP_INNER), lambda b, g, c: (b * nch + c, g)),
                   pl.BlockSpec((1, gh, SSM_DSTATE), lambda b, g, c: (b, g, 0)),
                   tail(GROUP_INNER), tail(SSM_DSTATE), tail(SSM_DSTATE)),
        compiler_params=_params(("parallel", "parallel", "arbitrary")),
        name="ssd_prompt",
    )(proj, proj, proj, proj, proj, *params, _ssd_tables(CHUNK))
    return y, hs, _conv_tail(csx, csb, csc)


def _ssd_sample(proj, row0, bsz, seq, params, state, conv_state):
    ntile = bsz // SAMPLE_PER_TILE
    rb0 = row0 // ROWS
    gn = SSM_GROUPS * SSM_DSTATE
    specs = _ssd_specs(lambda t, g: (rb0 + t, g))
    gh = HEADS_PER_GROUP * SSM_HEADDIM
    cs = jnp.pad(conv_state, ((0, 0), (SUBLANES - (CONV_WIDTH - 1), 0), (0, 0)))
    csx, csb, csc = cs[:, :, :SSM_INNER], cs[:, :, SSM_INNER:SSM_INNER + gn], cs[:, :, SSM_INNER + gn:]
    pred = lambda width: pl.BlockSpec((SAMPLE_PER_TILE, SUBLANES, width), lambda t, g: (t, 0, g))
    state_spec = pl.BlockSpec((SAMPLE_PER_TILE, gh, SSM_DSTATE), lambda t, g: (t, g, 0))
    specs += [
        pl.BlockSpec((ROWS, ROWS), lambda t, g: (0, 0)),
        pl.BlockSpec((ROWS, SAMPLE_PER_TILE * LANES), lambda t, g: (0, 0)),
        pred(GROUP_INNER), pred(SSM_DSTATE), pred(SSM_DSTATE),
        state_spec,
    ]
    state2 = state.reshape(bsz, SSM_HEADS * SSM_HEADDIM, SSM_DSTATE)
    assert seq == SUBLANES
    y, hs, csx, csb, csc = pl.pallas_call(
        functools.partial(_ssd_sample_kernel, seq),
        out_shape=(jax.ShapeDtypeStruct((bsz * seq, SSM_INNER), BF16), jax.ShapeDtypeStruct(state2.shape, F32),
                   jax.ShapeDtypeStruct((bsz, seq, SSM_INNER), F32),
                   jax.ShapeDtypeStruct((bsz, seq, gn), F32),
                   jax.ShapeDtypeStruct((bsz, seq, gn), F32)),
        grid=(ntile, SSM_GROUPS),
        in_specs=specs,
        out_specs=(pl.BlockSpec((ROWS, GROUP_INNER), lambda t, g: (t, g)), state_spec,
                   pred(GROUP_INNER), pred(SSM_DSTATE), pred(SSM_DSTATE)),
        compiler_params=_params(("parallel", "parallel")),
        name="ssd_sample",
    )(proj, proj, proj, proj, proj, *params, _ssd_tables(seq), _block_mask(BF16), csx, csb, csc, state2)
    return y, hs, _conv_tail(csx, csb, csc)


def _post_kernel(n_prompt_tiles, ap_ref, as_ref, yp_ref, ys_ref, ga_ref, gb_ref, xp_ref, xs_ref, wr_ref, ws_ref,
                 wo_ref, nw_ref, wrt_ref, brt_ref, x1_ref, h2_ref, lg_ref):
    is_prompt = pl.program_id(0) < n_prompt_tiles
    branch_a = _dot(jnp.where(is_prompt, ap_ref[...], as_ref[...]), wr_ref[...])
    branch_b = _dot(jnp.where(is_prompt, yp_ref[...], ys_ref[...]), ws_ref[...])
    merged = jax.nn.sigmoid(ga_ref[...]) * branch_a + jax.nn.sigmoid(gb_ref[...]) * branch_b
    x1 = jnp.where(is_prompt, xp_ref[...], xs_ref[...]) + _dot(merged.astype(BF16), wo_ref[...])
    x1_ref[...] = x1
    ms = jnp.mean(x1 * x1, axis=-1, keepdims=True)
    h2 = x1 * lax.rsqrt(ms + EPS) * nw_ref[...]
    _store_row_tiles(h2_ref, h2)
    lg_ref[...] = _dot(h2.astype(BF16), wrt_ref[...]) + brt_ref[...]


def _post(a_p, a_s, y_p, y_s, proj, xp, xs, w_ret, w_ssm, w_out, norm_ffn, w_router, b_router):
    t = xp.shape[0] + xs.shape[0]
    tm = _pick(math.gcd(xp.shape[0], xs.shape[0]), 512)
    npt = xp.shape[0] // tm
    row = lambda width, blk: pl.BlockSpec((tm, width), lambda i: (i, blk))
    full = lambda *shape: pl.BlockSpec(shape, lambda i: (0,) * len(shape))
    return pl.pallas_call(
        functools.partial(_post_kernel, npt),
        out_shape=(jax.ShapeDtypeStruct((t, D_MODEL), F32), jax.ShapeDtypeStruct((t * ROW_TILES, LANES), F32),
                   jax.ShapeDtypeStruct((t, LANES), F32)),
        grid=(t // tm,),
        in_specs=[*_split_rows(tm, npt, RET_V), *_split_rows(tm, npt, SSM_INNER),
                  row(D_MODEL, _Cols.GA // D_MODEL), row(D_MODEL, _Cols.GB // D_MODEL),
                  *_split_rows(tm, npt, D_MODEL),
                  full(RET_V, D_MODEL), full(SSM_INNER, D_MODEL), full(D_MODEL, D_MODEL), full(1, D_MODEL),
                  full(D_MODEL, LANES), full(1, LANES)],
        out_specs=(row(D_MODEL, 0), pl.BlockSpec((tm * ROW_TILES, LANES), lambda i: (i, 0)), row(LANES, 0)),
        compiler_params=_params(("parallel",)),
        name="post_mixer",
    )(a_p, a_s, y_p, y_s, proj, proj, xp, xs, w_ret, w_ssm, w_out, norm_ffn, w_router, b_router)


def _route_kernel(lg_ref, lstrict_ref, meta_ref, gate_ref, cnt_ref):
    @pl.when(pl.program_id(0) == 0)
    def _():
        cnt_ref[...] = jnp.zeros_like(cnt_ref)

    tm = lg_ref.shape[0]
    lane = lax.broadcasted_iota(jnp.int32, (tm, LANES), 1)
    lane_f = lane.astype(F32)
    cur = jnp.where(lane < N_EXPERTS, lg_ref[...], -jnp.inf)
    vals, hots = [], []
    for _ in range(TOP_K):
        m = jnp.max(cur, axis=1, keepdims=True)
        idx = jnp.min(jnp.where(cur == m, lane_f, float(LANES)), axis=1, keepdims=True)
        hot = lane_f == idx
        vals.append(m)
        hots.append(hot)
        cur = jnp.where(hot, -jnp.inf, cur)
    exps = [jnp.exp(v - vals[0]) for v in vals]
    denom = exps[0] + exps[1] + exps[2] + exps[3]
    sel = hots[0] | hots[1] | hots[2] | hots[3]
    self32 = sel.astype(F32)
    before = _dot(lstrict_ref[...], self32.astype(BF16)) + cnt_ref[...]
    cnt_ref[...] = cnt_ref[...] + jnp.sum(self32, axis=0, keepdims=True)
    meta = jnp.zeros((tm, LANES), F32)
    gates = jnp.zeros((tm, LANES), F32)
    for k in range(TOP_K):
        e_k = jnp.sum(jnp.where(hots[k], lane_f, 0.0), axis=1, keepdims=True)
        p_k = jnp.sum(jnp.where(hots[k], before, 0.0), axis=1, keepdims=True)
        meta = jnp.where(lane == k, e_k, meta)
        meta = jnp.where(lane == SUBLANES + k, p_k, meta)
        gates = jnp.where(lane == k, exps[k] / denom, gates)
    gate_ref[...] = gates
    meta_ref[...] = meta.T[:2 * SUBLANES, :].astype(jnp.int32)


def _route(logits):
    t = logits.shape[0]
    tm = _pick(t, 512)
    r = jnp.arange(tm)
    lstrict = (r[:, None] > r[None, :]).astype(BF16)
    return pl.pallas_call(
        _route_kernel,
        out_shape=(jax.ShapeDtypeStruct((2 * SUBLANES, t), jnp.int32),
                   jax.ShapeDtypeStruct((t, LANES), F32),
                   jax.ShapeDtypeStruct((1, LANES), F32)),
        grid=(t // tm,),
        in_specs=[pl.BlockSpec((tm, LANES), lambda i: (i, 0)), pl.BlockSpec((tm, tm), lambda i: (0, 0))],
        out_specs=(pl.BlockSpec((2 * SUBLANES, tm), lambda i: (0, i)),
                   pl.BlockSpec((tm, LANES), lambda i: (i, 0)),
                   pl.BlockSpec((1, LANES), lambda i: (0, 0))),
        compiler_params=_params(("arbitrary",)),
        name="route",
    )(logits, lstrict)


def _row_copy(src_ref, src_row, dst_ref, dst_row, sem):
    src = pl.ds(pl.multiple_of(src_row * ROW_TILES, ROW_TILES), ROW_TILES)
    dst = pl.ds(pl.multiple_of(dst_row * ROW_TILES, ROW_TILES), ROW_TILES)
    return pltpu.make_async_copy(src_ref.at[src, :], dst_ref.at[dst, :], sem)


def _dispatch_kernel(pstart_ref, pend_ref, dest_ref, h2_ref, xs_ref, zero_ref, sem):
    tm = h2_ref.shape[0] // ROW_TILES

    @pl.when(pl.program_id(0) == 0)
    def _():
        zero_ref[...] = jnp.zeros_like(zero_ref)

        def fill(e):
            n = EXPERT_BLOCK * ROW_TILES
            start = pl.multiple_of((pend_ref[e] - EXPERT_BLOCK) * ROW_TILES, n)
            return pltpu.make_async_copy(zero_ref, xs_ref.at[pl.ds(start, n), :], sem)

        def fill_start(e, carry):
            @pl.when(pend_ref[e] > pstart_ref[e])
            def _():
                fill(e).start()
            return carry

        def fill_wait(e, carry):
            @pl.when(pend_ref[e] > pstart_ref[e])
            def _():
                fill(e).wait()
            return carry

        lax.fori_loop(0, N_EXPERTS, fill_start, 0)
        lax.fori_loop(0, N_EXPERTS, fill_wait, 0)

        def tail(j):
            n = EXPERT_BLOCK * ROW_TILES
            return pltpu.make_async_copy(zero_ref, xs_ref.at[pl.ds(pl.multiple_of(j * n, n), n), :], sem)

        used = pend_ref[N_EXPERTS - 1] // EXPERT_BLOCK
        total = xs_ref.shape[0] // (EXPERT_BLOCK * ROW_TILES)
        lax.fori_loop(used, total, lambda j, c: (tail(j).start(), c)[1], 0)
        lax.fori_loop(used, total, lambda j, c: (tail(j).wait(), c)[1], 0)

    def copy(t, k):
        return _row_copy(h2_ref, t, xs_ref, dest_ref[k, t], sem)

    def issue(t, carry):
        for k in range(TOP_K):
            copy(t, k).start(priority=k % 2)
        return carry

    def drain(t, carry):
        for k in range(TOP_K):
            copy(t, k).wait()
        return carry

    lax.fori_loop(0, tm, issue, 0, unroll=ISSUE_UNROLL)
    lax.fori_loop(0, tm, drain, 0, unroll=ISSUE_UNROLL)


def _dispatch(pstart, pends, dest, h2, n_rows):
    t = h2.shape[0] // ROW_TILES
    tm = _pick(t, 256)
    return pl.pallas_call(
        _dispatch_kernel,
        out_shape=jax.ShapeDtypeStruct((n_rows * ROW_TILES, LANES), F32),
        grid_spec=pltpu.PrefetchScalarGridSpec(
            num_scalar_prefetch=2,
            grid=(t // tm,),
            in_specs=[pl.BlockSpec((SUBLANES, tm), lambda i, ps, pe: (0, i), memory_space=pltpu.SMEM),
                      pl.BlockSpec((tm * ROW_TILES, LANES), lambda i, ps, pe: (i, 0))],
            out_specs=pl.BlockSpec(memory_space=pl.ANY),
            scratch_shapes=[pltpu.VMEM((EXPERT_BLOCK * ROW_TILES, LANES), F32), pltpu.SemaphoreType.DMA],
        ),
        compiler_params=_params(("arbitrary",)),
        name="dispatch",
    )(pstart, pends, dest, h2)


def _combine_kernel(n_prompt_tiles, dest_ref, gate_ref, x1_ref, nw_ref, yb_ref, op_ref, os_ref, buf_ref, sem):
    tm = x1_ref.shape[0]

    def copy(t, k):
        return _row_copy(yb_ref, dest_ref[k, t], buf_ref.at[k], t, sem)

    def issue(t, carry):
        for k in range(TOP_K):
            copy(t, k).start(priority=k % 2)
        return carry

    def drain(t, carry):
        for k in range(TOP_K):
            copy(t, k).wait()
        return carry

    lax.fori_loop(0, tm, issue, 0, unroll=ISSUE_UNROLL)
    lax.fori_loop(0, tm, drain, 0, unroll=ISSUE_UNROLL)
    gates = gate_ref[...]
    moe = _load_row_tiles(buf_ref, tm, 0) * gates[:, 0:1]
    for k in range(1, TOP_K):
        moe = moe + _load_row_tiles(buf_ref, tm, k) * gates[:, k:k + 1]
    x2 = x1_ref[...] + moe
    ms = jnp.mean(x2 * x2, axis=-1, keepdims=True)
    out = x2 * lax.rsqrt(ms + EPS) * nw_ref[...]

    @pl.when(pl.program_id(0) < n_prompt_tiles)
    def _():
        op_ref[...] = out

    @pl.when(pl.program_id(0) >= n_prompt_tiles)
    def _():
        os_ref[...] = out


def _combine(dest, gates, x1, norm_final, yb, n_prompt):
    t = x1.shape[0]
    tm = _pick(math.gcd(n_prompt, t - n_prompt), 256)
    npt = n_prompt // tm
    return pl.pallas_call(
        functools.partial(_combine_kernel, npt),
        out_shape=(jax.ShapeDtypeStruct((n_prompt, D_MODEL), F32),
                   jax.ShapeDtypeStruct((t - n_prompt, D_MODEL), F32)),
        grid=(t // tm,),
        in_specs=[pl.BlockSpec((SUBLANES, tm), lambda i: (0, i), memory_space=pltpu.SMEM),
                  pl.BlockSpec((tm, LANES), lambda i: (i, 0)),
                  pl.BlockSpec((tm, D_MODEL), lambda i: (i, 0)),
                  pl.BlockSpec((1, D_MODEL), lambda i: (0, 0)),
                  pl.BlockSpec(memory_space=pl.ANY)],
        out_specs=_split_rows(tm, npt, D_MODEL),
        scratch_shapes=[pltpu.VMEM((TOP_K, tm * ROW_TILES, LANES), F32), pltpu.SemaphoreType.DMA],
        compiler_params=_params(("arbitrary",)),
        name="combine",
    )(dest, gates, x1, norm_final, yb)


def _expert_kernel(be_ref, nv_ref, xs_ref, wgu_ref, bgu_ref, wd_ref, bd_ref, yb_ref, wgu_b, wd_b):
    i = pl.program_id(0)

    @pl.when(jnp.logical_or(i == 0, be_ref[i] != be_ref[jnp.maximum(i - 1, 0)]))
    def _():
        wgu_b[...] = wgu_ref[0].astype(BF16)
        wd_b[...] = wd_ref[0].astype(BF16)

    @pl.when(i < nv_ref[0])
    def _():
        gu = _dot(_load_row_tiles(xs_ref, EXPERT_BLOCK).astype(BF16), wgu_b[...]) + bgu_ref[0]
        glu = jnp.minimum(gu[:, :D_FF], SWIGLU_LIMIT)
        lin = jnp.clip(gu[:, D_FF:], -SWIGLU_LIMIT, SWIGLU_LIMIT)
        act = glu * jax.nn.sigmoid(SWIGLU_ALPHA * glu) * (lin + 1.0)
        _store_row_tiles(yb_ref, _dot(act.astype(BF16), wd_b[...]) + bd_ref[0])

    @pl.when(i >= nv_ref[0])
    def _():
        yb_ref[...] = jnp.zeros_like(yb_ref)


def _experts(block_expert, n_valid, xs, w_gu, b_gu, w_d, b_d):
    rows = EXPERT_BLOCK * ROW_TILES
    nb = xs.shape[0] // rows
    return pl.pallas_call(
        _expert_kernel,
        out_shape=jax.ShapeDtypeStruct(xs.shape, F32),
        grid_spec=pltpu.PrefetchScalarGridSpec(
            num_scalar_prefetch=2,
            grid=(nb,),
            in_specs=[pl.BlockSpec((rows, LANES), lambda i, be, nv: (jnp.minimum(i, nv[0] - 1), 0)),
                      pl.BlockSpec((1, D_MODEL, 2 * D_FF), lambda i, be, nv: (be[i], 0, 0)),
                      pl.BlockSpec((1, 1, 2 * D_FF), lambda i, be, nv: (be[i], 0, 0)),
                      pl.BlockSpec((1, D_FF, D_MODEL), lambda i, be, nv: (be[i], 0, 0)),
                      pl.BlockSpec((1, 1, D_MODEL), lambda i, be, nv: (be[i], 0, 0))],
            out_specs=pl.BlockSpec((rows, LANES), lambda i, be, nv: (i, 0)),
            scratch_shapes=[pltpu.VMEM((D_MODEL, 2 * D_FF), BF16), pltpu.VMEM((D_FF, D_MODEL), BF16)],
        ),
        compiler_params=_params(("arbitrary",)),
        name="experts",
    )(block_expert, n_valid, xs, w_gu, b_gu, w_d, b_d)


def _expert_layout(counts, n_blocks):
    counts = counts.astype(jnp.int32)
    padded = (counts + EXPERT_BLOCK - 1) // EXPERT_BLOCK * EXPERT_BLOCK
    pends = jnp.cumsum(padded)
    pstart = pends - padded
    block_start = jnp.arange(n_blocks, dtype=jnp.int32) * EXPERT_BLOCK
    block_expert = jnp.minimum(jnp.sum(pends[None, :] <= block_start[:, None], axis=1), N_EXPERTS - 1)
    n_valid = (pends[-1:] // EXPERT_BLOCK).astype(jnp.int32)
    return pstart.astype(jnp.int32), pends.astype(jnp.int32), block_expert.astype(jnp.int32), n_valid


def _permute_w_in(w_in):
    sizes = [RET_QK, RET_QK, RET_V, RET_V, SSM_INNER, CONV_DIM, SSM_HEADS, D_MODEL, D_MODEL]
    offs = [0]
    for s in sizes:
        offs.append(offs[-1] + s)
    q, k, v, g_ret, z, xbc, dt, g_a, g_b = [w_in[:, offs[i]:offs[i + 1]] for i in range(len(sizes))]
    gn = SSM_GROUPS * SSM_DSTATE
    xs, bm, cm = xbc[:, :SSM_INNER], xbc[:, SSM_INNER:SSM_INNER + gn], xbc[:, SSM_INNER + gn:]
    dt4 = jnp.pad(dt.reshape(D_MODEL, SSM_GROUPS, HEADS_PER_GROUP),
                  ((0, 0), (0, 0), (0, LANES - HEADS_PER_GROUP))).reshape(D_MODEL, SSM_GROUPS * LANES)
    return jnp.concatenate([z, xs, v, g_ret, g_a, g_b, q, k, bm, cm, dt4], axis=1).astype(BF16)


def _forward(x_prompt, x_sample, state_ret, state_ssm, state_conv, norm_mix, w_in, ret_norm, w_out_ret,
             conv_w, conv_b, dt_bias, a_log, d_skip, ssm_norm, w_out_ssm, w_out, norm_ffn,
             w_router, b_router, w_gate_up, b_gate_up, w_down, b_down, norm_final):
    bp, lp, _ = x_prompt.shape
    bs, ls, _ = x_sample.shape
    assert lp % ROWS == 0 and bs % SAMPLE_PER_TILE == 0 and ls * SAMPLE_PER_TILE == ROWS
    tp, ts = bp * lp, bs * ls
    t_all = tp + ts
    xp, xs_in = x_prompt.reshape(tp, D_MODEL), x_sample.reshape(ts, D_MODEL)

    proj = _inproj(xp, xs_in, norm_mix[0].reshape(1, D_MODEL), _permute_w_in(w_in[0]))

    a_p, ret_p = _retention_prompt(proj, bp, lp, ret_norm[0])
    a_s, ret_s = _retention_sample(proj, tp, bs, ls, state_ret[0], ret_norm[0])

    sp = _ssd_params(conv_w[0], conv_b[0], dt_bias[0], a_log[0], d_skip[0], ssm_norm[0])
    y_p, ssm_p, conv_p = _ssd_prompt(proj, bp, lp, sp)
    y_s, ssm_s, conv_s = _ssd_sample(proj, tp, bs, ls, sp, state_ssm[0], state_conv[0])

    w_router_pad = jnp.pad(w_router[0], ((0, 0), (0, LANES - N_EXPERTS))).astype(BF16)
    b_router_pad = jnp.pad(b_router[0], (0, LANES - N_EXPERTS)).reshape(1, LANES)
    x1, h2, logits = _post(a_p, a_s, y_p, y_s, proj, xp, xs_in, w_out_ret[0].astype(BF16), w_out_ssm[0].astype(BF16),
                           w_out[0].astype(BF16), norm_ffn[0].reshape(1, D_MODEL), w_router_pad, b_router_pad)

    meta, gates, counts = _route(logits)
    n_blocks = -(-(t_all * TOP_K + N_EXPERTS * (EXPERT_BLOCK - 1)) // EXPERT_BLOCK)
    pstart, pends, block_expert, n_valid = _expert_layout(counts[0, :N_EXPERTS], n_blocks)
    hot = meta[:TOP_K, :, None] == jnp.arange(N_EXPERTS, dtype=jnp.int32)
    seg_start = jnp.sum(jnp.where(hot, pstart, 0), axis=-1)
    dest = jnp.pad(seg_start + meta[SUBLANES:SUBLANES + TOP_K], ((0, SUBLANES - TOP_K), (0, 0)))
    xs = _dispatch(pstart, pends, dest, h2, n_blocks * EXPERT_BLOCK)
    yb = _experts(block_expert, n_valid, xs, w_gate_up[0], b_gate_up[0].reshape(N_EXPERTS, 1, 2 * D_FF),
                  w_down[0], b_down[0].reshape(N_EXPERTS, 1, D_MODEL))
    out_p, out_s = _combine(dest, gates, x1, norm_final.reshape(1, D_MODEL), yb, tp)

    shape_s = (1, bs, SSM_HEADS, SSM_HEADDIM, SSM_DSTATE)
    shape_p = (1, bp, SSM_HEADS, SSM_HEADDIM, SSM_DSTATE)
    return (out_p.reshape(bp, lp, D_MODEL), out_s.reshape(bs, ls, D_MODEL),
            ret_p[None], ret_s[None], ssm_p.reshape(shape_p), ssm_s.reshape(shape_s),
            conv_p[None], conv_s[None])


def kernel(x_prompt, x_sample, state_ret, state_ssm, state_conv, norm_mix, w_in, ret_norm, w_out_ret, conv_w, conv_b, dt_bias, a_log, d_skip, ssm_norm, w_out_ssm, w_out, norm_ffn, w_router, b_router, w_gate_up, b_gate_up, w_down, b_down, norm_final):
    return _forward(x_prompt, x_sample, state_ret, state_ssm, state_conv, norm_mix, w_in, ret_norm, w_out_ret,
                    conv_w, conv_b, dt_bias, a_log, d_skip, ssm_norm, w_out_ssm, w_out, norm_ffn,
                    w_router, b_router, w_gate_up, b_gate_up, w_down, b_down, norm_final)
```

```python
import functools
import math

import jax
import jax.numpy as jnp
from jax import lax
from jax.experimental import pallas as pl
from jax.experimental.pallas import tpu as pltpu

F32 = jnp.float32
BF16 = jnp.bfloat16

D_MODEL = 1024
PAST_LEN = 16384
RET_HEADS = 4
RET_DK = 128
RET_DV = 256
RET_QK = RET_HEADS * RET_DK
RET_V = RET_HEADS * RET_DV
ROPE_BASE = 10000.0
SSM_INNER = 2 * D_MODEL
SSM_HEADDIM = 64
SSM_HEADS = SSM_INNER // SSM_HEADDIM
SSM_GROUPS = 4
SSM_DSTATE = 128
HEADS_PER_GROUP = SSM_HEADS // SSM_GROUPS
GROUP_INNER = SSM_INNER // SSM_GROUPS
CONV_WIDTH = 4
CONV_DIM = SSM_INNER + 2 * SSM_GROUPS * SSM_DSTATE
CHUNK = 128
N_EXPERTS = 32
TOP_K = 4
D_FF = D_MODEL
SWIGLU_LIMIT = 7.0
SWIGLU_ALPHA = 1.702
EPS = 1e-6

LANES = 128
SUBLANES = 8
ROWS = 128
SAMPLE_PER_TILE = 16
EXPERT_BLOCK = 512
ROW_TILES = D_MODEL // LANES
ISSUE_UNROLL = 4
VMEM_LIMIT = 56 * 1024 * 1024


class _Cols:
    Z = 0
    XS = Z + SSM_INNER
    V = XS + SSM_INNER
    GRET = V + RET_V
    GA = GRET + RET_V
    GB = GA + D_MODEL
    Q = GB + D_MODEL
    K = Q + RET_QK
    B = K + RET_QK
    C = B + SSM_GROUPS * SSM_DSTATE
    DT = C + SSM_GROUPS * SSM_DSTATE
    TOTAL = DT + 4 * LANES


def _pick(n, target):
    best = None
    for c in range(LANES, target + 1, LANES):
        if n % c == 0:
            best = c
    assert best is not None, (n, target)
    return best


def _params(sem, **kw):
    return pltpu.CompilerParams(dimension_semantics=sem, vmem_limit_bytes=VMEM_LIMIT, **kw)


def _dot(a, b):
    return jnp.dot(a, b, preferred_element_type=F32)


def _dot_nt(a, b):
    return lax.dot_general(a, b, (((1,), (1,)), ((), ())), preferred_element_type=F32)


def _dot_tn(a, b):
    return lax.dot_general(a, b, (((0,), (0,)), ((), ())), preferred_element_type=F32)


def _silu(x):
    return x * jax.nn.sigmoid(x)


def _store_row_tiles(ref, value, *lead):
    n = value.shape[0]
    for s in range(ROW_TILES):
        ref[(*lead, pl.ds(s, n, stride=ROW_TILES), slice(None))] = value[:, s * LANES:(s + 1) * LANES]


def _load_row_tiles(ref, n, *lead):
    return jnp.concatenate([ref[(*lead, pl.ds(s, n, stride=ROW_TILES), slice(None))] for s in range(ROW_TILES)],
                           axis=1)


def _inproj_kernel(n_prompt_tiles, xp_ref, xs_ref, nw_ref, w_ref, o_ref, h_ref):
    @pl.when(pl.program_id(1) == 0)
    def _():
        x = jnp.where(pl.program_id(0) < n_prompt_tiles, xp_ref[...], xs_ref[...])
        ms = jnp.mean(x * x, axis=-1, keepdims=True)
        h_ref[...] = (x * lax.rsqrt(ms + EPS) * nw_ref[...]).astype(BF16)

    o_ref[...] = _dot(h_ref[...], w_ref[...])


def _split_rows(tm, npt, width):
    prompt = pl.BlockSpec((tm, width), lambda i, *_: (jnp.minimum(i, npt - 1), 0))
    sample = pl.BlockSpec((tm, width), lambda i, *_: (jnp.maximum(i - npt, 0), 0))
    return prompt, sample


def _inproj(xp, xs, norm_w, w_perm):
    t = xp.shape[0] + xs.shape[0]
    tm = _pick(math.gcd(xp.shape[0], xs.shape[0]), 1024)
    npt = xp.shape[0] // tm
    tn = _pick(_Cols.TOTAL, 1536)
    return pl.pallas_call(
        functools.partial(_inproj_kernel, npt),
        out_shape=jax.ShapeDtypeStruct((t, _Cols.TOTAL), F32),
        grid=(t // tm, _Cols.TOTAL // tn),
        in_specs=[
            *_split_rows(tm, npt, D_MODEL),
            pl.BlockSpec((1, D_MODEL), lambda i, j: (0, 0)),
            pl.BlockSpec((D_MODEL, tn), lambda i, j: (0, j)),
        ],
        out_specs=pl.BlockSpec((tm, tn), lambda i, j: (i, j)),
        scratch_shapes=[pltpu.VMEM((tm, D_MODEL), BF16)],
        compiler_params=_params(("parallel", "arbitrary")),
        name="in_proj",
    )(xp, xs, norm_w, w_perm)


def _ret_log_decay():
    return jnp.log(1.0 - 2.0 ** (-5.0 - jnp.arange(RET_HEADS, dtype=F32)))


def _ret_tables(c):
    lg = _ret_log_decay()
    r = jnp.arange(ROWS)
    t = (r % c).astype(F32)
    seg = r // c
    diff = t[:, None] - t[None, :]
    ok = (seg[:, None] == seg[None, :]) & (diff >= 0)
    dm = jnp.where(ok[None], jnp.exp(lg[:, None, None] * jnp.maximum(diff, 0.0)[None]), 0.0)
    qd = jnp.exp(lg[:, None] * (t[None, :] + 1.0))
    kd = jnp.exp(lg[:, None] * (c - 1.0 - t[None, :]))
    cd = jnp.exp(lg * c)
    qd = jnp.broadcast_to(qd[:, :, None], (RET_HEADS, ROWS, LANES))
    kd = jnp.broadcast_to(kd[:, :, None], (RET_HEADS, ROWS, LANES))
    cd = jnp.broadcast_to(cd[:, None, None], (RET_HEADS, 1, RET_DV))
    return dm, qd, kd, cd


def _rope_tables(pos):
    half = RET_DK // 2
    inv_freq = 1.0 / (ROPE_BASE ** jnp.linspace(0.0, 1.0, half, dtype=F32))
    ang = pos.astype(F32)[:, None] * inv_freq[None, :]
    cos, sin = jnp.cos(ang), jnp.sin(ang)
    return jnp.concatenate([cos, cos], -1), jnp.concatenate([-sin, sin], -1)


def _block_mask(dtype):
    r = jnp.arange(ROWS) // (ROWS // SAMPLE_PER_TILE)
    b = jnp.arange(SAMPLE_PER_TILE * LANES) // LANES
    return (r[:, None] == b[None, :]).astype(dtype)


def _ret_common(q, k, v, g, cosf, sinf, dm, qd, kd, nw):
    qr = q * cosf + pltpu.roll(q, RET_DK // 2, 1) * sinf
    kr = (k * cosf + pltpu.roll(k, RET_DK // 2, 1) * sinf) * (RET_DK ** -0.5)
    vb = v.astype(BF16)
    scores = _dot_nt(qr.astype(BF16), kr.astype(BF16)) * dm
    o_intra = _dot(scores.astype(BF16), vb)
    q_state = (qr * qd).astype(BF16)
    k_state = kr * kd

    def finish(o):
        ms = jnp.mean(o * o, axis=-1, keepdims=True)
        on = o * lax.rsqrt(ms + EPS) * nw
        return (_silu(g) * on).astype(BF16)

    return o_intra, q_state, k_state, vb, finish


def _ret_prompt_kernel(q_ref, k_ref, v_ref, g_ref, cos_ref, sin_ref, dm_ref, qd_ref, kd_ref, cd_ref, nw_ref,
                       a_ref, s_ref):
    @pl.when(pl.program_id(1) == 0)
    def _():
        s_ref[...] = jnp.zeros_like(s_ref)

    cosf, sinf = cos_ref[...], sin_ref[...]
    for h in range(RET_HEADS):
        dk = slice(h * RET_DK, (h + 1) * RET_DK)
        dv = slice(h * RET_DV, (h + 1) * RET_DV)
        o_intra, q_state, k_state, vb, finish = _ret_common(
            q_ref[:, dk], k_ref[:, dk], v_ref[:, dv], g_ref[:, dv], cosf, sinf, dm_ref[h], qd_ref[h], kd_ref[h],
            nw_ref[h])
        s = s_ref[0, h]
        a_ref[:, dv] = finish(o_intra + _dot(q_state, s.astype(BF16)))
        s_ref[0, h] = s * cd_ref[h] + _dot_tn(k_state.astype(BF16), vb)


def _ret_sample_kernel(q_ref, k_ref, v_ref, g_ref, cos_ref, sin_ref, dm_ref, qd_ref, kd_ref, cd_ref, nw_ref,
                       bm_ref, bmt_ref, s0_ref, a_ref, s_ref):
    o_intra, q_state, k_state, vb, finish = _ret_common(
        q_ref[...], k_ref[...], v_ref[...], g_ref[...], cos_ref[...], sin_ref[...], dm_ref[0], qd_ref[0], kd_ref[0],
        nw_ref[0])
    nb = SAMPLE_PER_TILE
    s0 = s0_ref[:, 0]
    q_bd = jnp.concatenate([q_state] * nb, axis=1) * bm_ref[...]
    a_ref[...] = finish(o_intra + _dot(q_bd, s0.reshape(nb * RET_DK, RET_DV).astype(BF16)))
    k_t = k_state.T.astype(BF16)
    k_bd_t = jnp.concatenate([k_t] * nb, axis=0) * bmt_ref[...]
    ds = _dot(k_bd_t, vb).reshape(nb, RET_DK, RET_DV)
    s_ref[:, 0] = s0 * cd_ref[0] + ds


def _ret_specs(row_block):
    def at(width, off):
        return lambda *ids: (row_block(*ids)[0], off // width + row_block(*ids)[1])

    def head(*shape):
        return pl.BlockSpec((1,) + shape, lambda *ids: (row_block(*ids)[1],) + (0,) * len(shape))

    return [
        pl.BlockSpec((ROWS, RET_DK), at(RET_DK, _Cols.Q)),
        pl.BlockSpec((ROWS, RET_DK), at(RET_DK, _Cols.K)),
        pl.BlockSpec((ROWS, RET_DV), at(RET_DV, _Cols.V)),
        pl.BlockSpec((ROWS, RET_DV), at(RET_DV, _Cols.GRET)),
    ], head


def _retention_prompt(proj, bsz, seq, ret_norm):
    nch = seq // ROWS
    dm, qd, kd, cd = _ret_tables(CHUNK)
    cosf, sinf = _rope_tables(jnp.arange(seq, dtype=jnp.int32))
    row = lambda width, off: pl.BlockSpec((ROWS, width), lambda b, c: (b * nch + c, off // width))
    full = lambda *shape: pl.BlockSpec(shape, lambda b, c: (0,) * len(shape))
    specs = [
        row(RET_QK, _Cols.Q), row(RET_QK, _Cols.K), row(RET_V, _Cols.V), row(RET_V, _Cols.GRET),
        pl.BlockSpec((ROWS, RET_DK), lambda b, c: (c, 0)),
        pl.BlockSpec((ROWS, RET_DK), lambda b, c: (c, 0)),
        full(RET_HEADS, ROWS, ROWS), full(RET_HEADS, ROWS, LANES), full(RET_HEADS, ROWS, LANES),
        full(RET_HEADS, 1, RET_DV), full(RET_HEADS, 1, RET_DV),
    ]
    return pl.pallas_call(
        _ret_prompt_kernel,
        out_shape=(jax.ShapeDtypeStruct((bsz * seq, RET_V), BF16),
                   jax.ShapeDtypeStruct((bsz, RET_HEADS, RET_DK, RET_DV), F32)),
        grid=(bsz, nch),
        in_specs=specs,
        out_specs=(pl.BlockSpec((ROWS, RET_V), lambda b, c: (b * nch + c, 0)),
                   pl.BlockSpec((1, RET_HEADS, RET_DK, RET_DV), lambda b, c: (b, 0, 0, 0))),
        compiler_params=_params(("parallel", "arbitrary")),
        name="retention_prompt",
    )(proj, proj, proj, proj, cosf, sinf, dm, qd, kd, cd, ret_norm.reshape(RET_HEADS, 1, RET_DV))


def _retention_sample(proj, row0, bsz, seq, state, ret_norm):
    ntile = bsz // SAMPLE_PER_TILE
    rb0 = row0 // ROWS
    dm, qd, kd, cd = _ret_tables(seq)
    pos = PAST_LEN + jnp.arange(seq, dtype=jnp.int32)
    cosf, sinf = _rope_tables(jnp.tile(pos, SAMPLE_PER_TILE))
    rb = lambda t, h: (rb0 + t, h)
    specs, head = _ret_specs(rb)
    full = lambda *shape: pl.BlockSpec(shape, lambda t, h: (0,) * len(shape))
    specs += [
        full(ROWS, RET_DK), full(ROWS, RET_DK),
        head(ROWS, ROWS), head(ROWS, LANES), head(ROWS, LANES), head(1, RET_DV), head(1, RET_DV),
        full(ROWS, SAMPLE_PER_TILE * LANES), full(SAMPLE_PER_TILE * LANES, ROWS),
        pl.BlockSpec((SAMPLE_PER_TILE, 1, RET_DK, RET_DV), lambda t, h: (t, h, 0, 0)),
    ]
    bm = _block_mask(BF16)
    return pl.pallas_call(
        _ret_sample_kernel,
        out_shape=(jax.ShapeDtypeStruct((bsz * seq, RET_V), BF16),
                   jax.ShapeDtypeStruct(state.shape, F32)),
        grid=(ntile, RET_HEADS),
        in_specs=specs,
        out_specs=(pl.BlockSpec((ROWS, RET_DV), lambda t, h: (t, h)),
                   pl.BlockSpec((SAMPLE_PER_TILE, 1, RET_DK, RET_DV), lambda t, h: (t, h, 0, 0))),
        compiler_params=_params(("parallel", "parallel")),
        name="retention_sample",
    )(proj, proj, proj, proj, cosf, sinf, dm, qd, kd, cd, ret_norm.reshape(RET_HEADS, 1, RET_DV),
      bm, bm.T, state)


def _softplus(x):
    return jnp.maximum(x, 0.0) + jnp.log1p(jnp.exp(-jnp.abs(x)))


def _conv_piece(x, pred, w_ref, b_ref):
    width = x.shape[1]
    x3 = x.reshape(ROWS // SUBLANES, SUBLANES, width)
    t8 = lax.broadcasted_iota(jnp.int32, x3.shape, 1)
    acc = b_ref[...].reshape(1, 1, width)
    for i in range(CONV_WIDTH):
        s = CONV_WIDTH - 1 - i
        tap = x3 if s == 0 else pltpu.roll(jnp.where(t8 >= SUBLANES - s, pred, x3), s, 1)
        acc = acc + tap * w_ref[pl.ds(i, 1), :].reshape(1, 1, width)
    return _silu(acc).reshape(ROWS, width)


def _ssd_decay_terms(c, dt_pre, a_log, ltri):
    dt = _softplus(dt_pre)
    d_a = dt * (-jnp.exp(a_log))
    hi = d_a.astype(BF16)
    r1 = d_a - hi.astype(F32)
    mid = r1.astype(BF16)
    lo = (r1 - mid.astype(F32)).astype(BF16)
    cum = _dot(ltri, hi) + _dot(ltri, mid) + _dot(ltri, lo)
    cum3 = cum.reshape(ROWS // c, c, LANES)
    c_last = jnp.broadcast_to(cum3[:, c - 1:c, :], cum3.shape).reshape(ROWS, LANES)
    to_end = jnp.exp(c_last - cum) * dt
    e_cum = jnp.exp(cum)
    return cum, to_end, e_cum, cum.T, dt.T


def _ssd_group(c, terms, head0, z_ref, dsk_ref, nrm_ref, xs, bm, cm, y_ref, state_io):
    cum, to_end, e_cum, cum_t, dt_t = terms
    b_b, c_b = bm.astype(BF16), cm.astype(BF16)
    cb = _dot_nt(c_b, b_b)
    ri = lax.broadcasted_iota(jnp.int32, (ROWS, ROWS), 0)
    ci = lax.broadcasted_iota(jnp.int32, (ROWS, ROWS), 1)
    shift = c.bit_length() - 1
    causal = (ci <= ri) & (jnp.right_shift(ri, shift) == jnp.right_shift(ci, shift))
    lo_half = lax.broadcasted_iota(jnp.int32, (ROWS, LANES), 1) < SSM_HEADDIM

    def col(a, h):
        return jnp.broadcast_to(a[:, h:h + 1], (ROWS, LANES))

    def weights(h):
        seg = col(cum, h) - jnp.broadcast_to(cum_t[h:h + 1, :], (ROWS, ROWS))
        decay = jnp.exp(jnp.where(causal, seg, -jnp.inf))
        return (cb * decay * jnp.broadcast_to(dt_t[h:h + 1, :], (ROWS, ROWS))).astype(BF16)

    pieces = []
    for m in range(HEADS_PER_GROUP // 2):
        ha, hb = head0 + 2 * m, head0 + 2 * m + 1
        xp = xs[:, m * LANES:(m + 1) * LANES]
        w2 = jnp.concatenate([weights(ha), weights(hb)], axis=1)
        x2 = jnp.concatenate([jnp.where(lo_half, xp, 0.0), jnp.where(lo_half, 0.0, xp)], axis=0).astype(BF16)
        y = _dot(w2, x2)
        xw = (xp * jnp.where(lo_half, col(to_end, ha), col(to_end, hb))).astype(BF16)
        y_state = state_io(m, ha, c_b, xw, b_b, e_cum)
        y = y + y_state * jnp.where(lo_half, col(e_cum, ha), col(e_cum, hb))
        pieces.append(y + xp * dsk_ref[:, m * LANES:(m + 1) * LANES])
    yg = jnp.concatenate(pieces, axis=1) * _silu(z_ref[...])
    ms = jnp.mean(yg * yg, axis=-1, keepdims=True)
    y_ref[...] = (yg * lax.rsqrt(ms + EPS) * nrm_ref[...]).astype(BF16)


def _pair_decay(e_cum, row, ha):
    top = lax.broadcasted_iota(jnp.int32, (2 * SSM_HEADDIM, SSM_DSTATE), 0) < SSM_HEADDIM
    ea = jnp.broadcast_to(e_cum[row:row + 1, ha:ha + 1], top.shape)
    eb = jnp.broadcast_to(e_cum[row:row + 1, ha + 1:ha + 2], top.shape)
    return jnp.where(top, ea, eb)


def _ssd_prompt_kernel(z_ref, xs_ref, b_ref, c_ref, dt_ref, cwx_ref, cbx_ref, cwb_ref, cbb_ref, cwc_ref, cbc_ref,
                       dtb_ref, alog_ref, dsk_ref, nrm_ref, ltri_ref, y_ref, hs_ref, px_ref, pb_ref, pc_ref):
    @pl.when(pl.program_id(1) == 0)
    def _():
        hs_ref[...] = jnp.zeros_like(hs_ref)
        px_ref[...] = jnp.zeros_like(px_ref)
        pb_ref[...] = jnp.zeros_like(pb_ref)
        pc_ref[...] = jnp.zeros_like(pc_ref)

    def conv(x_ref, p_ref, w_ref, bias_ref):
        x = x_ref[...]
        x3 = x.reshape(ROWS // SUBLANES, SUBLANES, x.shape[1])
        pred = jnp.concatenate([p_ref[...], x3[:-1]], axis=0)
        out = _conv_piece(x, pred, w_ref, bias_ref)
        p_ref[0] = x3[-1]
        return out

    terms = _ssd_decay_terms(CHUNK, dt_ref[...] + dtb_ref[...], alog_ref[...], ltri_ref[...])
    for g in range(SSM_GROUPS):
        def cols(ref, width, g=g):
            return ref.at[..., g * width:(g + 1) * width]

        gi, n = GROUP_INNER, SSM_DSTATE
        xs = conv(cols(xs_ref, gi), cols(px_ref, gi), cols(cwx_ref, gi), cols(cbx_ref, gi))
        bm = conv(cols(b_ref, n), cols(pb_ref, n), cols(cwb_ref, n), cols(cbb_ref, n))
        cm = conv(cols(c_ref, n), cols(pc_ref, n), cols(cwc_ref, n), cols(cbc_ref, n))
        hs_g = hs_ref.at[0, g * gi:(g + 1) * gi, :]

        def state_io(m, ha, c_b, xw, b_b, e_cum, hs_g=hs_g):
            rows = pl.ds(m * 2 * SSM_HEADDIM, 2 * SSM_HEADDIM)
            h = hs_g[rows, :]
            hs_g[rows, :] = h * _pair_decay(e_cum, ROWS - 1, ha) + _dot_tn(xw, b_b)
            return _dot_nt(c_b, h.astype(BF16))

        _ssd_group(CHUNK, terms, g * HEADS_PER_GROUP, cols(z_ref, gi), cols(dsk_ref, gi), cols(nrm_ref, gi),
                   xs, bm, cm, cols(y_ref, gi), state_io)


def _ssd_sample_kernel(seq, z_ref, xs_ref, b_ref, c_ref, dt_ref, cwx_ref, cbx_ref, cwb_ref, cbb_ref, cwc_ref,
                       cbc_ref, dtb_ref, alog_ref, dsk_ref, nrm_ref, ltri_ref, bmask_ref, px_ref, pb_ref, pc_ref,
                       h0_ref, y_ref, hs_ref, csx_ref, csb_ref, csc_ref):
    nb = SAMPLE_PER_TILE

    def conv(x_ref, p_ref, w_ref, bias_ref, cs_ref):
        x = x_ref[...]
        cs_ref[...] = x.reshape(nb, seq, x.shape[1])
        return _conv_piece(x, p_ref[...], w_ref, bias_ref)

    xs = conv(xs_ref, px_ref, cwx_ref, cbx_ref, csx_ref)
    bm = conv(b_ref, pb_ref, cwb_ref, cbb_ref, csb_ref)
    cm = conv(c_ref, pc_ref, cwc_ref, cbc_ref, csc_ref)
    bmask = bmask_ref[...]

    def state_io(m, ha, c_b, xw, b_b, e_cum):
        rows = pl.ds(m * 2 * SSM_HEADDIM, 2 * SSM_HEADDIM)
        c_bd = jnp.concatenate([c_b] * nb, axis=1) * bmask
        b_bd = jnp.concatenate([b_b] * nb, axis=1) * bmask
        hs = [h0_ref[i, rows, :] for i in range(nb)]
        h_cat = jnp.concatenate(hs, axis=1).astype(BF16)
        dh = _dot_tn(xw, b_bd)
        for i in range(nb):
            decay = _pair_decay(e_cum, i * seq + seq - 1, ha)
            hs_ref[i, rows, :] = hs[i] * decay + dh[:, i * SSM_DSTATE:(i + 1) * SSM_DSTATE]
        return _dot_nt(c_bd, h_cat)

    shift = jnp.bitwise_and(LANES - HEADS_PER_GROUP * pl.program_id(1), LANES - 1)
    dt_pre = pltpu.roll(dt_ref[...] + dtb_ref[...], shift, 1)
    a_log = pltpu.roll(jnp.broadcast_to(alog_ref[...], (SUBLANES, LANES)), shift, 1)[:1]
    terms = _ssd_decay_terms(seq, dt_pre, a_log, ltri_ref[...])
    _ssd_group(seq, terms, 0, z_ref, dsk_ref, nrm_ref, xs, bm, cm, y_ref, state_io)


def _ssd_tables(c):
    r = jnp.arange(ROWS)
    ltri = ((r[:, None] >= r[None, :]) & ((r[:, None] // c) == (r[None, :] // c))).astype(BF16)
    return ltri


def _ssd_specs(row_block):
    def at(width, off):
        return lambda *ids: (row_block(*ids)[0], off // width + row_block(*ids)[1])

    def grp(rows, width):
        return pl.BlockSpec((rows, width), lambda *ids: (0, row_block(*ids)[1]))

    n = SSM_DSTATE
    return [
        pl.BlockSpec((ROWS, GROUP_INNER), at(GROUP_INNER, _Cols.Z)),
        pl.BlockSpec((ROWS, GROUP_INNER), at(GROUP_INNER, _Cols.XS)),
        pl.BlockSpec((ROWS, n), at(n, _Cols.B)),
        pl.BlockSpec((ROWS, n), at(n, _Cols.C)),
        pl.BlockSpec((ROWS, LANES), lambda *ids: (row_block(*ids)[0], _Cols.DT // LANES)),
        grp(CONV_WIDTH, GROUP_INNER), grp(1, GROUP_INNER),
        grp(CONV_WIDTH, n), grp(1, n), grp(CONV_WIDTH, n), grp(1, n),
        pl.BlockSpec((1, LANES), lambda *ids: (0, 0)), pl.BlockSpec((1, LANES), lambda *ids: (0, 0)),
        grp(1, GROUP_INNER), grp(1, GROUP_INNER),
    ]


def _ssd_params(conv_w, conv_b, dt_bias, a_log, d_skip, ssm_norm):
    gn = SSM_GROUPS * SSM_DSTATE
    cwx, cwb, cwc = conv_w[:, :SSM_INNER], conv_w[:, SSM_INNER:SSM_INNER + gn], conv_w[:, SSM_INNER + gn:]
    cb = conv_b.reshape(1, CONV_DIM)
    cbx, cbb, cbc = cb[:, :SSM_INNER], cb[:, SSM_INNER:SSM_INNER + gn], cb[:, SSM_INNER + gn:]

    def head_lanes(v):
        return jnp.pad(v, (0, LANES - SSM_HEADS)).reshape(1, LANES)

    dsk = jnp.repeat(d_skip, SSM_HEADDIM).reshape(1, SSM_INNER)
    return (cwx, cbx, cwb, cbb, cwc, cbc, head_lanes(dt_bias), head_lanes(a_log), dsk,
            ssm_norm.reshape(1, SSM_INNER))


def _conv_tail(csx, csb, csc):
    keep = SUBLANES - (CONV_WIDTH - 1)
    return jnp.concatenate([csx[:, keep:], csb[:, keep:], csc[:, keep:]], axis=-1)


def _ssd_prompt(proj, bsz, seq, params):
    nch = seq // ROWS
    gn = SSM_GROUPS * SSM_DSTATE
    row = lambda width, off: pl.BlockSpec((ROWS, width), lambda b, c: (b * nch + c, off // width))
    full = lambda a: pl.BlockSpec(a.shape, lambda b, c: (0,) * a.ndim)
    ltri = _ssd_tables(CHUNK)
    specs = [row(SSM_INNER, _Cols.Z), row(SSM_INNER, _Cols.XS), row(gn, _Cols.B), row(gn, _Cols.C),
             row(LANES, _Cols.DT), *[full(p) for p in params], full(ltri)]
    tail = lambda width: pl.BlockSpec((1, SUBLANES, width), lambda b, c: (b, 0, 0))
    y, hs, csx, csb, csc = pl.pallas_call(
        _ssd_prompt_kernel,
        out_shape=(jax.ShapeDtypeStruct((bsz * seq, SSM_INNER), BF16),
                   jax.ShapeDtypeStruct((bsz, SSM_HEADS * SSM_HEADDIM, SSM_DSTATE), F32),
                   jax.ShapeDtypeStruct((bsz, SUBLANES, SSM_INNER), F32),
                   jax.ShapeDtypeStruct((bsz, SUBLANES, gn), F32),
                   jax.ShapeDtypeStruct((bsz, SUBLANES, gn), F32)),
        grid=(bsz, nch),
        in_specs=specs,
        out_specs=(pl.BlockSpec((ROWS, SSM_INNER), lambda b, c: (b * nch + c, 0)),
                   pl.BlockSpec((1, SSM_HEADS * SSM_HEADDIM, SSM_DSTATE), lambda b, c: (b, 0, 0)),
                   tail(SSM_INNER), tail(gn), tail(gn)),
        compiler_params=_params(("parallel", "arbitrary")),
        name="ssd_prompt",
    )(proj, proj, proj, proj, proj, *params, ltri)
    return y, hs, _conv_tail(csx, csb, csc)


def _ssd_sample(proj, row0, bsz, seq, params, state, conv_state):
    ntile = bsz // SAMPLE_PER_TILE
    rb0 = row0 // ROWS
    gn = SSM_GROUPS * SSM_DSTATE
    specs = _ssd_specs(lambda t, g: (rb0 + t, g))
    gh = HEADS_PER_GROUP * SSM_HEADDIM
    cs = jnp.pad(conv_state, ((0, 0), (SUBLANES - (CONV_WIDTH - 1), 0), (0, 0)))
    csx, csb, csc = cs[:, :, :SSM_INNER], cs[:, :, SSM_INNER:SSM_INNER + gn], cs[:, :, SSM_INNER + gn:]
    pred = lambda width: pl.BlockSpec((SAMPLE_PER_TILE, SUBLANES, width), lambda t, g: (t, 0, g))
    state_spec = pl.BlockSpec((SAMPLE_PER_TILE, gh, SSM_DSTATE), lambda t, g: (t, g, 0))
    specs += [
        pl.BlockSpec((ROWS, ROWS), lambda t, g: (0, 0)),
        pl.BlockSpec((ROWS, SAMPLE_PER_TILE * LANES), lambda t, g: (0, 0)),
        pred(GROUP_INNER), pred(SSM_DSTATE), pred(SSM_DSTATE),
        state_spec,
    ]
    state2 = state.reshape(bsz, SSM_HEADS * SSM_HEADDIM, SSM_DSTATE)
    assert seq == SUBLANES
    y, hs, csx, csb, csc = pl.pallas_call(
        functools.partial(_ssd_sample_kernel, seq),
        out_shape=(jax.ShapeDtypeStruct((bsz * seq, SSM_INNER), BF16), jax.ShapeDtypeStruct(state2.shape, F32),
                   jax.ShapeDtypeStruct((bsz, seq, SSM_INNER), F32),
                   jax.ShapeDtypeStruct((bsz, seq, gn), F32),
                   jax.ShapeDtypeStruct((bsz, seq, gn), F32)),
        grid=(ntile, SSM_GROUPS),
        in_specs=specs,
        out_specs=(pl.BlockSpec((ROWS, GROUP_INNER), lambda t, g: (t, g)), state_spec,
                   pred(GROUP_INNER), pred(SSM_DSTATE), pred(SSM_DSTATE)),
        compiler_params=_params(("parallel", "parallel")),
        name="ssd_sample",
    )(proj, proj, proj, proj, proj, *params, _ssd_tables(seq), _block_mask(BF16), csx, csb, csc, state2)
    return y, hs, _conv_tail(csx, csb, csc)


def _post_kernel(n_prompt_tiles, ap_ref, as_ref, yp_ref, ys_ref, ga_ref, gb_ref, xp_ref, xs_ref, wr_ref, ws_ref,
                 wo_ref, nw_ref, wrt_ref, brt_ref, x1_ref, h2_ref, lg_ref):
    is_prompt = pl.program_id(0) < n_prompt_tiles
    branch_a = _dot(jnp.where(is_prompt, ap_ref[...], as_ref[...]), wr_ref[...])
    branch_b = _dot(jnp.where(is_prompt, yp_ref[...], ys_ref[...]), ws_ref[...])
    merged = jax.nn.sigmoid(ga_ref[...]) * branch_a + jax.nn.sigmoid(gb_ref[...]) * branch_b
    x1 = jnp.where(is_prompt, xp_ref[...], xs_ref[...]) + _dot(merged.astype(BF16), wo_ref[...])
    x1_ref[...] = x1
    ms = jnp.mean(x1 * x1, axis=-1, keepdims=True)
    h2 = x1 * lax.rsqrt(ms + EPS) * nw_ref[...]
    _store_row_tiles(h2_ref, h2)
    lg_ref[...] = _dot(h2.astype(BF16), wrt_ref[...]) + brt_ref[...]


def _post(a_p, a_s, y_p, y_s, proj, xp, xs, w_ret, w_ssm, w_out, norm_ffn, w_router, b_router):
    t = xp.shape[0] + xs.shape[0]
    tm = _pick(math.gcd(xp.shape[0], xs.shape[0]), 512)
    npt = xp.shape[0] // tm
    row = lambda width, blk: pl.BlockSpec((tm, width), lambda i: (i, blk))
    full = lambda *shape: pl.BlockSpec(shape, lambda i: (0,) * len(shape))
    return pl.pallas_call(
        functools.partial(_post_kernel, npt),
        out_shape=(jax.ShapeDtypeStruct((t, D_MODEL), F32), jax.ShapeDtypeStruct((t * ROW_TILES, LANES), F32),
                   jax.ShapeDtypeStruct((t, LANES), F32)),
        grid=(t // tm,),
        in_specs=[*_split_rows(tm, npt, RET_V), *_split_rows(tm, npt, SSM_INNER),
                  row(D_MODEL, _Cols.GA // D_MODEL), row(D_MODEL, _Cols.GB // D_MODEL),
                  *_split_rows(tm, npt, D_MODEL),
                  full(RET_V, D_MODEL), full(SSM_INNER, D_MODEL), full(D_MODEL, D_MODEL), full(1, D_MODEL),
                  full(D_MODEL, LANES), full(1, LANES)],
        out_specs=(row(D_MODEL, 0), pl.BlockSpec((tm * ROW_TILES, LANES), lambda i: (i, 0)), row(LANES, 0)),
        compiler_params=_params(("parallel",)),
        name="post_mixer",
    )(a_p, a_s, y_p, y_s, proj, proj, xp, xs, w_ret, w_ssm, w_out, norm_ffn, w_router, b_router)


def _route_kernel(lg_ref, lstrict_ref, meta_ref, gate_ref, cnt_ref):
    @pl.when(pl.program_id(0) == 0)
    def _():
        cnt_ref[...] = jnp.zeros_like(cnt_ref)

    tm = lg_ref.shape[0]
    lane = lax.broadcasted_iota(jnp.int32, (tm, LANES), 1)
    lane_f = lane.astype(F32)
    cur = jnp.where(lane < N_EXPERTS, lg_ref[...], -jnp.inf)
    vals, hots = [], []
    for _ in range(TOP_K):
        m = jnp.max(cur, axis=1, keepdims=True)
        idx = jnp.min(jnp.where(cur == m, lane_f, float(LANES)), axis=1, keepdims=True)
        hot = lane_f == idx
        vals.append(m)
        hots.append(hot)
        cur = jnp.where(hot, -jnp.inf, cur)
    exps = [jnp.exp(v - vals[0]) for v in vals]
    denom = exps[0] + exps[1] + exps[2] + exps[3]
    sel = hots[0] | hots[1] | hots[2] | hots[3]
    self32 = sel.astype(F32)
    before = _dot(lstrict_ref[...], self32.astype(BF16)) + cnt_ref[...]
    cnt_ref[...] = cnt_ref[...] + jnp.sum(self32, axis=0, keepdims=True)
    meta = jnp.zeros((tm, LANES), F32)
    gates = jnp.zeros((tm, LANES), F32)
    for k in range(TOP_K):
        e_k = jnp.sum(jnp.where(hots[k], lane_f, 0.0), axis=1, keepdims=True)
        p_k = jnp.sum(jnp.where(hots[k], before, 0.0), axis=1, keepdims=True)
        meta = jnp.where(lane == k, e_k, meta)
        meta = jnp.where(lane == SUBLANES + k, p_k, meta)
        gates = jnp.where(lane == k, exps[k] / denom, gates)
    gate_ref[...] = gates
    meta_ref[...] = meta.T[:2 * SUBLANES, :].astype(jnp.int32)


def _route(logits):
    t = logits.shape[0]
    tm = _pick(t, 512)
    r = jnp.arange(tm)
    lstrict = (r[:, None] > r[None, :]).astype(BF16)
    return pl.pallas_call(
        _route_kernel,
        out_shape=(jax.ShapeDtypeStruct((2 * SUBLANES, t), jnp.int32),
                   jax.ShapeDtypeStruct((t, LANES), F32),
                   jax.ShapeDtypeStruct((1, LANES), F32)),
        grid=(t // tm,),
        in_specs=[pl.BlockSpec((tm, LANES), lambda i: (i, 0)), pl.BlockSpec((tm, tm), lambda i: (0, 0))],
        out_specs=(pl.BlockSpec((2 * SUBLANES, tm), lambda i: (0, i)),
                   pl.BlockSpec((tm, LANES), lambda i: (i, 0)),
                   pl.BlockSpec((1, LANES), lambda i: (0, 0))),
        compiler_params=_params(("arbitrary",)),
        name="route",
    )(logits, lstrict)


def _row_copy(src_ref, src_row, dst_ref, dst_row, sem):
    src = pl.ds(pl.multiple_of(src_row * ROW_TILES, ROW_TILES), ROW_TILES)
    dst = pl.ds(pl.multiple_of(dst_row * ROW_TILES, ROW_TILES), ROW_TILES)
    return pltpu.make_async_copy(src_ref.at[src, :], dst_ref.at[dst, :], sem)


def _dispatch_kernel(pstart_ref, pend_ref, dest_ref, h2_ref, xs_ref, zero_ref, sem):
    tm = h2_ref.shape[0] // ROW_TILES

    @pl.when(pl.program_id(0) == 0)
    def _():
        zero_ref[...] = jnp.zeros_like(zero_ref)

        def fill(e):
            n = EXPERT_BLOCK * ROW_TILES
            start = pl.multiple_of((pend_ref[e] - EXPERT_BLOCK) * ROW_TILES, n)
            return pltpu.make_async_copy(zero_ref, xs_ref.at[pl.ds(start, n), :], sem)

        def fill_start(e, carry):
            @pl.when(pend_ref[e] > pstart_ref[e])
            def _():
                fill(e).start()
            return carry

        def fill_wait(e, carry):
            @pl.when(pend_ref[e] > pstart_ref[e])
            def _():
                fill(e).wait()
            return carry

        lax.fori_loop(0, N_EXPERTS, fill_start, 0)
        lax.fori_loop(0, N_EXPERTS, fill_wait, 0)

        def tail(j):
            n = EXPERT_BLOCK * ROW_TILES
            return pltpu.make_async_copy(zero_ref, xs_ref.at[pl.ds(pl.multiple_of(j * n, n), n), :], sem)

        used = pend_ref[N_EXPERTS - 1] // EXPERT_BLOCK
        total = xs_ref.shape[0] // (EXPERT_BLOCK * ROW_TILES)
        lax.fori_loop(used, total, lambda j, c: (tail(j).start(), c)[1], 0)
        lax.fori_loop(used, total, lambda j, c: (tail(j).wait(), c)[1], 0)

    def copy(t, k):
        return _row_copy(h2_ref, t, xs_ref, dest_ref[k, t], sem)

    def issue(t, carry):
        for k in range(TOP_K):
            copy(t, k).start(priority=k % 2)
        return carry

    def drain(t, carry):
        for k in range(TOP_K):
            copy(t, k).wait()
        return carry

    lax.fori_loop(0, tm, issue, 0, unroll=ISSUE_UNROLL)
    lax.fori_loop(0, tm, drain, 0, unroll=ISSUE_UNROLL)


def _dispatch(pstart, pends, dest, h2, n_rows):
    t = h2.shape[0] // ROW_TILES
    tm = _pick(t, 256)
    return pl.pallas_call(
        _dispatch_kernel,
        out_shape=jax.ShapeDtypeStruct((n_rows * ROW_TILES, LANES), F32),
        grid_spec=pltpu.PrefetchScalarGridSpec(
            num_scalar_prefetch=2,
            grid=(t // tm,),
            in_specs=[pl.BlockSpec((SUBLANES, tm), lambda i, ps, pe: (0, i), memory_space=pltpu.SMEM),
                      pl.BlockSpec((tm * ROW_TILES, LANES), lambda i, ps, pe: (i, 0))],
            out_specs=pl.BlockSpec(memory_space=pl.ANY),
            scratch_shapes=[pltpu.VMEM((EXPERT_BLOCK * ROW_TILES, LANES), F32), pltpu.SemaphoreType.DMA],
        ),
        compiler_params=_params(("arbitrary",)),
        name="dispatch",
    )(pstart, pends, dest, h2)


def _combine_kernel(n_prompt_tiles, dest_ref, gate_ref, x1_ref, nw_ref, yb_ref, op_ref, os_ref, buf_ref, sem):
    tm = x1_ref.shape[0]

    def copy(t, k):
        return _row_copy(yb_ref, dest_ref[k, t], buf_ref.at[k], t, sem)

    def issue(t, carry):
        for k in range(TOP_K):
            copy(t, k).start(priority=k % 2)
        return carry

    def drain(t, carry):
        for k in range(TOP_K):
            copy(t, k).wait()
        return carry

    lax.fori_loop(0, tm, issue, 0, unroll=ISSUE_UNROLL)
    lax.fori_loop(0, tm, drain, 0, unroll=ISSUE_UNROLL)
    gates = gate_ref[...]
    moe = _load_row_tiles(buf_ref, tm, 0) * gates[:, 0:1]
    for k in range(1, TOP_K):
        moe = moe + _load_row_tiles(buf_ref, tm, k) * gates[:, k:k + 1]
    x2 = x1_ref[...] + moe
    ms = jnp.mean(x2 * x2, axis=-1, keepdims=True)
    out = x2 * lax.rsqrt(ms + EPS) * nw_ref[...]

    @pl.when(pl.program_id(0) < n_prompt_tiles)
    def _():
        op_ref[...] = out

    @pl.when(pl.program_id(0) >= n_prompt_tiles)
    def _():
        os_ref[...] = out


def _combine(dest, gates, x1, norm_final, yb, n_prompt):
    t = x1.shape[0]
    tm = _pick(math.gcd(n_prompt, t - n_prompt), 256)
    npt = n_prompt // tm
    return pl.pallas_call(
        functools.partial(_combine_kernel, npt),
        out_shape=(jax.ShapeDtypeStruct((n_prompt, D_MODEL), F32),
                   jax.ShapeDtypeStruct((t - n_prompt, D_MODEL), F32)),
        grid=(t // tm,),
        in_specs=[pl.BlockSpec((SUBLANES, tm), lambda i: (0, i), memory_space=pltpu.SMEM),
                  pl.BlockSpec((tm, LANES), lambda i: (i, 0)),
                  pl.BlockSpec((tm, D_MODEL), lambda i: (i, 0)),
                  pl.BlockSpec((1, D_MODEL), lambda i: (0, 0)),
                  pl.BlockSpec(memory_space=pl.ANY)],
        out_specs=_split_rows(tm, npt, D_MODEL),
        scratch_shapes=[pltpu.VMEM((TOP_K, tm * ROW_TILES, LANES), F32), pltpu.SemaphoreType.DMA],
        compiler_params=_params(("arbitrary",)),
        name="combine",
    )(dest, gates, x1, norm_final, yb)


def _expert_kernel(be_ref, nv_ref, xs_ref, wgu_ref, bgu_ref, wd_ref, bd_ref, yb_ref, wgu_b, wd_b):
    i = pl.program_id(0)

    @pl.when(jnp.logical_or(i == 0, be_ref[i] != be_ref[jnp.maximum(i - 1, 0)]))
    def _():
        wgu_b[...] = wgu_ref[0].astype(BF16)
        wd_b[...] = wd_ref[0].astype(BF16)

    @pl.when(i < nv_ref[0])
    def _():
        gu = _dot(_load_row_tiles(xs_ref, EXPERT_BLOCK).astype(BF16), wgu_b[...]) + bgu_ref[0]
        glu = jnp.minimum(gu[:, :D_FF], SWIGLU_LIMIT)
        lin = jnp.clip(gu[:, D_FF:], -SWIGLU_LIMIT, SWIGLU_LIMIT)
        act = glu * jax.nn.sigmoid(SWIGLU_ALPHA * glu) * (lin + 1.0)
        _store_row_tiles(yb_ref, _dot(act.astype(BF16), wd_b[...]) + bd_ref[0])

    @pl.when(i >= nv_ref[0])
    def _():
        yb_ref[...] = jnp.zeros_like(yb_ref)


def _experts(block_expert, n_valid, xs, w_gu, b_gu, w_d, b_d):
    rows = EXPERT_BLOCK * ROW_TILES
    nb = xs.shape[0] // rows
    return pl.pallas_call(
        _expert_kernel,
        out_shape=jax.ShapeDtypeStruct(xs.shape, F32),
        grid_spec=pltpu.PrefetchScalarGridSpec(
            num_scalar_prefetch=2,
            grid=(nb,),
            in_specs=[pl.BlockSpec((rows, LANES), lambda i, be, nv: (jnp.minimum(i, nv[0] - 1), 0)),
                      pl.BlockSpec((1, D_MODEL, 2 * D_FF), lambda i, be, nv: (be[i], 0, 0)),
                      pl.BlockSpec((1, 1, 2 * D_FF), lambda i, be, nv: (be[i], 0, 0)),
                      pl.BlockSpec((1, D_FF, D_MODEL), lambda i, be, nv: (be[i], 0, 0)),
                      pl.BlockSpec((1, 1, D_MODEL), lambda i, be, nv: (be[i], 0, 0))],
            out_specs=pl.BlockSpec((rows, LANES), lambda i, be, nv: (i, 0)),
            scratch_shapes=[pltpu.VMEM((D_MODEL, 2 * D_FF), BF16), pltpu.VMEM((D_FF, D_MODEL), BF16)],
        ),
        compiler_params=_params(("arbitrary",)),
        name="experts",
    )(block_expert, n_valid, xs, w_gu, b_gu, w_d, b_d)


def _expert_layout(counts, n_blocks):
    counts = counts.astype(jnp.int32)
    padded = (counts + EXPERT_BLOCK - 1) // EXPERT_BLOCK * EXPERT_BLOCK
    pends = jnp.cumsum(padded)
    pstart = pends - padded
    block_start = jnp.arange(n_blocks, dtype=jnp.int32) * EXPERT_BLOCK
    block_expert = jnp.minimum(jnp.sum(pends[None, :] <= block_start[:, None], axis=1), N_EXPERTS - 1)
    n_valid = (pends[-1:] // EXPERT_BLOCK).astype(jnp.int32)
    return pstart.astype(jnp.int32), pends.astype(jnp.int32), block_expert.astype(jnp.int32), n_valid


def _permute_w_in(w_in):
    sizes = [RET_QK, RET_QK, RET_V, RET_V, SSM_INNER, CONV_DIM, SSM_HEADS, D_MODEL, D_MODEL]
    offs = [0]
    for s in sizes:
        offs.append(offs[-1] + s)
    q, k, v, g_ret, z, xbc, dt, g_a, g_b = [w_in[:, offs[i]:offs[i + 1]] for i in range(len(sizes))]
    gn = SSM_GROUPS * SSM_DSTATE
    xs, bm, cm = xbc[:, :SSM_INNER], xbc[:, SSM_INNER:SSM_INNER + gn], xbc[:, SSM_INNER + gn:]
    dt_pad = jnp.pad(dt, ((0, 0), (0, _Cols.TOTAL - _Cols.DT - SSM_HEADS)))
    return jnp.concatenate([z, xs, v, g_ret, g_a, g_b, q, k, bm, cm, dt_pad], axis=1).astype(BF16)


def _forward(x_prompt, x_sample, state_ret, state_ssm, state_conv, norm_mix, w_in, ret_norm, w_out_ret,
             conv_w, conv_b, dt_bias, a_log, d_skip, ssm_norm, w_out_ssm, w_out, norm_ffn,
             w_router, b_router, w_gate_up, b_gate_up, w_down, b_down, norm_final):
    bp, lp, _ = x_prompt.shape
    bs, ls, _ = x_sample.shape
    assert lp % ROWS == 0 and bs % SAMPLE_PER_TILE == 0 and ls * SAMPLE_PER_TILE == ROWS
    tp, ts = bp * lp, bs * ls
    t_all = tp + ts
    xp, xs_in = x_prompt.reshape(tp, D_MODEL), x_sample.reshape(ts, D_MODEL)

    proj = _inproj(xp, xs_in, norm_mix[0].reshape(1, D_MODEL), _permute_w_in(w_in[0]))

    a_p, ret_p = _retention_prompt(proj, bp, lp, ret_norm[0])
    a_s, ret_s = _retention_sample(proj, tp, bs, ls, state_ret[0], ret_norm[0])

    sp = _ssd_params(conv_w[0], conv_b[0], dt_bias[0], a_log[0], d_skip[0], ssm_norm[0])
    y_p, ssm_p, conv_p = _ssd_prompt(proj, bp, lp, sp)
    y_s, ssm_s, conv_s = _ssd_sample(proj, tp, bs, ls, sp, state_ssm[0], state_conv[0])

    w_router_pad = jnp.pad(w_router[0], ((0, 0), (0, LANES - N_EXPERTS))).astype(BF16)
    b_router_pad = jnp.pad(b_router[0], (0, LANES - N_EXPERTS)).reshape(1, LANES)
    x1, h2, logits = _post(a_p, a_s, y_p, y_s, proj, xp, xs_in, w_out_ret[0].astype(BF16), w_out_ssm[0].astype(BF16),
                           w_out[0].astype(BF16), norm_ffn[0].reshape(1, D_MODEL), w_router_pad, b_router_pad)

    meta, gates, counts = _route(logits)
    n_blocks = -(-(t_all * TOP_K + N_EXPERTS * (EXPERT_BLOCK - 1)) // EXPERT_BLOCK)
    pstart, pends, block_expert, n_valid = _expert_layout(counts[0, :N_EXPERTS], n_blocks)
    hot = meta[:TOP_K, :, None] == jnp.arange(N_EXPERTS, dtype=jnp.int32)
    seg_start = jnp.sum(jnp.where(hot, pstart, 0), axis=-1)
    dest = jnp.pad(seg_start + meta[SUBLANES:SUBLANES + TOP_K], ((0, SUBLANES - TOP_K), (0, 0)))
    xs = _dispatch(pstart, pends, dest, h2, n_blocks * EXPERT_BLOCK)
    yb = _experts(block_expert, n_valid, xs, w_gate_up[0], b_gate_up[0].reshape(N_EXPERTS, 1, 2 * D_FF),
                  w_down[0], b_down[0].reshape(N_EXPERTS, 1, D_MODEL))
    out_p, out_s = _combine(dest, gates, x1, norm_final.reshape(1, D_MODEL), yb, tp)

    shape_s = (1, bs, SSM_HEADS, SSM_HEADDIM, SSM_DSTATE)
    shape_p = (1, bp, SSM_HEADS, SSM_HEADDIM, SSM_DSTATE)
    return (out_p.reshape(bp, lp, D_MODEL), out_s.reshape(bs, ls, D_MODEL),
            ret_p[None], ret_s[None], ssm_p.reshape(shape_p), ssm_s.reshape(shape_s),
            conv_p[None], conv_s[None])


def kernel(x_prompt, x_sample, state_ret, state_ssm, state_conv, norm_mix, w_in, ret_norm, w_out_ret, conv_w, conv_b, dt_bias, a_log, d_skip, ssm_norm, w_out_ssm, w_out, norm_ffn, w_router, b_router, w_gate_up, b_gate_up, w_down, b_down, norm_final):
    return _forward(x_prompt, x_sample, state_ret, state_ssm, state_conv, norm_mix, w_in, ret_norm, w_out_ret,
                    conv_w, conv_b, dt_bias, a_log, d_skip, ssm_norm, w_out_ssm, w_out, norm_ffn,
                    w_router, b_router, w_gate_up, b_gate_up, w_down, b_down, norm_final)
```

```python
import functools
import math

import jax
import jax.numpy as jnp
from jax import lax
from jax.experimental import pallas as pl
from jax.experimental.pallas import tpu as pltpu

F32 = jnp.float32
BF16 = jnp.bfloat16

D_MODEL = 1024
PAST_LEN = 16384
RET_HEADS = 4
RET_DK = 128
RET_DV = 256
RET_QK = RET_HEADS * RET_DK
RET_V = RET_HEADS * RET_DV
ROPE_BASE = 10000.0
SSM_INNER = 2 * D_MODEL
SSM_HEADDIM = 64
SSM_HEADS = SSM_INNER // SSM_HEADDIM
SSM_GROUPS = 4
SSM_DSTATE = 128
HEADS_PER_GROUP = SSM_HEADS // SSM_GROUPS
GROUP_INNER = SSM_INNER // SSM_GROUPS
CONV_WIDTH = 4
CONV_DIM = SSM_INNER + 2 * SSM_GROUPS * SSM_DSTATE
CHUNK = 128
N_EXPERTS = 32
TOP_K = 4
D_FF = D_MODEL
SWIGLU_LIMIT = 7.0
SWIGLU_ALPHA = 1.702
EPS = 1e-6

LANES = 128
SUBLANES = 8
ROWS = 128
SAMPLE_PER_TILE = 16
EXPERT_BLOCK = 512
ROW_TILES = D_MODEL // LANES
ISSUE_UNROLL = 4
VMEM_LIMIT = 56 * 1024 * 1024


class _ColsA:
    V = 0
    GRET = V + RET_V
    GA = GRET + RET_V
    GB = GA + D_MODEL
    Q = GB + D_MODEL
    K = Q + RET_QK
    TOTAL = K + RET_QK


class _ColsB:
    Z = 0
    XS = Z + SSM_INNER
    B = XS + SSM_INNER
    C = B + SSM_GROUPS * SSM_DSTATE
    DT = C + SSM_GROUPS * SSM_DSTATE
    TOTAL = DT + 2 * LANES


def _pick(n, target):
    best = None
    for c in range(LANES, target + 1, LANES):
        if n % c == 0:
            best = c
    assert best is not None, (n, target)
    return best


def _params(sem, **kw):
    return pltpu.CompilerParams(dimension_semantics=sem, vmem_limit_bytes=VMEM_LIMIT, **kw)


def _dot(a, b):
    return jnp.dot(a, b, preferred_element_type=F32)


def _dot_nt(a, b):
    return lax.dot_general(a, b, (((1,), (1,)), ((), ())), preferred_element_type=F32)


def _dot_tn(a, b):
    return lax.dot_general(a, b, (((0,), (0,)), ((), ())), preferred_element_type=F32)


def _silu(x):
    return x * jax.nn.sigmoid(x)


def _store_row_tiles(ref, value, *lead):
    n = value.shape[0]
    for s in range(ROW_TILES):
        ref[(*lead, pl.ds(s, n, stride=ROW_TILES), slice(None))] = value[:, s * LANES:(s + 1) * LANES]


def _load_row_tiles(ref, n, *lead):
    return jnp.concatenate([ref[(*lead, pl.ds(s, n, stride=ROW_TILES), slice(None))] for s in range(ROW_TILES)],
                           axis=1)


def _inproj_kernel(n_prompt_tiles, xp_ref, xs_ref, nw_ref, w_ref, o_ref, h_ref):
    @pl.when(pl.program_id(1) == 0)
    def _():
        x = jnp.where(pl.program_id(0) < n_prompt_tiles, xp_ref[...], xs_ref[...])
        ms = jnp.mean(x * x, axis=-1, keepdims=True)
        h_ref[...] = (x * lax.rsqrt(ms + EPS) * nw_ref[...]).astype(BF16)

    o_ref[...] = _dot(h_ref[...], w_ref[...])


def _split_rows(tm, npt, width):
    prompt = pl.BlockSpec((tm, width), lambda i, *_: (jnp.minimum(i, npt - 1), 0))
    sample = pl.BlockSpec((tm, width), lambda i, *_: (jnp.maximum(i - npt, 0), 0))
    return prompt, sample


def _inproj(xp, xs, norm_w, w):
    total = w.shape[1]
    t = xs.shape[0] + (0 if xp is None else xp.shape[0])
    tm = _pick(xs.shape[0] if xp is None else math.gcd(xp.shape[0], xs.shape[0]), 1024)
    npt = 0 if xp is None else xp.shape[0] // tm
    tn = _pick(total, 2560)
    assert tn % (2 * LANES) == 0
    x_specs = list(_split_rows(tm, npt, D_MODEL)) if npt else [pl.BlockSpec((tm, D_MODEL), lambda i, j: (i, 0))] * 2
    return pl.pallas_call(
        functools.partial(_inproj_kernel, npt),
        out_shape=jax.ShapeDtypeStruct((t, total), F32),
        grid=(t // tm, total // tn),
        in_specs=[
            *x_specs,
            pl.BlockSpec((1, D_MODEL), lambda i, j: (0, 0)),
            pl.BlockSpec((D_MODEL, tn), lambda i, j: (0, j)),
        ],
        out_specs=pl.BlockSpec((tm, tn), lambda i, j: (i, j)),
        scratch_shapes=[pltpu.VMEM((tm, D_MODEL), BF16)],
        compiler_params=_params(("parallel", "arbitrary")),
        name="in_proj",
    )(xs if xp is None else xp, xs, norm_w, w)


def _ret_log_decay():
    return jnp.log(1.0 - 2.0 ** (-5.0 - jnp.arange(RET_HEADS, dtype=F32)))


def _ret_tables(c):
    lg = _ret_log_decay()
    r = jnp.arange(ROWS)
    t = (r % c).astype(F32)
    seg = r // c
    diff = t[:, None] - t[None, :]
    ok = (seg[:, None] == seg[None, :]) & (diff >= 0)
    dm = jnp.where(ok[None], jnp.exp(lg[:, None, None] * jnp.maximum(diff, 0.0)[None]), 0.0)
    qd = jnp.exp(lg[:, None] * (t[None, :] + 1.0))
    kd = jnp.exp(lg[:, None] * (c - 1.0 - t[None, :]))
    cd = jnp.exp(lg * c)
    qd = jnp.broadcast_to(qd[:, :, None], (RET_HEADS, ROWS, LANES))
    kd = jnp.broadcast_to(kd[:, :, None], (RET_HEADS, ROWS, LANES))
    cd = jnp.broadcast_to(cd[:, None, None], (RET_HEADS, 1, RET_DV))
    return dm, qd, kd, cd


def _rope_tables(pos):
    half = RET_DK // 2
    inv_freq = 1.0 / (ROPE_BASE ** jnp.linspace(0.0, 1.0, half, dtype=F32))
    ang = pos.astype(F32)[:, None] * inv_freq[None, :]
    cos, sin = jnp.cos(ang), jnp.sin(ang)
    return jnp.concatenate([cos, cos], -1), jnp.concatenate([-sin, sin], -1)


def _block_mask(dtype):
    r = jnp.arange(ROWS) // (ROWS // SAMPLE_PER_TILE)
    b = jnp.arange(SAMPLE_PER_TILE * LANES) // LANES
    return (r[:, None] == b[None, :]).astype(dtype)


def _ret_common(q, k, v, g, cosf, sinf, dm, qd, kd, nw):
    qr = q * cosf + pltpu.roll(q, RET_DK // 2, 1) * sinf
    kr = (k * cosf + pltpu.roll(k, RET_DK // 2, 1) * sinf) * (RET_DK ** -0.5)
    vb = v.astype(BF16)
    scores = _dot_nt(qr.astype(BF16), kr.astype(BF16)) * dm
    o_intra = _dot(scores.astype(BF16), vb)
    q_state = (qr * qd).astype(BF16)
    k_state = kr * kd

    def finish(o):
        ms = jnp.mean(o * o, axis=-1, keepdims=True)
        on = o * lax.rsqrt(ms + EPS) * nw
        return (_silu(g) * on).astype(BF16)

    return o_intra, q_state, k_state, vb, finish


def _ret_prompt_kernel(q_ref, k_ref, v_ref, g_ref, cos_ref, sin_ref, dm_ref, qd_ref, kd_ref, cd_ref, nw_ref,
                       a_ref, s_ref):
    @pl.when(pl.program_id(1) == 0)
    def _():
        s_ref[...] = jnp.zeros_like(s_ref)

    cosf, sinf = cos_ref[...], sin_ref[...]
    for h in range(RET_HEADS):
        dk = slice(h * RET_DK, (h + 1) * RET_DK)
        dv = slice(h * RET_DV, (h + 1) * RET_DV)
        o_intra, q_state, k_state, vb, finish = _ret_common(
            q_ref[:, dk], k_ref[:, dk], v_ref[:, dv], g_ref[:, dv], cosf, sinf, dm_ref[h], qd_ref[h], kd_ref[h],
            nw_ref[h])
        s = s_ref[0, h]
        a_ref[:, dv] = finish(o_intra + _dot(q_state, s.astype(BF16)))
        s_ref[0, h] = s * cd_ref[h] + _dot_tn(k_state.astype(BF16), vb)


def _ret_sample_kernel(q_ref, k_ref, v_ref, g_ref, cos_ref, sin_ref, dm_ref, qd_ref, kd_ref, cd_ref, nw_ref,
                       bm_ref, bmt_ref, s0_ref, a_ref, s_ref):
    o_intra, q_state, k_state, vb, finish = _ret_common(
        q_ref[...], k_ref[...], v_ref[...], g_ref[...], cos_ref[...], sin_ref[...], dm_ref[0], qd_ref[0], kd_ref[0],
        nw_ref[0])
    nb = SAMPLE_PER_TILE
    s0 = s0_ref[:, 0]
    q_bd = jnp.concatenate([q_state] * nb, axis=1) * bm_ref[...]
    a_ref[...] = finish(o_intra + _dot(q_bd, s0.reshape(nb * RET_DK, RET_DV).astype(BF16)))
    k_t = k_state.T.astype(BF16)
    k_bd_t = jnp.concatenate([k_t] * nb, axis=0) * bmt_ref[...]
    ds = _dot(k_bd_t, vb).reshape(nb, RET_DK, RET_DV)
    s_ref[:, 0] = s0 * cd_ref[0] + ds


def _ret_specs(row_block):
    def at(width, off):
        return lambda *ids: (row_block(*ids)[0], off // width + row_block(*ids)[1])

    def head(*shape):
        return pl.BlockSpec((1,) + shape, lambda *ids: (row_block(*ids)[1],) + (0,) * len(shape))

    return [
        pl.BlockSpec((ROWS, RET_DK), at(RET_DK, _ColsA.Q)),
        pl.BlockSpec((ROWS, RET_DK), at(RET_DK, _ColsA.K)),
        pl.BlockSpec((ROWS, RET_DV), at(RET_DV, _ColsA.V)),
        pl.BlockSpec((ROWS, RET_DV), at(RET_DV, _ColsA.GRET)),
    ], head


def _retention_prompt(proj, bsz, seq, ret_norm):
    nch = seq // ROWS
    dm, qd, kd, cd = _ret_tables(CHUNK)
    cosf, sinf = _rope_tables(jnp.arange(seq, dtype=jnp.int32))
    row = lambda width, off: pl.BlockSpec((ROWS, width), lambda b, c: (b * nch + c, off // width))
    full = lambda *shape: pl.BlockSpec(shape, lambda b, c: (0,) * len(shape))
    specs = [
        row(RET_QK, _ColsA.Q), row(RET_QK, _ColsA.K), row(RET_V, _ColsA.V), row(RET_V, _ColsA.GRET),
        pl.BlockSpec((ROWS, RET_DK), lambda b, c: (c, 0)),
        pl.BlockSpec((ROWS, RET_DK), lambda b, c: (c, 0)),
        full(RET_HEADS, ROWS, ROWS), full(RET_HEADS, ROWS, LANES), full(RET_HEADS, ROWS, LANES),
        full(RET_HEADS, 1, RET_DV), full(RET_HEADS, 1, RET_DV),
    ]
    return pl.pallas_call(
        _ret_prompt_kernel,
        out_shape=(jax.ShapeDtypeStruct((bsz * seq, RET_V), BF16),
                   jax.ShapeDtypeStruct((bsz, RET_HEADS, RET_DK, RET_DV), F32)),
        grid=(bsz, nch),
        in_specs=specs,
        out_specs=(pl.BlockSpec((ROWS, RET_V), lambda b, c: (b * nch + c, 0)),
                   pl.BlockSpec((1, RET_HEADS, RET_DK, RET_DV), lambda b, c: (b, 0, 0, 0))),
        compiler_params=_params(("parallel", "arbitrary")),
        name="retention_prompt",
    )(proj, proj, proj, proj, cosf, sinf, dm, qd, kd, cd, ret_norm.reshape(RET_HEADS, 1, RET_DV))


def _retention_sample(proj, row0, bsz, seq, state, ret_norm):
    ntile = bsz // SAMPLE_PER_TILE
    rb0 = row0 // ROWS
    dm, qd, kd, cd = _ret_tables(seq)
    pos = PAST_LEN + jnp.arange(seq, dtype=jnp.int32)
    cosf, sinf = _rope_tables(jnp.tile(pos, SAMPLE_PER_TILE))
    rb = lambda t, h: (rb0 + t, h)
    specs, head = _ret_specs(rb)
    full = lambda *shape: pl.BlockSpec(shape, lambda t, h: (0,) * len(shape))
    specs += [
        full(ROWS, RET_DK), full(ROWS, RET_DK),
        head(ROWS, ROWS), head(ROWS, LANES), head(ROWS, LANES), head(1, RET_DV), head(1, RET_DV),
        full(ROWS, SAMPLE_PER_TILE * LANES), full(SAMPLE_PER_TILE * LANES, ROWS),
        pl.BlockSpec((SAMPLE_PER_TILE, 1, RET_DK, RET_DV), lambda t, h: (t, h, 0, 0)),
    ]
    bm = _block_mask(BF16)
    return pl.pallas_call(
        _ret_sample_kernel,
        out_shape=(jax.ShapeDtypeStruct((bsz * seq, RET_V), BF16),
                   jax.ShapeDtypeStruct(state.shape, F32)),
        grid=(ntile, RET_HEADS),
        in_specs=specs,
        out_specs=(pl.BlockSpec((ROWS, RET_DV), lambda t, h: (t, h)),
                   pl.BlockSpec((SAMPLE_PER_TILE, 1, RET_DK, RET_DV), lambda t, h: (t, h, 0, 0))),
        compiler_params=_params(("parallel", "parallel")),
        name="retention_sample",
    )(proj, proj, proj, proj, cosf, sinf, dm, qd, kd, cd, ret_norm.reshape(RET_HEADS, 1, RET_DV),
      bm, bm.T, state)


def _softplus(x):
    return jnp.maximum(x, 0.0) + jnp.log1p(jnp.exp(-jnp.abs(x)))


def _conv_piece(x, pred, w_ref, b_ref):
    width = x.shape[1]
    x3 = x.reshape(ROWS // SUBLANES, SUBLANES, width)
    t8 = lax.broadcasted_iota(jnp.int32, x3.shape, 1)
    acc = b_ref[...].reshape(1, 1, width)
    for i in range(CONV_WIDTH):
        s = CONV_WIDTH - 1 - i
        tap = x3 if s == 0 else pltpu.roll(jnp.where(t8 >= SUBLANES - s, pred, x3), s, 1)
        acc = acc + tap * w_ref[pl.ds(i, 1), :].reshape(1, 1, width)
    return _silu(acc).reshape(ROWS, width)


def _ssd_decay_terms(c, dt_pre, a_log, ltri):
    dt = _softplus(dt_pre)
    d_a = dt * (-jnp.exp(a_log))
    hi = d_a.astype(BF16)
    r1 = d_a - hi.astype(F32)
    mid = r1.astype(BF16)
    lo = (r1 - mid.astype(F32)).astype(BF16)
    cum = _dot(ltri, hi) + _dot(ltri, mid) + _dot(ltri, lo)
    cum3 = cum.reshape(ROWS // c, c, LANES)
    c_last = jnp.broadcast_to(cum3[:, c - 1:c, :], cum3.shape).reshape(ROWS, LANES)
    to_end = jnp.exp(c_last - cum) * dt
    e_cum = jnp.exp(cum)
    return cum, to_end, e_cum, cum.T, dt.T


def _ssd_group(c, terms, head0, z_ref, dsk_ref, nrm_ref, xs, bm, cm, y_ref, state_io):
    cum, to_end, e_cum, cum_t, dt_t = terms
    b_b, c_b = bm.astype(BF16), cm.astype(BF16)
    cb = _dot_nt(c_b, b_b)
    ri = lax.broadcasted_iota(jnp.int32, (ROWS, ROWS), 0)
    ci = lax.broadcasted_iota(jnp.int32, (ROWS, ROWS), 1)
    shift = c.bit_length() - 1
    causal = (ci <= ri) & (jnp.right_shift(ri, shift) == jnp.right_shift(ci, shift))
    lo_half = lax.broadcasted_iota(jnp.int32, (ROWS, LANES), 1) < SSM_HEADDIM

    def col(a, h):
        return jnp.broadcast_to(a[:, h:h + 1], (ROWS, LANES))

    def weights(h):
        seg = col(cum, h) - jnp.broadcast_to(cum_t[h:h + 1, :], (ROWS, ROWS))
        decay = jnp.exp(jnp.where(causal, seg, -jnp.inf))
        return (cb * decay * jnp.broadcast_to(dt_t[h:h + 1, :], (ROWS, ROWS))).astype(BF16)

    pieces = []
    for m in range(HEADS_PER_GROUP // 2):
        ha, hb = head0 + 2 * m, head0 + 2 * m + 1
        xp = xs[:, m * LANES:(m + 1) * LANES]
        w2 = jnp.concatenate([weights(ha), weights(hb)], axis=1)
        x2 = jnp.concatenate([jnp.where(lo_half, xp, 0.0), jnp.where(lo_half, 0.0, xp)], axis=0).astype(BF16)
        y = _dot(w2, x2)
        xw = (xp * jnp.where(lo_half, col(to_end, ha), col(to_end, hb))).astype(BF16)
        y_state = state_io(m, ha, c_b, xw, b_b, e_cum)
        y = y + y_state * jnp.where(lo_half, col(e_cum, ha), col(e_cum, hb))
        pieces.append(y + xp * dsk_ref[:, m * LANES:(m + 1) * LANES])
    yg = jnp.concatenate(pieces, axis=1) * _silu(z_ref[...])
    ms = jnp.mean(yg * yg, axis=-1, keepdims=True)
    y_ref[...] = (yg * lax.rsqrt(ms + EPS) * nrm_ref[...]).astype(BF16)


def _pair_decay(e_cum, row, ha):
    top = lax.broadcasted_iota(jnp.int32, (2 * SSM_HEADDIM, SSM_DSTATE), 0) < SSM_HEADDIM
    ea = jnp.broadcast_to(e_cum[row:row + 1, ha:ha + 1], top.shape)
    eb = jnp.broadcast_to(e_cum[row:row + 1, ha + 1:ha + 2], top.shape)
    return jnp.where(top, ea, eb)


def _ssd_prompt_kernel(x0_ref, xn_ref, nw_ref, w_ref, cwx_ref, cbx_ref, cwb_ref, cbb_ref, cwc_ref, cbc_ref,
                       dtb_ref, alog_ref, dsk_ref, nrm_ref, ltri_ref, y_ref, hs_ref, px_ref, pb_ref, pc_ref,
                       proj_ref):
    c = pl.program_id(1)

    def project(x_ref, slot):
        x = x_ref[...]
        ms = jnp.mean(x * x, axis=-1, keepdims=True)
        h = (x * lax.rsqrt(ms + EPS) * nw_ref[...]).astype(BF16)
        proj_ref[slot] = _dot(h, w_ref[...])

    @pl.when(c == 0)
    def _():
        hs_ref[...] = jnp.zeros_like(hs_ref)
        px_ref[...] = jnp.zeros_like(px_ref)
        pb_ref[...] = jnp.zeros_like(pb_ref)
        pc_ref[...] = jnp.zeros_like(pc_ref)
        project(x0_ref, 0)

    def conv(x_ref, p_ref, w_ref, bias_ref):
        x = x_ref[...]
        x3 = x.reshape(ROWS // SUBLANES, SUBLANES, x.shape[1])
        pred = jnp.concatenate([p_ref[...], x3[:-1]], axis=0)
        out = _conv_piece(x, pred, w_ref, bias_ref)
        p_ref[0] = x3[-1]
        return out

    def step(cur_slot, next_slot):
        project(xn_ref, next_slot)
        cur = proj_ref.at[cur_slot]
        gn = SSM_GROUPS * SSM_DSTATE
        z_ref = cur.at[:, _ColsB.Z:_ColsB.Z + SSM_INNER]
        xs_ref = cur.at[:, _ColsB.XS:_ColsB.XS + SSM_INNER]
        b_ref = cur.at[:, _ColsB.B:_ColsB.B + gn]
        c_ref = cur.at[:, _ColsB.C:_ColsB.C + gn]
        dt_ref = cur.at[:, _ColsB.DT:_ColsB.DT + LANES]
        terms = _ssd_decay_terms(CHUNK, dt_ref[...] + dtb_ref[...], alog_ref[...], ltri_ref[...])
        for g in range(SSM_GROUPS):
            def cols(ref, width, g=g):
                return ref.at[..., g * width:(g + 1) * width]

            gi, n = GROUP_INNER, SSM_DSTATE
            xs = conv(cols(xs_ref, gi), cols(px_ref, gi), cols(cwx_ref, gi), cols(cbx_ref, gi))
            bm = conv(cols(b_ref, n), cols(pb_ref, n), cols(cwb_ref, n), cols(cbb_ref, n))
            cm = conv(cols(c_ref, n), cols(pc_ref, n), cols(cwc_ref, n), cols(cbc_ref, n))
            hs_g = hs_ref.at[0, g * gi:(g + 1) * gi, :]

            def state_io(m, ha, c_b, xw, b_b, e_cum, hs_g=hs_g):
                rows = pl.ds(m * 2 * SSM_HEADDIM, 2 * SSM_HEADDIM)
                h = hs_g[rows, :]
                hs_g[rows, :] = h * _pair_decay(e_cum, ROWS - 1, ha) + _dot_tn(xw, b_b)
                return _dot_nt(c_b, h.astype(BF16))

            _ssd_group(CHUNK, terms, g * HEADS_PER_GROUP, cols(z_ref, gi), cols(dsk_ref, gi), cols(nrm_ref, gi),
                       xs, bm, cm, cols(y_ref, gi), state_io)

    @pl.when(c % 2 == 0)
    def _():
        step(0, 1)

    @pl.when(c % 2 == 1)
    def _():
        step(1, 0)


def _ssd_sample_kernel(seq, z_ref, xs_ref, b_ref, c_ref, dt_ref, cwx_ref, cbx_ref, cwb_ref, cbb_ref, cwc_ref,
                       cbc_ref, dtb_ref, alog_ref, dsk_ref, nrm_ref, ltri_ref, bmask_ref, px_ref, pb_ref, pc_ref,
                       h0_ref, y_ref, hs_ref, csx_ref, csb_ref, csc_ref):
    nb = SAMPLE_PER_TILE

    def conv(x_ref, p_ref, w_ref, bias_ref, cs_ref):
        x = x_ref[...]
        cs_ref[...] = x.reshape(nb, seq, x.shape[1])
        return _conv_piece(x, p_ref[...], w_ref, bias_ref)

    xs = conv(xs_ref, px_ref, cwx_ref, cbx_ref, csx_ref)
    bm = conv(b_ref, pb_ref, cwb_ref, cbb_ref, csb_ref)
    cm = conv(c_ref, pc_ref, cwc_ref, cbc_ref, csc_ref)
    bmask = bmask_ref[...]

    def state_io(m, ha, c_b, xw, b_b, e_cum):
        rows = pl.ds(m * 2 * SSM_HEADDIM, 2 * SSM_HEADDIM)
        c_bd = jnp.concatenate([c_b] * nb, axis=1) * bmask
        b_bd = jnp.concatenate([b_b] * nb, axis=1) * bmask
        hs = [h0_ref[i, rows, :] for i in range(nb)]
        h_cat = jnp.concatenate(hs, axis=1).astype(BF16)
        dh = _dot_tn(xw, b_bd)
        for i in range(nb):
            decay = _pair_decay(e_cum, i * seq + seq - 1, ha)
            hs_ref[i, rows, :] = hs[i] * decay + dh[:, i * SSM_DSTATE:(i + 1) * SSM_DSTATE]
        return _dot_nt(c_bd, h_cat)

    shift = jnp.bitwise_and(LANES - HEADS_PER_GROUP * pl.program_id(1), LANES - 1)
    dt_pre = pltpu.roll(dt_ref[...] + dtb_ref[...], shift, 1)
    a_log = pltpu.roll(jnp.broadcast_to(alog_ref[...], (SUBLANES, LANES)), shift, 1)[:1]
    terms = _ssd_decay_terms(seq, dt_pre, a_log, ltri_ref[...])
    _ssd_group(seq, terms, 0, z_ref, dsk_ref, nrm_ref, xs, bm, cm, y_ref, state_io)


def _ssd_tables(c):
    r = jnp.arange(ROWS)
    ltri = ((r[:, None] >= r[None, :]) & ((r[:, None] // c) == (r[None, :] // c))).astype(BF16)
    return ltri


def _ssd_specs(row_block):
    def at(width, off):
        return lambda *ids: (row_block(*ids)[0], off // width + row_block(*ids)[1])

    def grp(rows, width):
        return pl.BlockSpec((rows, width), lambda *ids: (0, row_block(*ids)[1]))

    n = SSM_DSTATE
    return [
        pl.BlockSpec((ROWS, GROUP_INNER), at(GROUP_INNER, _ColsB.Z)),
        pl.BlockSpec((ROWS, GROUP_INNER), at(GROUP_INNER, _ColsB.XS)),
        pl.BlockSpec((ROWS, n), at(n, _ColsB.B)),
        pl.BlockSpec((ROWS, n), at(n, _ColsB.C)),
        pl.BlockSpec((ROWS, LANES), lambda *ids: (row_block(*ids)[0], _ColsB.DT // LANES)),
        grp(CONV_WIDTH, GROUP_INNER), grp(1, GROUP_INNER),
        grp(CONV_WIDTH, n), grp(1, n), grp(CONV_WIDTH, n), grp(1, n),
        pl.BlockSpec((1, LANES), lambda *ids: (0, 0)), pl.BlockSpec((1, LANES), lambda *ids: (0, 0)),
        grp(1, GROUP_INNER), grp(1, GROUP_INNER),
    ]


def _ssd_params(conv_w, conv_b, dt_bias, a_log, d_skip, ssm_norm):
    gn = SSM_GROUPS * SSM_DSTATE
    cwx, cwb, cwc = conv_w[:, :SSM_INNER], conv_w[:, SSM_INNER:SSM_INNER + gn], conv_w[:, SSM_INNER + gn:]
    cb = conv_b.reshape(1, CONV_DIM)
    cbx, cbb, cbc = cb[:, :SSM_INNER], cb[:, SSM_INNER:SSM_INNER + gn], cb[:, SSM_INNER + gn:]

    def head_lanes(v):
        return jnp.pad(v, (0, LANES - SSM_HEADS)).reshape(1, LANES)

    dsk = jnp.repeat(d_skip, SSM_HEADDIM).reshape(1, SSM_INNER)
    return (cwx, cbx, cwb, cbb, cwc, cbc, head_lanes(dt_bias), head_lanes(a_log), dsk,
            ssm_norm.reshape(1, SSM_INNER))


def _conv_tail(csx, csb, csc):
    keep = SUBLANES - (CONV_WIDTH - 1)
    return jnp.concatenate([csx[:, keep:], csb[:, keep:], csc[:, keep:]], axis=-1)


def _ssd_prompt(x, norm_w, w_b, bsz, seq, params):
    nch = seq // ROWS
    gn = SSM_GROUPS * SSM_DSTATE
    full = lambda a: pl.BlockSpec(a.shape, lambda b, c: (0,) * a.ndim)
    ltri = _ssd_tables(CHUNK)
    specs = [pl.BlockSpec((ROWS, D_MODEL), lambda b, c: (b * nch, 0)),
             pl.BlockSpec((ROWS, D_MODEL), lambda b, c: (b * nch + jnp.minimum(c + 1, nch - 1), 0)),
             full(norm_w), full(w_b), *[full(p) for p in params], full(ltri)]
    tail = lambda width: pl.BlockSpec((1, SUBLANES, width), lambda b, c: (b, 0, 0))
    y, hs, csx, csb, csc = pl.pallas_call(
        _ssd_prompt_kernel,
        out_shape=(jax.ShapeDtypeStruct((bsz * seq, SSM_INNER), BF16),
                   jax.ShapeDtypeStruct((bsz, SSM_HEADS * SSM_HEADDIM, SSM_DSTATE), F32),
                   jax.ShapeDtypeStruct((bsz, SUBLANES, SSM_INNER), F32),
                   jax.ShapeDtypeStruct((bsz, SUBLANES, gn), F32),
                   jax.ShapeDtypeStruct((bsz, SUBLANES, gn), F32)),
        grid=(bsz, nch),
        in_specs=specs,
        out_specs=(pl.BlockSpec((ROWS, SSM_INNER), lambda b, c: (b * nch + c, 0)),
                   pl.BlockSpec((1, SSM_HEADS * SSM_HEADDIM, SSM_DSTATE), lambda b, c: (b, 0, 0)),
                   tail(SSM_INNER), tail(gn), tail(gn)),
        scratch_shapes=[pltpu.VMEM((2, ROWS, _ColsB.TOTAL), F32)],
        compiler_params=_params(("parallel", "arbitrary")),
        name="ssd_prompt",
    )(x, x, norm_w, w_b, *params, ltri)
    return y, hs, _conv_tail(csx, csb, csc)


def _ssd_sample(proj, bsz, seq, params, state, conv_state):
    ntile = bsz // SAMPLE_PER_TILE
    gn = SSM_GROUPS * SSM_DSTATE
    specs = _ssd_specs(lambda t, g: (t, g))
    gh = HEADS_PER_GROUP * SSM_HEADDIM
    cs = jnp.pad(conv_state, ((0, 0), (SUBLANES - (CONV_WIDTH - 1), 0), (0, 0)))
    csx, csb, csc = cs[:, :, :SSM_INNER], cs[:, :, SSM_INNER:SSM_INNER + gn], cs[:, :, SSM_INNER + gn:]
    pred = lambda width: pl.BlockSpec((SAMPLE_PER_TILE, SUBLANES, width), lambda t, g: (t, 0, g))
    state_spec = pl.BlockSpec((SAMPLE_PER_TILE, gh, SSM_DSTATE), lambda t, g: (t, g, 0))
    specs += [
        pl.BlockSpec((ROWS, ROWS), lambda t, g: (0, 0)),
        pl.BlockSpec((ROWS, SAMPLE_PER_TILE * LANES), lambda t, g: (0, 0)),
        pred(GROUP_INNER), pred(SSM_DSTATE), pred(SSM_DSTATE),
        state_spec,
    ]
    state2 = state.reshape(bsz, SSM_HEADS * SSM_HEADDIM, SSM_DSTATE)
    assert seq == SUBLANES
    y, hs, csx, csb, csc = pl.pallas_call(
        functools.partial(_ssd_sample_kernel, seq),
        out_shape=(jax.ShapeDtypeStruct((bsz * seq, SSM_INNER), BF16), jax.ShapeDtypeStruct(state2.shape, F32),
                   jax.ShapeDtypeStruct((bsz, seq, SSM_INNER), F32),
                   jax.ShapeDtypeStruct((bsz, seq, gn), F32),
                   jax.ShapeDtypeStruct((bsz, seq, gn), F32)),
        grid=(ntile, SSM_GROUPS),
        in_specs=specs,
        out_specs=(pl.BlockSpec((ROWS, GROUP_INNER), lambda t, g: (t, g)), state_spec,
                   pred(GROUP_INNER), pred(SSM_DSTATE), pred(SSM_DSTATE)),
        compiler_params=_params(("parallel", "parallel")),
        name="ssd_sample",
    )(proj, proj, proj, proj, proj, *params, _ssd_tables(seq), _block_mask(BF16), csx, csb, csc, state2)
    return y, hs, _conv_tail(csx, csb, csc)


def _post_kernel(n_prompt_tiles, ap_ref, as_ref, yp_ref, ys_ref, ga_ref, gb_ref, xp_ref, xs_ref, wr_ref, ws_ref,
                 wo_ref, nw_ref, wrt_ref, brt_ref, x1_ref, h2_ref, lg_ref):
    is_prompt = pl.program_id(0) < n_prompt_tiles
    branch_a = _dot(jnp.where(is_prompt, ap_ref[...], as_ref[...]), wr_ref[...])
    branch_b = _dot(jnp.where(is_prompt, yp_ref[...], ys_ref[...]), ws_ref[...])
    merged = jax.nn.sigmoid(ga_ref[...]) * branch_a + jax.nn.sigmoid(gb_ref[...]) * branch_b
    x1 = jnp.where(is_prompt, xp_ref[...], xs_ref[...]) + _dot(merged.astype(BF16), wo_ref[...])
    x1_ref[...] = x1
    ms = jnp.mean(x1 * x1, axis=-1, keepdims=True)
    h2 = x1 * lax.rsqrt(ms + EPS) * nw_ref[...]
    _store_row_tiles(h2_ref, h2)
    lg_ref[...] = _dot(h2.astype(BF16), wrt_ref[...]) + brt_ref[...]


def _post(a_p, a_s, y_p, y_s, proj, xp, xs, w_ret, w_ssm, w_out, norm_ffn, w_router, b_router):
    t = xp.shape[0] + xs.shape[0]
    tm = _pick(math.gcd(xp.shape[0], xs.shape[0]), 512)
    npt = xp.shape[0] // tm
    row = lambda width, blk: pl.BlockSpec((tm, width), lambda i: (i, blk))
    full = lambda *shape: pl.BlockSpec(shape, lambda i: (0,) * len(shape))
    return pl.pallas_call(
        functools.partial(_post_kernel, npt),
        out_shape=(jax.ShapeDtypeStruct((t, D_MODEL), F32), jax.ShapeDtypeStruct((t * ROW_TILES, LANES), F32),
                   jax.ShapeDtypeStruct((t, LANES), F32)),
        grid=(t // tm,),
        in_specs=[*_split_rows(tm, npt, RET_V), *_split_rows(tm, npt, SSM_INNER),
                  row(D_MODEL, _ColsA.GA // D_MODEL), row(D_MODEL, _ColsA.GB // D_MODEL),
                  *_split_rows(tm, npt, D_MODEL),
                  full(RET_V, D_MODEL), full(SSM_INNER, D_MODEL), full(D_MODEL, D_MODEL), full(1, D_MODEL),
                  full(D_MODEL, LANES), full(1, LANES)],
        out_specs=(row(D_MODEL, 0), pl.BlockSpec((tm * ROW_TILES, LANES), lambda i: (i, 0)), row(LANES, 0)),
        compiler_params=_params(("parallel",)),
        name="post_mixer",
    )(a_p, a_s, y_p, y_s, proj, proj, xp, xs, w_ret, w_ssm, w_out, norm_ffn, w_router, b_router)


def _route_kernel(lg_ref, lstrict_ref, meta_ref, gate_ref, cnt_ref):
    @pl.when(pl.program_id(0) == 0)
    def _():
        cnt_ref[...] = jnp.zeros_like(cnt_ref)

    tm = lg_ref.shape[0]
    lane = lax.broadcasted_iota(jnp.int32, (tm, LANES), 1)
    lane_f = lane.astype(F32)
    cur = jnp.where(lane < N_EXPERTS, lg_ref[...], -jnp.inf)
    vals, hots = [], []
    for _ in range(TOP_K):
        m = jnp.max(cur, axis=1, keepdims=True)
        idx = jnp.min(jnp.where(cur == m, lane_f, float(LANES)), axis=1, keepdims=True)
        hot = lane_f == idx
        vals.append(m)
        hots.append(hot)
        cur = jnp.where(hot, -jnp.inf, cur)
    exps = [jnp.exp(v - vals[0]) for v in vals]
    denom = exps[0] + exps[1] + exps[2] + exps[3]
    sel = hots[0] | hots[1] | hots[2] | hots[3]
    self32 = sel.astype(F32)
    before = _dot(lstrict_ref[...], self32.astype(BF16)) + cnt_ref[...]
    cnt_ref[...] = cnt_ref[...] + jnp.sum(self32, axis=0, keepdims=True)
    meta = jnp.zeros((tm, LANES), F32)
    gates = jnp.zeros((tm, LANES), F32)
    for k in range(TOP_K):
        e_k = jnp.sum(jnp.where(hots[k], lane_f, 0.0), axis=1, keepdims=True)
        p_k = jnp.sum(jnp.where(hots[k], before, 0.0), axis=1, keepdims=True)
        meta = jnp.where(lane == k, e_k, meta)
        meta = jnp.where(lane == SUBLANES + k, p_k, meta)
        gates = jnp.where(lane == k, exps[k] / denom, gates)
    gate_ref[...] = gates
    meta_ref[...] = meta.T[:2 * SUBLANES, :].astype(jnp.int32)


def _route(logits):
    t = logits.shape[0]
    tm = _pick(t, 512)
    r = jnp.arange(tm)
    lstrict = (r[:, None] > r[None, :]).astype(BF16)
    return pl.pallas_call(
        _route_kernel,
        out_shape=(jax.ShapeDtypeStruct((2 * SUBLANES, t), jnp.int32),
                   jax.ShapeDtypeStruct((t, LANES), F32),
                   jax.ShapeDtypeStruct((1, LANES), F32)),
        grid=(t // tm,),
        in_specs=[pl.BlockSpec((tm, LANES), lambda i: (i, 0)), pl.BlockSpec((tm, tm), lambda i: (0, 0))],
        out_specs=(pl.BlockSpec((2 * SUBLANES, tm), lambda i: (0, i)),
                   pl.BlockSpec((tm, LANES), lambda i: (i, 0)),
                   pl.BlockSpec((1, LANES), lambda i: (0, 0))),
        compiler_params=_params(("arbitrary",)),
        name="route",
    )(logits, lstrict)


def _row_copy(src_ref, src_row, dst_ref, dst_row, sem):
    src = pl.ds(pl.multiple_of(src_row * ROW_TILES, ROW_TILES), ROW_TILES)
    dst = pl.ds(pl.multiple_of(dst_row * ROW_TILES, ROW_TILES), ROW_TILES)
    return pltpu.make_async_copy(src_ref.at[src, :], dst_ref.at[dst, :], sem)


def _dispatch_kernel(pstart_ref, pend_ref, dest_ref, h2_ref, xs_ref, zero_ref, sem):
    tm = h2_ref.shape[0] // ROW_TILES

    @pl.when(pl.program_id(0) == 0)
    def _():
        zero_ref[...] = jnp.zeros_like(zero_ref)

        def fill(e):
            n = EXPERT_BLOCK * ROW_TILES
            start = pl.multiple_of((pend_ref[e] - EXPERT_BLOCK) * ROW_TILES, n)
            return pltpu.make_async_copy(zero_ref, xs_ref.at[pl.ds(start, n), :], sem)

        def fill_start(e, carry):
            @pl.when(pend_ref[e] > pstart_ref[e])
            def _():
                fill(e).start()
            return carry

        def fill_wait(e, carry):
            @pl.when(pend_ref[e] > pstart_ref[e])
            def _():
                fill(e).wait()
            return carry

        lax.fori_loop(0, N_EXPERTS, fill_start, 0)
        lax.fori_loop(0, N_EXPERTS, fill_wait, 0)

        def tail(j):
            n = EXPERT_BLOCK * ROW_TILES
            return pltpu.make_async_copy(zero_ref, xs_ref.at[pl.ds(pl.multiple_of(j * n, n), n), :], sem)

        used = pend_ref[N_EXPERTS - 1] // EXPERT_BLOCK
        total = xs_ref.shape[0] // (EXPERT_BLOCK * ROW_TILES)
        lax.fori_loop(used, total, lambda j, c: (tail(j).start(), c)[1], 0)
        lax.fori_loop(used, total, lambda j, c: (tail(j).wait(), c)[1], 0)

    def copy(t, k):
        return _row_copy(h2_ref, t, xs_ref, dest_ref[k, t], sem)

    def issue(t, carry):
        for k in range(TOP_K):
            copy(t, k).start(priority=k % 2)
        return carry

    def drain(t, carry):
        for k in range(TOP_K):
            copy(t, k).wait()
        return carry

    lax.fori_loop(0, tm, issue, 0, unroll=ISSUE_UNROLL)
    lax.fori_loop(0, tm, drain, 0, unroll=ISSUE_UNROLL)


def _dispatch(pstart, pends, dest, h2, n_rows):
    t = h2.shape[0] // ROW_TILES
    tm = _pick(t, 256)
    return pl.pallas_call(
        _dispatch_kernel,
        out_shape=jax.ShapeDtypeStruct((n_rows * ROW_TILES, LANES), F32),
        grid_spec=pltpu.PrefetchScalarGridSpec(
            num_scalar_prefetch=2,
            grid=(t // tm,),
            in_specs=[pl.BlockSpec((SUBLANES, tm), lambda i, ps, pe: (0, i), memory_space=pltpu.SMEM),
                      pl.BlockSpec((tm * ROW_TILES, LANES), lambda i, ps, pe: (i, 0))],
            out_specs=pl.BlockSpec(memory_space=pl.ANY),
            scratch_shapes=[pltpu.VMEM((EXPERT_BLOCK * ROW_TILES, LANES), F32), pltpu.SemaphoreType.DMA],
        ),
        compiler_params=_params(("arbitrary",)),
        name="dispatch",
    )(pstart, pends, dest, h2)


def _combine_kernel(n_prompt_tiles, dest_ref, gate_ref, x1_ref, nw_ref, yb_ref, op_ref, os_ref, buf_ref, sem):
    tm = x1_ref.shape[0]

    def copy(t, k):
        return _row_copy(yb_ref, dest_ref[k, t], buf_ref.at[k], t, sem)

    def issue(t, carry):
        for k in range(TOP_K):
            copy(t, k).start(priority=k % 2)
        return carry

    def drain(t, carry):
        for k in range(TOP_K):
            copy(t, k).wait()
        return carry

    lax.fori_loop(0, tm, issue, 0, unroll=ISSUE_UNROLL)
    lax.fori_loop(0, tm, drain, 0, unroll=ISSUE_UNROLL)
    gates = gate_ref[...]
    moe = _load_row_tiles(buf_ref, tm, 0) * gates[:, 0:1]
    for k in range(1, TOP_K):
        moe = moe + _load_row_tiles(buf_ref, tm, k) * gates[:, k:k + 1]
    x2 = x1_ref[...] + moe
    ms = jnp.mean(x2 * x2, axis=-1, keepdims=True)
    out = x2 * lax.rsqrt(ms + EPS) * nw_ref[...]

    @pl.when(pl.program_id(0) < n_prompt_tiles)
    def _():
        op_ref[...] = out

    @pl.when(pl.program_id(0) >= n_prompt_tiles)
    def _():
        os_ref[...] = out


def _combine(dest, gates, x1, norm_final, yb, n_prompt):
    t = x1.shape[0]
    tm = _pick(math.gcd(n_prompt, t - n_prompt), 256)
    npt = n_prompt // tm
    return pl.pallas_call(
        functools.partial(_combine_kernel, npt),
        out_shape=(jax.ShapeDtypeStruct((n_prompt, D_MODEL), F32),
                   jax.ShapeDtypeStruct((t - n_prompt, D_MODEL), F32)),
        grid=(t // tm,),
        in_specs=[pl.BlockSpec((SUBLANES, tm), lambda i: (0, i), memory_space=pltpu.SMEM),
                  pl.BlockSpec((tm, LANES), lambda i: (i, 0)),
                  pl.BlockSpec((tm, D_MODEL), lambda i: (i, 0)),
                  pl.BlockSpec((1, D_MODEL), lambda i: (0, 0)),
                  pl.BlockSpec(memory_space=pl.ANY)],
        out_specs=_split_rows(tm, npt, D_MODEL),
        scratch_shapes=[pltpu.VMEM((TOP_K, tm * ROW_TILES, LANES), F32), pltpu.SemaphoreType.DMA],
        compiler_params=_params(("arbitrary",)),
        name="combine",
    )(dest, gates, x1, norm_final, yb)


def _expert_kernel(be_ref, nv_ref, xs_ref, wgu_ref, bgu_ref, wd_ref, bd_ref, yb_ref, wgu_b, wd_b):
    i = pl.program_id(0)

    @pl.when(jnp.logical_or(i == 0, be_ref[i] != be_ref[jnp.maximum(i - 1, 0)]))
    def _():
        wgu_b[...] = wgu_ref[0].astype(BF16)
        wd_b[...] = wd_ref[0].astype(BF16)

    @pl.when(i < nv_ref[0])
    def _():
        gu = _dot(_load_row_tiles(xs_ref, EXPERT_BLOCK).astype(BF16), wgu_b[...]) + bgu_ref[0]
        glu = jnp.minimum(gu[:, :D_FF], SWIGLU_LIMIT)
        lin = jnp.clip(gu[:, D_FF:], -SWIGLU_LIMIT, SWIGLU_LIMIT)
        act = glu * jax.nn.sigmoid(SWIGLU_ALPHA * glu) * (lin + 1.0)
        _store_row_tiles(yb_ref, _dot(act.astype(BF16), wd_b[...]) + bd_ref[0])

    @pl.when(i >= nv_ref[0])
    def _():
        yb_ref[...] = jnp.zeros_like(yb_ref)


def _experts(block_expert, n_valid, xs, w_gu, b_gu, w_d, b_d):
    rows = EXPERT_BLOCK * ROW_TILES
    nb = xs.shape[0] // rows
    return pl.pallas_call(
        _expert_kernel,
        out_shape=jax.ShapeDtypeStruct(xs.shape, F32),
        grid_spec=pltpu.PrefetchScalarGridSpec(
            num_scalar_prefetch=2,
            grid=(nb,),
            in_specs=[pl.BlockSpec((rows, LANES), lambda i, be, nv: (jnp.minimum(i, nv[0] - 1), 0)),
                      pl.BlockSpec((1, D_MODEL, 2 * D_FF), lambda i, be, nv: (be[i], 0, 0)),
                      pl.BlockSpec((1, 1, 2 * D_FF), lambda i, be, nv: (be[i], 0, 0)),
                      pl.BlockSpec((1, D_FF, D_MODEL), lambda i, be, nv: (be[i], 0, 0)),
                      pl.BlockSpec((1, 1, D_MODEL), lambda i, be, nv: (be[i], 0, 0))],
            out_specs=pl.BlockSpec((rows, LANES), lambda i, be, nv: (i, 0)),
            scratch_shapes=[pltpu.VMEM((D_MODEL, 2 * D_FF), BF16), pltpu.VMEM((D_FF, D_MODEL), BF16)],
        ),
        compiler_params=_params(("arbitrary",)),
        name="experts",
    )(block_expert, n_valid, xs, w_gu, b_gu, w_d, b_d)


def _expert_layout(counts, n_blocks):
    counts = counts.astype(jnp.int32)
    padded = (counts + EXPERT_BLOCK - 1) // EXPERT_BLOCK * EXPERT_BLOCK
    pends = jnp.cumsum(padded)
    pstart = pends - padded
    block_start = jnp.arange(n_blocks, dtype=jnp.int32) * EXPERT_BLOCK
    block_expert = jnp.minimum(jnp.sum(pends[None, :] <= block_start[:, None], axis=1), N_EXPERTS - 1)
    n_valid = (pends[-1:] // EXPERT_BLOCK).astype(jnp.int32)
    return pstart.astype(jnp.int32), pends.astype(jnp.int32), block_expert.astype(jnp.int32), n_valid


def _permute_w_in(w_in):
    sizes = [RET_QK, RET_QK, RET_V, RET_V, SSM_INNER, CONV_DIM, SSM_HEADS, D_MODEL, D_MODEL]
    offs = [0]
    for s in sizes:
        offs.append(offs[-1] + s)
    q, k, v, g_ret, z, xbc, dt, g_a, g_b = [w_in[:, offs[i]:offs[i + 1]] for i in range(len(sizes))]
    gn = SSM_GROUPS * SSM_DSTATE
    xs, bm, cm = xbc[:, :SSM_INNER], xbc[:, SSM_INNER:SSM_INNER + gn], xbc[:, SSM_INNER + gn:]
    dt_pad = jnp.pad(dt, ((0, 0), (0, _ColsB.TOTAL - _ColsB.DT - SSM_HEADS)))
    w_a = jnp.concatenate([v, g_ret, g_a, g_b, q, k], axis=1).astype(BF16)
    w_b = jnp.concatenate([z, xs, bm, cm, dt_pad], axis=1).astype(BF16)
    return w_a, w_b


def _forward(x_prompt, x_sample, state_ret, state_ssm, state_conv, norm_mix, w_in, ret_norm, w_out_ret,
             conv_w, conv_b, dt_bias, a_log, d_skip, ssm_norm, w_out_ssm, w_out, norm_ffn,
             w_router, b_router, w_gate_up, b_gate_up, w_down, b_down, norm_final):
    bp, lp, _ = x_prompt.shape
    bs, ls, _ = x_sample.shape
    assert lp % ROWS == 0 and bs % SAMPLE_PER_TILE == 0 and ls * SAMPLE_PER_TILE == ROWS
    tp, ts = bp * lp, bs * ls
    t_all = tp + ts
    xp, xs_in = x_prompt.reshape(tp, D_MODEL), x_sample.reshape(ts, D_MODEL)

    w_a, w_b = _permute_w_in(w_in[0])
    norm_w = norm_mix[0].reshape(1, D_MODEL)
    proj = _inproj(xp, xs_in, norm_w, w_a)
    proj_ssd_s = _inproj(None, xs_in, norm_w, w_b)

    a_p, ret_p = _retention_prompt(proj, bp, lp, ret_norm[0])
    a_s, ret_s = _retention_sample(proj, tp, bs, ls, state_ret[0], ret_norm[0])

    sp = _ssd_params(conv_w[0], conv_b[0], dt_bias[0], a_log[0], d_skip[0], ssm_norm[0])
    y_p, ssm_p, conv_p = _ssd_prompt(xp, norm_w, w_b, bp, lp, sp)
    y_s, ssm_s, conv_s = _ssd_sample(proj_ssd_s, bs, ls, sp, state_ssm[0], state_conv[0])

    w_router_pad = jnp.pad(w_router[0], ((0, 0), (0, LANES - N_EXPERTS))).astype(BF16)
    b_router_pad = jnp.pad(b_router[0], (0, LANES - N_EXPERTS)).reshape(1, LANES)
    x1, h2, logits = _post(a_p, a_s, y_p, y_s, proj, xp, xs_in, w_out_ret[0].astype(BF16), w_out_ssm[0].astype(BF16),
                           w_out[0].astype(BF16), norm_ffn[0].reshape(1, D_MODEL), w_router_pad, b_router_pad)

    meta, gates, counts = _route(logits)
    n_blocks = -(-(t_all * TOP_K + N_EXPERTS * (EXPERT_BLOCK - 1)) // EXPERT_BLOCK)
    pstart, pends, block_expert, n_valid = _expert_layout(counts[0, :N_EXPERTS], n_blocks)
    hot = meta[:TOP_K, :, None] == jnp.arange(N_EXPERTS, dtype=jnp.int32)
    seg_start = jnp.sum(jnp.where(hot, pstart, 0), axis=-1)
    dest = jnp.pad(seg_start + meta[SUBLANES:SUBLANES + TOP_K], ((0, SUBLANES - TOP_K), (0, 0)))
    xs = _dispatch(pstart, pends, dest, h2, n_blocks * EXPERT_BLOCK)
    yb = _experts(block_expert, n_valid, xs, w_gate_up[0], b_gate_up[0].reshape(N_EXPERTS, 1, 2 * D_FF),
                  w_down[0], b_down[0].reshape(N_EXPERTS, 1, D_MODEL))
    out_p, out_s = _combine(dest, gates, x1, norm_final.reshape(1, D_MODEL), yb, tp)

    shape_s = (1, bs, SSM_HEADS, SSM_HEADDIM, SSM_DSTATE)
    shape_p = (1, bp, SSM_HEADS, SSM_HEADDIM, SSM_DSTATE)
    return (out_p.reshape(bp, lp, D_MODEL), out_s.reshape(bs, ls, D_MODEL),
            ret_p[None], ret_s[None], ssm_p.reshape(shape_p), ssm_s.reshape(shape_s),
            conv_p[None], conv_s[None])


def kernel(x_prompt, x_sample, state_ret, state_ssm, state_conv, norm_mix, w_in, ret_norm, w_out_ret, conv_w, conv_b, dt_bias, a_log, d_skip, ssm_norm, w_out_ssm, w_out, norm_ffn, w_router, b_router, w_gate_up, b_gate_up, w_down, b_down, norm_final):
    return _forward(x_prompt, x_sample, state_ret, state_ssm, state_conv, norm_mix, w_in, ret_norm, w_out_ret,
                    conv_w, conv_b, dt_bias, a_log, d_skip, ssm_norm, w_out_ssm, w_out, norm_ffn,
                    w_router, b_router, w_gate_up, b_gate_up, w_down, b_down, norm_final)
```

```python
import functools
import math

import jax
import jax.numpy as jnp
from jax import lax
from jax.experimental import pallas as pl
from jax.experimental.pallas import tpu as pltpu

F32 = jnp.float32
BF16 = jnp.bfloat16

D_MODEL = 1024
PAST_LEN = 16384
RET_HEADS = 4
RET_DK = 128
RET_DV = 256
RET_QK = RET_HEADS * RET_DK
RET_V = RET_HEADS * RET_DV
ROPE_BASE = 10000.0
SSM_INNER = 2 * D_MODEL
SSM_HEADDIM = 64
SSM_HEADS = SSM_INNER // SSM_HEADDIM
SSM_GROUPS = 4
SSM_DSTATE = 128
HEADS_PER_GROUP = SSM_HEADS // SSM_GROUPS
GROUP_INNER = SSM_INNER // SSM_GROUPS
CONV_WIDTH = 4
CONV_DIM = SSM_INNER + 2 * SSM_GROUPS * SSM_DSTATE
CHUNK = 128
N_EXPERTS = 32
TOP_K = 4
D_FF = D_MODEL
SWIGLU_LIMIT = 7.0
SWIGLU_ALPHA = 1.702
EPS = 1e-6

LANES = 128
SUBLANES = 8
ROWS = 128
SAMPLE_PER_TILE = 16
EXPERT_BLOCK = 512
ROW_TILES = D_MODEL // LANES
ISSUE_UNROLL = 4
VMEM_LIMIT = 56 * 1024 * 1024


class _ColsA:
    V = 0
    GRET = V + RET_V
    GA = GRET + RET_V
    GB = GA + D_MODEL
    Q = GB + D_MODEL
    K = Q + RET_QK
    TOTAL = K + RET_QK


class _ColsB:
    Z = 0
    XS = Z + SSM_INNER
    B = XS + SSM_INNER
    C = B + SSM_GROUPS * SSM_DSTATE
    DT = C + SSM_GROUPS * SSM_DSTATE
    TOTAL = DT + 2 * LANES


def _pick(n, target):
    best = None
    for c in range(LANES, target + 1, LANES):
        if n % c == 0:
            best = c
    assert best is not None, (n, target)
    return best


def _params(sem, **kw):
    return pltpu.CompilerParams(dimension_semantics=sem, vmem_limit_bytes=VMEM_LIMIT, **kw)


def _dot(a, b):
    return jnp.dot(a, b, preferred_element_type=F32)


def _dot_nt(a, b):
    return lax.dot_general(a, b, (((1,), (1,)), ((), ())), preferred_element_type=F32)


def _dot_tn(a, b):
    return lax.dot_general(a, b, (((0,), (0,)), ((), ())), preferred_element_type=F32)


def _silu(x):
    return x * jax.nn.sigmoid(x)


def _store_row_tiles(ref, value, *lead):
    n = value.shape[0]
    for s in range(ROW_TILES):
        ref[(*lead, pl.ds(s, n, stride=ROW_TILES), slice(None))] = value[:, s * LANES:(s + 1) * LANES]


def _load_row_tiles(ref, n, *lead):
    return jnp.concatenate([ref[(*lead, pl.ds(s, n, stride=ROW_TILES), slice(None))] for s in range(ROW_TILES)],
                           axis=1)


def _inproj_kernel(n_prompt_tiles, xp_ref, xs_ref, nw_ref, w_ref, o_ref, h_ref):
    @pl.when(pl.program_id(1) == 0)
    def _():
        x = jnp.where(pl.program_id(0) < n_prompt_tiles, xp_ref[...], xs_ref[...])
        ms = jnp.mean(x * x, axis=-1, keepdims=True)
        h_ref[...] = (x * lax.rsqrt(ms + EPS) * nw_ref[...]).astype(BF16)

    o_ref[...] = _dot(h_ref[...], w_ref[...])


def _split_rows(tm, npt, width):
    prompt = pl.BlockSpec((tm, width), lambda i, *_: (jnp.minimum(i, npt - 1), 0))
    sample = pl.BlockSpec((tm, width), lambda i, *_: (jnp.maximum(i - npt, 0), 0))
    return prompt, sample


def _inproj(xp, xs, norm_w, w):
    total = w.shape[1]
    t = xs.shape[0] + (0 if xp is None else xp.shape[0])
    tm = _pick(xs.shape[0] if xp is None else math.gcd(xp.shape[0], xs.shape[0]), 1024)
    npt = 0 if xp is None else xp.shape[0] // tm
    tn = _pick(total, 2560)
    assert tn % (2 * LANES) == 0
    x_specs = list(_split_rows(tm, npt, D_MODEL)) if npt else [pl.BlockSpec((tm, D_MODEL), lambda i, j: (i, 0))] * 2
    return pl.pallas_call(
        functools.partial(_inproj_kernel, npt),
        out_shape=jax.ShapeDtypeStruct((t, total), F32),
        grid=(t // tm, total // tn),
        in_specs=[
            *x_specs,
            pl.BlockSpec((1, D_MODEL), lambda i, j: (0, 0)),
            pl.BlockSpec((D_MODEL, tn), lambda i, j: (0, j)),
        ],
        out_specs=pl.BlockSpec((tm, tn), lambda i, j: (i, j)),
        scratch_shapes=[pltpu.VMEM((tm, D_MODEL), BF16)],
        compiler_params=_params(("parallel", "arbitrary")),
        name="in_proj",
    )(xs if xp is None else xp, xs, norm_w, w)


def _ret_log_decay():
    return jnp.log(1.0 - 2.0 ** (-5.0 - jnp.arange(RET_HEADS, dtype=F32)))


def _ret_tables(c):
    lg = _ret_log_decay()
    r = jnp.arange(ROWS)
    t = (r % c).astype(F32)
    seg = r // c
    diff = t[:, None] - t[None, :]
    ok = (seg[:, None] == seg[None, :]) & (diff >= 0)
    dm = jnp.where(ok[None], jnp.exp(lg[:, None, None] * jnp.maximum(diff, 0.0)[None]), 0.0)
    qd = jnp.exp(lg[:, None] * (t[None, :] + 1.0))
    kd = jnp.exp(lg[:, None] * (c - 1.0 - t[None, :]))
    cd = jnp.exp(lg * c)
    qd = jnp.broadcast_to(qd[:, :, None], (RET_HEADS, ROWS, LANES))
    kd = jnp.broadcast_to(kd[:, :, None], (RET_HEADS, ROWS, LANES))
    cd = jnp.broadcast_to(cd[:, None, None], (RET_HEADS, 1, RET_DV))
    return dm, qd, kd, cd


def _rope_tables(pos):
    half = RET_DK // 2
    inv_freq = 1.0 / (ROPE_BASE ** jnp.linspace(0.0, 1.0, half, dtype=F32))
    ang = pos.astype(F32)[:, None] * inv_freq[None, :]
    cos, sin = jnp.cos(ang), jnp.sin(ang)
    return jnp.concatenate([cos, cos], -1), jnp.concatenate([-sin, sin], -1)


def _block_mask(dtype):
    r = jnp.arange(ROWS) // (ROWS // SAMPLE_PER_TILE)
    b = jnp.arange(SAMPLE_PER_TILE * LANES) // LANES
    return (r[:, None] == b[None, :]).astype(dtype)


def _ret_common(q, k, v, g, cosf, sinf, dm, qd, kd, nw):
    qr = q * cosf + pltpu.roll(q, RET_DK // 2, 1) * sinf
    kr = (k * cosf + pltpu.roll(k, RET_DK // 2, 1) * sinf) * (RET_DK ** -0.5)
    vb = v.astype(BF16)
    scores = _dot_nt(qr.astype(BF16), kr.astype(BF16)) * dm
    o_intra = _dot(scores.astype(BF16), vb)
    q_state = (qr * qd).astype(BF16)
    k_state = kr * kd

    def finish(o):
        ms = jnp.mean(o * o, axis=-1, keepdims=True)
        on = o * lax.rsqrt(ms + EPS) * nw
        return (_silu(g) * on).astype(BF16)

    return o_intra, q_state, k_state, vb, finish


def _ret_prompt_step(q_ref, k_ref, v_ref, g_ref, cos_ref, sin_ref, dm_ref, qd_ref, kd_ref, cd_ref, nw_ref,
                     a_ref, s_ref):
    cosf, sinf = cos_ref[...], sin_ref[...]
    for h in range(RET_HEADS):
        dk = slice(h * RET_DK, (h + 1) * RET_DK)
        dv = slice(h * RET_DV, (h + 1) * RET_DV)
        o_intra, q_state, k_state, vb, finish = _ret_common(
            q_ref[:, dk], k_ref[:, dk], v_ref[:, dv], g_ref[:, dv], cosf, sinf, dm_ref[h], qd_ref[h], kd_ref[h],
            nw_ref[h])
        s = s_ref[0, h]
        a_ref[:, dv] = finish(o_intra + _dot(q_state, s.astype(BF16)))
        s_ref[0, h] = s * cd_ref[h] + _dot_tn(k_state.astype(BF16), vb)


def _ret_sample_kernel(q_ref, k_ref, v_ref, g_ref, cos_ref, sin_ref, dm_ref, qd_ref, kd_ref, cd_ref, nw_ref,
                       bm_ref, bmt_ref, s0_ref, a_ref, s_ref):
    o_intra, q_state, k_state, vb, finish = _ret_common(
        q_ref[...], k_ref[...], v_ref[...], g_ref[...], cos_ref[...], sin_ref[...], dm_ref[0], qd_ref[0], kd_ref[0],
        nw_ref[0])
    nb = SAMPLE_PER_TILE
    s0 = s0_ref[:, 0]
    q_bd = jnp.concatenate([q_state] * nb, axis=1) * bm_ref[...]
    a_ref[...] = finish(o_intra + _dot(q_bd, s0.reshape(nb * RET_DK, RET_DV).astype(BF16)))
    k_t = k_state.T.astype(BF16)
    k_bd_t = jnp.concatenate([k_t] * nb, axis=0) * bmt_ref[...]
    ds = _dot(k_bd_t, vb).reshape(nb, RET_DK, RET_DV)
    s_ref[:, 0] = s0 * cd_ref[0] + ds


def _ret_specs(row_block):
    def at(width, off):
        return lambda *ids: (row_block(*ids)[0], off // width + row_block(*ids)[1])

    def head(*shape):
        return pl.BlockSpec((1,) + shape, lambda *ids: (row_block(*ids)[1],) + (0,) * len(shape))

    return [
        pl.BlockSpec((ROWS, RET_DK), at(RET_DK, _ColsA.Q)),
        pl.BlockSpec((ROWS, RET_DK), at(RET_DK, _ColsA.K)),
        pl.BlockSpec((ROWS, RET_DV), at(RET_DV, _ColsA.V)),
        pl.BlockSpec((ROWS, RET_DV), at(RET_DV, _ColsA.GRET)),
    ], head


def _retention_sample(proj, row0, bsz, seq, state, ret_norm):
    ntile = bsz // SAMPLE_PER_TILE
    rb0 = row0 // ROWS
    dm, qd, kd, cd = _ret_tables(seq)
    pos = PAST_LEN + jnp.arange(seq, dtype=jnp.int32)
    cosf, sinf = _rope_tables(jnp.tile(pos, SAMPLE_PER_TILE))
    rb = lambda t, h: (rb0 + t, h)
    specs, head = _ret_specs(rb)
    full = lambda *shape: pl.BlockSpec(shape, lambda t, h: (0,) * len(shape))
    specs += [
        full(ROWS, RET_DK), full(ROWS, RET_DK),
        head(ROWS, ROWS), head(ROWS, LANES), head(ROWS, LANES), head(1, RET_DV), head(1, RET_DV),
        full(ROWS, SAMPLE_PER_TILE * LANES), full(SAMPLE_PER_TILE * LANES, ROWS),
        pl.BlockSpec((SAMPLE_PER_TILE, 1, RET_DK, RET_DV), lambda t, h: (t, h, 0, 0)),
    ]
    bm = _block_mask(BF16)
    return pl.pallas_call(
        _ret_sample_kernel,
        out_shape=(jax.ShapeDtypeStruct((bsz * seq, RET_V), BF16),
                   jax.ShapeDtypeStruct(state.shape, F32)),
        grid=(ntile, RET_HEADS),
        in_specs=specs,
        out_specs=(pl.BlockSpec((ROWS, RET_DV), lambda t, h: (t, h)),
                   pl.BlockSpec((SAMPLE_PER_TILE, 1, RET_DK, RET_DV), lambda t, h: (t, h, 0, 0))),
        compiler_params=_params(("parallel", "parallel")),
        name="retention_sample",
    )(proj, proj, proj, proj, cosf, sinf, dm, qd, kd, cd, ret_norm.reshape(RET_HEADS, 1, RET_DV),
      bm, bm.T, state)


def _softplus(x):
    return jnp.maximum(x, 0.0) + jnp.log1p(jnp.exp(-jnp.abs(x)))


def _conv_piece(x, pred, w_ref, b_ref):
    width = x.shape[1]
    x3 = x.reshape(ROWS // SUBLANES, SUBLANES, width)
    t8 = lax.broadcasted_iota(jnp.int32, x3.shape, 1)
    acc = b_ref[...].reshape(1, 1, width)
    for i in range(CONV_WIDTH):
        s = CONV_WIDTH - 1 - i
        tap = x3 if s == 0 else pltpu.roll(jnp.where(t8 >= SUBLANES - s, pred, x3), s, 1)
        acc = acc + tap * w_ref[pl.ds(i, 1), :].reshape(1, 1, width)
    return _silu(acc).reshape(ROWS, width)


def _ssd_decay_terms(c, dt_pre, a_log, ltri):
    dt = _softplus(dt_pre)
    d_a = dt * (-jnp.exp(a_log))
    hi = d_a.astype(BF16)
    r1 = d_a - hi.astype(F32)
    mid = r1.astype(BF16)
    lo = (r1 - mid.astype(F32)).astype(BF16)
    cum = _dot(ltri, hi) + _dot(ltri, mid) + _dot(ltri, lo)
    cum3 = cum.reshape(ROWS // c, c, LANES)
    c_last = jnp.broadcast_to(cum3[:, c - 1:c, :], cum3.shape).reshape(ROWS, LANES)
    to_end = jnp.exp(c_last - cum) * dt
    e_cum = jnp.exp(cum)
    return cum, to_end, e_cum, cum.T, dt.T


def _ssd_group(c, terms, head0, z_ref, dsk_ref, nrm_ref, xs, bm, cm, y_ref, state_io):
    cum, to_end, e_cum, cum_t, dt_t = terms
    b_b, c_b = bm.astype(BF16), cm.astype(BF16)
    cb = _dot_nt(c_b, b_b)
    ri = lax.broadcasted_iota(jnp.int32, (ROWS, ROWS), 0)
    ci = lax.broadcasted_iota(jnp.int32, (ROWS, ROWS), 1)
    shift = c.bit_length() - 1
    causal = (ci <= ri) & (jnp.right_shift(ri, shift) == jnp.right_shift(ci, shift))
    lo_half = lax.broadcasted_iota(jnp.int32, (ROWS, LANES), 1) < SSM_HEADDIM

    def col(a, h):
        return jnp.broadcast_to(a[:, h:h + 1], (ROWS, LANES))

    def weights(h):
        seg = col(cum, h) - jnp.broadcast_to(cum_t[h:h + 1, :], (ROWS, ROWS))
        decay = jnp.exp(jnp.where(causal, seg, -jnp.inf))
        return (cb * decay * jnp.broadcast_to(dt_t[h:h + 1, :], (ROWS, ROWS))).astype(BF16)

    pieces = []
    for m in range(HEADS_PER_GROUP // 2):
        ha, hb = head0 + 2 * m, head0 + 2 * m + 1
        xp = xs[:, m * LANES:(m + 1) * LANES]
        w2 = jnp.concatenate([weights(ha), weights(hb)], axis=1)
        x2 = jnp.concatenate([jnp.where(lo_half, xp, 0.0), jnp.where(lo_half, 0.0, xp)], axis=0).astype(BF16)
        y = _dot(w2, x2)
        xw = (xp * jnp.where(lo_half, col(to_end, ha), col(to_end, hb))).astype(BF16)
        y_state = state_io(m, ha, c_b, xw, b_b, e_cum)
        y = y + y_state * jnp.where(lo_half, col(e_cum, ha), col(e_cum, hb))
        pieces.append(y + xp * dsk_ref[:, m * LANES:(m + 1) * LANES])
    yg = jnp.concatenate(pieces, axis=1) * _silu(z_ref[...])
    ms = jnp.mean(yg * yg, axis=-1, keepdims=True)
    y_ref[...] = (yg * lax.rsqrt(ms + EPS) * nrm_ref[...]).astype(BF16)


def _pair_decay(e_cum, row, ha):
    top = lax.broadcasted_iota(jnp.int32, (2 * SSM_HEADDIM, SSM_DSTATE), 0) < SSM_HEADDIM
    ea = jnp.broadcast_to(e_cum[row:row + 1, ha:ha + 1], top.shape)
    eb = jnp.broadcast_to(e_cum[row:row + 1, ha + 1:ha + 2], top.shape)
    return jnp.where(top, ea, eb)


def _mixer_prompt_kernel(x0_ref, xn_ref, nw_ref, w_ref, cwx_ref, cbx_ref, cwb_ref, cbb_ref, cwc_ref, cbc_ref,
                         dtb_ref, alog_ref, dsk_ref, nrm_ref, ltri_ref, *rest):
    ret_in, (y_ref, hs_ref, px_ref, pb_ref, pc_ref, a_ref, s_ref, proj_ref) = rest[:11], rest[11:]
    c = pl.program_id(1)

    def project(x_ref, slot):
        x = x_ref[...]
        ms = jnp.mean(x * x, axis=-1, keepdims=True)
        h = (x * lax.rsqrt(ms + EPS) * nw_ref[...]).astype(BF16)
        proj_ref[slot] = _dot(h, w_ref[...])

    @pl.when(c == 0)
    def _():
        hs_ref[...] = jnp.zeros_like(hs_ref)
        px_ref[...] = jnp.zeros_like(px_ref)
        pb_ref[...] = jnp.zeros_like(pb_ref)
        pc_ref[...] = jnp.zeros_like(pc_ref)
        s_ref[...] = jnp.zeros_like(s_ref)
        project(x0_ref, 0)

    def conv(x_ref, p_ref, w_ref, bias_ref):
        x = x_ref[...]
        x3 = x.reshape(ROWS // SUBLANES, SUBLANES, x.shape[1])
        pred = jnp.concatenate([p_ref[...], x3[:-1]], axis=0)
        out = _conv_piece(x, pred, w_ref, bias_ref)
        p_ref[0] = x3[-1]
        return out

    def step(cur_slot, next_slot):
        project(xn_ref, next_slot)
        _ret_prompt_step(*ret_in, a_ref, s_ref)
        cur = proj_ref.at[cur_slot]
        gn = SSM_GROUPS * SSM_DSTATE
        z_ref = cur.at[:, _ColsB.Z:_ColsB.Z + SSM_INNER]
        xs_ref = cur.at[:, _ColsB.XS:_ColsB.XS + SSM_INNER]
        b_ref = cur.at[:, _ColsB.B:_ColsB.B + gn]
        c_ref = cur.at[:, _ColsB.C:_ColsB.C + gn]
        dt_ref = cur.at[:, _ColsB.DT:_ColsB.DT + LANES]
        terms = _ssd_decay_terms(CHUNK, dt_ref[...] + dtb_ref[...], alog_ref[...], ltri_ref[...])
        for g in range(SSM_GROUPS):
            def cols(ref, width, g=g):
                return ref.at[..., g * width:(g + 1) * width]

            gi, n = GROUP_INNER, SSM_DSTATE
            xs = conv(cols(xs_ref, gi), cols(px_ref, gi), cols(cwx_ref, gi), cols(cbx_ref, gi))
            bm = conv(cols(b_ref, n), cols(pb_ref, n), cols(cwb_ref, n), cols(cbb_ref, n))
            cm = conv(cols(c_ref, n), cols(pc_ref, n), cols(cwc_ref, n), cols(cbc_ref, n))
            hs_g = hs_ref.at[0, g * gi:(g + 1) * gi, :]

            def state_io(m, ha, c_b, xw, b_b, e_cum, hs_g=hs_g):
                rows = pl.ds(m * 2 * SSM_HEADDIM, 2 * SSM_HEADDIM)
                h = hs_g[rows, :]
                hs_g[rows, :] = h * _pair_decay(e_cum, ROWS - 1, ha) + _dot_tn(xw, b_b)
                return _dot_nt(c_b, h.astype(BF16))

            _ssd_group(CHUNK, terms, g * HEADS_PER_GROUP, cols(z_ref, gi), cols(dsk_ref, gi), cols(nrm_ref, gi),
                       xs, bm, cm, cols(y_ref, gi), state_io)

    @pl.when(c % 2 == 0)
    def _():
        step(0, 1)

    @pl.when(c % 2 == 1)
    def _():
        step(1, 0)


def _ssd_sample_kernel(seq, z_ref, xs_ref, b_ref, c_ref, dt_ref, cwx_ref, cbx_ref, cwb_ref, cbb_ref, cwc_ref,
                       cbc_ref, dtb_ref, alog_ref, dsk_ref, nrm_ref, ltri_ref, bmask_ref, px_ref, pb_ref, pc_ref,
                       h0_ref, y_ref, hs_ref, csx_ref, csb_ref, csc_ref):
    nb = SAMPLE_PER_TILE

    def conv(x_ref, p_ref, w_ref, bias_ref, cs_ref):
        x = x_ref[...]
        cs_ref[...] = x.reshape(nb, seq, x.shape[1])
        return _conv_piece(x, p_ref[...], w_ref, bias_ref)

    xs = conv(xs_ref, px_ref, cwx_ref, cbx_ref, csx_ref)
    bm = conv(b_ref, pb_ref, cwb_ref, cbb_ref, csb_ref)
    cm = conv(c_ref, pc_ref, cwc_ref, cbc_ref, csc_ref)
    bmask = bmask_ref[...]

    def state_io(m, ha, c_b, xw, b_b, e_cum):
        rows = pl.ds(m * 2 * SSM_HEADDIM, 2 * SSM_HEADDIM)
        c_bd = jnp.concatenate([c_b] * nb, axis=1) * bmask
        b_bd = jnp.concatenate([b_b] * nb, axis=1) * bmask
        hs = [h0_ref[i, rows, :] for i in range(nb)]
        h_cat = jnp.concatenate(hs, axis=1).astype(BF16)
        dh = _dot_tn(xw, b_bd)
        for i in range(nb):
            decay = _pair_decay(e_cum, i * seq + seq - 1, ha)
            hs_ref[i, rows, :] = hs[i] * decay + dh[:, i * SSM_DSTATE:(i + 1) * SSM_DSTATE]
        return _dot_nt(c_bd, h_cat)

    shift = jnp.bitwise_and(LANES - HEADS_PER_GROUP * pl.program_id(1), LANES - 1)
    dt_pre = pltpu.roll(dt_ref[...] + dtb_ref[...], shift, 1)
    a_log = pltpu.roll(jnp.broadcast_to(alog_ref[...], (SUBLANES, LANES)), shift, 1)[:1]
    terms = _ssd_decay_terms(seq, dt_pre, a_log, ltri_ref[...])
    _ssd_group(seq, terms, 0, z_ref, dsk_ref, nrm_ref, xs, bm, cm, y_ref, state_io)


def _ssd_tables(c):
    r = jnp.arange(ROWS)
    ltri = ((r[:, None] >= r[None, :]) & ((r[:, None] // c) == (r[None, :] // c))).astype(BF16)
    return ltri


def _ssd_specs(row_block):
    def at(width, off):
        return lambda *ids: (row_block(*ids)[0], off // width + row_block(*ids)[1])

    def grp(rows, width):
        return pl.BlockSpec((rows, width), lambda *ids: (0, row_block(*ids)[1]))

    n = SSM_DSTATE
    return [
        pl.BlockSpec((ROWS, GROUP_INNER), at(GROUP_INNER, _ColsB.Z)),
        pl.BlockSpec((ROWS, GROUP_INNER), at(GROUP_INNER, _ColsB.XS)),
        pl.BlockSpec((ROWS, n), at(n, _ColsB.B)),
        pl.BlockSpec((ROWS, n), at(n, _ColsB.C)),
        pl.BlockSpec((ROWS, LANES), lambda *ids: (row_block(*ids)[0], _ColsB.DT // LANES)),
        grp(CONV_WIDTH, GROUP_INNER), grp(1, GROUP_INNER),
        grp(CONV_WIDTH, n), grp(1, n), grp(CONV_WIDTH, n), grp(1, n),
        pl.BlockSpec((1, LANES), lambda *ids: (0, 0)), pl.BlockSpec((1, LANES), lambda *ids: (0, 0)),
        grp(1, GROUP_INNER), grp(1, GROUP_INNER),
    ]


def _ssd_params(conv_w, conv_b, dt_bias, a_log, d_skip, ssm_norm):
    gn = SSM_GROUPS * SSM_DSTATE
    cwx, cwb, cwc = conv_w[:, :SSM_INNER], conv_w[:, SSM_INNER:SSM_INNER + gn], conv_w[:, SSM_INNER + gn:]
    cb = conv_b.reshape(1, CONV_DIM)
    cbx, cbb, cbc = cb[:, :SSM_INNER], cb[:, SSM_INNER:SSM_INNER + gn], cb[:, SSM_INNER + gn:]

    def head_lanes(v):
        return jnp.pad(v, (0, LANES - SSM_HEADS)).reshape(1, LANES)

    dsk = jnp.repeat(d_skip, SSM_HEADDIM).reshape(1, SSM_INNER)
    return (cwx, cbx, cwb, cbb, cwc, cbc, head_lanes(dt_bias), head_lanes(a_log), dsk,
            ssm_norm.reshape(1, SSM_INNER))


def _conv_tail(csx, csb, csc):
    keep = SUBLANES - (CONV_WIDTH - 1)
    return jnp.concatenate([csx[:, keep:], csb[:, keep:], csc[:, keep:]], axis=-1)


def _mixer_prompt(x, norm_w, w_b, proj, bsz, seq, params, ret_norm):
    nch = seq // ROWS
    gn = SSM_GROUPS * SSM_DSTATE
    full = lambda a: pl.BlockSpec(a.shape, lambda b, c: (0,) * a.ndim)
    row = lambda width, off: pl.BlockSpec((ROWS, width), lambda b, c: (b * nch + c, off // width))
    ltri = _ssd_tables(CHUNK)
    dm, qd, kd, cd = _ret_tables(CHUNK)
    cosf, sinf = _rope_tables(jnp.arange(seq, dtype=jnp.int32))
    rnw = ret_norm.reshape(RET_HEADS, 1, RET_DV)
    specs = [pl.BlockSpec((ROWS, D_MODEL), lambda b, c: (b * nch, 0)),
             pl.BlockSpec((ROWS, D_MODEL), lambda b, c: (b * nch + jnp.minimum(c + 1, nch - 1), 0)),
             full(norm_w), full(w_b), *[full(p) for p in params], full(ltri),
             row(RET_QK, _ColsA.Q), row(RET_QK, _ColsA.K), row(RET_V, _ColsA.V), row(RET_V, _ColsA.GRET),
             pl.BlockSpec((ROWS, RET_DK), lambda b, c: (c, 0)), pl.BlockSpec((ROWS, RET_DK), lambda b, c: (c, 0)),
             full(dm), full(qd), full(kd), full(cd), full(rnw)]
    tail = lambda width: pl.BlockSpec((1, SUBLANES, width), lambda b, c: (b, 0, 0))
    y, hs, csx, csb, csc, a, s = pl.pallas_call(
        _mixer_prompt_kernel,
        out_shape=(jax.ShapeDtypeStruct((bsz * seq, SSM_INNER), BF16),
                   jax.ShapeDtypeStruct((bsz, SSM_HEADS * SSM_HEADDIM, SSM_DSTATE), F32),
                   jax.ShapeDtypeStruct((bsz, SUBLANES, SSM_INNER), F32),
                   jax.ShapeDtypeStruct((bsz, SUBLANES, gn), F32),
                   jax.ShapeDtypeStruct((bsz, SUBLANES, gn), F32),
                   jax.ShapeDtypeStruct((bsz * seq, RET_V), BF16),
                   jax.ShapeDtypeStruct((bsz, RET_HEADS, RET_DK, RET_DV), F32)),
        grid=(bsz, nch),
        in_specs=specs,
        out_specs=(pl.BlockSpec((ROWS, SSM_INNER), lambda b, c: (b * nch + c, 0)),
                   pl.BlockSpec((1, SSM_HEADS * SSM_HEADDIM, SSM_DSTATE), lambda b, c: (b, 0, 0)),
                   tail(SSM_INNER), tail(gn), tail(gn),
                   pl.BlockSpec((ROWS, RET_V), lambda b, c: (b * nch + c, 0)),
                   pl.BlockSpec((1, RET_HEADS, RET_DK, RET_DV), lambda b, c: (b, 0, 0, 0))),
        scratch_shapes=[pltpu.VMEM((2, ROWS, _ColsB.TOTAL), F32)],
        compiler_params=_params(("parallel", "arbitrary")),
        name="mixer_prompt",
    )(x, x, norm_w, w_b, *params, ltri, proj, proj, proj, proj, cosf, sinf, dm, qd, kd, cd, rnw)
    return y, hs, _conv_tail(csx, csb, csc), a, s


def _ssd_sample(proj, bsz, seq, params, state, conv_state):
    ntile = bsz // SAMPLE_PER_TILE
    gn = SSM_GROUPS * SSM_DSTATE
    specs = _ssd_specs(lambda t, g: (t, g))
    gh = HEADS_PER_GROUP * SSM_HEADDIM
    cs = jnp.pad(conv_state, ((0, 0), (SUBLANES - (CONV_WIDTH - 1), 0), (0, 0)))
    csx, csb, csc = cs[:, :, :SSM_INNER], cs[:, :, SSM_INNER:SSM_INNER + gn], cs[:, :, SSM_INNER + gn:]
    pred = lambda width: pl.BlockSpec((SAMPLE_PER_TILE, SUBLANES, width), lambda t, g: (t, 0, g))
    state_spec = pl.BlockSpec((SAMPLE_PER_TILE, gh, SSM_DSTATE), lambda t, g: (t, g, 0))
    specs += [
        pl.BlockSpec((ROWS, ROWS), lambda t, g: (0, 0)),
        pl.BlockSpec((ROWS, SAMPLE_PER_TILE * LANES), lambda t, g: (0, 0)),
        pred(GROUP_INNER), pred(SSM_DSTATE), pred(SSM_DSTATE),
        state_spec,
    ]
    state2 = state.reshape(bsz, SSM_HEADS * SSM_HEADDIM, SSM_DSTATE)
    assert seq == SUBLANES
    y, hs, csx, csb, csc = pl.pallas_call(
        functools.partial(_ssd_sample_kernel, seq),
        out_shape=(jax.ShapeDtypeStruct((bsz * seq, SSM_INNER), BF16), jax.ShapeDtypeStruct(state2.shape, F32),
                   jax.ShapeDtypeStruct((bsz, seq, SSM_INNER), F32),
                   jax.ShapeDtypeStruct((bsz, seq, gn), F32),
                   jax.ShapeDtypeStruct((bsz, seq, gn), F32)),
        grid=(ntile, SSM_GROUPS),
        in_specs=specs,
        out_specs=(pl.BlockSpec((ROWS, GROUP_INNER), lambda t, g: (t, g)), state_spec,
                   pred(GROUP_INNER), pred(SSM_DSTATE), pred(SSM_DSTATE)),
        compiler_params=_params(("parallel", "parallel")),
        name="ssd_sample",
    )(proj, proj, proj, proj, proj, *params, _ssd_tables(seq), _block_mask(BF16), csx, csb, csc, state2)
    return y, hs, _conv_tail(csx, csb, csc)


def _post_kernel(n_prompt_tiles, ap_ref, as_ref, yp_ref, ys_ref, ga_ref, gb_ref, xp_ref, xs_ref, wr_ref, ws_ref,
                 wo_ref, nw_ref, wrt_ref, brt_ref, x1_ref, h2_ref, lg_ref):
    is_prompt = pl.program_id(0) < n_prompt_tiles
    half = x1_ref.shape[0] // 2
    for r in (pl.ds(0, half), pl.ds(half, half)):
        branch_a = _dot(jnp.where(is_prompt, ap_ref[r, :], as_ref[r, :]), wr_ref[...])
        branch_b = _dot(jnp.where(is_prompt, yp_ref[r, :], ys_ref[r, :]), ws_ref[...])
        merged = jax.nn.sigmoid(ga_ref[r, :]) * branch_a + jax.nn.sigmoid(gb_ref[r, :]) * branch_b
        x1 = jnp.where(is_prompt, xp_ref[r, :], xs_ref[r, :]) + _dot(merged.astype(BF16), wo_ref[...])
        x1_ref[r, :] = x1
        ms = jnp.mean(x1 * x1, axis=-1, keepdims=True)
        h2 = x1 * lax.rsqrt(ms + EPS) * nw_ref[...]
        _store_row_tiles(h2_ref.at[pl.ds(r.start * ROW_TILES, half * ROW_TILES), :], h2)
        lg_ref[r, :] = _dot(h2.astype(BF16), wrt_ref[...]) + brt_ref[...]


def _post(a_p, a_s, y_p, y_s, proj, xp, xs, w_ret, w_ssm, w_out, norm_ffn, w_router, b_router):
    t = xp.shape[0] + xs.shape[0]
    tm = _pick(math.gcd(xp.shape[0], xs.shape[0]), 512)
    npt = xp.shape[0] // tm
    row = lambda width, blk: pl.BlockSpec((tm, width), lambda i: (i, blk))
    full = lambda *shape: pl.BlockSpec(shape, lambda i: (0,) * len(shape))
    return pl.pallas_call(
        functools.partial(_post_kernel, npt),
        out_shape=(jax.ShapeDtypeStruct((t, D_MODEL), F32), jax.ShapeDtypeStruct((t * ROW_TILES, LANES), F32),
                   jax.ShapeDtypeStruct((t, LANES), F32)),
        grid=(t // tm,),
        in_specs=[*_split_rows(tm, npt, RET_V), *_split_rows(tm, npt, SSM_INNER),
                  row(D_MODEL, _ColsA.GA // D_MODEL), row(D_MODEL, _ColsA.GB // D_MODEL),
                  *_split_rows(tm, npt, D_MODEL),
                  full(RET_V, D_MODEL), full(SSM_INNER, D_MODEL), full(D_MODEL, D_MODEL), full(1, D_MODEL),
                  full(D_MODEL, LANES), full(1, LANES)],
        out_specs=(row(D_MODEL, 0), pl.BlockSpec((tm * ROW_TILES, LANES), lambda i: (i, 0)), row(LANES, 0)),
        compiler_params=_params(("parallel",)),
        name="post_mixer",
    )(a_p, a_s, y_p, y_s, proj, proj, xp, xs, w_ret, w_ssm, w_out, norm_ffn, w_router, b_router)


def _route_kernel(lg_ref, lstrict_ref, meta_ref, gate_ref, cnt_ref):
    @pl.when(pl.program_id(0) == 0)
    def _():
        cnt_ref[...] = jnp.zeros_like(cnt_ref)

    tm = lg_ref.shape[0]
    lane = lax.broadcasted_iota(jnp.int32, (tm, LANES), 1)
    lane_f = lane.astype(F32)
    cur = jnp.where(lane < N_EXPERTS, lg_ref[...], -jnp.inf)
    vals, hots = [], []
    for _ in range(TOP_K):
        m = jnp.max(cur, axis=1, keepdims=True)
        idx = jnp.min(jnp.where(cur == m, lane_f, float(LANES)), axis=1, keepdims=True)
        hot = lane_f == idx
        vals.append(m)
        hots.append(hot)
        cur = jnp.where(hot, -jnp.inf, cur)
    exps = [jnp.exp(v - vals[0]) for v in vals]
    denom = exps[0] + exps[1] + exps[2] + exps[3]
    sel = hots[0] | hots[1] | hots[2] | hots[3]
    self32 = sel.astype(F32)
    before = _dot(lstrict_ref[...], self32.astype(BF16)) + cnt_ref[...]
    cnt_ref[...] = cnt_ref[...] + jnp.sum(self32, axis=0, keepdims=True)
    meta = jnp.zeros((tm, LANES), F32)
    gates = jnp.zeros((tm, LANES), F32)
    for k in range(TOP_K):
        e_k = jnp.sum(jnp.where(hots[k], lane_f, 0.0), axis=1, keepdims=True)
        p_k = jnp.sum(jnp.where(hots[k], before, 0.0), axis=1, keepdims=True)
        meta = jnp.where(lane == k, e_k, meta)
        meta = jnp.where(lane == SUBLANES + k, p_k, meta)
        gates = jnp.where(lane == k, exps[k] / denom, gates)
    gate_ref[...] = gates
    meta_ref[...] = meta.T[:2 * SUBLANES, :].astype(jnp.int32)


def _route(logits):
    t = logits.shape[0]
    tm = _pick(t, 512)
    r = jnp.arange(tm)
    lstrict = (r[:, None] > r[None, :]).astype(BF16)
    return pl.pallas_call(
        _route_kernel,
        out_shape=(jax.ShapeDtypeStruct((2 * SUBLANES, t), jnp.int32),
                   jax.ShapeDtypeStruct((t, LANES), F32),
                   jax.ShapeDtypeStruct((1, LANES), F32)),
        grid=(t // tm,),
        in_specs=[pl.BlockSpec((tm, LANES), lambda i: (i, 0)), pl.BlockSpec((tm, tm), lambda i: (0, 0))],
        out_specs=(pl.BlockSpec((2 * SUBLANES, tm), lambda i: (0, i)),
                   pl.BlockSpec((tm, LANES), lambda i: (i, 0)),
                   pl.BlockSpec((1, LANES), lambda i: (0, 0))),
        compiler_params=_params(("arbitrary",)),
        name="route",
    )(logits, lstrict)


def _row_copy(src_ref, src_row, dst_ref, dst_row, sem):
    src = pl.ds(pl.multiple_of(src_row * ROW_TILES, ROW_TILES), ROW_TILES)
    dst = pl.ds(pl.multiple_of(dst_row * ROW_TILES, ROW_TILES), ROW_TILES)
    return pltpu.make_async_copy(src_ref.at[src, :], dst_ref.at[dst, :], sem)


def _dispatch_kernel(pstart_ref, pend_ref, dest_ref, h2_ref, xs_ref, zero_ref, sem):
    tm = h2_ref.shape[0] // ROW_TILES

    @pl.when(pl.program_id(0) == 0)
    def _():
        zero_ref[...] = jnp.zeros_like(zero_ref)

        def fill(e):
            n = EXPERT_BLOCK * ROW_TILES
            start = pl.multiple_of((pend_ref[e] - EXPERT_BLOCK) * ROW_TILES, n)
            return pltpu.make_async_copy(zero_ref, xs_ref.at[pl.ds(start, n), :], sem)

        def fill_start(e, carry):
            @pl.when(pend_ref[e] > pstart_ref[e])
            def _():
                fill(e).start()
            return carry

        def fill_wait(e, carry):
            @pl.when(pend_ref[e] > pstart_ref[e])
            def _():
                fill(e).wait()
            return carry

        lax.fori_loop(0, N_EXPERTS, fill_start, 0)
        lax.fori_loop(0, N_EXPERTS, fill_wait, 0)

        def tail(j):
            n = EXPERT_BLOCK * ROW_TILES
            return pltpu.make_async_copy(zero_ref, xs_ref.at[pl.ds(pl.multiple_of(j * n, n), n), :], sem)

        used = pend_ref[N_EXPERTS - 1] // EXPERT_BLOCK
        total = xs_ref.shape[0] // (EXPERT_BLOCK * ROW_TILES)
        lax.fori_loop(used, total, lambda j, c: (tail(j).start(), c)[1], 0)
        lax.fori_loop(used, total, lambda j, c: (tail(j).wait(), c)[1], 0)

    def copy(t, k):
        return _row_copy(h2_ref, t, xs_ref, dest_ref[k, t], sem)

    def issue(t, carry):
        for k in range(TOP_K):
            copy(t, k).start(priority=k % 2)
        return carry

    def drain(t, carry):
        for k in range(TOP_K):
            copy(t, k).wait()
        return carry

    lax.fori_loop(0, tm, issue, 0, unroll=ISSUE_UNROLL)
    lax.fori_loop(0, tm, drain, 0, unroll=ISSUE_UNROLL)


def _dispatch(pstart, pends, dest, h2, n_rows):
    t = h2.shape[0] // ROW_TILES
    tm = _pick(t, 256)
    return pl.pallas_call(
        _dispatch_kernel,
        out_shape=jax.ShapeDtypeStruct((n_rows * ROW_TILES, LANES), F32),
        grid_spec=pltpu.PrefetchScalarGridSpec(
            num_scalar_prefetch=2,
            grid=(t // tm,),
            in_specs=[pl.BlockSpec((SUBLANES, tm), lambda i, ps, pe: (0, i), memory_space=pltpu.SMEM),
                      pl.BlockSpec((tm * ROW_TILES, LANES), lambda i, ps, pe: (i, 0))],
            out_specs=pl.BlockSpec(memory_space=pl.ANY),
            scratch_shapes=[pltpu.VMEM((EXPERT_BLOCK * ROW_TILES, LANES), F32), pltpu.SemaphoreType.DMA],
        ),
        compiler_params=_params(("arbitrary",)),
        name="dispatch",
    )(pstart, pends, dest, h2)


def _combine_kernel(n_prompt_tiles, dest_ref, dest_next_ref, gate_ref, x1_ref, nw_ref, yb_ref, op_ref, os_ref,
                    buf_ref, sem):
    tm = x1_ref.shape[0]
    i = pl.program_id(0)

    def copy(d_ref, slot, t, k):
        return _row_copy(yb_ref, d_ref[k, t], buf_ref.at[slot, k], t, sem.at[slot])

    def issue(d_ref, slot):
        def body(t, carry):
            for k in range(TOP_K):
                copy(d_ref, slot, t, k).start(priority=k % 2)
            return carry
        lax.fori_loop(0, tm, body, 0, unroll=ISSUE_UNROLL)

    def drain(d_ref, slot):
        def body(t, carry):
            for k in range(TOP_K):
                copy(d_ref, slot, t, k).wait()
            return carry
        lax.fori_loop(0, tm, body, 0, unroll=ISSUE_UNROLL)

    @pl.when(i == 0)
    def _():
        issue(dest_ref, 0)

    def step(slot):
        @pl.when(i + 1 < pl.num_programs(0))
        def _():
            issue(dest_next_ref, 1 - slot)

        drain(dest_ref, slot)
        gates = gate_ref[...]
        moe = _load_row_tiles(buf_ref, tm, slot, 0) * gates[:, 0:1]
        for k in range(1, TOP_K):
            moe = moe + _load_row_tiles(buf_ref, tm, slot, k) * gates[:, k:k + 1]
        x2 = x1_ref[...] + moe
        ms = jnp.mean(x2 * x2, axis=-1, keepdims=True)
        out = x2 * lax.rsqrt(ms + EPS) * nw_ref[...]

        @pl.when(i < n_prompt_tiles)
        def _():
            op_ref[...] = out

        @pl.when(i >= n_prompt_tiles)
        def _():
            os_ref[...] = out

    @pl.when(i % 2 == 0)
    def _():
        step(0)

    @pl.when(i % 2 == 1)
    def _():
        step(1)


def _combine(dest, gates, x1, norm_final, yb, n_prompt):
    t = x1.shape[0]
    tm = _pick(math.gcd(n_prompt, t - n_prompt), 256)
    npt = n_prompt // tm
    return pl.pallas_call(
        functools.partial(_combine_kernel, npt),
        out_shape=(jax.ShapeDtypeStruct((n_prompt, D_MODEL), F32),
                   jax.ShapeDtypeStruct((t - n_prompt, D_MODEL), F32)),
        grid=(t // tm,),
        in_specs=[pl.BlockSpec((SUBLANES, tm), lambda i: (0, i), memory_space=pltpu.SMEM),
                  pl.BlockSpec((SUBLANES, tm), lambda i: (0, jnp.minimum(i + 1, t // tm - 1)),
                               memory_space=pltpu.SMEM),
                  pl.BlockSpec((tm, LANES), lambda i: (i, 0)),
                  pl.BlockSpec((tm, D_MODEL), lambda i: (i, 0)),
                  pl.BlockSpec((1, D_MODEL), lambda i: (0, 0)),
                  pl.BlockSpec(memory_space=pl.ANY)],
        out_specs=_split_rows(tm, npt, D_MODEL),
        scratch_shapes=[pltpu.VMEM((2, TOP_K, tm * ROW_TILES, LANES), F32), pltpu.SemaphoreType.DMA((2,))],
        compiler_params=_params(("arbitrary",)),
        name="combine",
    )(dest, dest, gates, x1, norm_final, yb)


def _expert_kernel(be_ref, nv_ref, xs_ref, wgu_ref, bgu_ref, wd_ref, bd_ref, yb_ref, wgu_b, wd_b):
    i = pl.program_id(0)

    @pl.when(jnp.logical_or(i == 0, be_ref[i] != be_ref[jnp.maximum(i - 1, 0)]))
    def _():
        wgu_b[...] = wgu_ref[0].astype(BF16)
        wd_b[...] = wd_ref[0].astype(BF16)

    @pl.when(i < nv_ref[0])
    def _():
        gu = _dot(_load_row_tiles(xs_ref, EXPERT_BLOCK).astype(BF16), wgu_b[...]) + bgu_ref[0]
        glu = jnp.minimum(gu[:, :D_FF], SWIGLU_LIMIT)
        lin = jnp.clip(gu[:, D_FF:], -SWIGLU_LIMIT, SWIGLU_LIMIT)
        act = glu * jax.nn.sigmoid(SWIGLU_ALPHA * glu) * (lin + 1.0)
        _store_row_tiles(yb_ref, _dot(act.astype(BF16), wd_b[...]) + bd_ref[0])

    @pl.when(i >= nv_ref[0])
    def _():
        yb_ref[...] = jnp.zeros_like(yb_ref)


def _experts(block_expert, n_valid, xs, w_gu, b_gu, w_d, b_d):
    rows = EXPERT_BLOCK * ROW_TILES
    nb = xs.shape[0] // rows
    return pl.pallas_call(
        _expert_kernel,
        out_shape=jax.ShapeDtypeStruct(xs.shape, F32),
        grid_spec=pltpu.PrefetchScalarGridSpec(
            num_scalar_prefetch=2,
            grid=(nb,),
            in_specs=[pl.BlockSpec((rows, LANES), lambda i, be, nv: (jnp.minimum(i, nv[0] - 1), 0)),
                      pl.BlockSpec((1, D_MODEL, 2 * D_FF), lambda i, be, nv: (be[i], 0, 0)),
                      pl.BlockSpec((1, 1, 2 * D_FF), lambda i, be, nv: (be[i], 0, 0)),
                      pl.BlockSpec((1, D_FF, D_MODEL), lambda i, be, nv: (be[i], 0, 0)),
                      pl.BlockSpec((1, 1, D_MODEL), lambda i, be, nv: (be[i], 0, 0))],
            out_specs=pl.BlockSpec((rows, LANES), lambda i, be, nv: (i, 0)),
            scratch_shapes=[pltpu.VMEM((D_MODEL, 2 * D_FF), BF16), pltpu.VMEM((D_FF, D_MODEL), BF16)],
        ),
        compiler_params=_params(("arbitrary",)),
        name="experts",
    )(block_expert, n_valid, xs, w_gu, b_gu, w_d, b_d)


def _expert_layout(counts, n_blocks):
    counts = counts.astype(jnp.int32)
    padded = (counts + EXPERT_BLOCK - 1) // EXPERT_BLOCK * EXPERT_BLOCK
    pends = jnp.cumsum(padded)
    pstart = pends - padded
    block_start = jnp.arange(n_blocks, dtype=jnp.int32) * EXPERT_BLOCK
    block_expert = jnp.minimum(jnp.sum(pends[None, :] <= block_start[:, None], axis=1), N_EXPERTS - 1)
    n_valid = (pends[-1:] // EXPERT_BLOCK).astype(jnp.int32)
    return pstart.astype(jnp.int32), pends.astype(jnp.int32), block_expert.astype(jnp.int32), n_valid


def _permute_w_in(w_in):
    sizes = [RET_QK, RET_QK, RET_V, RET_V, SSM_INNER, CONV_DIM, SSM_HEADS, D_MODEL, D_MODEL]
    offs = [0]
    for s in sizes:
        offs.append(offs[-1] + s)
    q, k, v, g_ret, z, xbc, dt, g_a, g_b = [w_in[:, offs[i]:offs[i + 1]] for i in range(len(sizes))]
    gn = SSM_GROUPS * SSM_DSTATE
    xs, bm, cm = xbc[:, :SSM_INNER], xbc[:, SSM_INNER:SSM_INNER + gn], xbc[:, SSM_INNER + gn:]
    dt_pad = jnp.pad(dt, ((0, 0), (0, _ColsB.TOTAL - _ColsB.DT - SSM_HEADS)))
    w_a = jnp.concatenate([v, g_ret, g_a, g_b, q, k], axis=1).astype(BF16)
    w_b = jnp.concatenate([z, xs, bm, cm, dt_pad], axis=1).astype(BF16)
    return w_a, w_b


def _forward(x_prompt, x_sample, state_ret, state_ssm, state_conv, norm_mix, w_in, ret_norm, w_out_ret,
             conv_w, conv_b, dt_bias, a_log, d_skip, ssm_norm, w_out_ssm, w_out, norm_ffn,
             w_router, b_router, w_gate_up, b_gate_up, w_down, b_down, norm_final):
    bp, lp, _ = x_prompt.shape
    bs, ls, _ = x_sample.shape
    assert lp % ROWS == 0 and bs % SAMPLE_PER_TILE == 0 and ls * SAMPLE_PER_TILE == ROWS
    tp, ts = bp * lp, bs * ls
    t_all = tp + ts
    xp, xs_in = x_prompt.reshape(tp, D_MODEL), x_sample.reshape(ts, D_MODEL)

    w_a, w_b = _permute_w_in(w_in[0])
    norm_w = norm_mix[0].reshape(1, D_MODEL)
    proj = _inproj(xp, xs_in, norm_w, w_a)
    proj_ssd_s = _inproj(None, xs_in, norm_w, w_b)

    a_s, ret_s = _retention_sample(proj, tp, bs, ls, state_ret[0], ret_norm[0])

    sp = _ssd_params(conv_w[0], conv_b[0], dt_bias[0], a_log[0], d_skip[0], ssm_norm[0])
    y_p, ssm_p, conv_p, a_p, ret_p = _mixer_prompt(xp, norm_w, w_b, proj, bp, lp, sp, ret_norm[0])
    y_s, ssm_s, conv_s = _ssd_sample(proj_ssd_s, bs, ls, sp, state_ssm[0], state_conv[0])

    w_router_pad = jnp.pad(w_router[0], ((0, 0), (0, LANES - N_EXPERTS))).astype(BF16)
    b_router_pad = jnp.pad(b_router[0], (0, LANES - N_EXPERTS)).reshape(1, LANES)
    x1, h2, logits = _post(a_p, a_s, y_p, y_s, proj, xp, xs_in, w_out_ret[0].astype(BF16), w_out_ssm[0].astype(BF16),
                           w_out[0].astype(BF16), norm_ffn[0].reshape(1, D_MODEL), w_router_pad, b_router_pad)

    meta, gates, counts = _route(logits)
    n_blocks = -(-(t_all * TOP_K + N_EXPERTS * (EXPERT_BLOCK - 1)) // EXPERT_BLOCK)
    pstart, pends, block_expert, n_valid = _expert_layout(counts[0, :N_EXPERTS], n_blocks)
    hot = meta[:TOP_K, :, None] == jnp.arange(N_EXPERTS, dtype=jnp.int32)
    seg_start = jnp.sum(jnp.where(hot, pstart, 0), axis=-1)
    dest = jnp.pad(seg_start + meta[SUBLANES:SUBLANES + TOP_K], ((0, SUBLANES - TOP_K), (0, 0)))
    xs = _dispatch(pstart, pends, dest, h2, n_blocks * EXPERT_BLOCK)
    yb = _experts(block_expert, n_valid, xs, w_gate_up[0], b_gate_up[0].reshape(N_EXPERTS, 1, 2 * D_FF),
                  w_down[0], b_down[0].reshape(N_EXPERTS, 1, D_MODEL))
    out_p, out_s = _combine(dest, gates, x1, norm_final.reshape(1, D_MODEL), yb, tp)

    shape_s = (1, bs, SSM_HEADS, SSM_HEADDIM, SSM_DSTATE)
    shape_p = (1, bp, SSM_HEADS, SSM_HEADDIM, SSM_DSTATE)
    return (out_p.reshape(bp, lp, D_MODEL), out_s.reshape(bs, ls, D_MODEL),
            ret_p[None], ret_s[None], ssm_p.reshape(shape_p), ssm_s.reshape(shape_s),
            conv_p[None], conv_s[None])


def kernel(x_prompt, x_sample, state_ret, state_ssm, state_conv, norm_mix, w_in, ret_norm, w_out_ret, conv_w, conv_b, dt_bias, a_log, d_skip, ssm_norm, w_out_ssm, w_out, norm_ffn, w_router, b_router, w_gate_up, b_gate_up, w_down, b_down, norm_final):
    return _forward(x_prompt, x_sample, state_ret, state_ssm, state_conv, norm_mix, w_in, ret_norm, w_out_ret,
                    conv_w, conv_b, dt_bias, a_log, d_skip, ssm_norm, w_out_ssm, w_out, norm_ffn,
                    w_router, b_router, w_gate_up, b_gate_up, w_down, b_down, norm_final)
```

```python
import functools
import math

import jax
import jax.numpy as jnp
from jax import lax
from jax.experimental import pallas as pl
from jax.experimental.pallas import tpu as pltpu

F32 = jnp.float32
BF16 = jnp.bfloat16

D_MODEL = 1024
PAST_LEN = 16384
RET_HEADS = 4
RET_DK = 128
RET_DV = 256
RET_QK = RET_HEADS * RET_DK
RET_V = RET_HEADS * RET_DV
ROPE_BASE = 10000.0
SSM_INNER = 2 * D_MODEL
SSM_HEADDIM = 64
SSM_HEADS = SSM_INNER // SSM_HEADDIM
SSM_GROUPS = 4
SSM_DSTATE = 128
HEADS_PER_GROUP = SSM_HEADS // SSM_GROUPS
GROUP_INNER = SSM_INNER // SSM_GROUPS
CONV_WIDTH = 4
CONV_DIM = SSM_INNER + 2 * SSM_GROUPS * SSM_DSTATE
CHUNK = 128
N_EXPERTS = 32
TOP_K = 4
D_FF = D_MODEL
SWIGLU_LIMIT = 7.0
SWIGLU_ALPHA = 1.702
EPS = 1e-6

LANES = 128
SUBLANES = 8
ROWS = 128
SAMPLE_PER_TILE = 16
EXPERT_BLOCK = 512
ROW_TILES = D_MODEL // LANES
ISSUE_UNROLL = 4
VMEM_LIMIT = 56 * 1024 * 1024


class _ColsA:
    V = 0
    GRET = V + RET_V
    GA = GRET + RET_V
    GB = GA + D_MODEL
    Q = GB + D_MODEL
    K = Q + RET_QK
    TOTAL = K + RET_QK


class _ColsB:
    Z = 0
    XS = Z + SSM_INNER
    B = XS + SSM_INNER
    C = B + SSM_GROUPS * SSM_DSTATE
    DT = C + SSM_GROUPS * SSM_DSTATE
    TOTAL = DT + 2 * LANES


def _pick(n, target):
    best = None
    for c in range(LANES, target + 1, LANES):
        if n % c == 0:
            best = c
    assert best is not None, (n, target)
    return best


def _params(sem, **kw):
    return pltpu.CompilerParams(dimension_semantics=sem, vmem_limit_bytes=VMEM_LIMIT, **kw)


def _dot(a, b):
    return jnp.dot(a, b, preferred_element_type=F32)


def _dot_nt(a, b):
    return lax.dot_general(a, b, (((1,), (1,)), ((), ())), preferred_element_type=F32)


def _dot_tn(a, b):
    return lax.dot_general(a, b, (((0,), (0,)), ((), ())), preferred_element_type=F32)


def _silu(x):
    return x * jax.nn.sigmoid(x)


def _store_row_tiles(ref, value, *lead):
    n = value.shape[0]
    for s in range(ROW_TILES):
        ref[(*lead, pl.ds(s, n, stride=ROW_TILES), slice(None))] = value[:, s * LANES:(s + 1) * LANES]


def _load_row_tiles(ref, n, *lead):
    return jnp.concatenate([ref[(*lead, pl.ds(s, n, stride=ROW_TILES), slice(None))] for s in range(ROW_TILES)],
                           axis=1)


def _inproj_kernel(n_prompt_tiles, xp_ref, xs_ref, nw_ref, w_ref, o_ref, h_ref):
    @pl.when(pl.program_id(1) == 0)
    def _():
        x = jnp.where(pl.program_id(0) < n_prompt_tiles, xp_ref[...], xs_ref[...])
        ms = jnp.mean(x * x, axis=-1, keepdims=True)
        h_ref[...] = (x * lax.rsqrt(ms + EPS) * nw_ref[...]).astype(BF16)

    o_ref[...] = _dot(h_ref[...], w_ref[...])


def _split_rows(tm, npt, width):
    prompt = pl.BlockSpec((tm, width), lambda i, *_: (jnp.minimum(i, npt - 1), 0))
    sample = pl.BlockSpec((tm, width), lambda i, *_: (jnp.maximum(i - npt, 0), 0))
    return prompt, sample


def _inproj(xp, xs, norm_w, w):
    total = w.shape[1]
    t = xs.shape[0] + (0 if xp is None else xp.shape[0])
    tm = _pick(xs.shape[0] if xp is None else math.gcd(xp.shape[0], xs.shape[0]), 1024)
    npt = 0 if xp is None else xp.shape[0] // tm
    tn = _pick(total, 2560)
    assert tn % (2 * LANES) == 0
    x_specs = list(_split_rows(tm, npt, D_MODEL)) if npt else [pl.BlockSpec((tm, D_MODEL), lambda i, j: (i, 0))] * 2
    return pl.pallas_call(
        functools.partial(_inproj_kernel, npt),
        out_shape=jax.ShapeDtypeStruct((t, total), F32),
        grid=(t // tm, total // tn),
        in_specs=[
            *x_specs,
            pl.BlockSpec((1, D_MODEL), lambda i, j: (0, 0)),
            pl.BlockSpec((D_MODEL, tn), lambda i, j: (0, j)),
        ],
        out_specs=pl.BlockSpec((tm, tn), lambda i, j: (i, j)),
        scratch_shapes=[pltpu.VMEM((tm, D_MODEL), BF16)],
        compiler_params=_params(("parallel", "arbitrary")),
        name="in_proj",
    )(xs if xp is None else xp, xs, norm_w, w)


def _ret_log_decay():
    return jnp.log(1.0 - 2.0 ** (-5.0 - jnp.arange(RET_HEADS, dtype=F32)))


def _ret_tables(c):
    lg = _ret_log_decay()
    r = jnp.arange(ROWS)
    t = (r % c).astype(F32)
    seg = r // c
    diff = t[:, None] - t[None, :]
    ok = (seg[:, None] == seg[None, :]) & (diff >= 0)
    dm = jnp.where(ok[None], jnp.exp(lg[:, None, None] * jnp.maximum(diff, 0.0)[None]), 0.0)
    qd = jnp.exp(lg[:, None] * (t[None, :] + 1.0))
    kd = jnp.exp(lg[:, None] * (c - 1.0 - t[None, :]))
    cd = jnp.exp(lg * c)
    qd = jnp.broadcast_to(qd[:, :, None], (RET_HEADS, ROWS, LANES))
    kd = jnp.broadcast_to(kd[:, :, None], (RET_HEADS, ROWS, LANES))
    cd = jnp.broadcast_to(cd[:, None, None], (RET_HEADS, 1, RET_DV))
    return dm, qd, kd, cd


def _rope_tables(pos):
    half = RET_DK // 2
    inv_freq = 1.0 / (ROPE_BASE ** jnp.linspace(0.0, 1.0, half, dtype=F32))
    ang = pos.astype(F32)[:, None] * inv_freq[None, :]
    cos, sin = jnp.cos(ang), jnp.sin(ang)
    return jnp.concatenate([cos, cos], -1), jnp.concatenate([-sin, sin], -1)


def _block_mask(dtype):
    r = jnp.arange(ROWS) // (ROWS // SAMPLE_PER_TILE)
    b = jnp.arange(SAMPLE_PER_TILE * LANES) // LANES
    return (r[:, None] == b[None, :]).astype(dtype)


def _ret_common(q, k, v, g, cosf, sinf, dm, qd, kd, nw):
    qr = q * cosf + pltpu.roll(q, RET_DK // 2, 1) * sinf
    kr = (k * cosf + pltpu.roll(k, RET_DK // 2, 1) * sinf) * (RET_DK ** -0.5)
    vb = v.astype(BF16)
    scores = _dot_nt(qr.astype(BF16), kr.astype(BF16)) * dm
    o_intra = _dot(scores.astype(BF16), vb)
    q_state = (qr * qd).astype(BF16)
    k_state = kr * kd

    def finish(o):
        ms = jnp.mean(o * o, axis=-1, keepdims=True)
        on = o * lax.rsqrt(ms + EPS) * nw
        return (_silu(g) * on).astype(BF16)

    return o_intra, q_state, k_state, vb, finish


def _ret_prompt_step(q_ref, k_ref, v_ref, g_ref, cos_ref, sin_ref, dm_ref, qd_ref, kd_ref, cd_ref, nw_ref,
                     a_ref, s_ref):
    cosf, sinf = cos_ref[...], sin_ref[...]
    for h in range(RET_HEADS):
        dk = slice(h * RET_DK, (h + 1) * RET_DK)
        dv = slice(h * RET_DV, (h + 1) * RET_DV)
        o_intra, q_state, k_state, vb, finish = _ret_common(
            q_ref[:, dk], k_ref[:, dk], v_ref[:, dv], g_ref[:, dv], cosf, sinf, dm_ref[h], qd_ref[h], kd_ref[h],
            nw_ref[h])
        s = s_ref[0, h]
        a_ref[:, dv] = finish(o_intra + _dot(q_state, s.astype(BF16)))
        s_ref[0, h] = s * cd_ref[h] + _dot_tn(k_state.astype(BF16), vb)


def _ret_sample_kernel(q_ref, k_ref, v_ref, g_ref, cos_ref, sin_ref, dm_ref, qd_ref, kd_ref, cd_ref, nw_ref,
                       bm_ref, bmt_ref, s0_ref, a_ref, s_ref):
    o_intra, q_state, k_state, vb, finish = _ret_common(
        q_ref[...], k_ref[...], v_ref[...], g_ref[...], cos_ref[...], sin_ref[...], dm_ref[0], qd_ref[0], kd_ref[0],
        nw_ref[0])
    nb = SAMPLE_PER_TILE
    s0 = s0_ref[:, 0]
    q_bd = jnp.concatenate([q_state] * nb, axis=1) * bm_ref[...]
    a_ref[...] = finish(o_intra + _dot(q_bd, s0.reshape(nb * RET_DK, RET_DV).astype(BF16)))
    k_t = k_state.T.astype(BF16)
    k_bd_t = jnp.concatenate([k_t] * nb, axis=0) * bmt_ref[...]
    ds = _dot(k_bd_t, vb).reshape(nb, RET_DK, RET_DV)
    s_ref[:, 0] = s0 * cd_ref[0] + ds


def _ret_specs(row_block):
    def at(width, off):
        return lambda *ids: (row_block(*ids)[0], off // width + row_block(*ids)[1])

    def head(*shape):
        return pl.BlockSpec((1,) + shape, lambda *ids: (row_block(*ids)[1],) + (0,) * len(shape))

    return [
        pl.BlockSpec((ROWS, RET_DK), at(RET_DK, _ColsA.Q)),
        pl.BlockSpec((ROWS, RET_DK), at(RET_DK, _ColsA.K)),
        pl.BlockSpec((ROWS, RET_DV), at(RET_DV, _ColsA.V)),
        pl.BlockSpec((ROWS, RET_DV), at(RET_DV, _ColsA.GRET)),
    ], head


def _retention_sample(proj, row0, bsz, seq, state, ret_norm):
    ntile = bsz // SAMPLE_PER_TILE
    rb0 = row0 // ROWS
    dm, qd, kd, cd = _ret_tables(seq)
    pos = PAST_LEN + jnp.arange(seq, dtype=jnp.int32)
    cosf, sinf = _rope_tables(jnp.tile(pos, SAMPLE_PER_TILE))
    rb = lambda t, h: (rb0 + t, h)
    specs, head = _ret_specs(rb)
    full = lambda *shape: pl.BlockSpec(shape, lambda t, h: (0,) * len(shape))
    specs += [
        full(ROWS, RET_DK), full(ROWS, RET_DK),
        head(ROWS, ROWS), head(ROWS, LANES), head(ROWS, LANES), head(1, RET_DV), head(1, RET_DV),
        full(ROWS, SAMPLE_PER_TILE * LANES), full(SAMPLE_PER_TILE * LANES, ROWS),
        pl.BlockSpec((SAMPLE_PER_TILE, 1, RET_DK, RET_DV), lambda t, h: (t, h, 0, 0)),
    ]
    bm = _block_mask(BF16)
    return pl.pallas_call(
        _ret_sample_kernel,
        out_shape=(jax.ShapeDtypeStruct((bsz * seq, RET_V), BF16),
                   jax.ShapeDtypeStruct(state.shape, F32)),
        grid=(ntile, RET_HEADS),
        in_specs=specs,
        out_specs=(pl.BlockSpec((ROWS, RET_DV), lambda t, h: (t, h)),
                   pl.BlockSpec((SAMPLE_PER_TILE, 1, RET_DK, RET_DV), lambda t, h: (t, h, 0, 0))),
        compiler_params=_params(("parallel", "parallel")),
        name="retention_sample",
    )(proj, proj, proj, proj, cosf, sinf, dm, qd, kd, cd, ret_norm.reshape(RET_HEADS, 1, RET_DV),
      bm, bm.T, state)


def _softplus(x):
    return jnp.maximum(x, 0.0) + jnp.log1p(jnp.exp(-jnp.abs(x)))


def _conv_piece(x, pred, w_ref, b_ref):
    width = x.shape[1]
    x3 = x.reshape(ROWS // SUBLANES, SUBLANES, width)
    t8 = lax.broadcasted_iota(jnp.int32, x3.shape, 1)
    acc = b_ref[...].reshape(1, 1, width)
    for i in range(CONV_WIDTH):
        s = CONV_WIDTH - 1 - i
        tap = x3 if s == 0 else pltpu.roll(jnp.where(t8 >= SUBLANES - s, pred, x3), s, 1)
        acc = acc + tap * w_ref[pl.ds(i, 1), :].reshape(1, 1, width)
    return _silu(acc).reshape(ROWS, width)


def _ssd_decay_terms(c, dt_pre, a_log, ltri):
    dt = _softplus(dt_pre)
    d_a = dt * (-jnp.exp(a_log))
    hi = d_a.astype(BF16)
    r1 = d_a - hi.astype(F32)
    mid = r1.astype(BF16)
    lo = (r1 - mid.astype(F32)).astype(BF16)
    cum = _dot(ltri, hi) + _dot(ltri, mid) + _dot(ltri, lo)
    cum3 = cum.reshape(ROWS // c, c, LANES)
    c_last = jnp.broadcast_to(cum3[:, c - 1:c, :], cum3.shape).reshape(ROWS, LANES)
    to_end = jnp.exp(c_last - cum) * dt
    e_cum = jnp.exp(cum)
    return cum, to_end, e_cum, cum.T, dt.T


def _ssd_group(c, terms, head0, z_ref, dsk_ref, nrm_ref, xs, bm, cm, y_ref, state_io, zero=None):
    cum, to_end, e_cum, cum_t, dt_t = terms
    b_b, c_b = bm.astype(BF16), cm.astype(BF16)
    cb = _dot_nt(c_b, b_b)
    ri = lax.broadcasted_iota(jnp.int32, (ROWS, ROWS), 0)
    ci = lax.broadcasted_iota(jnp.int32, (ROWS, ROWS), 1)
    shift = c.bit_length() - 1
    causal = (ci <= ri) & (jnp.right_shift(ri, shift) == jnp.right_shift(ci, shift))
    lo_half = lax.broadcasted_iota(jnp.int32, (ROWS, LANES), 1) < SSM_HEADDIM

    def col(a, h):
        return jnp.broadcast_to(a[:, h:h + 1], (ROWS, LANES))

    def weights(h):
        seg = col(cum, h) - jnp.broadcast_to(cum_t[h:h + 1, :], (ROWS, ROWS))
        decay = jnp.exp(jnp.where(causal, seg, -jnp.inf))
        return (cb * decay * jnp.broadcast_to(dt_t[h:h + 1, :], (ROWS, ROWS))).astype(BF16)

    pieces = []
    for m in range(HEADS_PER_GROUP // 2):
        ha, hb = head0 + 2 * m, head0 + 2 * m + 1
        xp = xs[:, m * LANES:(m + 1) * LANES]
        w2 = jnp.concatenate([weights(ha), weights(hb)], axis=1)
        x2 = jnp.concatenate([jnp.where(lo_half, xp, 0.0), jnp.where(lo_half, 0.0, xp)], axis=0).astype(BF16)
        y = _dot(w2, x2)
        xw = (xp * jnp.where(lo_half, col(to_end, ha), col(to_end, hb))).astype(BF16)
        y_state = state_io(m, ha, c_b, xw, b_b, e_cum)
        y = y + y_state * jnp.where(lo_half, col(e_cum, ha), col(e_cum, hb))
        pieces.append(y + xp * dsk_ref[:, m * LANES:(m + 1) * LANES])
    if zero is not None:
        pieces[0] = jnp.concatenate([pieces[0][:SUBLANES] + zero, pieces[0][SUBLANES:]], axis=0)
    yg = jnp.concatenate(pieces, axis=1) * _silu(z_ref[...])
    ms = jnp.mean(yg * yg, axis=-1, keepdims=True)
    y_ref[...] = (yg * lax.rsqrt(ms + EPS) * nrm_ref[...]).astype(BF16)


def _pair_decay(e_cum, row, ha):
    top = lax.broadcasted_iota(jnp.int32, (2 * SSM_HEADDIM, SSM_DSTATE), 0) < SSM_HEADDIM
    ea = jnp.broadcast_to(e_cum[row:row + 1, ha:ha + 1], top.shape)
    eb = jnp.broadcast_to(e_cum[row:row + 1, ha + 1:ha + 2], top.shape)
    return jnp.where(top, ea, eb)


def _mixer_prompt_kernel(x0_ref, xn_ref, nw_ref, w_ref, cwx_ref, cbx_ref, cwb_ref, cbb_ref, cwc_ref, cbc_ref,
                         dtb_ref, alog_ref, dsk_ref, nrm_ref, ltri_ref, *rest):
    ret_in, (y_ref, hs_ref, px_ref, pb_ref, pc_ref, a_ref, s_ref, proj_ref) = rest[:11], rest[11:]
    c = pl.program_id(1)

    def normed(x_ref):
        x = x_ref[...]
        ms = jnp.mean(x * x, axis=-1, keepdims=True)
        return (x * lax.rsqrt(ms + EPS) * nw_ref[...]).astype(BF16)

    def project(x_ref, slot):
        proj_ref[slot] = _dot(normed(x_ref), w_ref[...])

    @pl.when(c == 0)
    def _():
        hs_ref[...] = jnp.zeros_like(hs_ref)
        px_ref[...] = jnp.zeros_like(px_ref)
        pb_ref[...] = jnp.zeros_like(pb_ref)
        pc_ref[...] = jnp.zeros_like(pc_ref)
        s_ref[...] = jnp.zeros_like(s_ref)
        project(x0_ref, 0)

    def conv(x_ref, p_ref, w_ref, bias_ref):
        x = x_ref[...]
        x3 = x.reshape(ROWS // SUBLANES, SUBLANES, x.shape[1])
        pred = jnp.concatenate([p_ref[...], x3[:-1]], axis=0)
        out = _conv_piece(x, pred, w_ref, bias_ref)
        p_ref[0] = x3[-1]
        return out

    def step(cur_slot, next_slot):
        h_next = normed(xn_ref)
        bounds = [0, 5 * 256, 10 * 256, 15 * 256, _ColsB.TOTAL]

        def project_slice(g):
            lo, hi = bounds[g], bounds[g + 1]
            part = _dot(h_next, w_ref[:, lo:hi])
            proj_ref[next_slot, :, lo:hi] = part
            bits = pltpu.bitcast(part[ROWS - SUBLANES:, hi - lo - LANES:], jnp.uint32)
            sixteen = jnp.uint32(16)
            return pltpu.bitcast(lax.shift_right_logical(lax.shift_right_logical(bits, sixteen), sixteen), F32)

        _ret_prompt_step(*ret_in, a_ref, s_ref)
        cur = proj_ref.at[cur_slot]
        gn = SSM_GROUPS * SSM_DSTATE
        z_ref = cur.at[:, _ColsB.Z:_ColsB.Z + SSM_INNER]
        xs_ref = cur.at[:, _ColsB.XS:_ColsB.XS + SSM_INNER]
        b_ref = cur.at[:, _ColsB.B:_ColsB.B + gn]
        c_ref = cur.at[:, _ColsB.C:_ColsB.C + gn]
        dt_ref = cur.at[:, _ColsB.DT:_ColsB.DT + LANES]
        terms = _ssd_decay_terms(CHUNK, dt_ref[...] + dtb_ref[...], alog_ref[...], ltri_ref[...])
        for g in range(SSM_GROUPS):
            def cols(ref, width, g=g):
                return ref.at[..., g * width:(g + 1) * width]

            gi, n = GROUP_INNER, SSM_DSTATE
            xs = conv(cols(xs_ref, gi), cols(px_ref, gi), cols(cwx_ref, gi), cols(cbx_ref, gi))
            bm = conv(cols(b_ref, n), cols(pb_ref, n), cols(cwb_ref, n), cols(cbb_ref, n))
            cm = conv(cols(c_ref, n), cols(pc_ref, n), cols(cwc_ref, n), cols(cbc_ref, n))
            hs_g = hs_ref.at[0, g * gi:(g + 1) * gi, :]

            def state_io(m, ha, c_b, xw, b_b, e_cum, hs_g=hs_g):
                rows = pl.ds(m * 2 * SSM_HEADDIM, 2 * SSM_HEADDIM)
                h = hs_g[rows, :]
                hs_g[rows, :] = h * _pair_decay(e_cum, ROWS - 1, ha) + _dot_tn(xw, b_b)
                return _dot_nt(c_b, h.astype(BF16))

            _ssd_group(CHUNK, terms, g * HEADS_PER_GROUP, cols(z_ref, gi), cols(dsk_ref, gi), cols(nrm_ref, gi),
                       xs, bm, cm, cols(y_ref, gi), state_io, zero=project_slice(g))

    @pl.when(c % 2 == 0)
    def _():
        step(0, 1)

    @pl.when(c % 2 == 1)
    def _():
        step(1, 0)


def _ssd_sample_kernel(seq, z_ref, xs_ref, b_ref, c_ref, dt_ref, cwx_ref, cbx_ref, cwb_ref, cbb_ref, cwc_ref,
                       cbc_ref, dtb_ref, alog_ref, dsk_ref, nrm_ref, ltri_ref, bmask_ref, px_ref, pb_ref, pc_ref,
                       h0_ref, y_ref, hs_ref, csx_ref, csb_ref, csc_ref):
    nb = SAMPLE_PER_TILE

    def conv(x_ref, p_ref, w_ref, bias_ref, cs_ref):
        x = x_ref[...]
        cs_ref[...] = x.reshape(nb, seq, x.shape[1])
        return _conv_piece(x, p_ref[...], w_ref, bias_ref)

    xs = conv(xs_ref, px_ref, cwx_ref, cbx_ref, csx_ref)
    bm = conv(b_ref, pb_ref, cwb_ref, cbb_ref, csb_ref)
    cm = conv(c_ref, pc_ref, cwc_ref, cbc_ref, csc_ref)
    bmask = bmask_ref[...]

    def state_io(m, ha, c_b, xw, b_b, e_cum):
        rows = pl.ds(m * 2 * SSM_HEADDIM, 2 * SSM_HEADDIM)
        c_bd = jnp.concatenate([c_b] * nb, axis=1) * bmask
        b_bd = jnp.concatenate([b_b] * nb, axis=1) * bmask
        hs = [h0_ref[i, rows, :] for i in range(nb)]
        h_cat = jnp.concatenate(hs, axis=1).astype(BF16)
        dh = _dot_tn(xw, b_bd)
        for i in range(nb):
            decay = _pair_decay(e_cum, i * seq + seq - 1, ha)
            hs_ref[i, rows, :] = hs[i] * decay + dh[:, i * SSM_DSTATE:(i + 1) * SSM_DSTATE]
        return _dot_nt(c_bd, h_cat)

    shift = jnp.bitwise_and(LANES - HEADS_PER_GROUP * pl.program_id(1), LANES - 1)
    dt_pre = pltpu.roll(dt_ref[...] + dtb_ref[...], shift, 1)
    a_log = pltpu.roll(jnp.broadcast_to(alog_ref[...], (SUBLANES, LANES)), shift, 1)[:1]
    terms = _ssd_decay_terms(seq, dt_pre, a_log, ltri_ref[...])
    _ssd_group(seq, terms, 0, z_ref, dsk_ref, nrm_ref, xs, bm, cm, y_ref, state_io)


def _ssd_tables(c):
    r = jnp.arange(ROWS)
    ltri = ((r[:, None] >= r[None, :]) & ((r[:, None] // c) == (r[None, :] // c))).astype(BF16)
    return ltri


def _ssd_specs(row_block):
    def at(width, off):
        return lambda *ids: (row_block(*ids)[0], off // width + row_block(*ids)[1])

    def grp(rows, width):
        return pl.BlockSpec((rows, width), lambda *ids: (0, row_block(*ids)[1]))

    n = SSM_DSTATE
    return [
        pl.BlockSpec((ROWS, GROUP_INNER), at(GROUP_INNER, _ColsB.Z)),
        pl.BlockSpec((ROWS, GROUP_INNER), at(GROUP_INNER, _ColsB.XS)),
        pl.BlockSpec((ROWS, n), at(n, _ColsB.B)),
        pl.BlockSpec((ROWS, n), at(n, _ColsB.C)),
        pl.BlockSpec((ROWS, LANES), lambda *ids: (row_block(*ids)[0], _ColsB.DT // LANES)),
        grp(CONV_WIDTH, GROUP_INNER), grp(1, GROUP_INNER),
        grp(CONV_WIDTH, n), grp(1, n), grp(CONV_WIDTH, n), grp(1, n),
        pl.BlockSpec((1, LANES), lambda *ids: (0, 0)), pl.BlockSpec((1, LANES), lambda *ids: (0, 0)),
        grp(1, GROUP_INNER), grp(1, GROUP_INNER),
    ]


def _ssd_params(conv_w, conv_b, dt_bias, a_log, d_skip, ssm_norm):
    gn = SSM_GROUPS * SSM_DSTATE
    cwx, cwb, cwc = conv_w[:, :SSM_INNER], conv_w[:, SSM_INNER:SSM_INNER + gn], conv_w[:, SSM_INNER + gn:]
    cb = conv_b.reshape(1, CONV_DIM)
    cbx, cbb, cbc = cb[:, :SSM_INNER], cb[:, SSM_INNER:SSM_INNER + gn], cb[:, SSM_INNER + gn:]

    def head_lanes(v):
        return jnp.pad(v, (0, LANES - SSM_HEADS)).reshape(1, LANES)

    dsk = jnp.repeat(d_skip, SSM_HEADDIM).reshape(1, SSM_INNER)
    return (cwx, cbx, cwb, cbb, cwc, cbc, head_lanes(dt_bias), head_lanes(a_log), dsk,
            ssm_norm.reshape(1, SSM_INNER))


def _conv_tail(csx, csb, csc):
    keep = SUBLANES - (CONV_WIDTH - 1)
    return jnp.concatenate([csx[:, keep:], csb[:, keep:], csc[:, keep:]], axis=-1)


def _mixer_prompt(x, norm_w, w_b, proj, bsz, seq, params, ret_norm):
    nch = seq // ROWS
    gn = SSM_GROUPS * SSM_DSTATE
    full = lambda a: pl.BlockSpec(a.shape, lambda b, c: (0,) * a.ndim)
    row = lambda width, off: pl.BlockSpec((ROWS, width), lambda b, c: (b * nch + c, off // width))
    ltri = _ssd_tables(CHUNK)
    dm, qd, kd, cd = _ret_tables(CHUNK)
    cosf, sinf = _rope_tables(jnp.arange(seq, dtype=jnp.int32))
    rnw = ret_norm.reshape(RET_HEADS, 1, RET_DV)
    specs = [pl.BlockSpec((ROWS, D_MODEL), lambda b, c: (b * nch, 0)),
             pl.BlockSpec((ROWS, D_MODEL), lambda b, c: (b * nch + jnp.minimum(c + 1, nch - 1), 0)),
             full(norm_w), full(w_b), *[full(p) for p in params], full(ltri),
             row(RET_QK, _ColsA.Q), row(RET_QK, _ColsA.K), row(RET_V, _ColsA.V), row(RET_V, _ColsA.GRET),
             pl.BlockSpec((ROWS, RET_DK), lambda b, c: (c, 0)), pl.BlockSpec((ROWS, RET_DK), lambda b, c: (c, 0)),
             full(dm), full(qd), full(kd), full(cd), full(rnw)]
    tail = lambda width: pl.BlockSpec((1, SUBLANES, width), lambda b, c: (b, 0, 0))
    y, hs, csx, csb, csc, a, s = pl.pallas_call(
        _mixer_prompt_kernel,
        out_shape=(jax.ShapeDtypeStruct((bsz * seq, SSM_INNER), BF16),
                   jax.ShapeDtypeStruct((bsz, SSM_HEADS * SSM_HEADDIM, SSM_DSTATE), F32),
                   jax.ShapeDtypeStruct((bsz, SUBLANES, SSM_INNER), F32),
                   jax.ShapeDtypeStruct((bsz, SUBLANES, gn), F32),
                   jax.ShapeDtypeStruct((bsz, SUBLANES, gn), F32),
                   jax.ShapeDtypeStruct((bsz * seq, RET_V), BF16),
                   jax.ShapeDtypeStruct((bsz, RET_HEADS, RET_DK, RET_DV), F32)),
        grid=(bsz, nch),
        in_specs=specs,
        out_specs=(pl.BlockSpec((ROWS, SSM_INNER), lambda b, c: (b * nch + c, 0)),
                   pl.BlockSpec((1, SSM_HEADS * SSM_HEADDIM, SSM_DSTATE), lambda b, c: (b, 0, 0)),
                   tail(SSM_INNER), tail(gn), tail(gn),
                   pl.BlockSpec((ROWS, RET_V), lambda b, c: (b * nch + c, 0)),
                   pl.BlockSpec((1, RET_HEADS, RET_DK, RET_DV), lambda b, c: (b, 0, 0, 0))),
        scratch_shapes=[pltpu.VMEM((2, ROWS, _ColsB.TOTAL), F32)],
        compiler_params=_params(("parallel", "arbitrary")),
        name="mixer_prompt",
    )(x, x, norm_w, w_b, *params, ltri, proj, proj, proj, proj, cosf, sinf, dm, qd, kd, cd, rnw)
    return y, hs, _conv_tail(csx, csb, csc), a, s


def _ssd_sample(proj, bsz, seq, params, state, conv_state):
    ntile = bsz // SAMPLE_PER_TILE
    gn = SSM_GROUPS * SSM_DSTATE
    specs = _ssd_specs(lambda t, g: (t, g))
    gh = HEADS_PER_GROUP * SSM_HEADDIM
    cs = jnp.pad(conv_state, ((0, 0), (SUBLANES - (CONV_WIDTH - 1), 0), (0, 0)))
    csx, csb, csc = cs[:, :, :SSM_INNER], cs[:, :, SSM_INNER:SSM_INNER + gn], cs[:, :, SSM_INNER + gn:]
    pred = lambda width: pl.BlockSpec((SAMPLE_PER_TILE, SUBLANES, width), lambda t, g: (t, 0, g))
    state_spec = pl.BlockSpec((SAMPLE_PER_TILE, gh, SSM_DSTATE), lambda t, g: (t, g, 0))
    specs += [
        pl.BlockSpec((ROWS, ROWS), lambda t, g: (0, 0)),
        pl.BlockSpec((ROWS, SAMPLE_PER_TILE * LANES), lambda t, g: (0, 0)),
        pred(GROUP_INNER), pred(SSM_DSTATE), pred(SSM_DSTATE),
        state_spec,
    ]
    state2 = state.reshape(bsz, SSM_HEADS * SSM_HEADDIM, SSM_DSTATE)
    assert seq == SUBLANES
    y, hs, csx, csb, csc = pl.pallas_call(
        functools.partial(_ssd_sample_kernel, seq),
        out_shape=(jax.ShapeDtypeStruct((bsz * seq, SSM_INNER), BF16), jax.ShapeDtypeStruct(state2.shape, F32),
                   jax.ShapeDtypeStruct((bsz, seq, SSM_INNER), F32),
                   jax.ShapeDtypeStruct((bsz, seq, gn), F32),
                   jax.ShapeDtypeStruct((bsz, seq, gn), F32)),
        grid=(ntile, SSM_GROUPS),
        in_specs=specs,
        out_specs=(pl.BlockSpec((ROWS, GROUP_INNER), lambda t, g: (t, g)), state_spec,
                   pred(GROUP_INNER), pred(SSM_DSTATE), pred(SSM_DSTATE)),
        compiler_params=_params(("parallel", "parallel")),
        name="ssd_sample",
    )(proj, proj, proj, proj, proj, *params, _ssd_tables(seq), _block_mask(BF16), csx, csb, csc, state2)
    return y, hs, _conv_tail(csx, csb, csc)


def _post_kernel(n_prompt_tiles, ap_ref, as_ref, yp_ref, ys_ref, ga_ref, gb_ref, xp_ref, xs_ref, wr_ref, ws_ref,
                 wo_ref, nw_ref, wrt_ref, brt_ref, x1_ref, h2_ref, lg_ref):
    is_prompt = pl.program_id(0) < n_prompt_tiles
    half = x1_ref.shape[0] // 2
    for r in (pl.ds(0, half), pl.ds(half, half)):
        branch_a = _dot(jnp.where(is_prompt, ap_ref[r, :], as_ref[r, :]), wr_ref[...])
        branch_b = _dot(jnp.where(is_prompt, yp_ref[r, :], ys_ref[r, :]), ws_ref[...])
        merged = jax.nn.sigmoid(ga_ref[r, :]) * branch_a + jax.nn.sigmoid(gb_ref[r, :]) * branch_b
        x1 = jnp.where(is_prompt, xp_ref[r, :], xs_ref[r, :]) + _dot(merged.astype(BF16), wo_ref[...])
        x1_ref[r, :] = x1
        ms = jnp.mean(x1 * x1, axis=-1, keepdims=True)
        h2 = x1 * lax.rsqrt(ms + EPS) * nw_ref[...]
        _store_row_tiles(h2_ref.at[pl.ds(r.start * ROW_TILES, half * ROW_TILES), :], h2)
        lg_ref[r, :] = _dot(h2.astype(BF16), wrt_ref[...]) + brt_ref[...]


def _post(a_p, a_s, y_p, y_s, proj, xp, xs, w_ret, w_ssm, w_out, norm_ffn, w_router, b_router):
    t = xp.shape[0] + xs.shape[0]
    tm = _pick(math.gcd(xp.shape[0], xs.shape[0]), 512)
    npt = xp.shape[0] // tm
    row = lambda width, blk: pl.BlockSpec((tm, width), lambda i: (i, blk))
    full = lambda *shape: pl.BlockSpec(shape, lambda i: (0,) * len(shape))
    return pl.pallas_call(
        functools.partial(_post_kernel, npt),
        out_shape=(jax.ShapeDtypeStruct((t, D_MODEL), F32), jax.ShapeDtypeStruct((t * ROW_TILES, LANES), F32),
                   jax.ShapeDtypeStruct((t, LANES), F32)),
        grid=(t // tm,),
        in_specs=[*_split_rows(tm, npt, RET_V), *_split_rows(tm, npt, SSM_INNER),
                  row(D_MODEL, _ColsA.GA // D_MODEL), row(D_MODEL, _ColsA.GB // D_MODEL),
                  *_split_rows(tm, npt, D_MODEL),
                  full(RET_V, D_MODEL), full(SSM_INNER, D_MODEL), full(D_MODEL, D_MODEL), full(1, D_MODEL),
                  full(D_MODEL, LANES), full(1, LANES)],
        out_specs=(row(D_MODEL, 0), pl.BlockSpec((tm * ROW_TILES, LANES), lambda i: (i, 0)), row(LANES, 0)),
        compiler_params=_params(("parallel",)),
        name="post_mixer",
    )(a_p, a_s, y_p, y_s, proj, proj, xp, xs, w_ret, w_ssm, w_out, norm_ffn, w_router, b_router)


def _route_kernel(lg_ref, lstrict_ref, meta_ref, gate_ref, cnt_ref):
    @pl.when(pl.program_id(0) == 0)
    def _():
        cnt_ref[...] = jnp.zeros_like(cnt_ref)

    tm = lg_ref.shape[0]
    lane = lax.broadcasted_iota(jnp.int32, (tm, LANES), 1)
    lane_f = lane.astype(F32)
    cur = jnp.where(lane < N_EXPERTS, lg_ref[...], -jnp.inf)
    vals, hots = [], []
    for _ in range(TOP_K):
        m = jnp.max(cur, axis=1, keepdims=True)
        idx = jnp.min(jnp.where(cur == m, lane_f, float(LANES)), axis=1, keepdims=True)
        hot = lane_f == idx
        vals.append(m)
        hots.append(hot)
        cur = jnp.where(hot, -jnp.inf, cur)
    exps = [jnp.exp(v - vals[0]) for v in vals]
    denom = exps[0] + exps[1] + exps[2] + exps[3]
    sel = hots[0] | hots[1] | hots[2] | hots[3]
    self32 = sel.astype(F32)
    before = _dot(lstrict_ref[...], self32.astype(BF16)) + cnt_ref[...]
    cnt_ref[...] = cnt_ref[...] + jnp.sum(self32, axis=0, keepdims=True)
    meta = jnp.zeros((tm, LANES), F32)
    gates = jnp.zeros((tm, LANES), F32)
    for k in range(TOP_K):
        e_k = jnp.sum(jnp.where(hots[k], lane_f, 0.0), axis=1, keepdims=True)
        p_k = jnp.sum(jnp.where(hots[k], before, 0.0), axis=1, keepdims=True)
        meta = jnp.where(lane == k, e_k, meta)
        meta = jnp.where(lane == SUBLANES + k, p_k, meta)
        gates = jnp.where(lane == k, exps[k] / denom, gates)
    gate_ref[...] = gates
    meta_ref[...] = meta.T[:2 * SUBLANES, :].astype(jnp.int32)


def _route(logits):
    t = logits.shape[0]
    tm = _pick(t, 512)
    r = jnp.arange(tm)
    lstrict = (r[:, None] > r[None, :]).astype(BF16)
    return pl.pallas_call(
        _route_kernel,
        out_shape=(jax.ShapeDtypeStruct((2 * SUBLANES, t), jnp.int32),
                   jax.ShapeDtypeStruct((t, LANES), F32),
                   jax.ShapeDtypeStruct((1, LANES), F32)),
        grid=(t // tm,),
        in_specs=[pl.BlockSpec((tm, LANES), lambda i: (i, 0)), pl.BlockSpec((tm, tm), lambda i: (0, 0))],
        out_specs=(pl.BlockSpec((2 * SUBLANES, tm), lambda i: (0, i)),
                   pl.BlockSpec((tm, LANES), lambda i: (i, 0)),
                   pl.BlockSpec((1, LANES), lambda i: (0, 0))),
        compiler_params=_params(("arbitrary",)),
        name="route",
    )(logits, lstrict)


def _row_copy(src_ref, src_row, dst_ref, dst_row, sem):
    src = pl.ds(pl.multiple_of(src_row * ROW_TILES, ROW_TILES), ROW_TILES)
    dst = pl.ds(pl.multiple_of(dst_row * ROW_TILES, ROW_TILES), ROW_TILES)
    return pltpu.make_async_copy(src_ref.at[src, :], dst_ref.at[dst, :], sem)


def _dispatch_kernel(pstart_ref, pend_ref, dest_ref, h2_ref, xs_ref, zero_ref, sem):
    tm = h2_ref.shape[0] // ROW_TILES

    @pl.when(pl.program_id(0) == 0)
    def _():
        zero_ref[...] = jnp.zeros_like(zero_ref)

        def fill(e):
            n = EXPERT_BLOCK * ROW_TILES
            start = pl.multiple_of((pend_ref[e] - EXPERT_BLOCK) * ROW_TILES, n)
            return pltpu.make_async_copy(zero_ref, xs_ref.at[pl.ds(start, n), :], sem)

        def fill_start(e, carry):
            @pl.when(pend_ref[e] > pstart_ref[e])
            def _():
                fill(e).start()
            return carry

        def fill_wait(e, carry):
            @pl.when(pend_ref[e] > pstart_ref[e])
            def _():
                fill(e).wait()
            return carry

        lax.fori_loop(0, N_EXPERTS, fill_start, 0)
        lax.fori_loop(0, N_EXPERTS, fill_wait, 0)

        def tail(j):
            n = EXPERT_BLOCK * ROW_TILES
            return pltpu.make_async_copy(zero_ref, xs_ref.at[pl.ds(pl.multiple_of(j * n, n), n), :], sem)

        used = pend_ref[N_EXPERTS - 1] // EXPERT_BLOCK
        total = xs_ref.shape[0] // (EXPERT_BLOCK * ROW_TILES)
        lax.fori_loop(used, total, lambda j, c: (tail(j).start(), c)[1], 0)
        lax.fori_loop(used, total, lambda j, c: (tail(j).wait(), c)[1], 0)

    def copy(t, k):
        return _row_copy(h2_ref, t, xs_ref, dest_ref[k, t], sem)

    def issue(t, carry):
        for k in range(TOP_K):
            copy(t, k).start(priority=k % 2)
        return carry

    def drain(t, carry):
        for k in range(TOP_K):
            copy(t, k).wait()
        return carry

    lax.fori_loop(0, tm, issue, 0, unroll=ISSUE_UNROLL)
    lax.fori_loop(0, tm, drain, 0, unroll=ISSUE_UNROLL)


def _dispatch(pstart, pends, dest, h2, n_rows):
    t = h2.shape[0] // ROW_TILES
    tm = _pick(t, 256)
    return pl.pallas_call(
        _dispatch_kernel,
        out_shape=jax.ShapeDtypeStruct((n_rows * ROW_TILES, LANES), F32),
        grid_spec=pltpu.PrefetchScalarGridSpec(
            num_scalar_prefetch=2,
            grid=(t // tm,),
            in_specs=[pl.BlockSpec((SUBLANES, tm), lambda i, ps, pe: (0, i), memory_space=pltpu.SMEM),
                      pl.BlockSpec((tm * ROW_TILES, LANES), lambda i, ps, pe: (i, 0))],
            out_specs=pl.BlockSpec(memory_space=pl.ANY),
            scratch_shapes=[pltpu.VMEM((EXPERT_BLOCK * ROW_TILES, LANES), F32), pltpu.SemaphoreType.DMA],
        ),
        compiler_params=_params(("arbitrary",)),
        name="dispatch",
    )(pstart, pends, dest, h2)


def _combine_kernel(n_prompt_tiles, dest_ref, dest_next_ref, gate_ref, x1_ref, nw_ref, yb_ref, op_ref, os_ref,
                    buf_ref, sem):
    tm = x1_ref.shape[0]
    i = pl.program_id(0)

    def copy(d_ref, slot, t, k):
        return _row_copy(yb_ref, d_ref[k, t], buf_ref.at[slot, k], t, sem.at[slot])

    def issue(d_ref, slot):
        def body(t, carry):
            for k in range(TOP_K):
                copy(d_ref, slot, t, k).start(priority=k % 2)
            return carry
        lax.fori_loop(0, tm, body, 0, unroll=ISSUE_UNROLL)

    def drain(d_ref, slot):
        def body(t, carry):
            for k in range(TOP_K):
                copy(d_ref, slot, t, k).wait()
            return carry
        lax.fori_loop(0, tm, body, 0, unroll=ISSUE_UNROLL)

    @pl.when(i == 0)
    def _():
        issue(dest_ref, 0)

    def step(slot):
        @pl.when(i + 1 < pl.num_programs(0))
        def _():
            issue(dest_next_ref, 1 - slot)

        drain(dest_ref, slot)
        gates = gate_ref[...]
        moe = _load_row_tiles(buf_ref, tm, slot, 0) * gates[:, 0:1]
        for k in range(1, TOP_K):
            moe = moe + _load_row_tiles(buf_ref, tm, slot, k) * gates[:, k:k + 1]
        x2 = x1_ref[...] + moe
        ms = jnp.mean(x2 * x2, axis=-1, keepdims=True)
        out = x2 * lax.rsqrt(ms + EPS) * nw_ref[...]

        @pl.when(i < n_prompt_tiles)
        def _():
            op_ref[...] = out

        @pl.when(i >= n_prompt_tiles)
        def _():
            os_ref[...] = out

    @pl.when(i % 2 == 0)
    def _():
        step(0)

    @pl.when(i % 2 == 1)
    def _():
        step(1)


def _combine(dest, gates, x1, norm_final, yb, n_prompt):
    t = x1.shape[0]
    tm = _pick(math.gcd(n_prompt, t - n_prompt), 256)
    npt = n_prompt // tm
    return pl.pallas_call(
        functools.partial(_combine_kernel, npt),
        out_shape=(jax.ShapeDtypeStruct((n_prompt, D_MODEL), F32),
                   jax.ShapeDtypeStruct((t - n_prompt, D_MODEL), F32)),
        grid=(t // tm,),
        in_specs=[pl.BlockSpec((SUBLANES, tm), lambda i: (0, i), memory_space=pltpu.SMEM),
                  pl.BlockSpec((SUBLANES, tm), lambda i: (0, jnp.minimum(i + 1, t // tm - 1)),
                               memory_space=pltpu.SMEM),
                  pl.BlockSpec((tm, LANES), lambda i: (i, 0)),
                  pl.BlockSpec((tm, D_MODEL), lambda i: (i, 0)),
                  pl.BlockSpec((1, D_MODEL), lambda i: (0, 0)),
                  pl.BlockSpec(memory_space=pl.ANY)],
        out_specs=_split_rows(tm, npt, D_MODEL),
        scratch_shapes=[pltpu.VMEM((2, TOP_K, tm * ROW_TILES, LANES), F32), pltpu.SemaphoreType.DMA((2,))],
        compiler_params=_params(("arbitrary",)),
        name="combine",
    )(dest, dest, gates, x1, norm_final, yb)


def _expert_kernel(be_ref, nv_ref, next_ref, xs_ref, bgu_ref, bd_ref, wgu_hbm, wd_hbm, yb_ref,
                   wgu_f, wd_f, wgu_b, wd_b, sem):
    i = pl.program_id(0)

    def fetch(e):
        return (pltpu.make_async_copy(wgu_hbm.at[e], wgu_f, sem.at[0]),
                pltpu.make_async_copy(wd_hbm.at[e], wd_f, sem.at[1]))

    @pl.when(i == 0)
    def _():
        for copy in fetch(be_ref[0]):
            copy.start()

    first = jnp.logical_or(i == 0, be_ref[i] != be_ref[jnp.maximum(i - 1, 0)])

    @pl.when(jnp.logical_and(first, i < nv_ref[0]))
    def _():
        for copy in fetch(be_ref[i]):
            copy.wait()
        wgu_b[...] = wgu_f[...].astype(BF16)
        wd_b[...] = wd_f[...].astype(BF16)

        @pl.when(next_ref[i] >= 0)
        def _():
            for copy in fetch(next_ref[i]):
                copy.start()

    @pl.when(i < nv_ref[0])
    def _():
        gu = _dot(_load_row_tiles(xs_ref, EXPERT_BLOCK).astype(BF16), wgu_b[...]) + bgu_ref[0]
        glu = jnp.minimum(gu[:, :D_FF], SWIGLU_LIMIT)
        lin = jnp.clip(gu[:, D_FF:], -SWIGLU_LIMIT, SWIGLU_LIMIT)
        act = glu * jax.nn.sigmoid(SWIGLU_ALPHA * glu) * (lin + 1.0)
        _store_row_tiles(yb_ref, _dot(act.astype(BF16), wd_b[...]) + bd_ref[0])

    @pl.when(i >= nv_ref[0])
    def _():
        yb_ref[...] = jnp.zeros_like(yb_ref)


def _experts(block_expert, n_valid, next_expert, xs, w_gu, b_gu, w_d, b_d):
    rows = EXPERT_BLOCK * ROW_TILES
    nb = xs.shape[0] // rows
    return pl.pallas_call(
        _expert_kernel,
        out_shape=jax.ShapeDtypeStruct(xs.shape, F32),
        grid_spec=pltpu.PrefetchScalarGridSpec(
            num_scalar_prefetch=3,
            grid=(nb,),
            in_specs=[pl.BlockSpec((rows, LANES), lambda i, be, nv, nx: (jnp.minimum(i, nv[0] - 1), 0)),
                      pl.BlockSpec((1, 1, 2 * D_FF), lambda i, be, nv, nx: (be[i], 0, 0)),
                      pl.BlockSpec((1, 1, D_MODEL), lambda i, be, nv, nx: (be[i], 0, 0)),
                      pl.BlockSpec(memory_space=pl.ANY),
                      pl.BlockSpec(memory_space=pl.ANY)],
            out_specs=pl.BlockSpec((rows, LANES), lambda i, be, nv, nx: (i, 0)),
            scratch_shapes=[pltpu.VMEM((D_MODEL, 2 * D_FF), F32), pltpu.VMEM((D_FF, D_MODEL), F32),
                            pltpu.VMEM((D_MODEL, 2 * D_FF), BF16), pltpu.VMEM((D_FF, D_MODEL), BF16),
                            pltpu.SemaphoreType.DMA((2,))],
        ),
        compiler_params=_params(("arbitrary",)),
        name="experts",
    )(block_expert, n_valid, next_expert, xs, b_gu, b_d, w_gu, w_d)


def _expert_layout(counts, n_blocks):
    counts = counts.astype(jnp.int32)
    padded = (counts + EXPERT_BLOCK - 1) // EXPERT_BLOCK * EXPERT_BLOCK
    pends = jnp.cumsum(padded)
    pstart = pends - padded
    block_start = jnp.arange(n_blocks, dtype=jnp.int32) * EXPERT_BLOCK
    block_expert = jnp.minimum(jnp.sum(pends[None, :] <= block_start[:, None], axis=1), N_EXPERTS - 1)
    n_valid = (pends[-1:] // EXPERT_BLOCK).astype(jnp.int32)
    seg_end = jnp.sum(jnp.where(block_expert[:, None] == jnp.arange(N_EXPERTS), pends[None, :], 0), axis=1)
    after = seg_end // EXPERT_BLOCK
    next_expert = jnp.where(after < n_valid[0], block_expert[jnp.minimum(after, n_blocks - 1)], -1)
    return (pstart.astype(jnp.int32), pends.astype(jnp.int32), block_expert.astype(jnp.int32), n_valid,
            next_expert.astype(jnp.int32))


def _permute_w_in(w_in):
    sizes = [RET_QK, RET_QK, RET_V, RET_V, SSM_INNER, CONV_DIM, SSM_HEADS, D_MODEL, D_MODEL]
    offs = [0]
    for s in sizes:
        offs.append(offs[-1] + s)
    q, k, v, g_ret, z, xbc, dt, g_a, g_b = [w_in[:, offs[i]:offs[i + 1]] for i in range(len(sizes))]
    gn = SSM_GROUPS * SSM_DSTATE
    xs, bm, cm = xbc[:, :SSM_INNER], xbc[:, SSM_INNER:SSM_INNER + gn], xbc[:, SSM_INNER + gn:]
    dt_pad = jnp.pad(dt, ((0, 0), (0, _ColsB.TOTAL - _ColsB.DT - SSM_HEADS)))
    w_a = jnp.concatenate([v, g_ret, g_a, g_b, q, k], axis=1).astype(BF16)
    w_b = jnp.concatenate([z, xs, bm, cm, dt_pad], axis=1).astype(BF16)
    return w_a, w_b


def _forward(x_prompt, x_sample, state_ret, state_ssm, state_conv, norm_mix, w_in, ret_norm, w_out_ret,
             conv_w, conv_b, dt_bias, a_log, d_skip, ssm_norm, w_out_ssm, w_out, norm_ffn,
             w_router, b_router, w_gate_up, b_gate_up, w_down, b_down, norm_final):
    bp, lp, _ = x_prompt.shape
    bs, ls, _ = x_sample.shape
    assert lp % ROWS == 0 and bs % SAMPLE_PER_TILE == 0 and ls * SAMPLE_PER_TILE == ROWS
    tp, ts = bp * lp, bs * ls
    t_all = tp + ts
    xp, xs_in = x_prompt.reshape(tp, D_MODEL), x_sample.reshape(ts, D_MODEL)

    w_a, w_b = _permute_w_in(w_in[0])
    norm_w = norm_mix[0].reshape(1, D_MODEL)
    proj = _inproj(xp, xs_in, norm_w, w_a)
    proj_ssd_s = _inproj(None, xs_in, norm_w, w_b)

    a_s, ret_s = _retention_sample(proj, tp, bs, ls, state_ret[0], ret_norm[0])

    sp = _ssd_params(conv_w[0], conv_b[0], dt_bias[0], a_log[0], d_skip[0], ssm_norm[0])
    y_p, ssm_p, conv_p, a_p, ret_p = _mixer_prompt(xp, norm_w, w_b, proj, bp, lp, sp, ret_norm[0])
    y_s, ssm_s, conv_s = _ssd_sample(proj_ssd_s, bs, ls, sp, state_ssm[0], state_conv[0])

    w_router_pad = jnp.pad(w_router[0], ((0, 0), (0, LANES - N_EXPERTS))).astype(BF16)
    b_router_pad = jnp.pad(b_router[0], (0, LANES - N_EXPERTS)).reshape(1, LANES)
    x1, h2, logits = _post(a_p, a_s, y_p, y_s, proj, xp, xs_in, w_out_ret[0].astype(BF16), w_out_ssm[0].astype(BF16),
                           w_out[0].astype(BF16), norm_ffn[0].reshape(1, D_MODEL), w_router_pad, b_router_pad)

    meta, gates, counts = _route(logits)
    n_blocks = -(-(t_all * TOP_K + N_EXPERTS * (EXPERT_BLOCK - 1)) // EXPERT_BLOCK)
    pstart, pends, block_expert, n_valid, next_expert = _expert_layout(counts[0, :N_EXPERTS], n_blocks)
    hot = meta[:TOP_K, :, None] == jnp.arange(N_EXPERTS, dtype=jnp.int32)
    seg_start = jnp.sum(jnp.where(hot, pstart, 0), axis=-1)
    dest = jnp.pad(seg_start + meta[SUBLANES:SUBLANES + TOP_K], ((0, SUBLANES - TOP_K), (0, 0)))
    xs = _dispatch(pstart, pends, dest, h2, n_blocks * EXPERT_BLOCK)
    yb = _experts(block_expert, n_valid, next_expert, xs, w_gate_up[0],
                  b_gate_up[0].reshape(N_EXPERTS, 1, 2 * D_FF), w_down[0], b_down[0].reshape(N_EXPERTS, 1, D_MODEL))
    out_p, out_s = _combine(dest, gates, x1, norm_final.reshape(1, D_MODEL), yb, tp)

    shape_s = (1, bs, SSM_HEADS, SSM_HEADDIM, SSM_DSTATE)
    shape_p = (1, bp, SSM_HEADS, SSM_HEADDIM, SSM_DSTATE)
    return (out_p.reshape(bp, lp, D_MODEL), out_s.reshape(bs, ls, D_MODEL),
            ret_p[None], ret_s[None], ssm_p.reshape(shape_p), ssm_s.reshape(shape_s),
            conv_p[None], conv_s[None])


def kernel(x_prompt, x_sample, state_ret, state_ssm, state_conv, norm_mix, w_in, ret_norm, w_out_ret, conv_w, conv_b, dt_bias, a_log, d_skip, ssm_norm, w_out_ssm, w_out, norm_ffn, w_router, b_router, w_gate_up, b_gate_up, w_down, b_down, norm_final):
    return _forward(x_prompt, x_sample, state_ret, state_ssm, state_conv, norm_mix, w_in, ret_norm, w_out_ret,
                    conv_w, conv_b, dt_bias, a_log, d_skip, ssm_norm, w_out_ssm, w_out, norm_ffn,
                    w_router, b_router, w_gate_up, b_gate_up, w_down, b_down, norm_final)
```

```python
import functools
import math

import jax
import jax.numpy as jnp
from jax import lax
from jax.experimental import pallas as pl
from jax.experimental.pallas import tpu as pltpu

F32 = jnp.float32
BF16 = jnp.bfloat16

D_MODEL = 1024
PAST_LEN = 16384
RET_HEADS = 4
RET_DK = 128
RET_DV = 256
RET_QK = RET_HEADS * RET_DK
RET_V = RET_HEADS * RET_DV
ROPE_BASE = 10000.0
SSM_INNER = 2 * D_MODEL
SSM_HEADDIM = 64
SSM_HEADS = SSM_INNER // SSM_HEADDIM
SSM_GROUPS = 4
SSM_DSTATE = 128
HEADS_PER_GROUP = SSM_HEADS // SSM_GROUPS
GROUP_INNER = SSM_INNER // SSM_GROUPS
CONV_WIDTH = 4
CONV_DIM = SSM_INNER + 2 * SSM_GROUPS * SSM_DSTATE
CHUNK = 128
N_EXPERTS = 32
TOP_K = 4
D_FF = D_MODEL
SWIGLU_LIMIT = 7.0
SWIGLU_ALPHA = 1.702
EPS = 1e-6

LANES = 128
SUBLANES = 8
ROWS = 128
SAMPLE_PER_TILE = 16
EXPERT_BLOCK = 512
ROW_TILES = D_MODEL // LANES
ISSUE_UNROLL = 4
VMEM_LIMIT = 56 * 1024 * 1024


class _ColsA:
    V = 0
    GRET = V + RET_V
    GA = GRET + RET_V
    GB = GA + D_MODEL
    Q = GB + D_MODEL
    K = Q + RET_QK
    TOTAL = K + RET_QK


class _ColsB:
    Z = 0
    XS = Z + SSM_INNER
    B = XS + SSM_INNER
    C = B + SSM_GROUPS * SSM_DSTATE
    DT = C + SSM_GROUPS * SSM_DSTATE
    TOTAL = DT + 2 * LANES


def _pick(n, target):
    best = None
    for c in range(LANES, target + 1, LANES):
        if n % c == 0:
            best = c
    assert best is not None, (n, target)
    return best


def _params(sem, **kw):
    return pltpu.CompilerParams(dimension_semantics=sem, vmem_limit_bytes=VMEM_LIMIT, **kw)


def _dot(a, b):
    return jnp.dot(a, b, preferred_element_type=F32)


def _dot_nt(a, b):
    return lax.dot_general(a, b, (((1,), (1,)), ((), ())), preferred_element_type=F32)


def _dot_tn(a, b):
    return lax.dot_general(a, b, (((0,), (0,)), ((), ())), preferred_element_type=F32)


def _silu(x):
    return x * jax.nn.sigmoid(x)


def _store_row_tiles(ref, value, *lead):
    n = value.shape[0]
    for s in range(ROW_TILES):
        ref[(*lead, pl.ds(s, n, stride=ROW_TILES), slice(None))] = value[:, s * LANES:(s + 1) * LANES]


def _load_row_tiles(ref, n, *lead):
    return jnp.concatenate([ref[(*lead, pl.ds(s, n, stride=ROW_TILES), slice(None))] for s in range(ROW_TILES)],
                           axis=1)


def _inproj_kernel(n_prompt_tiles, xp_ref, xs_ref, nw_ref, w_ref, o_ref, h_ref):
    @pl.when(pl.program_id(1) == 0)
    def _():
        x = jnp.where(pl.program_id(0) < n_prompt_tiles, xp_ref[...], xs_ref[...])
        ms = jnp.mean(x * x, axis=-1, keepdims=True)
        h_ref[...] = (x * lax.rsqrt(ms + EPS) * nw_ref[...]).astype(BF16)

    o_ref[...] = _dot(h_ref[...], w_ref[...])


def _split_rows(tm, npt, width):
    prompt = pl.BlockSpec((tm, width), lambda i, *_: (jnp.minimum(i, npt - 1), 0))
    sample = pl.BlockSpec((tm, width), lambda i, *_: (jnp.maximum(i - npt, 0), 0))
    return prompt, sample


def _inproj(xp, xs, norm_w, w):
    total = w.shape[1]
    t = xs.shape[0] + (0 if xp is None else xp.shape[0])
    tm = _pick(xs.shape[0] if xp is None else math.gcd(xp.shape[0], xs.shape[0]), 1024)
    npt = 0 if xp is None else xp.shape[0] // tm
    tn = _pick(total, 2560)
    assert tn % (2 * LANES) == 0
    x_specs = list(_split_rows(tm, npt, D_MODEL)) if npt else [pl.BlockSpec((tm, D_MODEL), lambda i, j: (i, 0))] * 2
    return pl.pallas_call(
        functools.partial(_inproj_kernel, npt),
        out_shape=jax.ShapeDtypeStruct((t, total), F32),
        grid=(t // tm, total // tn),
        in_specs=[
            *x_specs,
            pl.BlockSpec((1, D_MODEL), lambda i, j: (0, 0)),
            pl.BlockSpec((D_MODEL, tn), lambda i, j: (0, j)),
        ],
        out_specs=pl.BlockSpec((tm, tn), lambda i, j: (i, j)),
        scratch_shapes=[pltpu.VMEM((tm, D_MODEL), BF16)],
        compiler_params=_params(("parallel", "arbitrary")),
        name="in_proj",
    )(xs if xp is None else xp, xs, norm_w, w)


def _ret_log_decay():
    return jnp.log(1.0 - 2.0 ** (-5.0 - jnp.arange(RET_HEADS, dtype=F32)))


def _ret_tables(c):
    lg = _ret_log_decay()
    r = jnp.arange(ROWS)
    t = (r % c).astype(F32)
    seg = r // c
    diff = t[:, None] - t[None, :]
    ok = (seg[:, None] == seg[None, :]) & (diff >= 0)
    dm = jnp.where(ok[None], jnp.exp(lg[:, None, None] * jnp.maximum(diff, 0.0)[None]), 0.0)
    qd = jnp.exp(lg[:, None] * (t[None, :] + 1.0))
    kd = jnp.exp(lg[:, None] * (c - 1.0 - t[None, :]))
    cd = jnp.exp(lg * c)
    qd = jnp.broadcast_to(qd[:, :, None], (RET_HEADS, ROWS, LANES))
    kd = jnp.broadcast_to(kd[:, :, None], (RET_HEADS, ROWS, LANES))
    cd = jnp.broadcast_to(cd[:, None, None], (RET_HEADS, 1, RET_DV))
    return dm, qd, kd, cd


def _rope_tables(pos):
    half = RET_DK // 2
    inv_freq = 1.0 / (ROPE_BASE ** jnp.linspace(0.0, 1.0, half, dtype=F32))
    ang = pos.astype(F32)[:, None] * inv_freq[None, :]
    cos, sin = jnp.cos(ang), jnp.sin(ang)
    return jnp.concatenate([cos, cos], -1), jnp.concatenate([-sin, sin], -1)


def _block_mask(dtype):
    r = jnp.arange(ROWS) // (ROWS // SAMPLE_PER_TILE)
    b = jnp.arange(SAMPLE_PER_TILE * LANES) // LANES
    return (r[:, None] == b[None, :]).astype(dtype)


def _ret_common(q, k, v, g, cosf, sinf, dm, qd, kd, nw):
    qr = q * cosf + pltpu.roll(q, RET_DK // 2, 1) * sinf
    kr = (k * cosf + pltpu.roll(k, RET_DK // 2, 1) * sinf) * (RET_DK ** -0.5)
    vb = v.astype(BF16)
    scores = _dot_nt(qr.astype(BF16), kr.astype(BF16)) * dm
    o_intra = _dot(scores.astype(BF16), vb)
    q_state = (qr * qd).astype(BF16)
    k_state = kr * kd

    def finish(o):
        ms = jnp.mean(o * o, axis=-1, keepdims=True)
        on = o * lax.rsqrt(ms + EPS) * nw
        return (_silu(g) * on).astype(BF16)

    return o_intra, q_state, k_state, vb, finish


def _ret_prompt_step(q_ref, k_ref, v_ref, g_ref, cos_ref, sin_ref, dm_ref, qd_ref, kd_ref, cd_ref, nw_ref,
                     a_ref, s_ref):
    cosf, sinf = cos_ref[...], sin_ref[...]
    for h in range(RET_HEADS):
        dk = slice(h * RET_DK, (h + 1) * RET_DK)
        dv = slice(h * RET_DV, (h + 1) * RET_DV)
        o_intra, q_state, k_state, vb, finish = _ret_common(
            q_ref[:, dk], k_ref[:, dk], v_ref[:, dv], g_ref[:, dv], cosf, sinf, dm_ref[h], qd_ref[h], kd_ref[h],
            nw_ref[h])
        s = s_ref[0, h]
        a_ref[:, dv] = finish(o_intra + _dot(q_state, s.astype(BF16)))
        s_ref[0, h] = s * cd_ref[h] + _dot_tn(k_state.astype(BF16), vb)


def _ret_sample_kernel(q_ref, k_ref, v_ref, g_ref, cos_ref, sin_ref, dm_ref, qd_ref, kd_ref, cd_ref, nw_ref,
                       bm_ref, bmt_ref, s0_ref, a_ref, s_ref):
    o_intra, q_state, k_state, vb, finish = _ret_common(
        q_ref[...], k_ref[...], v_ref[...], g_ref[...], cos_ref[...], sin_ref[...], dm_ref[0], qd_ref[0], kd_ref[0],
        nw_ref[0])
    nb = SAMPLE_PER_TILE
    s0 = s0_ref[:, 0]
    q_bd = jnp.concatenate([q_state] * nb, axis=1) * bm_ref[...]
    a_ref[...] = finish(o_intra + _dot(q_bd, s0.reshape(nb * RET_DK, RET_DV).astype(BF16)))
    k_t = k_state.T.astype(BF16)
    k_bd_t = jnp.concatenate([k_t] * nb, axis=0) * bmt_ref[...]
    ds = _dot(k_bd_t, vb).reshape(nb, RET_DK, RET_DV)
    s_ref[:, 0] = s0 * cd_ref[0] + ds


def _ret_specs(row_block):
    def at(width, off):
        return lambda *ids: (row_block(*ids)[0], off // width + row_block(*ids)[1])

    def head(*shape):
        return pl.BlockSpec((1,) + shape, lambda *ids: (row_block(*ids)[1],) + (0,) * len(shape))

    return [
        pl.BlockSpec((ROWS, RET_DK), at(RET_DK, _ColsA.Q)),
        pl.BlockSpec((ROWS, RET_DK), at(RET_DK, _ColsA.K)),
        pl.BlockSpec((ROWS, RET_DV), at(RET_DV, _ColsA.V)),
        pl.BlockSpec((ROWS, RET_DV), at(RET_DV, _ColsA.GRET)),
    ], head


def _retention_sample_call(proj, row0, bsz, seq, state, ret_norm):
    rb0 = row0 // ROWS
    dm, qd, kd, cd = _ret_tables(seq)
    pos = PAST_LEN + jnp.arange(seq, dtype=jnp.int32)
    cosf, sinf = _rope_tables(jnp.tile(pos, SAMPLE_PER_TILE))
    rb = lambda t, h: (rb0 + t, h)
    specs, head = _ret_specs(rb)
    full = lambda *shape: pl.BlockSpec(shape, lambda t, h: (0,) * len(shape))
    specs += [
        full(ROWS, RET_DK), full(ROWS, RET_DK),
        head(ROWS, ROWS), head(ROWS, LANES), head(ROWS, LANES), head(1, RET_DV), head(1, RET_DV),
        full(ROWS, SAMPLE_PER_TILE * LANES), full(SAMPLE_PER_TILE * LANES, ROWS),
        pl.BlockSpec((SAMPLE_PER_TILE, 1, RET_DK, RET_DV), lambda t, h: (t, h, 0, 0)),
    ]
    bm = _block_mask(BF16)
    operands = (proj, proj, proj, proj, cosf, sinf, dm, qd, kd, cd, ret_norm.reshape(RET_HEADS, 1, RET_DV),
                bm, bm.T, state)
    out_shapes = (jax.ShapeDtypeStruct((bsz * seq, RET_V), BF16), jax.ShapeDtypeStruct(state.shape, F32))
    out_specs = (pl.BlockSpec((ROWS, RET_DV), lambda t, h: (t, h)),
                 pl.BlockSpec((SAMPLE_PER_TILE, 1, RET_DK, RET_DV), lambda t, h: (t, h, 0, 0)))
    return specs, operands, out_shapes, out_specs


def _softplus(x):
    return jnp.maximum(x, 0.0) + jnp.log1p(jnp.exp(-jnp.abs(x)))


def _conv_piece(x, pred, w_ref, b_ref):
    width = x.shape[1]
    x3 = x.reshape(ROWS // SUBLANES, SUBLANES, width)
    t8 = lax.broadcasted_iota(jnp.int32, x3.shape, 1)
    acc = b_ref[...].reshape(1, 1, width)
    for i in range(CONV_WIDTH):
        s = CONV_WIDTH - 1 - i
        tap = x3 if s == 0 else pltpu.roll(jnp.where(t8 >= SUBLANES - s, pred, x3), s, 1)
        acc = acc + tap * w_ref[pl.ds(i, 1), :].reshape(1, 1, width)
    return _silu(acc).reshape(ROWS, width)


def _ssd_decay_terms(c, dt_pre, a_log, ltri):
    dt = _softplus(dt_pre)
    d_a = dt * (-jnp.exp(a_log))
    hi = d_a.astype(BF16)
    r1 = d_a - hi.astype(F32)
    mid = r1.astype(BF16)
    lo = (r1 - mid.astype(F32)).astype(BF16)
    cum = _dot(ltri, hi) + _dot(ltri, mid) + _dot(ltri, lo)
    cum3 = cum.reshape(ROWS // c, c, LANES)
    c_last = jnp.broadcast_to(cum3[:, c - 1:c, :], cum3.shape).reshape(ROWS, LANES)
    to_end = jnp.exp(c_last - cum) * dt
    e_cum = jnp.exp(cum)
    return cum, to_end, e_cum, cum.T, dt.T


def _ssd_group(c, terms, head0, z_ref, dsk_ref, nrm_ref, xs, bm, cm, y_ref, state_io, zero=None):
    cum, to_end, e_cum, cum_t, dt_t = terms
    b_b, c_b = bm.astype(BF16), cm.astype(BF16)
    cb = _dot_nt(c_b, b_b)
    ri = lax.broadcasted_iota(jnp.int32, (ROWS, ROWS), 0)
    ci = lax.broadcasted_iota(jnp.int32, (ROWS, ROWS), 1)
    shift = c.bit_length() - 1
    causal = (ci <= ri) & (jnp.right_shift(ri, shift) == jnp.right_shift(ci, shift))
    lo_half = lax.broadcasted_iota(jnp.int32, (ROWS, LANES), 1) < SSM_HEADDIM

    def col(a, h):
        return jnp.broadcast_to(a[:, h:h + 1], (ROWS, LANES))

    def weights(h):
        seg = col(cum, h) - jnp.broadcast_to(cum_t[h:h + 1, :], (ROWS, ROWS))
        decay = jnp.exp(jnp.where(causal, seg, -jnp.inf))
        return (cb * decay * jnp.broadcast_to(dt_t[h:h + 1, :], (ROWS, ROWS))).astype(BF16)

    pieces = []
    for m in range(HEADS_PER_GROUP // 2):
        ha, hb = head0 + 2 * m, head0 + 2 * m + 1
        xp = xs[:, m * LANES:(m + 1) * LANES]
        w2 = jnp.concatenate([weights(ha), weights(hb)], axis=1)
        x2 = jnp.concatenate([jnp.where(lo_half, xp, 0.0), jnp.where(lo_half, 0.0, xp)], axis=0).astype(BF16)
        y = _dot(w2, x2)
        xw = (xp * jnp.where(lo_half, col(to_end, ha), col(to_end, hb))).astype(BF16)
        y_state = state_io(m, ha, c_b, xw, b_b, e_cum)
        y = y + y_state * jnp.where(lo_half, col(e_cum, ha), col(e_cum, hb))
        pieces.append(y + xp * dsk_ref[:, m * LANES:(m + 1) * LANES])
    if zero is not None:
        pieces[0] = jnp.concatenate([pieces[0][:SUBLANES] + zero, pieces[0][SUBLANES:]], axis=0)
    yg = jnp.concatenate(pieces, axis=1) * _silu(z_ref[...])
    ms = jnp.mean(yg * yg, axis=-1, keepdims=True)
    y_ref[...] = (yg * lax.rsqrt(ms + EPS) * nrm_ref[...]).astype(BF16)


def _pair_decay(e_cum, row, ha):
    top = lax.broadcasted_iota(jnp.int32, (2 * SSM_HEADDIM, SSM_DSTATE), 0) < SSM_HEADDIM
    ea = jnp.broadcast_to(e_cum[row:row + 1, ha:ha + 1], top.shape)
    eb = jnp.broadcast_to(e_cum[row:row + 1, ha + 1:ha + 2], top.shape)
    return jnp.where(top, ea, eb)


def _mixer_prompt_kernel(x0_ref, xn_ref, nw_ref, w_ref, cwx_ref, cbx_ref, cwb_ref, cbb_ref, cwc_ref, cbc_ref,
                         dtb_ref, alog_ref, dsk_ref, nrm_ref, ltri_ref, *rest):
    ret_in, (y_ref, hs_ref, px_ref, pb_ref, pc_ref, a_ref, s_ref, proj_ref) = rest[:11], rest[11:]
    c = pl.program_id(1)

    def normed(x_ref):
        x = x_ref[...]
        ms = jnp.mean(x * x, axis=-1, keepdims=True)
        return (x * lax.rsqrt(ms + EPS) * nw_ref[...]).astype(BF16)

    def project(x_ref, slot):
        proj_ref[slot] = _dot(normed(x_ref), w_ref[...])

    @pl.when(c == 0)
    def _():
        hs_ref[...] = jnp.zeros_like(hs_ref)
        px_ref[...] = jnp.zeros_like(px_ref)
        pb_ref[...] = jnp.zeros_like(pb_ref)
        pc_ref[...] = jnp.zeros_like(pc_ref)
        s_ref[...] = jnp.zeros_like(s_ref)
        project(x0_ref, 0)

    def conv(x_ref, p_ref, w_ref, bias_ref):
        x = x_ref[...]
        x3 = x.reshape(ROWS // SUBLANES, SUBLANES, x.shape[1])
        pred = jnp.concatenate([p_ref[...], x3[:-1]], axis=0)
        out = _conv_piece(x, pred, w_ref, bias_ref)
        p_ref[0] = x3[-1]
        return out

    def step(cur_slot, next_slot):
        h_next = normed(xn_ref)
        bounds = [0, 5 * 256, 10 * 256, 15 * 256, _ColsB.TOTAL]

        def project_slice(g):
            lo, hi = bounds[g], bounds[g + 1]
            part = _dot(h_next, w_ref[:, lo:hi])
            proj_ref[next_slot, :, lo:hi] = part
            bits = pltpu.bitcast(part[ROWS - SUBLANES:, hi - lo - LANES:], jnp.uint32)
            sixteen = jnp.uint32(16)
            return pltpu.bitcast(lax.shift_right_logical(lax.shift_right_logical(bits, sixteen), sixteen), F32)

        _ret_prompt_step(*ret_in, a_ref, s_ref)
        cur = proj_ref.at[cur_slot]
        gn = SSM_GROUPS * SSM_DSTATE
        z_ref = cur.at[:, _ColsB.Z:_ColsB.Z + SSM_INNER]
        xs_ref = cur.at[:, _ColsB.XS:_ColsB.XS + SSM_INNER]
        b_ref = cur.at[:, _ColsB.B:_ColsB.B + gn]
        c_ref = cur.at[:, _ColsB.C:_ColsB.C + gn]
        dt_ref = cur.at[:, _ColsB.DT:_ColsB.DT + LANES]
        terms = _ssd_decay_terms(CHUNK, dt_ref[...] + dtb_ref[...], alog_ref[...], ltri_ref[...])
        for g in range(SSM_GROUPS):
            def cols(ref, width, g=g):
                return ref.at[..., g * width:(g + 1) * width]

            gi, n = GROUP_INNER, SSM_DSTATE
            xs = conv(cols(xs_ref, gi), cols(px_ref, gi), cols(cwx_ref, gi), cols(cbx_ref, gi))
            bm = conv(cols(b_ref, n), cols(pb_ref, n), cols(cwb_ref, n), cols(cbb_ref, n))
            cm = conv(cols(c_ref, n), cols(pc_ref, n), cols(cwc_ref, n), cols(cbc_ref, n))
            hs_g = hs_ref.at[0, g * gi:(g + 1) * gi, :]

            def state_io(m, ha, c_b, xw, b_b, e_cum, hs_g=hs_g):
                rows = pl.ds(m * 2 * SSM_HEADDIM, 2 * SSM_HEADDIM)
                h = hs_g[rows, :]
                hs_g[rows, :] = h * _pair_decay(e_cum, ROWS - 1, ha) + _dot_tn(xw, b_b)
                return _dot_nt(c_b, h.astype(BF16))

            _ssd_group(CHUNK, terms, g * HEADS_PER_GROUP, cols(z_ref, gi), cols(dsk_ref, gi), cols(nrm_ref, gi),
                       xs, bm, cm, cols(y_ref, gi), state_io, zero=project_slice(g))

    @pl.when(c % 2 == 0)
    def _():
        step(0, 1)

    @pl.when(c % 2 == 1)
    def _():
        step(1, 0)


def _ssd_sample_kernel(seq, z_ref, xs_ref, b_ref, c_ref, dt_ref, cwx_ref, cbx_ref, cwb_ref, cbb_ref, cwc_ref,
                       cbc_ref, dtb_ref, alog_ref, dsk_ref, nrm_ref, ltri_ref, bmask_ref, px_ref, pb_ref, pc_ref,
                       h0_ref, y_ref, hs_ref, csx_ref, csb_ref, csc_ref):
    nb = SAMPLE_PER_TILE

    def conv(x_ref, p_ref, w_ref, bias_ref, cs_ref):
        x = x_ref[...]
        cs_ref[...] = x.reshape(nb, seq, x.shape[1])
        return _conv_piece(x, p_ref[...], w_ref, bias_ref)

    xs = conv(xs_ref, px_ref, cwx_ref, cbx_ref, csx_ref)
    bm = conv(b_ref, pb_ref, cwb_ref, cbb_ref, csb_ref)
    cm = conv(c_ref, pc_ref, cwc_ref, cbc_ref, csc_ref)
    bmask = bmask_ref[...]

    def state_io(m, ha, c_b, xw, b_b, e_cum):
        rows = pl.ds(m * 2 * SSM_HEADDIM, 2 * SSM_HEADDIM)
        c_bd = jnp.concatenate([c_b] * nb, axis=1) * bmask
        b_bd = jnp.concatenate([b_b] * nb, axis=1) * bmask
        hs = [h0_ref[i, rows, :] for i in range(nb)]
        h_cat = jnp.concatenate(hs, axis=1).astype(BF16)
        dh = _dot_tn(xw, b_bd)
        for i in range(nb):
            decay = _pair_decay(e_cum, i * seq + seq - 1, ha)
            hs_ref[i, rows, :] = hs[i] * decay + dh[:, i * SSM_DSTATE:(i + 1) * SSM_DSTATE]
        return _dot_nt(c_bd, h_cat)

    shift = jnp.bitwise_and(LANES - HEADS_PER_GROUP * pl.program_id(1), LANES - 1)
    dt_pre = pltpu.roll(dt_ref[...] + dtb_ref[...], shift, 1)
    a_log = pltpu.roll(jnp.broadcast_to(alog_ref[...], (SUBLANES, LANES)), shift, 1)[:1]
    terms = _ssd_decay_terms(seq, dt_pre, a_log, ltri_ref[...])
    _ssd_group(seq, terms, 0, z_ref, dsk_ref, nrm_ref, xs, bm, cm, y_ref, state_io)


def _ssd_tables(c):
    r = jnp.arange(ROWS)
    ltri = ((r[:, None] >= r[None, :]) & ((r[:, None] // c) == (r[None, :] // c))).astype(BF16)
    return ltri


def _ssd_specs(row_block):
    def at(width, off):
        return lambda *ids: (row_block(*ids)[0], off // width + row_block(*ids)[1])

    def grp(rows, width):
        return pl.BlockSpec((rows, width), lambda *ids: (0, row_block(*ids)[1]))

    n = SSM_DSTATE
    return [
        pl.BlockSpec((ROWS, GROUP_INNER), at(GROUP_INNER, _ColsB.Z)),
        pl.BlockSpec((ROWS, GROUP_INNER), at(GROUP_INNER, _ColsB.XS)),
        pl.BlockSpec((ROWS, n), at(n, _ColsB.B)),
        pl.BlockSpec((ROWS, n), at(n, _ColsB.C)),
        pl.BlockSpec((ROWS, LANES), lambda *ids: (row_block(*ids)[0], _ColsB.DT // LANES)),
        grp(CONV_WIDTH, GROUP_INNER), grp(1, GROUP_INNER),
        grp(CONV_WIDTH, n), grp(1, n), grp(CONV_WIDTH, n), grp(1, n),
        pl.BlockSpec((1, LANES), lambda *ids: (0, 0)), pl.BlockSpec((1, LANES), lambda *ids: (0, 0)),
        grp(1, GROUP_INNER), grp(1, GROUP_INNER),
    ]


def _ssd_params(conv_w, conv_b, dt_bias, a_log, d_skip, ssm_norm):
    gn = SSM_GROUPS * SSM_DSTATE
    cwx, cwb, cwc = conv_w[:, :SSM_INNER], conv_w[:, SSM_INNER:SSM_INNER + gn], conv_w[:, SSM_INNER + gn:]
    cb = conv_b.reshape(1, CONV_DIM)
    cbx, cbb, cbc = cb[:, :SSM_INNER], cb[:, SSM_INNER:SSM_INNER + gn], cb[:, SSM_INNER + gn:]

    def head_lanes(v):
        return jnp.pad(v, (0, LANES - SSM_HEADS)).reshape(1, LANES)

    dsk = jnp.repeat(d_skip, SSM_HEADDIM).reshape(1, SSM_INNER)
    return (cwx, cbx, cwb, cbb, cwc, cbc, head_lanes(dt_bias), head_lanes(a_log), dsk,
            ssm_norm.reshape(1, SSM_INNER))


def _conv_tail(csx, csb, csc):
    keep = SUBLANES - (CONV_WIDTH - 1)
    return jnp.concatenate([csx[:, keep:], csb[:, keep:], csc[:, keep:]], axis=-1)


def _mixer_prompt(x, norm_w, w_b, proj, bsz, seq, params, ret_norm):
    nch = seq // ROWS
    gn = SSM_GROUPS * SSM_DSTATE
    full = lambda a: pl.BlockSpec(a.shape, lambda b, c: (0,) * a.ndim)
    row = lambda width, off: pl.BlockSpec((ROWS, width), lambda b, c: (b * nch + c, off // width))
    ltri = _ssd_tables(CHUNK)
    dm, qd, kd, cd = _ret_tables(CHUNK)
    cosf, sinf = _rope_tables(jnp.arange(seq, dtype=jnp.int32))
    rnw = ret_norm.reshape(RET_HEADS, 1, RET_DV)
    specs = [pl.BlockSpec((ROWS, D_MODEL), lambda b, c: (b * nch, 0)),
             pl.BlockSpec((ROWS, D_MODEL), lambda b, c: (b * nch + jnp.minimum(c + 1, nch - 1), 0)),
             full(norm_w), full(w_b), *[full(p) for p in params], full(ltri),
             row(RET_QK, _ColsA.Q), row(RET_QK, _ColsA.K), row(RET_V, _ColsA.V), row(RET_V, _ColsA.GRET),
             pl.BlockSpec((ROWS, RET_DK), lambda b, c: (c, 0)), pl.BlockSpec((ROWS, RET_DK), lambda b, c: (c, 0)),
             full(dm), full(qd), full(kd), full(cd), full(rnw)]
    tail = lambda width: pl.BlockSpec((1, SUBLANES, width), lambda b, c: (b, 0, 0))
    y, hs, csx, csb, csc, a, s = pl.pallas_call(
        _mixer_prompt_kernel,
        out_shape=(jax.ShapeDtypeStruct((bsz * seq, SSM_INNER), BF16),
                   jax.ShapeDtypeStruct((bsz, SSM_HEADS * SSM_HEADDIM, SSM_DSTATE), F32),
                   jax.ShapeDtypeStruct((bsz, SUBLANES, SSM_INNER), F32),
                   jax.ShapeDtypeStruct((bsz, SUBLANES, gn), F32),
                   jax.ShapeDtypeStruct((bsz, SUBLANES, gn), F32),
                   jax.ShapeDtypeStruct((bsz * seq, RET_V), BF16),
                   jax.ShapeDtypeStruct((bsz, RET_HEADS, RET_DK, RET_DV), F32)),
        grid=(bsz, nch),
        in_specs=specs,
        out_specs=(pl.BlockSpec((ROWS, SSM_INNER), lambda b, c: (b * nch + c, 0)),
                   pl.BlockSpec((1, SSM_HEADS * SSM_HEADDIM, SSM_DSTATE), lambda b, c: (b, 0, 0)),
                   tail(SSM_INNER), tail(gn), tail(gn),
                   pl.BlockSpec((ROWS, RET_V), lambda b, c: (b * nch + c, 0)),
                   pl.BlockSpec((1, RET_HEADS, RET_DK, RET_DV), lambda b, c: (b, 0, 0, 0))),
        scratch_shapes=[pltpu.VMEM((2, ROWS, _ColsB.TOTAL), F32)],
        compiler_params=_params(("parallel", "arbitrary")),
        name="mixer_prompt",
    )(x, x, norm_w, w_b, *params, ltri, proj, proj, proj, proj, cosf, sinf, dm, qd, kd, cd, rnw)
    return y, hs, _conv_tail(csx, csb, csc), a, s


def _mixer_sample_kernel(seq, n_ssd_in, n_ret_in, *refs):
    ssd_in, refs = refs[:n_ssd_in], refs[n_ssd_in:]
    ret_in, refs = refs[:n_ret_in], refs[n_ret_in:]
    _ssd_sample_kernel(seq, *ssd_in, *refs[:5])
    _ret_sample_kernel(*ret_in, *refs[5:])


def _mixer_sample(proj_a, proj_b, row0, bsz, seq, params, state, conv_state, ret_state, ret_norm):
    assert SSM_GROUPS == RET_HEADS
    ntile = bsz // SAMPLE_PER_TILE
    gn = SSM_GROUPS * SSM_DSTATE
    proj = proj_b
    specs = _ssd_specs(lambda t, g: (t, g))
    gh = HEADS_PER_GROUP * SSM_HEADDIM
    cs = jnp.pad(conv_state, ((0, 0), (SUBLANES - (CONV_WIDTH - 1), 0), (0, 0)))
    csx, csb, csc = cs[:, :, :SSM_INNER], cs[:, :, SSM_INNER:SSM_INNER + gn], cs[:, :, SSM_INNER + gn:]
    pred = lambda width: pl.BlockSpec((SAMPLE_PER_TILE, SUBLANES, width), lambda t, g: (t, 0, g))
    state_spec = pl.BlockSpec((SAMPLE_PER_TILE, gh, SSM_DSTATE), lambda t, g: (t, g, 0))
    specs += [
        pl.BlockSpec((ROWS, ROWS), lambda t, g: (0, 0)),
        pl.BlockSpec((ROWS, SAMPLE_PER_TILE * LANES), lambda t, g: (0, 0)),
        pred(GROUP_INNER), pred(SSM_DSTATE), pred(SSM_DSTATE),
        state_spec,
    ]
    state2 = state.reshape(bsz, SSM_HEADS * SSM_HEADDIM, SSM_DSTATE)
    assert seq == SUBLANES
    ssd_operands = (proj, proj, proj, proj, proj, *params, _ssd_tables(seq), _block_mask(BF16), csx, csb, csc,
                    state2)
    r_specs, r_operands, r_shapes, r_out_specs = _retention_sample_call(proj_a, row0, bsz, seq, ret_state, ret_norm)
    y, hs, csx, csb, csc, a, s = pl.pallas_call(
        functools.partial(_mixer_sample_kernel, seq, len(ssd_operands), len(r_operands)),
        out_shape=(jax.ShapeDtypeStruct((bsz * seq, SSM_INNER), BF16), jax.ShapeDtypeStruct(state2.shape, F32),
                   jax.ShapeDtypeStruct((bsz, seq, SSM_INNER), F32),
                   jax.ShapeDtypeStruct((bsz, seq, gn), F32),
                   jax.ShapeDtypeStruct((bsz, seq, gn), F32), *r_shapes),
        grid=(ntile, SSM_GROUPS),
        in_specs=specs + r_specs,
        out_specs=(pl.BlockSpec((ROWS, GROUP_INNER), lambda t, g: (t, g)), state_spec,
                   pred(GROUP_INNER), pred(SSM_DSTATE), pred(SSM_DSTATE), *r_out_specs),
        compiler_params=_params(("parallel", "parallel")),
        name="mixer_sample",
    )(*ssd_operands, *r_operands)
    return y, hs, _conv_tail(csx, csb, csc), a, s


def _post_kernel(n_prompt_tiles, ap_ref, as_ref, yp_ref, ys_ref, ga_ref, gb_ref, xp_ref, xs_ref, wr_ref, ws_ref,
                 wo_ref, nw_ref, wrt_ref, brt_ref, x1_ref, h2_ref, lg_ref):
    is_prompt = pl.program_id(0) < n_prompt_tiles
    half = x1_ref.shape[0] // 2
    for r in (pl.ds(0, half), pl.ds(half, half)):
        branch_a = _dot(jnp.where(is_prompt, ap_ref[r, :], as_ref[r, :]), wr_ref[...])
        branch_b = _dot(jnp.where(is_prompt, yp_ref[r, :], ys_ref[r, :]), ws_ref[...])
        merged = jax.nn.sigmoid(ga_ref[r, :]) * branch_a + jax.nn.sigmoid(gb_ref[r, :]) * branch_b
        x1 = jnp.where(is_prompt, xp_ref[r, :], xs_ref[r, :]) + _dot(merged.astype(BF16), wo_ref[...])
        x1_ref[r, :] = x1
        ms = jnp.mean(x1 * x1, axis=-1, keepdims=True)
        h2 = x1 * lax.rsqrt(ms + EPS) * nw_ref[...]
        _store_row_tiles(h2_ref.at[pl.ds(r.start * ROW_TILES, half * ROW_TILES), :], h2)
        lg_ref[r, :] = _dot(h2.astype(BF16), wrt_ref[...]) + brt_ref[...]


def _post(a_p, a_s, y_p, y_s, proj, xp, xs, w_ret, w_ssm, w_out, norm_ffn, w_router, b_router):
    t = xp.shape[0] + xs.shape[0]
    tm = _pick(math.gcd(xp.shape[0], xs.shape[0]), 512)
    npt = xp.shape[0] // tm
    row = lambda width, blk: pl.BlockSpec((tm, width), lambda i: (i, blk))
    full = lambda *shape: pl.BlockSpec(shape, lambda i: (0,) * len(shape))
    return pl.pallas_call(
        functools.partial(_post_kernel, npt),
        out_shape=(jax.ShapeDtypeStruct((t, D_MODEL), F32), jax.ShapeDtypeStruct((t * ROW_TILES, LANES), F32),
                   jax.ShapeDtypeStruct((t, LANES), F32)),
        grid=(t // tm,),
        in_specs=[*_split_rows(tm, npt, RET_V), *_split_rows(tm, npt, SSM_INNER),
                  row(D_MODEL, _ColsA.GA // D_MODEL), row(D_MODEL, _ColsA.GB // D_MODEL),
                  *_split_rows(tm, npt, D_MODEL),
                  full(RET_V, D_MODEL), full(SSM_INNER, D_MODEL), full(D_MODEL, D_MODEL), full(1, D_MODEL),
                  full(D_MODEL, LANES), full(1, LANES)],
        out_specs=(row(D_MODEL, 0), pl.BlockSpec((tm * ROW_TILES, LANES), lambda i: (i, 0)), row(LANES, 0)),
        compiler_params=_params(("parallel",)),
        name="post_mixer",
    )(a_p, a_s, y_p, y_s, proj, proj, xp, xs, w_ret, w_ssm, w_out, norm_ffn, w_router, b_router)


def _route_kernel(lg_ref, lstrict_ref, meta_ref, gate_ref, cnt_ref):
    @pl.when(pl.program_id(0) == 0)
    def _():
        cnt_ref[...] = jnp.zeros_like(cnt_ref)

    tm = lg_ref.shape[0]
    lane = lax.broadcasted_iota(jnp.int32, (tm, LANES), 1)
    lane_f = lane.astype(F32)
    cur = jnp.where(lane < N_EXPERTS, lg_ref[...], -jnp.inf)
    vals, hots = [], []
    for _ in range(TOP_K):
        m = jnp.max(cur, axis=1, keepdims=True)
        idx = jnp.min(jnp.where(cur == m, lane_f, float(LANES)), axis=1, keepdims=True)
        hot = lane_f == idx
        vals.append(m)
        hots.append(hot)
        cur = jnp.where(hot, -jnp.inf, cur)
    exps = [jnp.exp(v - vals[0]) for v in vals]
    denom = exps[0] + exps[1] + exps[2] + exps[3]
    sel = hots[0] | hots[1] | hots[2] | hots[3]
    self32 = sel.astype(F32)
    before = _dot(lstrict_ref[...], self32.astype(BF16)) + cnt_ref[...]
    cnt_ref[...] = cnt_ref[...] + jnp.sum(self32, axis=0, keepdims=True)
    meta = jnp.zeros((tm, LANES), F32)
    gates = jnp.zeros((tm, LANES), F32)
    for k in range(TOP_K):
        e_k = jnp.sum(jnp.where(hots[k], lane_f, 0.0), axis=1, keepdims=True)
        p_k = jnp.sum(jnp.where(hots[k], before, 0.0), axis=1, keepdims=True)
        meta = jnp.where(lane == k, e_k, meta)
        meta = jnp.where(lane == SUBLANES + k, p_k, meta)
        gates = jnp.where(lane == k, exps[k] / denom, gates)
    gate_ref[...] = gates
    meta_ref[...] = meta.T[:2 * SUBLANES, :].astype(jnp.int32)


def _route(logits):
    t = logits.shape[0]
    tm = _pick(t, 512)
    r = jnp.arange(tm)
    lstrict = (r[:, None] > r[None, :]).astype(BF16)
    return pl.pallas_call(
        _route_kernel,
        out_shape=(jax.ShapeDtypeStruct((2 * SUBLANES, t), jnp.int32),
                   jax.ShapeDtypeStruct((t, LANES), F32),
                   jax.ShapeDtypeStruct((1, LANES), F32)),
        grid=(t // tm,),
        in_specs=[pl.BlockSpec((tm, LANES), lambda i: (i, 0)), pl.BlockSpec((tm, tm), lambda i: (0, 0))],
        out_specs=(pl.BlockSpec((2 * SUBLANES, tm), lambda i: (0, i)),
                   pl.BlockSpec((tm, LANES), lambda i: (i, 0)),
                   pl.BlockSpec((1, LANES), lambda i: (0, 0))),
        compiler_params=_params(("arbitrary",)),
        name="route",
    )(logits, lstrict)


def _row_copy(src_ref, src_row, dst_ref, dst_row, sem):
    src = pl.ds(pl.multiple_of(src_row * ROW_TILES, ROW_TILES), ROW_TILES)
    dst = pl.ds(pl.multiple_of(dst_row * ROW_TILES, ROW_TILES), ROW_TILES)
    return pltpu.make_async_copy(src_ref.at[src, :], dst_ref.at[dst, :], sem)


def _dispatch_kernel(pstart_ref, pend_ref, dest_ref, h2_ref, xs_ref, zero_ref, sem):
    tm = h2_ref.shape[0] // ROW_TILES

    @pl.when(pl.program_id(0) == 0)
    def _():
        zero_ref[...] = jnp.zeros_like(zero_ref)

        def fill(e):
            n = EXPERT_BLOCK * ROW_TILES
            start = pl.multiple_of((pend_ref[e] - EXPERT_BLOCK) * ROW_TILES, n)
            return pltpu.make_async_copy(zero_ref, xs_ref.at[pl.ds(start, n), :], sem)

        def fill_start(e, carry):
            @pl.when(pend_ref[e] > pstart_ref[e])
            def _():
                fill(e).start()
            return carry

        def fill_wait(e, carry):
            @pl.when(pend_ref[e] > pstart_ref[e])
            def _():
                fill(e).wait()
            return carry

        lax.fori_loop(0, N_EXPERTS, fill_start, 0)
        lax.fori_loop(0, N_EXPERTS, fill_wait, 0)

        def tail(j):
            n = EXPERT_BLOCK * ROW_TILES
            return pltpu.make_async_copy(zero_ref, xs_ref.at[pl.ds(pl.multiple_of(j * n, n), n), :], sem)

        used = pend_ref[N_EXPERTS - 1] // EXPERT_BLOCK
        total = xs_ref.shape[0] // (EXPERT_BLOCK * ROW_TILES)
        lax.fori_loop(used, total, lambda j, c: (tail(j).start(), c)[1], 0)
        lax.fori_loop(used, total, lambda j, c: (tail(j).wait(), c)[1], 0)

    def copy(t, k):
        return _row_copy(h2_ref, t, xs_ref, dest_ref[k, t], sem)

    def issue(t, carry):
        for k in range(TOP_K):
            copy(t, k).start(priority=k % 2)
        return carry

    def drain(t, carry):
        for k in range(TOP_K):
            copy(t, k).wait()
        return carry

    lax.fori_loop(0, tm, issue, 0, unroll=ISSUE_UNROLL)
    lax.fori_loop(0, tm, drain, 0, unroll=ISSUE_UNROLL)


def _dispatch(pstart, pends, dest, h2, n_rows):
    t = h2.shape[0] // ROW_TILES
    tm = _pick(t, 256)
    return pl.pallas_call(
        _dispatch_kernel,
        out_shape=jax.ShapeDtypeStruct((n_rows * ROW_TILES, LANES), F32),
        grid_spec=pltpu.PrefetchScalarGridSpec(
            num_scalar_prefetch=2,
            grid=(t // tm,),
            in_specs=[pl.BlockSpec((SUBLANES, tm), lambda i, ps, pe: (0, i), memory_space=pltpu.SMEM),
                      pl.BlockSpec((tm * ROW_TILES, LANES), lambda i, ps, pe: (i, 0))],
            out_specs=pl.BlockSpec(memory_space=pl.ANY),
            scratch_shapes=[pltpu.VMEM((EXPERT_BLOCK * ROW_TILES, LANES), F32), pltpu.SemaphoreType.DMA],
        ),
        compiler_params=_params(("arbitrary",)),
        name="dispatch",
    )(pstart, pends, dest, h2)


def _combine_kernel(n_prompt_tiles, dest_ref, dest_next_ref, gate_ref, x1_ref, nw_ref, yb_ref, op_ref, os_ref,
                    buf_ref, sem):
    tm = x1_ref.shape[0]
    i = pl.program_id(0)

    def copy(d_ref, slot, t, k):
        return _row_copy(yb_ref, d_ref[k, t], buf_ref.at[slot, k], t, sem.at[slot])

    def issue(d_ref, slot):
        def body(t, carry):
            for k in range(TOP_K):
                copy(d_ref, slot, t, k).start(priority=k % 2)
            return carry
        lax.fori_loop(0, tm, body, 0, unroll=ISSUE_UNROLL)

    def drain(d_ref, slot):
        def body(t, carry):
            for k in range(TOP_K):
                copy(d_ref, slot, t, k).wait()
            return carry
        lax.fori_loop(0, tm, body, 0, unroll=ISSUE_UNROLL)

    @pl.when(i == 0)
    def _():
        issue(dest_ref, 0)

    def step(slot):
        @pl.when(i + 1 < pl.num_programs(0))
        def _():
            issue(dest_next_ref, 1 - slot)

        drain(dest_ref, slot)
        gates = gate_ref[...]
        moe = _load_row_tiles(buf_ref, tm, slot, 0) * gates[:, 0:1]
        for k in range(1, TOP_K):
            moe = moe + _load_row_tiles(buf_ref, tm, slot, k) * gates[:, k:k + 1]
        x2 = x1_ref[...] + moe
        ms = jnp.mean(x2 * x2, axis=-1, keepdims=True)
        out = x2 * lax.rsqrt(ms + EPS) * nw_ref[...]

        @pl.when(i < n_prompt_tiles)
        def _():
            op_ref[...] = out

        @pl.when(i >= n_prompt_tiles)
        def _():
            os_ref[...] = out

    @pl.when(i % 2 == 0)
    def _():
        step(0)

    @pl.when(i % 2 == 1)
    def _():
        step(1)


def _combine(dest, gates, x1, norm_final, yb, n_prompt):
    t = x1.shape[0]
    tm = _pick(math.gcd(n_prompt, t - n_prompt), 256)
    npt = n_prompt // tm
    return pl.pallas_call(
        functools.partial(_combine_kernel, npt),
        out_shape=(jax.ShapeDtypeStruct((n_prompt, D_MODEL), F32),
                   jax.ShapeDtypeStruct((t - n_prompt, D_MODEL), F32)),
        grid=(t // tm,),
        in_specs=[pl.BlockSpec((SUBLANES, tm), lambda i: (0, i), memory_space=pltpu.SMEM),
                  pl.BlockSpec((SUBLANES, tm), lambda i: (0, jnp.minimum(i + 1, t // tm - 1)),
                               memory_space=pltpu.SMEM),
                  pl.BlockSpec((tm, LANES), lambda i: (i, 0)),
                  pl.BlockSpec((tm, D_MODEL), lambda i: (i, 0)),
                  pl.BlockSpec((1, D_MODEL), lambda i: (0, 0)),
                  pl.BlockSpec(memory_space=pl.ANY)],
        out_specs=_split_rows(tm, npt, D_MODEL),
        scratch_shapes=[pltpu.VMEM((2, TOP_K, tm * ROW_TILES, LANES), F32), pltpu.SemaphoreType.DMA((2,))],
        compiler_params=_params(("arbitrary",)),
        name="combine",
    )(dest, dest, gates, x1, norm_final, yb)


def _expert_kernel(be_ref, nv_ref, next_ref, xs_ref, bgu_ref, bd_ref, wgu_hbm, wd_hbm, yb_ref,
                   wgu_f, wd_f, wgu_b, wd_b, sem):
    i = pl.program_id(0)

    def fetch(e):
        return (pltpu.make_async_copy(wgu_hbm.at[e], wgu_f, sem.at[0]),
                pltpu.make_async_copy(wd_hbm.at[e], wd_f, sem.at[1]))

    @pl.when(i == 0)
    def _():
        for copy in fetch(be_ref[0]):
            copy.start()

    first = jnp.logical_or(i == 0, be_ref[i] != be_ref[jnp.maximum(i - 1, 0)])

    @pl.when(jnp.logical_and(first, i < nv_ref[0]))
    def _():
        for copy in fetch(be_ref[i]):
            copy.wait()
        wgu_b[...] = wgu_f[...].astype(BF16)
        wd_b[...] = wd_f[...].astype(BF16)

        @pl.when(next_ref[i] >= 0)
        def _():
            for copy in fetch(next_ref[i]):
                copy.start()

    @pl.when(i < nv_ref[0])
    def _():
        gu = _dot(_load_row_tiles(xs_ref, EXPERT_BLOCK).astype(BF16), wgu_b[...]) + bgu_ref[0]
        glu = jnp.minimum(gu[:, :D_FF], SWIGLU_LIMIT)
        lin = jnp.clip(gu[:, D_FF:], -SWIGLU_LIMIT, SWIGLU_LIMIT)
        act = glu * jax.nn.sigmoid(SWIGLU_ALPHA * glu) * (lin + 1.0)
        _store_row_tiles(yb_ref, _dot(act.astype(BF16), wd_b[...]) + bd_ref[0])

    @pl.when(i >= nv_ref[0])
    def _():
        yb_ref[...] = jnp.zeros_like(yb_ref)


def _experts(block_expert, n_valid, next_expert, xs, w_gu, b_gu, w_d, b_d):
    rows = EXPERT_BLOCK * ROW_TILES
    nb = xs.shape[0] // rows
    return pl.pallas_call(
        _expert_kernel,
        out_shape=jax.ShapeDtypeStruct(xs.shape, F32),
        grid_spec=pltpu.PrefetchScalarGridSpec(
            num_scalar_prefetch=3,
            grid=(nb,),
            in_specs=[pl.BlockSpec((rows, LANES), lambda i, be, nv, nx: (jnp.minimum(i, nv[0] - 1), 0)),
                      pl.BlockSpec((1, 1, 2 * D_FF), lambda i, be, nv, nx: (be[i], 0, 0)),
                      pl.BlockSpec((1, 1, D_MODEL), lambda i, be, nv, nx: (be[i], 0, 0)),
                      pl.BlockSpec(memory_space=pl.ANY),
                      pl.BlockSpec(memory_space=pl.ANY)],
            out_specs=pl.BlockSpec((rows, LANES), lambda i, be, nv, nx: (i, 0)),
            scratch_shapes=[pltpu.VMEM((D_MODEL, 2 * D_FF), F32), pltpu.VMEM((D_FF, D_MODEL), F32),
                            pltpu.VMEM((D_MODEL, 2 * D_FF), BF16), pltpu.VMEM((D_FF, D_MODEL), BF16),
                            pltpu.SemaphoreType.DMA((2,))],
        ),
        compiler_params=_params(("arbitrary",)),
        name="experts",
    )(block_expert, n_valid, next_expert, xs, b_gu, b_d, w_gu, w_d)


def _expert_layout(counts, n_blocks):
    counts = counts.astype(jnp.int32)
    padded = (counts + EXPERT_BLOCK - 1) // EXPERT_BLOCK * EXPERT_BLOCK
    pends = jnp.cumsum(padded)
    pstart = pends - padded
    block_start = jnp.arange(n_blocks, dtype=jnp.int32) * EXPERT_BLOCK
    block_expert = jnp.minimum(jnp.sum(pends[None, :] <= block_start[:, None], axis=1), N_EXPERTS - 1)
    n_valid = (pends[-1:] // EXPERT_BLOCK).astype(jnp.int32)
    seg_end = jnp.sum(jnp.where(block_expert[:, None] == jnp.arange(N_EXPERTS), pends[None, :], 0), axis=1)
    after = seg_end // EXPERT_BLOCK
    next_expert = jnp.where(after < n_valid[0], block_expert[jnp.minimum(after, n_blocks - 1)], -1)
    return (pstart.astype(jnp.int32), pends.astype(jnp.int32), block_expert.astype(jnp.int32), n_valid,
            next_expert.astype(jnp.int32))


def _permute_w_in(w_in):
    sizes = [RET_QK, RET_QK, RET_V, RET_V, SSM_INNER, CONV_DIM, SSM_HEADS, D_MODEL, D_MODEL]
    offs = [0]
    for s in sizes:
        offs.append(offs[-1] + s)
    q, k, v, g_ret, z, xbc, dt, g_a, g_b = [w_in[:, offs[i]:offs[i + 1]] for i in range(len(sizes))]
    gn = SSM_GROUPS * SSM_DSTATE
    xs, bm, cm = xbc[:, :SSM_INNER], xbc[:, SSM_INNER:SSM_INNER + gn], xbc[:, SSM_INNER + gn:]
    dt_pad = jnp.pad(dt, ((0, 0), (0, _ColsB.TOTAL - _ColsB.DT - SSM_HEADS)))
    w_a = jnp.concatenate([v, g_ret, g_a, g_b, q, k], axis=1).astype(BF16)
    w_b = jnp.concatenate([z, xs, bm, cm, dt_pad], axis=1).astype(BF16)
    return w_a, w_b


def _forward(x_prompt, x_sample, state_ret, state_ssm, state_conv, norm_mix, w_in, ret_norm, w_out_ret,
             conv_w, conv_b, dt_bias, a_log, d_skip, ssm_norm, w_out_ssm, w_out, norm_ffn,
             w_router, b_router, w_gate_up, b_gate_up, w_down, b_down, norm_final):
    bp, lp, _ = x_prompt.shape
    bs, ls, _ = x_sample.shape
    assert lp % ROWS == 0 and bs % SAMPLE_PER_TILE == 0 and ls * SAMPLE_PER_TILE == ROWS
    tp, ts = bp * lp, bs * ls
    t_all = tp + ts
    xp, xs_in = x_prompt.reshape(tp, D_MODEL), x_sample.reshape(ts, D_MODEL)

    w_a, w_b = _permute_w_in(w_in[0])
    norm_w = norm_mix[0].reshape(1, D_MODEL)
    proj = _inproj(xp, xs_in, norm_w, w_a)
    proj_ssd_s = _inproj(None, xs_in, norm_w, w_b)


    sp = _ssd_params(conv_w[0], conv_b[0], dt_bias[0], a_log[0], d_skip[0], ssm_norm[0])
    y_p, ssm_p, conv_p, a_p, ret_p = _mixer_prompt(xp, norm_w, w_b, proj, bp, lp, sp, ret_norm[0])
    y_s, ssm_s, conv_s, a_s, ret_s = _mixer_sample(proj, proj_ssd_s, tp, bs, ls, sp, state_ssm[0], state_conv[0],
                                                   state_ret[0], ret_norm[0])

    w_router_pad = jnp.pad(w_router[0], ((0, 0), (0, LANES - N_EXPERTS))).astype(BF16)
    b_router_pad = jnp.pad(b_router[0], (0, LANES - N_EXPERTS)).reshape(1, LANES)
    x1, h2, logits = _post(a_p, a_s, y_p, y_s, proj, xp, xs_in, w_out_ret[0].astype(BF16), w_out_ssm[0].astype(BF16),
                           w_out[0].astype(BF16), norm_ffn[0].reshape(1, D_MODEL), w_router_pad, b_router_pad)

    meta, gates, counts = _route(logits)
    n_blocks = -(-(t_all * TOP_K + N_EXPERTS * (EXPERT_BLOCK - 1)) // EXPERT_BLOCK)
    pstart, pends, block_expert, n_valid, next_expert = _expert_layout(counts[0, :N_EXPERTS], n_blocks)
    hot = meta[:TOP_K, :, None] == jnp.arange(N_EXPERTS, dtype=jnp.int32)
    seg_start = jnp.sum(jnp.where(hot, pstart, 0), axis=-1)
    dest = jnp.pad(seg_start + meta[SUBLANES:SUBLANES + TOP_K], ((0, SUBLANES - TOP_K), (0, 0)))
    xs = _dispatch(pstart, pends, dest, h2, n_blocks * EXPERT_BLOCK)
    yb = _experts(block_expert, n_valid, next_expert, xs, w_gate_up[0],
                  b_gate_up[0].reshape(N_EXPERTS, 1, 2 * D_FF), w_down[0], b_down[0].reshape(N_EXPERTS, 1, D_MODEL))
    out_p, out_s = _combine(dest, gates, x1, norm_final.reshape(1, D_MODEL), yb, tp)

    shape_s = (1, bs, SSM_HEADS, SSM_HEADDIM, SSM_DSTATE)
    shape_p = (1, bp, SSM_HEADS, SSM_HEADDIM, SSM_DSTATE)
    return (out_p.reshape(bp, lp, D_MODEL), out_s.reshape(bs, ls, D_MODEL),
            ret_p[None], ret_s[None], ssm_p.reshape(shape_p), ssm_s.reshape(shape_s),
            conv_p[None], conv_s[None])


def kernel(x_prompt, x_sample, state_ret, state_ssm, state_conv, norm_mix, w_in, ret_norm, w_out_ret, conv_w, conv_b, dt_bias, a_log, d_skip, ssm_norm, w_out_ssm, w_out, norm_ffn, w_router, b_router, w_gate_up, b_gate_up, w_down, b_down, norm_final):
    return _forward(x_prompt, x_sample, state_ret, state_ssm, state_conv, norm_mix, w_in, ret_norm, w_out_ret,
                    conv_w, conv_b, dt_bias, a_log, d_skip, ssm_norm, w_out_ssm, w_out, norm_ffn,
                    w_router, b_router, w_gate_up, b_gate_up, w_down, b_down, norm_final)
```

```python
import functools
import math

import jax
import jax.numpy as jnp
from jax import lax
from jax.experimental import pallas as pl
from jax.experimental.pallas import tpu as pltpu

F32 = jnp.float32
BF16 = jnp.bfloat16

D_MODEL = 1024
PAST_LEN = 16384
RET_HEADS = 4
RET_DK = 128
RET_DV = 256
RET_QK = RET_HEADS * RET_DK
RET_V = RET_HEADS * RET_DV
ROPE_BASE = 10000.0
SSM_INNER = 2 * D_MODEL
SSM_HEADDIM = 64
SSM_HEADS = SSM_INNER // SSM_HEADDIM
SSM_GROUPS = 4
SSM_DSTATE = 128
HEADS_PER_GROUP = SSM_HEADS // SSM_GROUPS
GROUP_INNER = SSM_INNER // SSM_GROUPS
CONV_WIDTH = 4
CONV_DIM = SSM_INNER + 2 * SSM_GROUPS * SSM_DSTATE
CHUNK = 128
N_EXPERTS = 32
TOP_K = 4
D_FF = D_MODEL
SWIGLU_LIMIT = 7.0
SWIGLU_ALPHA = 1.702
EPS = 1e-6

LANES = 128
SUBLANES = 8
ROWS = 128
SAMPLE_PER_TILE = 16
EXPERT_BLOCK = 512
ROW_TILES = D_MODEL // LANES
ISSUE_UNROLL = 4
STRIP_BITS = 5
STRIP = 1 << STRIP_BITS
VMEM_LIMIT = 56 * 1024 * 1024


class _ColsA:
    V = 0
    GRET = V + RET_V
    GA = GRET + RET_V
    GB = GA + D_MODEL
    Q = GB + D_MODEL
    K = Q + RET_QK
    TOTAL = K + RET_QK


class _ColsB:
    Z = 0
    XS = Z + SSM_INNER
    B = XS + SSM_INNER
    C = B + SSM_GROUPS * SSM_DSTATE
    DT = C + SSM_GROUPS * SSM_DSTATE
    TOTAL = DT + 2 * LANES


def _pick(n, target):
    best = None
    for c in range(LANES, target + 1, LANES):
        if n % c == 0:
            best = c
    assert best is not None, (n, target)
    return best


def _params(sem, **kw):
    return pltpu.CompilerParams(dimension_semantics=sem, vmem_limit_bytes=VMEM_LIMIT, **kw)


def _dot(a, b):
    return jnp.dot(a, b, preferred_element_type=F32)


def _dot_nt(a, b):
    return lax.dot_general(a, b, (((1,), (1,)), ((), ())), preferred_element_type=F32)


def _dot_tn(a, b):
    return lax.dot_general(a, b, (((0,), (0,)), ((), ())), preferred_element_type=F32)


def _silu(x):
    return x * jax.nn.sigmoid(x)


def _store_row_tiles(ref, value, *lead):
    n = value.shape[0]
    for s in range(ROW_TILES):
        ref[(*lead, pl.ds(s, n, stride=ROW_TILES), slice(None))] = value[:, s * LANES:(s + 1) * LANES]


def _load_row_tiles(ref, n, *lead):
    return jnp.concatenate([ref[(*lead, pl.ds(s, n, stride=ROW_TILES), slice(None))] for s in range(ROW_TILES)],
                           axis=1)


def _inproj_kernel(n_prompt_tiles, xp_ref, xs_ref, nw_ref, w_ref, o_ref, h_ref):
    @pl.when(pl.program_id(1) == 0)
    def _():
        x = jnp.where(pl.program_id(0) < n_prompt_tiles, xp_ref[...], xs_ref[...])
        ms = jnp.mean(x * x, axis=-1, keepdims=True)
        h_ref[...] = (x * lax.rsqrt(ms + EPS) * nw_ref[...]).astype(BF16)

    o_ref[...] = _dot(h_ref[...], w_ref[...])


def _split_rows(tm, npt, width):
    prompt = pl.BlockSpec((tm, width), lambda i, *_: (jnp.minimum(i, npt - 1), 0))
    sample = pl.BlockSpec((tm, width), lambda i, *_: (jnp.maximum(i - npt, 0), 0))
    return prompt, sample


def _inproj(xp, xs, norm_w, w):
    total = w.shape[1]
    t = xs.shape[0] + (0 if xp is None else xp.shape[0])
    tm = _pick(xs.shape[0] if xp is None else math.gcd(xp.shape[0], xs.shape[0]), 1024)
    npt = 0 if xp is None else xp.shape[0] // tm
    tn = _pick(total, 2560)
    assert tn % (2 * LANES) == 0
    x_specs = list(_split_rows(tm, npt, D_MODEL)) if npt else [pl.BlockSpec((tm, D_MODEL), lambda i, j: (i, 0))] * 2
    return pl.pallas_call(
        functools.partial(_inproj_kernel, npt),
        out_shape=jax.ShapeDtypeStruct((t, total), F32),
        grid=(t // tm, total // tn),
        in_specs=[
            *x_specs,
            pl.BlockSpec((1, D_MODEL), lambda i, j: (0, 0)),
            pl.BlockSpec((D_MODEL, tn), lambda i, j: (0, j)),
        ],
        out_specs=pl.BlockSpec((tm, tn), lambda i, j: (i, j)),
        scratch_shapes=[pltpu.VMEM((tm, D_MODEL), BF16)],
        compiler_params=_params(("parallel", "arbitrary")),
        name="in_proj",
    )(xs if xp is None else xp, xs, norm_w, w)


def _ret_log_decay():
    return jnp.log(1.0 - 2.0 ** (-5.0 - jnp.arange(RET_HEADS, dtype=F32)))


def _ret_tables(c):
    lg = _ret_log_decay()
    r = jnp.arange(ROWS)
    t = (r % c).astype(F32)
    seg = r // c
    diff = t[:, None] - t[None, :]
    ok = (seg[:, None] == seg[None, :]) & (diff >= 0)
    dm = jnp.where(ok[None], jnp.exp(lg[:, None, None] * jnp.maximum(diff, 0.0)[None]), 0.0)
    qd = jnp.exp(lg[:, None] * (t[None, :] + 1.0))
    kd = jnp.exp(lg[:, None] * (c - 1.0 - t[None, :]))
    cd = jnp.exp(lg * c)
    qd = jnp.broadcast_to(qd[:, :, None], (RET_HEADS, ROWS, LANES))
    kd = jnp.broadcast_to(kd[:, :, None], (RET_HEADS, ROWS, LANES))
    cd = jnp.broadcast_to(cd[:, None, None], (RET_HEADS, 1, RET_DV))
    return dm, qd, kd, cd


def _rope_tables(pos):
    half = RET_DK // 2
    inv_freq = 1.0 / (ROPE_BASE ** jnp.linspace(0.0, 1.0, half, dtype=F32))
    ang = pos.astype(F32)[:, None] * inv_freq[None, :]
    cos, sin = jnp.cos(ang), jnp.sin(ang)
    return jnp.concatenate([cos, cos], -1), jnp.concatenate([-sin, sin], -1)


def _block_mask(dtype):
    r = jnp.arange(ROWS) // (ROWS // SAMPLE_PER_TILE)
    b = jnp.arange(SAMPLE_PER_TILE * LANES) // LANES
    return (r[:, None] == b[None, :]).astype(dtype)


def _ret_common(q, k, v, g, cosf, sinf, dm, qd, kd, nw):
    qr = q * cosf + pltpu.roll(q, RET_DK // 2, 1) * sinf
    kr = (k * cosf + pltpu.roll(k, RET_DK // 2, 1) * sinf) * (RET_DK ** -0.5)
    vb = v.astype(BF16)
    scores = _dot_nt(qr.astype(BF16), kr.astype(BF16)) * dm
    o_intra = _dot(scores.astype(BF16), vb)
    q_state = (qr * qd).astype(BF16)
    k_state = kr * kd

    def finish(o):
        ms = jnp.mean(o * o, axis=-1, keepdims=True)
        on = o * lax.rsqrt(ms + EPS) * nw
        return (_silu(g) * on).astype(BF16)

    return o_intra, q_state, k_state, vb, finish


def _ret_prompt_step(q_ref, k_ref, v_ref, g_ref, cos_ref, sin_ref, dm_ref, qd_ref, kd_ref, cd_ref, nw_ref,
                     a_ref, s_ref):
    cosf, sinf = cos_ref[...], sin_ref[...]
    for h in range(RET_HEADS):
        dk = slice(h * RET_DK, (h + 1) * RET_DK)
        dv = slice(h * RET_DV, (h + 1) * RET_DV)
        o_intra, q_state, k_state, vb, finish = _ret_common(
            q_ref[:, dk], k_ref[:, dk], v_ref[:, dv], g_ref[:, dv], cosf, sinf, dm_ref[h], qd_ref[h], kd_ref[h],
            nw_ref[h])
        s = s_ref[0, h]
        a_ref[:, dv] = finish(o_intra + _dot(q_state, s.astype(BF16)))
        s_ref[0, h] = s * cd_ref[h] + _dot_tn(k_state.astype(BF16), vb)


def _ret_sample_kernel(q_ref, k_ref, v_ref, g_ref, cos_ref, sin_ref, dm_ref, qd_ref, kd_ref, cd_ref, nw_ref,
                       bm_ref, bmt_ref, s0_ref, a_ref, s_ref):
    o_intra, q_state, k_state, vb, finish = _ret_common(
        q_ref[...], k_ref[...], v_ref[...], g_ref[...], cos_ref[...], sin_ref[...], dm_ref[0], qd_ref[0], kd_ref[0],
        nw_ref[0])
    nb = SAMPLE_PER_TILE
    s0 = s0_ref[:, 0]
    q_bd = jnp.concatenate([q_state] * nb, axis=1) * bm_ref[...]
    a_ref[...] = finish(o_intra + _dot(q_bd, s0.reshape(nb * RET_DK, RET_DV).astype(BF16)))
    k_t = k_state.T.astype(BF16)
    k_bd_t = jnp.concatenate([k_t] * nb, axis=0) * bmt_ref[...]
    ds = _dot(k_bd_t, vb).reshape(nb, RET_DK, RET_DV)
    s_ref[:, 0] = s0 * cd_ref[0] + ds


def _ret_specs(row_block):
    def at(width, off):
        return lambda *ids: (row_block(*ids)[0], off // width + row_block(*ids)[1])

    def head(*shape):
        return pl.BlockSpec((1,) + shape, lambda *ids: (row_block(*ids)[1],) + (0,) * len(shape))

    return [
        pl.BlockSpec((ROWS, RET_DK), at(RET_DK, _ColsA.Q)),
        pl.BlockSpec((ROWS, RET_DK), at(RET_DK, _ColsA.K)),
        pl.BlockSpec((ROWS, RET_DV), at(RET_DV, _ColsA.V)),
        pl.BlockSpec((ROWS, RET_DV), at(RET_DV, _ColsA.GRET)),
    ], head


def _retention_sample_call(proj, row0, bsz, seq, state, ret_norm):
    rb0 = row0 // ROWS
    dm, qd, kd, cd = _ret_tables(seq)
    pos = PAST_LEN + jnp.arange(seq, dtype=jnp.int32)
    cosf, sinf = _rope_tables(jnp.tile(pos, SAMPLE_PER_TILE))
    rb = lambda t, h: (rb0 + t, h)
    specs, head = _ret_specs(rb)
    full = lambda *shape: pl.BlockSpec(shape, lambda t, h: (0,) * len(shape))
    specs += [
        full(ROWS, RET_DK), full(ROWS, RET_DK),
        head(ROWS, ROWS), head(ROWS, LANES), head(ROWS, LANES), head(1, RET_DV), head(1, RET_DV),
        full(ROWS, SAMPLE_PER_TILE * LANES), full(SAMPLE_PER_TILE * LANES, ROWS),
        pl.BlockSpec((SAMPLE_PER_TILE, 1, RET_DK, RET_DV), lambda t, h: (t, h, 0, 0)),
    ]
    bm = _block_mask(BF16)
    operands = (proj, proj, proj, proj, cosf, sinf, dm, qd, kd, cd, ret_norm.reshape(RET_HEADS, 1, RET_DV),
                bm, bm.T, state)
    out_shapes = (jax.ShapeDtypeStruct((bsz * seq, RET_V), BF16), jax.ShapeDtypeStruct(state.shape, F32))
    out_specs = (pl.BlockSpec((ROWS, RET_DV), lambda t, h: (t, h)),
                 pl.BlockSpec((SAMPLE_PER_TILE, 1, RET_DK, RET_DV), lambda t, h: (t, h, 0, 0)))
    return specs, operands, out_shapes, out_specs


def _softplus(x):
    return jnp.maximum(x, 0.0) + jnp.log1p(jnp.exp(-jnp.abs(x)))


def _conv_piece(x, pred, w_ref, b_ref):
    width = x.shape[1]
    x3 = x.reshape(ROWS // SUBLANES, SUBLANES, width)
    t8 = lax.broadcasted_iota(jnp.int32, x3.shape, 1)
    acc = b_ref[...].reshape(1, 1, width)
    for i in range(CONV_WIDTH):
        s = CONV_WIDTH - 1 - i
        tap = x3 if s == 0 else pltpu.roll(jnp.where(t8 >= SUBLANES - s, pred, x3), s, 1)
        acc = acc + tap * w_ref[pl.ds(i, 1), :].reshape(1, 1, width)
    return _silu(acc).reshape(ROWS, width)


def _ssd_decay_terms(c, dt_pre, a_log, ltri):
    dt = _softplus(dt_pre)
    d_a = dt * (-jnp.exp(a_log))
    hi = d_a.astype(BF16)
    r1 = d_a - hi.astype(F32)
    mid = r1.astype(BF16)
    lo = (r1 - mid.astype(F32)).astype(BF16)
    cum = _dot(ltri, hi) + _dot(ltri, mid) + _dot(ltri, lo)
    cum3 = cum.reshape(ROWS // c, c, LANES)
    c_last = jnp.broadcast_to(cum3[:, c - 1:c, :], cum3.shape).reshape(ROWS, LANES)
    to_end = jnp.exp(c_last - cum) * dt
    e_cum = jnp.exp(cum)
    return cum, to_end, e_cum, cum.T, dt.T


def _ssd_group(c, terms, head0, z_ref, dsk_ref, nrm_ref, xs, bm, cm, y_ref, state_io, zero=None):
    cum, to_end, e_cum, cum_t, dt_t = terms
    b_b, c_b = bm.astype(BF16), cm.astype(BF16)
    cb = _dot_nt(c_b, b_b)
    ri = lax.broadcasted_iota(jnp.int32, (ROWS, ROWS), 0)
    ci = lax.broadcasted_iota(jnp.int32, (ROWS, ROWS), 1)
    shift = c.bit_length() - 1
    causal = (ci <= ri) & (jnp.right_shift(ri, shift) == jnp.right_shift(ci, shift))
    lo_half = lax.broadcasted_iota(jnp.int32, (ROWS, LANES), 1) < SSM_HEADDIM

    def col(a, h):
        return jnp.broadcast_to(a[:, h:h + 1], (ROWS, LANES))

    def weights(h):
        seg = col(cum, h) - jnp.broadcast_to(cum_t[h:h + 1, :], (ROWS, ROWS))
        decay = jnp.exp(jnp.where(causal, seg, -jnp.inf))
        return (cb * decay * jnp.broadcast_to(dt_t[h:h + 1, :], (ROWS, ROWS))).astype(BF16)

    pieces = []
    for m in range(HEADS_PER_GROUP // 2):
        ha, hb = head0 + 2 * m, head0 + 2 * m + 1
        xp = xs[:, m * LANES:(m + 1) * LANES]
        w2 = jnp.concatenate([weights(ha), weights(hb)], axis=1)
        x2 = jnp.concatenate([jnp.where(lo_half, xp, 0.0), jnp.where(lo_half, 0.0, xp)], axis=0).astype(BF16)
        y = _dot(w2, x2)
        xw = (xp * jnp.where(lo_half, col(to_end, ha), col(to_end, hb))).astype(BF16)
        y_state = state_io(m, ha, c_b, xw, b_b, e_cum)
        y = y + y_state * jnp.where(lo_half, col(e_cum, ha), col(e_cum, hb))
        pieces.append(y + xp * dsk_ref[:, m * LANES:(m + 1) * LANES])
    if zero is not None:
        pieces[0] = jnp.concatenate([pieces[0][:SUBLANES] + zero, pieces[0][SUBLANES:]], axis=0)
    yg = jnp.concatenate(pieces, axis=1) * _silu(z_ref[...])
    ms = jnp.mean(yg * yg, axis=-1, keepdims=True)
    y_ref[...] = (yg * lax.rsqrt(ms + EPS) * nrm_ref[...]).astype(BF16)


def _pair_decay(e_cum, row, ha):
    top = lax.broadcasted_iota(jnp.int32, (2 * SSM_HEADDIM, SSM_DSTATE), 0) < SSM_HEADDIM
    ea = jnp.broadcast_to(e_cum[row:row + 1, ha:ha + 1], top.shape)
    eb = jnp.broadcast_to(e_cum[row:row + 1, ha + 1:ha + 2], top.shape)
    return jnp.where(top, ea, eb)


def _mixer_prompt_kernel(x0_ref, xn_ref, nw_ref, w_ref, cwx_ref, cbx_ref, cwb_ref, cbb_ref, cwc_ref, cbc_ref,
                         dtb_ref, alog_ref, dsk_ref, nrm_ref, ltri_ref, *rest):
    ret_in, (y_ref, hs_ref, px_ref, pb_ref, pc_ref, a_ref, s_ref, proj_ref) = rest[:11], rest[11:]
    c = pl.program_id(1)

    def normed(x_ref):
        x = x_ref[...]
        ms = jnp.mean(x * x, axis=-1, keepdims=True)
        return (x * lax.rsqrt(ms + EPS) * nw_ref[...]).astype(BF16)

    def project(x_ref, slot):
        proj_ref[slot] = _dot(normed(x_ref), w_ref[...])

    @pl.when(c == 0)
    def _():
        hs_ref[...] = jnp.zeros_like(hs_ref)
        px_ref[...] = jnp.zeros_like(px_ref)
        pb_ref[...] = jnp.zeros_like(pb_ref)
        pc_ref[...] = jnp.zeros_like(pc_ref)
        s_ref[...] = jnp.zeros_like(s_ref)
        project(x0_ref, 0)

    def conv(x_ref, p_ref, w_ref, bias_ref):
        x = x_ref[...]
        x3 = x.reshape(ROWS // SUBLANES, SUBLANES, x.shape[1])
        pred = jnp.concatenate([p_ref[...], x3[:-1]], axis=0)
        out = _conv_piece(x, pred, w_ref, bias_ref)
        p_ref[0] = x3[-1]
        return out

    def step(cur_slot, next_slot):
        h_next = normed(xn_ref)
        bounds = [0, 5 * 256, 10 * 256, 15 * 256, _ColsB.TOTAL]

        def project_slice(g):
            lo, hi = bounds[g], bounds[g + 1]
            part = _dot(h_next, w_ref[:, lo:hi])
            proj_ref[next_slot, :, lo:hi] = part
            bits = pltpu.bitcast(part[ROWS - SUBLANES:, hi - lo - LANES:], jnp.uint32)
            sixteen = jnp.uint32(16)
            return pltpu.bitcast(lax.shift_right_logical(lax.shift_right_logical(bits, sixteen), sixteen), F32)

        _ret_prompt_step(*ret_in, a_ref, s_ref)
        cur = proj_ref.at[cur_slot]
        gn = SSM_GROUPS * SSM_DSTATE
        z_ref = cur.at[:, _ColsB.Z:_ColsB.Z + SSM_INNER]
        xs_ref = cur.at[:, _ColsB.XS:_ColsB.XS + SSM_INNER]
        b_ref = cur.at[:, _ColsB.B:_ColsB.B + gn]
        c_ref = cur.at[:, _ColsB.C:_ColsB.C + gn]
        dt_ref = cur.at[:, _ColsB.DT:_ColsB.DT + LANES]
        terms = _ssd_decay_terms(CHUNK, dt_ref[...] + dtb_ref[...], alog_ref[...], ltri_ref[...])
        for g in range(SSM_GROUPS):
            def cols(ref, width, g=g):
                return ref.at[..., g * width:(g + 1) * width]

            gi, n = GROUP_INNER, SSM_DSTATE
            xs = conv(cols(xs_ref, gi), cols(px_ref, gi), cols(cwx_ref, gi), cols(cbx_ref, gi))
            bm = conv(cols(b_ref, n), cols(pb_ref, n), cols(cwb_ref, n), cols(cbb_ref, n))
            cm = conv(cols(c_ref, n), cols(pc_ref, n), cols(cwc_ref, n), cols(cbc_ref, n))
            hs_g = hs_ref.at[0, g * gi:(g + 1) * gi, :]

            def state_io(m, ha, c_b, xw, b_b, e_cum, hs_g=hs_g):
                rows = pl.ds(m * 2 * SSM_HEADDIM, 2 * SSM_HEADDIM)
                h = hs_g[rows, :]
                hs_g[rows, :] = h * _pair_decay(e_cum, ROWS - 1, ha) + _dot_tn(xw, b_b)
                return _dot_nt(c_b, h.astype(BF16))

            _ssd_group(CHUNK, terms, g * HEADS_PER_GROUP, cols(z_ref, gi), cols(dsk_ref, gi), cols(nrm_ref, gi),
                       xs, bm, cm, cols(y_ref, gi), state_io, zero=project_slice(g))

    @pl.when(c % 2 == 0)
    def _():
        step(0, 1)

    @pl.when(c % 2 == 1)
    def _():
        step(1, 0)


def _ssd_sample_kernel(seq, z_ref, xs_ref, b_ref, c_ref, dt_ref, cwx_ref, cbx_ref, cwb_ref, cbb_ref, cwc_ref,
                       cbc_ref, dtb_ref, alog_ref, dsk_ref, nrm_ref, ltri_ref, bmask_ref, px_ref, pb_ref, pc_ref,
                       h0_ref, y_ref, hs_ref, csx_ref, csb_ref, csc_ref):
    nb = SAMPLE_PER_TILE

    def conv(x_ref, p_ref, w_ref, bias_ref, cs_ref):
        x = x_ref[...]
        cs_ref[...] = x.reshape(nb, seq, x.shape[1])
        return _conv_piece(x, p_ref[...], w_ref, bias_ref)

    xs = conv(xs_ref, px_ref, cwx_ref, cbx_ref, csx_ref)
    bm = conv(b_ref, pb_ref, cwb_ref, cbb_ref, csb_ref)
    cm = conv(c_ref, pc_ref, cwc_ref, cbc_ref, csc_ref)
    bmask = bmask_ref[...]

    def state_io(m, ha, c_b, xw, b_b, e_cum):
        rows = pl.ds(m * 2 * SSM_HEADDIM, 2 * SSM_HEADDIM)
        c_bd = jnp.concatenate([c_b] * nb, axis=1) * bmask
        b_bd = jnp.concatenate([b_b] * nb, axis=1) * bmask
        hs = [h0_ref[i, rows, :] for i in range(nb)]
        h_cat = jnp.concatenate(hs, axis=1).astype(BF16)
        dh = _dot_tn(xw, b_bd)
        for i in range(nb):
            decay = _pair_decay(e_cum, i * seq + seq - 1, ha)
            hs_ref[i, rows, :] = hs[i] * decay + dh[:, i * SSM_DSTATE:(i + 1) * SSM_DSTATE]
        return _dot_nt(c_bd, h_cat)

    shift = jnp.bitwise_and(LANES - HEADS_PER_GROUP * pl.program_id(1), LANES - 1)
    dt_pre = pltpu.roll(dt_ref[...] + dtb_ref[...], shift, 1)
    a_log = pltpu.roll(jnp.broadcast_to(alog_ref[...], (SUBLANES, LANES)), shift, 1)[:1]
    terms = _ssd_decay_terms(seq, dt_pre, a_log, ltri_ref[...])
    _ssd_group(seq, terms, 0, z_ref, dsk_ref, nrm_ref, xs, bm, cm, y_ref, state_io)


def _ssd_tables(c):
    r = jnp.arange(ROWS)
    ltri = ((r[:, None] >= r[None, :]) & ((r[:, None] // c) == (r[None, :] // c))).astype(BF16)
    return ltri


def _ssd_specs(row_block):
    def at(width, off):
        return lambda *ids: (row_block(*ids)[0], off // width + row_block(*ids)[1])

    def grp(rows, width):
        return pl.BlockSpec((rows, width), lambda *ids: (0, row_block(*ids)[1]))

    n = SSM_DSTATE
    return [
        pl.BlockSpec((ROWS, GROUP_INNER), at(GROUP_INNER, _ColsB.Z)),
        pl.BlockSpec((ROWS, GROUP_INNER), at(GROUP_INNER, _ColsB.XS)),
        pl.BlockSpec((ROWS, n), at(n, _ColsB.B)),
        pl.BlockSpec((ROWS, n), at(n, _ColsB.C)),
        pl.BlockSpec((ROWS, LANES), lambda *ids: (row_block(*ids)[0], _ColsB.DT // LANES)),
        grp(CONV_WIDTH, GROUP_INNER), grp(1, GROUP_INNER),
        grp(CONV_WIDTH, n), grp(1, n), grp(CONV_WIDTH, n), grp(1, n),
        pl.BlockSpec((1, LANES), lambda *ids: (0, 0)), pl.BlockSpec((1, LANES), lambda *ids: (0, 0)),
        grp(1, GROUP_INNER), grp(1, GROUP_INNER),
    ]


def _ssd_params(conv_w, conv_b, dt_bias, a_log, d_skip, ssm_norm):
    gn = SSM_GROUPS * SSM_DSTATE
    cwx, cwb, cwc = conv_w[:, :SSM_INNER], conv_w[:, SSM_INNER:SSM_INNER + gn], conv_w[:, SSM_INNER + gn:]
    cb = conv_b.reshape(1, CONV_DIM)
    cbx, cbb, cbc = cb[:, :SSM_INNER], cb[:, SSM_INNER:SSM_INNER + gn], cb[:, SSM_INNER + gn:]

    def head_lanes(v):
        return jnp.pad(v, (0, LANES - SSM_HEADS)).reshape(1, LANES)

    dsk = jnp.repeat(d_skip, SSM_HEADDIM).reshape(1, SSM_INNER)
    return (cwx, cbx, cwb, cbb, cwc, cbc, head_lanes(dt_bias), head_lanes(a_log), dsk,
            ssm_norm.reshape(1, SSM_INNER))


def _conv_tail(csx, csb, csc):
    keep = SUBLANES - (CONV_WIDTH - 1)
    return jnp.concatenate([csx[:, keep:], csb[:, keep:], csc[:, keep:]], axis=-1)


def _mixer_prompt(x, norm_w, w_b, proj, bsz, seq, params, ret_norm):
    nch = seq // ROWS
    gn = SSM_GROUPS * SSM_DSTATE
    full = lambda a: pl.BlockSpec(a.shape, lambda b, c: (0,) * a.ndim)
    row = lambda width, off: pl.BlockSpec((ROWS, width), lambda b, c: (b * nch + c, off // width))
    ltri = _ssd_tables(CHUNK)
    dm, qd, kd, cd = _ret_tables(CHUNK)
    cosf, sinf = _rope_tables(jnp.arange(seq, dtype=jnp.int32))
    rnw = ret_norm.reshape(RET_HEADS, 1, RET_DV)
    specs = [pl.BlockSpec((ROWS, D_MODEL), lambda b, c: (b * nch, 0)),
             pl.BlockSpec((ROWS, D_MODEL), lambda b, c: (b * nch + jnp.minimum(c + 1, nch - 1), 0)),
             full(norm_w), full(w_b), *[full(p) for p in params], full(ltri),
             row(RET_QK, _ColsA.Q), row(RET_QK, _ColsA.K), row(RET_V, _ColsA.V), row(RET_V, _ColsA.GRET),
             pl.BlockSpec((ROWS, RET_DK), lambda b, c: (c, 0)), pl.BlockSpec((ROWS, RET_DK), lambda b, c: (c, 0)),
             full(dm), full(qd), full(kd), full(cd), full(rnw)]
    tail = lambda width: pl.BlockSpec((1, SUBLANES, width), lambda b, c: (b, 0, 0))
    y, hs, csx, csb, csc, a, s = pl.pallas_call(
        _mixer_prompt_kernel,
        out_shape=(jax.ShapeDtypeStruct((bsz * seq, SSM_INNER), BF16),
                   jax.ShapeDtypeStruct((bsz, SSM_HEADS * SSM_HEADDIM, SSM_DSTATE), F32),
                   jax.ShapeDtypeStruct((bsz, SUBLANES, SSM_INNER), F32),
                   jax.ShapeDtypeStruct((bsz, SUBLANES, gn), F32),
                   jax.ShapeDtypeStruct((bsz, SUBLANES, gn), F32),
                   jax.ShapeDtypeStruct((bsz * seq, RET_V), BF16),
                   jax.ShapeDtypeStruct((bsz, RET_HEADS, RET_DK, RET_DV), F32)),
        grid=(bsz, nch),
        in_specs=specs,
        out_specs=(pl.BlockSpec((ROWS, SSM_INNER), lambda b, c: (b * nch + c, 0)),
                   pl.BlockSpec((1, SSM_HEADS * SSM_HEADDIM, SSM_DSTATE), lambda b, c: (b, 0, 0)),
                   tail(SSM_INNER), tail(gn), tail(gn),
                   pl.BlockSpec((ROWS, RET_V), lambda b, c: (b * nch + c, 0)),
                   pl.BlockSpec((1, RET_HEADS, RET_DK, RET_DV), lambda b, c: (b, 0, 0, 0))),
        scratch_shapes=[pltpu.VMEM((2, ROWS, _ColsB.TOTAL), F32)],
        compiler_params=_params(("parallel", "arbitrary")),
        name="mixer_prompt",
    )(x, x, norm_w, w_b, *params, ltri, proj, proj, proj, proj, cosf, sinf, dm, qd, kd, cd, rnw)
    return y, hs, _conv_tail(csx, csb, csc), a, s


def _mixer_sample_kernel(seq, n_ssd_in, n_ret_in, *refs):
    ssd_in, refs = refs[:n_ssd_in], refs[n_ssd_in:]
    ret_in, refs = refs[:n_ret_in], refs[n_ret_in:]
    _ssd_sample_kernel(seq, *ssd_in, *refs[:5])
    _ret_sample_kernel(*ret_in, *refs[5:])


def _mixer_sample(proj_a, proj_b, row0, bsz, seq, params, state, conv_state, ret_state, ret_norm):
    assert SSM_GROUPS == RET_HEADS
    ntile = bsz // SAMPLE_PER_TILE
    gn = SSM_GROUPS * SSM_DSTATE
    proj = proj_b
    specs = _ssd_specs(lambda t, g: (t, g))
    gh = HEADS_PER_GROUP * SSM_HEADDIM
    cs = jnp.pad(conv_state, ((0, 0), (SUBLANES - (CONV_WIDTH - 1), 0), (0, 0)))
    csx, csb, csc = cs[:, :, :SSM_INNER], cs[:, :, SSM_INNER:SSM_INNER + gn], cs[:, :, SSM_INNER + gn:]
    pred = lambda width: pl.BlockSpec((SAMPLE_PER_TILE, SUBLANES, width), lambda t, g: (t, 0, g))
    state_spec = pl.BlockSpec((SAMPLE_PER_TILE, gh, SSM_DSTATE), lambda t, g: (t, g, 0))
    specs += [
        pl.BlockSpec((ROWS, ROWS), lambda t, g: (0, 0)),
        pl.BlockSpec((ROWS, SAMPLE_PER_TILE * LANES), lambda t, g: (0, 0)),
        pred(GROUP_INNER), pred(SSM_DSTATE), pred(SSM_DSTATE),
        state_spec,
    ]
    state2 = state.reshape(bsz, SSM_HEADS * SSM_HEADDIM, SSM_DSTATE)
    assert seq == SUBLANES
    ssd_operands = (proj, proj, proj, proj, proj, *params, _ssd_tables(seq), _block_mask(BF16), csx, csb, csc,
                    state2)
    r_specs, r_operands, r_shapes, r_out_specs = _retention_sample_call(proj_a, row0, bsz, seq, ret_state, ret_norm)
    y, hs, csx, csb, csc, a, s = pl.pallas_call(
        functools.partial(_mixer_sample_kernel, seq, len(ssd_operands), len(r_operands)),
        out_shape=(jax.ShapeDtypeStruct((bsz * seq, SSM_INNER), BF16), jax.ShapeDtypeStruct(state2.shape, F32),
                   jax.ShapeDtypeStruct((bsz, seq, SSM_INNER), F32),
                   jax.ShapeDtypeStruct((bsz, seq, gn), F32),
                   jax.ShapeDtypeStruct((bsz, seq, gn), F32), *r_shapes),
        grid=(ntile, SSM_GROUPS),
        in_specs=specs + r_specs,
        out_specs=(pl.BlockSpec((ROWS, GROUP_INNER), lambda t, g: (t, g)), state_spec,
                   pred(GROUP_INNER), pred(SSM_DSTATE), pred(SSM_DSTATE), *r_out_specs),
        compiler_params=_params(("parallel", "parallel")),
        name="mixer_sample",
    )(*ssd_operands, *r_operands)
    return y, hs, _conv_tail(csx, csb, csc), a, s


def _post_kernel(n_prompt_tiles, ap_ref, as_ref, yp_ref, ys_ref, ga_ref, gb_ref, xp_ref, xs_ref, wr_ref, ws_ref,
                 wo_ref, nw_ref, wrt_ref, brt_ref, x1_ref, h2_ref, lg_ref):
    is_prompt = pl.program_id(0) < n_prompt_tiles
    half = x1_ref.shape[0] // 2
    for r in (pl.ds(0, half), pl.ds(half, half)):
        branch_a = _dot(jnp.where(is_prompt, ap_ref[r, :], as_ref[r, :]), wr_ref[...])
        branch_b = _dot(jnp.where(is_prompt, yp_ref[r, :], ys_ref[r, :]), ws_ref[...])
        merged = jax.nn.sigmoid(ga_ref[r, :]) * branch_a + jax.nn.sigmoid(gb_ref[r, :]) * branch_b
        x1 = jnp.where(is_prompt, xp_ref[r, :], xs_ref[r, :]) + _dot(merged.astype(BF16), wo_ref[...])
        x1_ref[r, :] = x1
        ms = jnp.mean(x1 * x1, axis=-1, keepdims=True)
        h2 = x1 * lax.rsqrt(ms + EPS) * nw_ref[...]
        _store_row_tiles(h2_ref.at[pl.ds(r.start * ROW_TILES, half * ROW_TILES), :], h2)
        lg_ref[r, :] = _dot(h2.astype(BF16), wrt_ref[...]) + brt_ref[...]


def _post(a_p, a_s, y_p, y_s, proj, xp, xs, w_ret, w_ssm, w_out, norm_ffn, w_router, b_router):
    t = xp.shape[0] + xs.shape[0]
    tm = _pick(math.gcd(xp.shape[0], xs.shape[0]), 512)
    npt = xp.shape[0] // tm
    row = lambda width, blk: pl.BlockSpec((tm, width), lambda i: (i, blk))
    full = lambda *shape: pl.BlockSpec(shape, lambda i: (0,) * len(shape))
    return pl.pallas_call(
        functools.partial(_post_kernel, npt),
        out_shape=(jax.ShapeDtypeStruct((t, D_MODEL), F32), jax.ShapeDtypeStruct((t * ROW_TILES, LANES), F32),
                   jax.ShapeDtypeStruct((t, LANES), F32)),
        grid=(t // tm,),
        in_specs=[*_split_rows(tm, npt, RET_V), *_split_rows(tm, npt, SSM_INNER),
                  row(D_MODEL, _ColsA.GA // D_MODEL), row(D_MODEL, _ColsA.GB // D_MODEL),
                  *_split_rows(tm, npt, D_MODEL),
                  full(RET_V, D_MODEL), full(SSM_INNER, D_MODEL), full(D_MODEL, D_MODEL), full(1, D_MODEL),
                  full(D_MODEL, LANES), full(1, LANES)],
        out_specs=(row(D_MODEL, 0), pl.BlockSpec((tm * ROW_TILES, LANES), lambda i: (i, 0)), row(LANES, 0)),
        compiler_params=_params(("parallel",)),
        name="post_mixer",
    )(a_p, a_s, y_p, y_s, proj, proj, xp, xs, w_ret, w_ssm, w_out, norm_ffn, w_router, b_router)


def _route_kernel(lg_ref, lstrict_ref, srow_ref, gate_ref, tmeta_ref, cnt_ref):
    @pl.when(pl.program_id(0) == 0)
    def _():
        cnt_ref[...] = jnp.zeros_like(cnt_ref)

    tm = lg_ref.shape[0]
    lane = lax.broadcasted_iota(jnp.int32, (tm, LANES), 1)
    lane_f = lane.astype(F32)
    cur = jnp.where(lane < N_EXPERTS, lg_ref[...], -jnp.inf)
    vals, hots = [], []
    for _ in range(TOP_K):
        m = jnp.max(cur, axis=1, keepdims=True)
        idx = jnp.min(jnp.where(cur == m, lane_f, float(LANES)), axis=1, keepdims=True)
        hot = lane_f == idx
        vals.append(m)
        hots.append(hot)
        cur = jnp.where(hot, -jnp.inf, cur)
    exps = [jnp.exp(v - vals[0]) for v in vals]
    denom = exps[0] + exps[1] + exps[2] + exps[3]
    sel = hots[0] | hots[1] | hots[2] | hots[3]
    self32 = sel.astype(F32)
    base = cnt_ref[...]
    num = jnp.sum(self32, axis=0, keepdims=True)
    rank = _dot(lstrict_ref[...], self32.astype(BF16))
    cnt_ref[...] = base + num
    lane8 = lax.broadcasted_iota(jnp.int32, (SUBLANES, LANES), 1)
    incl = jnp.broadcast_to(num, (SUBLANES, LANES))
    shift = 1
    while shift < N_EXPERTS:
        incl = incl + jnp.where(lane8 >= shift, pltpu.roll(incl, shift, 1), 0.0)
        shift *= 2
    off = incl[:1] - num
    slot = rank + off
    meta = jnp.zeros((tm, LANES), F32)
    for k in range(TOP_K):
        s_k = jnp.sum(jnp.where(hots[k], slot, 0.0), axis=1, keepdims=True)
        meta = jnp.where(lane == k, s_k, meta)
        meta = jnp.where(lane == SUBLANES + k, exps[k] / denom, meta)
    meta_t = meta.T
    srow_ref[...] = meta_t[:SUBLANES].astype(jnp.int32)
    gate_ref[...] = meta_t[SUBLANES:2 * SUBLANES]
    row8 = lax.broadcasted_iota(jnp.int32, (SUBLANES, LANES), 0)
    tmeta_ref[0] = jnp.where(row8 == 0, base, jnp.where(row8 == 1, num, jnp.where(row8 == 2, off, 0.0)))


def _route(logits, tm):
    t = logits.shape[0]
    r = jnp.arange(tm)
    lstrict = (r[:, None] > r[None, :]).astype(BF16)
    return pl.pallas_call(
        _route_kernel,
        out_shape=(jax.ShapeDtypeStruct((SUBLANES, t), jnp.int32),
                   jax.ShapeDtypeStruct((SUBLANES, t), F32),
                   jax.ShapeDtypeStruct((t // tm, SUBLANES, LANES), F32),
                   jax.ShapeDtypeStruct((1, LANES), F32)),
        grid=(t // tm,),
        in_specs=[pl.BlockSpec((tm, LANES), lambda i: (i, 0)), pl.BlockSpec((tm, tm), lambda i: (0, 0))],
        out_specs=(pl.BlockSpec((SUBLANES, tm), lambda i: (0, i)),
                   pl.BlockSpec((SUBLANES, tm), lambda i: (0, i)),
                   pl.BlockSpec((1, SUBLANES, LANES), lambda i: (i, 0, 0)),
                   pl.BlockSpec((1, LANES), lambda i: (0, 0))),
        compiler_params=_params(("arbitrary",)),
        name="route",
    )(logits, lstrict)


def _for_each_strip(tmeta_ref, tm, make_copy, action):
    del tm

    def per_expert(e, carry):
        n, hbm0, tile0 = tmeta_ref[0, 1, e], tmeta_ref[0, 0, e], tmeta_ref[0, 2, e]
        n_chunks = lax.shift_right_logical(n, STRIP_BITS)

        def chunk(c, inner):
            action(make_copy(hbm0 + c * STRIP, tile0 + c * STRIP, STRIP))
            return inner

        lax.fori_loop(0, n_chunks, chunk, 0)
        for bit in reversed(range(STRIP_BITS)):
            size = 1 << bit

            @pl.when(jnp.bitwise_and(n, size) != 0)
            def _():
                done = jnp.bitwise_and(n, ~(2 * size - 1))
                action(make_copy(hbm0 + done, tile0 + done, size))
        return carry

    lax.fori_loop(0, N_EXPERTS, per_expert, 0)


def _rows(ref, row, n):
    return ref.at[pl.ds(pl.multiple_of(row * ROW_TILES, ROW_TILES), n * ROW_TILES), :]


def _dispatch_kernel(pstart_ref, pend_ref, srow_ref, tmeta_ref, h2_ref, xs_ref, stage_ref, zero_ref, sem):
    tm = h2_ref.shape[0] // ROW_TILES
    i = pl.program_id(0)
    slot = i % 2

    @pl.when(i == 0)
    def _():
        zero_ref[...] = jnp.zeros_like(zero_ref)

        def fill(e):
            return pltpu.make_async_copy(zero_ref, _rows(xs_ref, pend_ref[e] - EXPERT_BLOCK, EXPERT_BLOCK),
                                         sem.at[2])

        def fill_start(e, carry):
            @pl.when(pend_ref[e] > pstart_ref[e])
            def _():
                fill(e).start()
            return carry

        def fill_wait(e, carry):
            @pl.when(pend_ref[e] > pstart_ref[e])
            def _():
                fill(e).wait()
            return carry

        lax.fori_loop(0, N_EXPERTS, fill_start, 0)
        lax.fori_loop(0, N_EXPERTS, fill_wait, 0)

        def tail(j):
            return pltpu.make_async_copy(zero_ref, _rows(xs_ref, j * EXPERT_BLOCK, EXPERT_BLOCK), sem.at[2])

        used = pend_ref[N_EXPERTS - 1] // EXPERT_BLOCK
        total = xs_ref.shape[0] // (EXPERT_BLOCK * ROW_TILES)
        lax.fori_loop(used, total, lambda j, c: (tail(j).start(), c)[1], 0)
        lax.fori_loop(used, total, lambda j, c: (tail(j).wait(), c)[1], 0)

    def place(t, carry):
        row = h2_ref[pl.ds(pl.multiple_of(t * ROW_TILES, ROW_TILES), ROW_TILES), :]
        for k in range(TOP_K):
            stage_ref[slot, pl.ds(pl.multiple_of(srow_ref[t * TOP_K + k], ROW_TILES), ROW_TILES), :] = row
        return carry

    lax.fori_loop(0, tm, place, 0, unroll=ISSUE_UNROLL)

    def strips(meta_ref, s):
        return functools.partial(
            _for_each_strip, meta_ref, tm,
            lambda hbm_row, tile_row, n: pltpu.make_async_copy(_rows(stage_ref.at[s], tile_row, n),
                                                               _rows(xs_ref, hbm_row, n), sem.at[s]))

    strips(tmeta_ref, slot)(lambda copy: copy.start())

    def wait_all(s):
        pltpu.make_async_copy(stage_ref.at[s], _rows(xs_ref, 0, TOP_K * tm), sem.at[s]).wait()

    @pl.when(i > 0)
    def _():
        wait_all(1 - slot)

    @pl.when(i == pl.num_programs(0) - 1)
    def _():
        wait_all(slot)


def _dispatch(pstart, pends, srow, tmeta, h2, n_rows, tm):
    t = h2.shape[0] // ROW_TILES
    smem = lambda shape, imap: pl.BlockSpec(shape, imap, memory_space=pltpu.SMEM)
    return pl.pallas_call(
        _dispatch_kernel,
        out_shape=jax.ShapeDtypeStruct((n_rows * ROW_TILES, LANES), F32),
        grid_spec=pltpu.PrefetchScalarGridSpec(
            num_scalar_prefetch=2,
            grid=(t // tm,),
            in_specs=[smem((TOP_K * tm,), lambda i, ps, pe: (i,)),
                      smem((1, SUBLANES, LANES), lambda i, ps, pe: (i, 0, 0)),
                      pl.BlockSpec((tm * ROW_TILES, LANES), lambda i, ps, pe: (i, 0))],
            out_specs=pl.BlockSpec(memory_space=pl.ANY),
            scratch_shapes=[pltpu.VMEM((2, TOP_K * tm * ROW_TILES, LANES), F32),
                            pltpu.VMEM((EXPERT_BLOCK * ROW_TILES, LANES), F32),
                            pltpu.SemaphoreType.DMA((3,))],
        ),
        compiler_params=_params(("arbitrary",)),
        name="dispatch",
    )(pstart, pends, srow, tmeta, h2)


def _combine_kernel(n_prompt_tiles, srow_ref, gate_ref, tmeta_ref, tmeta_next_ref, x1_ref, nw_ref, yb_ref,
                    op_ref, os_ref, buf_ref, acc_ref, sem):
    tm = x1_ref.shape[0]
    i = pl.program_id(0)
    slot = i % 2

    def strips(meta_ref, s):
        return functools.partial(
            _for_each_strip, meta_ref, tm,
            lambda hbm_row, tile_row, n: pltpu.make_async_copy(_rows(yb_ref, hbm_row, n),
                                                               _rows(buf_ref.at[s], tile_row, n), sem.at[s]))

    @pl.when(i == 0)
    def _():
        strips(tmeta_ref, 0)(lambda copy: copy.start())

    @pl.when(i + 1 < pl.num_programs(0))
    def _():
        strips(tmeta_next_ref, 1 - slot)(lambda copy: copy.start())

    pltpu.make_async_copy(_rows(yb_ref, 0, TOP_K * tm), buf_ref.at[slot], sem.at[slot]).wait()

    def gather(t, carry):
        def term(k):
            row = buf_ref[slot, pl.ds(pl.multiple_of(srow_ref[t * TOP_K + k], ROW_TILES), ROW_TILES), :]
            return row * gate_ref[t * TOP_K + k]

        moe = term(0)
        for k in range(1, TOP_K):
            moe = moe + term(k)
        acc_ref[pl.ds(pl.multiple_of(t * ROW_TILES, ROW_TILES), ROW_TILES), :] = moe
        return carry

    lax.fori_loop(0, tm, gather, 0, unroll=ISSUE_UNROLL)
    x2 = x1_ref[...] + _load_row_tiles(acc_ref, tm)
    ms = jnp.mean(x2 * x2, axis=-1, keepdims=True)
    out = x2 * lax.rsqrt(ms + EPS) * nw_ref[...]

    @pl.when(i < n_prompt_tiles)
    def _():
        op_ref[...] = out

    @pl.when(i >= n_prompt_tiles)
    def _():
        os_ref[...] = out


def _combine(srow, gates, tmeta, x1, norm_final, yb, n_prompt, tm):
    t = x1.shape[0]
    nt = t // tm
    npt = n_prompt // tm
    smem = lambda shape, imap: pl.BlockSpec(shape, imap, memory_space=pltpu.SMEM)
    return pl.pallas_call(
        functools.partial(_combine_kernel, npt),
        out_shape=(jax.ShapeDtypeStruct((n_prompt, D_MODEL), F32),
                   jax.ShapeDtypeStruct((t - n_prompt, D_MODEL), F32)),
        grid=(nt,),
        in_specs=[smem((TOP_K * tm,), lambda i: (i,)),
                  smem((TOP_K * tm,), lambda i: (i,)),
                  smem((1, SUBLANES, LANES), lambda i: (i, 0, 0)),
                  smem((1, SUBLANES, LANES), lambda i: (jnp.minimum(i + 1, nt - 1), 0, 0)),
                  pl.BlockSpec((tm, D_MODEL), lambda i: (i, 0)),
                  pl.BlockSpec((1, D_MODEL), lambda i: (0, 0)),
                  pl.BlockSpec(memory_space=pl.ANY)],
        out_specs=_split_rows(tm, npt, D_MODEL),
        scratch_shapes=[pltpu.VMEM((2, TOP_K * tm * ROW_TILES, LANES), F32),
                        pltpu.VMEM((tm * ROW_TILES, LANES), F32),
                        pltpu.SemaphoreType.DMA((2,))],
        compiler_params=_params(("arbitrary",)),
        name="combine",
    )(srow, gates, tmeta, tmeta, x1, norm_final, yb)


def _expert_kernel(be_ref, nv_ref, next_ref, xs_ref, bgu_ref, bd_ref, wgu_hbm, wd_hbm, yb_ref,
                   wgu_f, wd_f, wgu_b, wd_b, sem):
    i = pl.program_id(0)

    def fetch(e):
        return (pltpu.make_async_copy(wgu_hbm.at[e], wgu_f, sem.at[0]),
                pltpu.make_async_copy(wd_hbm.at[e], wd_f, sem.at[1]))

    @pl.when(i == 0)
    def _():
        for copy in fetch(be_ref[0]):
            copy.start()

    first = jnp.logical_or(i == 0, be_ref[i] != be_ref[jnp.maximum(i - 1, 0)])

    @pl.when(jnp.logical_and(first, i < nv_ref[0]))
    def _():
        for copy in fetch(be_ref[i]):
            copy.wait()
        wgu_b[...] = wgu_f[...].astype(BF16)
        wd_b[...] = wd_f[...].astype(BF16)

        @pl.when(next_ref[i] >= 0)
        def _():
            for copy in fetch(next_ref[i]):
                copy.start()

    @pl.when(i < nv_ref[0])
    def _():
        gu = _dot(_load_row_tiles(xs_ref, EXPERT_BLOCK).astype(BF16), wgu_b[...]) + bgu_ref[0]
        glu = jnp.minimum(gu[:, :D_FF], SWIGLU_LIMIT)
        lin = jnp.clip(gu[:, D_FF:], -SWIGLU_LIMIT, SWIGLU_LIMIT)
        act = glu * jax.nn.sigmoid(SWIGLU_ALPHA * glu) * (lin + 1.0)
        _store_row_tiles(yb_ref, _dot(act.astype(BF16), wd_b[...]) + bd_ref[0])

    @pl.when(i >= nv_ref[0])
    def _():
        yb_ref[...] = jnp.zeros_like(yb_ref)


def _experts(block_expert, n_valid, next_expert, xs, w_gu, b_gu, w_d, b_d):
    rows = EXPERT_BLOCK * ROW_TILES
    nb = xs.shape[0] // rows
    return pl.pallas_call(
        _expert_kernel,
        out_shape=jax.ShapeDtypeStruct(xs.shape, F32),
        grid_spec=pltpu.PrefetchScalarGridSpec(
            num_scalar_prefetch=3,
            grid=(nb,),
            in_specs=[pl.BlockSpec((rows, LANES), lambda i, be, nv, nx: (jnp.minimum(i, nv[0] - 1), 0)),
                      pl.BlockSpec((1, 1, 2 * D_FF), lambda i, be, nv, nx: (be[i], 0, 0)),
                      pl.BlockSpec((1, 1, D_MODEL), lambda i, be, nv, nx: (be[i], 0, 0)),
                      pl.BlockSpec(memory_space=pl.ANY),
                      pl.BlockSpec(memory_space=pl.ANY)],
            out_specs=pl.BlockSpec((rows, LANES), lambda i, be, nv, nx: (i, 0)),
            scratch_shapes=[pltpu.VMEM((D_MODEL, 2 * D_FF), F32), pltpu.VMEM((D_FF, D_MODEL), F32),
                            pltpu.VMEM((D_MODEL, 2 * D_FF), BF16), pltpu.VMEM((D_FF, D_MODEL), BF16),
                            pltpu.SemaphoreType.DMA((2,))],
        ),
        compiler_params=_params(("arbitrary",)),
        name="experts",
    )(block_expert, n_valid, next_expert, xs, b_gu, b_d, w_gu, w_d)


def _expert_layout(counts, n_blocks):
    counts = counts.astype(jnp.int32)
    padded = (counts + EXPERT_BLOCK - 1) // EXPERT_BLOCK * EXPERT_BLOCK
    pends = jnp.cumsum(padded)
    pstart = pends - padded
    block_start = jnp.arange(n_blocks, dtype=jnp.int32) * EXPERT_BLOCK
    block_expert = jnp.minimum(jnp.sum(pends[None, :] <= block_start[:, None], axis=1), N_EXPERTS - 1)
    n_valid = (pends[-1:] // EXPERT_BLOCK).astype(jnp.int32)
    seg_end = jnp.sum(jnp.where(block_expert[:, None] == jnp.arange(N_EXPERTS), pends[None, :], 0), axis=1)
    after = seg_end // EXPERT_BLOCK
    next_expert = jnp.where(after < n_valid[0], block_expert[jnp.minimum(after, n_blocks - 1)], -1)
    return (pstart.astype(jnp.int32), pends.astype(jnp.int32), block_expert.astype(jnp.int32), n_valid,
            next_expert.astype(jnp.int32))


def _permute_w_in(w_in):
    sizes = [RET_QK, RET_QK, RET_V, RET_V, SSM_INNER, CONV_DIM, SSM_HEADS, D_MODEL, D_MODEL]
    offs = [0]
    for s in sizes:
        offs.append(offs[-1] + s)
    q, k, v, g_ret, z, xbc, dt, g_a, g_b = [w_in[:, offs[i]:offs[i + 1]] for i in range(len(sizes))]
    gn = SSM_GROUPS * SSM_DSTATE
    xs, bm, cm = xbc[:, :SSM_INNER], xbc[:, SSM_INNER:SSM_INNER + gn], xbc[:, SSM_INNER + gn:]
    dt_pad = jnp.pad(dt, ((0, 0), (0, _ColsB.TOTAL - _ColsB.DT - SSM_HEADS)))
    w_a = jnp.concatenate([v, g_ret, g_a, g_b, q, k], axis=1).astype(BF16)
    w_b = jnp.concatenate([z, xs, bm, cm, dt_pad], axis=1).astype(BF16)
    return w_a, w_b


def _forward(x_prompt, x_sample, state_ret, state_ssm, state_conv, norm_mix, w_in, ret_norm, w_out_ret,
             conv_w, conv_b, dt_bias, a_log, d_skip, ssm_norm, w_out_ssm, w_out, norm_ffn,
             w_router, b_router, w_gate_up, b_gate_up, w_down, b_down, norm_final):
    bp, lp, _ = x_prompt.shape
    bs, ls, _ = x_sample.shape
    assert lp % ROWS == 0 and bs % SAMPLE_PER_TILE == 0 and ls * SAMPLE_PER_TILE == ROWS
    tp, ts = bp * lp, bs * ls
    t_all = tp + ts
    xp, xs_in = x_prompt.reshape(tp, D_MODEL), x_sample.reshape(ts, D_MODEL)

    w_a, w_b = _permute_w_in(w_in[0])
    norm_w = norm_mix[0].reshape(1, D_MODEL)
    proj = _inproj(xp, xs_in, norm_w, w_a)
    proj_ssd_s = _inproj(None, xs_in, norm_w, w_b)


    sp = _ssd_params(conv_w[0], conv_b[0], dt_bias[0], a_log[0], d_skip[0], ssm_norm[0])
    y_p, ssm_p, conv_p, a_p, ret_p = _mixer_prompt(xp, norm_w, w_b, proj, bp, lp, sp, ret_norm[0])
    y_s, ssm_s, conv_s, a_s, ret_s = _mixer_sample(proj, proj_ssd_s, tp, bs, ls, sp, state_ssm[0], state_conv[0],
                                                   state_ret[0], ret_norm[0])

    w_router_pad = jnp.pad(w_router[0], ((0, 0), (0, LANES - N_EXPERTS))).astype(BF16)
    b_router_pad = jnp.pad(b_router[0], (0, LANES - N_EXPERTS)).reshape(1, LANES)
    x1, h2, logits = _post(a_p, a_s, y_p, y_s, proj, xp, xs_in, w_out_ret[0].astype(BF16), w_out_ssm[0].astype(BF16),
                           w_out[0].astype(BF16), norm_ffn[0].reshape(1, D_MODEL), w_router_pad, b_router_pad)

    tm_moe = _pick(math.gcd(tp, ts), 512)
    srow, gates, tmeta, counts = _route(logits, tm_moe)
    n_blocks = -(-(t_all * TOP_K + N_EXPERTS * (EXPERT_BLOCK - 1)) // EXPERT_BLOCK)
    pstart, pends, block_expert, n_valid, next_expert = _expert_layout(counts[0, :N_EXPERTS], n_blocks)
    tmeta = tmeta.astype(jnp.int32)
    run_start = tmeta[:, :1, :] + jnp.pad(pstart, (0, LANES - N_EXPERTS))
    tmeta = jnp.concatenate([run_start, tmeta[:, 1:, :]], axis=1)
    srow = (srow[:TOP_K].T * ROW_TILES).reshape(-1)
    gates = gates[:TOP_K].T.reshape(-1)
    xs = _dispatch(pstart, pends, srow, tmeta, h2, n_blocks * EXPERT_BLOCK, tm_moe)
    yb = _experts(block_expert, n_valid, next_expert, xs, w_gate_up[0],
                  b_gate_up[0].reshape(N_EXPERTS, 1, 2 * D_FF), w_down[0], b_down[0].reshape(N_EXPERTS, 1, D_MODEL))
    out_p, out_s = _combine(srow, gates, tmeta, x1, norm_final.reshape(1, D_MODEL), yb, tp, tm_moe)

    shape_s = (1, bs, SSM_HEADS, SSM_HEADDIM, SSM_DSTATE)
    shape_p = (1, bp, SSM_HEADS, SSM_HEADDIM, SSM_DSTATE)
    return (out_p.reshape(bp, lp, D_MODEL), out_s.reshape(bs, ls, D_MODEL),
            ret_p[None], ret_s[None], ssm_p.reshape(shape_p), ssm_s.reshape(shape_s),
            conv_p[None], conv_s[None])


def kernel(x_prompt, x_sample, state_ret, state_ssm, state_conv, norm_mix, w_in, ret_norm, w_out_ret, conv_w, conv_b, dt_bias, a_log, d_skip, ssm_norm, w_out_ssm, w_out, norm_ffn, w_router, b_router, w_gate_up, b_gate_up, w_down, b_down, norm_final):
    return _forward(x_prompt, x_sample, state_ret, state_ssm, state_conv, norm_mix, w_in, ret_norm, w_out_ret,
                    conv_w, conv_b, dt_bias, a_log, d_skip, ssm_norm, w_out_ssm, w_out, norm_ffn,
                    w_router, b_router, w_gate_up, b_gate_up, w_down, b_down, norm_final)
```

```python
import functools
import math

import jax
import jax.numpy as jnp
from jax import lax
from jax.experimental import pallas as pl
from jax.experimental.pallas import tpu as pltpu

F32 = jnp.float32
BF16 = jnp.bfloat16

D_MODEL = 1024
PAST_LEN = 16384
RET_HEADS = 4
RET_DK = 128
RET_DV = 256
RET_QK = RET_HEADS * RET_DK
RET_V = RET_HEADS * RET_DV
ROPE_BASE = 10000.0
SSM_INNER = 2 * D_MODEL
SSM_HEADDIM = 64
SSM_HEADS = SSM_INNER // SSM_HEADDIM
SSM_GROUPS = 4
SSM_DSTATE = 128
HEADS_PER_GROUP = SSM_HEADS // SSM_GROUPS
GROUP_INNER = SSM_INNER // SSM_GROUPS
CONV_WIDTH = 4
CONV_DIM = SSM_INNER + 2 * SSM_GROUPS * SSM_DSTATE
CHUNK = 128
N_EXPERTS = 32
TOP_K = 4
D_FF = D_MODEL
SWIGLU_LIMIT = 7.0
SWIGLU_ALPHA = 1.702
EPS = 1e-6

LANES = 128
SUBLANES = 8
ROWS = 128
SAMPLE_PER_TILE = 16
EXPERT_BLOCK = 512
ROW_TILES = D_MODEL // LANES
ISSUE_UNROLL = 4
STRIP_BITS = 5
STRIP = 1 << STRIP_BITS
VMEM_LIMIT = 56 * 1024 * 1024


class _ColsA:
    V = 0
    GRET = V + RET_V
    GA = GRET + RET_V
    GB = GA + D_MODEL
    Q = GB + D_MODEL
    K = Q + RET_QK
    TOTAL = K + RET_QK


class _ColsB:
    Z = 0
    XS = Z + SSM_INNER
    B = XS + SSM_INNER
    C = B + SSM_GROUPS * SSM_DSTATE
    DT = C + SSM_GROUPS * SSM_DSTATE
    TOTAL = DT + 2 * LANES


def _pick(n, target):
    best = None
    for c in range(LANES, target + 1, LANES):
        if n % c == 0:
            best = c
    assert best is not None, (n, target)
    return best


def _params(sem, **kw):
    return pltpu.CompilerParams(dimension_semantics=sem, vmem_limit_bytes=VMEM_LIMIT, **kw)


def _dot(a, b):
    return jnp.dot(a, b, preferred_element_type=F32)


def _dot_nt(a, b):
    return lax.dot_general(a, b, (((1,), (1,)), ((), ())), preferred_element_type=F32)


def _dot_tn(a, b):
    return lax.dot_general(a, b, (((0,), (0,)), ((), ())), preferred_element_type=F32)


def _silu(x):
    return x * jax.nn.sigmoid(x)


def _store_row_tiles(ref, value, *lead):
    n = value.shape[0]
    for s in range(ROW_TILES):
        ref[(*lead, pl.ds(s, n, stride=ROW_TILES), slice(None))] = value[:, s * LANES:(s + 1) * LANES]


def _load_row_tiles(ref, n, *lead):
    return jnp.concatenate([ref[(*lead, pl.ds(s, n, stride=ROW_TILES), slice(None))] for s in range(ROW_TILES)],
                           axis=1)


def _inproj_kernel(n_prompt_tiles, xp_ref, xs_ref, nw_ref, w_ref, o_ref, h_ref):
    @pl.when(pl.program_id(1) == 0)
    def _():
        x = jnp.where(pl.program_id(0) < n_prompt_tiles, xp_ref[...], xs_ref[...])
        ms = jnp.mean(x * x, axis=-1, keepdims=True)
        h_ref[...] = (x * lax.rsqrt(ms + EPS) * nw_ref[...]).astype(BF16)

    o_ref[...] = _dot(h_ref[...], w_ref[...])


def _split_rows(tm, npt, width):
    prompt = pl.BlockSpec((tm, width), lambda i, *_: (jnp.minimum(i, npt - 1), 0))
    sample = pl.BlockSpec((tm, width), lambda i, *_: (jnp.maximum(i - npt, 0), 0))
    return prompt, sample


def _inproj(xp, xs, norm_w, w):
    total = w.shape[1]
    t = xs.shape[0] + (0 if xp is None else xp.shape[0])
    tm = _pick(xs.shape[0] if xp is None else math.gcd(xp.shape[0], xs.shape[0]), 1024)
    npt = 0 if xp is None else xp.shape[0] // tm
    tn = _pick(total, 2560)
    assert tn % (2 * LANES) == 0
    x_specs = list(_split_rows(tm, npt, D_MODEL)) if npt else [pl.BlockSpec((tm, D_MODEL), lambda i, j: (i, 0))] * 2
    return pl.pallas_call(
        functools.partial(_inproj_kernel, npt),
        out_shape=jax.ShapeDtypeStruct((t, total), F32),
        grid=(t // tm, total // tn),
        in_specs=[
            *x_specs,
            pl.BlockSpec((1, D_MODEL), lambda i, j: (0, 0)),
            pl.BlockSpec((D_MODEL, tn), lambda i, j: (0, j)),
        ],
        out_specs=pl.BlockSpec((tm, tn), lambda i, j: (i, j)),
        scratch_shapes=[pltpu.VMEM((tm, D_MODEL), BF16)],
        compiler_params=_params(("parallel", "arbitrary")),
        name="in_proj",
    )(xs if xp is None else xp, xs, norm_w, w)


def _ret_log_decay():
    return jnp.log(1.0 - 2.0 ** (-5.0 - jnp.arange(RET_HEADS, dtype=F32)))


def _ret_tables(c):
    lg = _ret_log_decay()
    r = jnp.arange(ROWS)
    t = (r % c).astype(F32)
    seg = r // c
    diff = t[:, None] - t[None, :]
    ok = (seg[:, None] == seg[None, :]) & (diff >= 0)
    dm = jnp.where(ok[None], jnp.exp(lg[:, None, None] * jnp.maximum(diff, 0.0)[None]), 0.0)
    qd = jnp.exp(lg[:, None] * (t[None, :] + 1.0))
    kd = jnp.exp(lg[:, None] * (c - 1.0 - t[None, :]))
    cd = jnp.exp(lg * c)
    qd = jnp.broadcast_to(qd[:, :, None], (RET_HEADS, ROWS, LANES))
    kd = jnp.broadcast_to(kd[:, :, None], (RET_HEADS, ROWS, LANES))
    cd = jnp.broadcast_to(cd[:, None, None], (RET_HEADS, 1, RET_DV))
    return dm, qd, kd, cd


def _rope_tables(pos):
    half = RET_DK // 2
    inv_freq = 1.0 / (ROPE_BASE ** jnp.linspace(0.0, 1.0, half, dtype=F32))
    ang = pos.astype(F32)[:, None] * inv_freq[None, :]
    cos, sin = jnp.cos(ang), jnp.sin(ang)
    return jnp.concatenate([cos, cos], -1), jnp.concatenate([-sin, sin], -1)


def _block_mask(dtype):
    r = jnp.arange(ROWS) // (ROWS // SAMPLE_PER_TILE)
    b = jnp.arange(SAMPLE_PER_TILE * LANES) // LANES
    return (r[:, None] == b[None, :]).astype(dtype)


def _ret_common(q, k, v, g, cosf, sinf, dm, qd, kd, nw):
    qr = q * cosf + pltpu.roll(q, RET_DK // 2, 1) * sinf
    kr = (k * cosf + pltpu.roll(k, RET_DK // 2, 1) * sinf) * (RET_DK ** -0.5)
    vb = v.astype(BF16)
    scores = _dot_nt(qr.astype(BF16), kr.astype(BF16)) * dm
    o_intra = _dot(scores.astype(BF16), vb)
    q_state = (qr * qd).astype(BF16)
    k_state = kr * kd

    def finish(o):
        ms = jnp.mean(o * o, axis=-1, keepdims=True)
        on = o * lax.rsqrt(ms + EPS) * nw
        return (_silu(g) * on).astype(BF16)

    return o_intra, q_state, k_state, vb, finish


def _ret_prompt_step(q_ref, k_ref, v_ref, g_ref, cos_ref, sin_ref, dm_ref, qd_ref, kd_ref, cd_ref, nw_ref,
                     a_ref, s_ref):
    cosf, sinf = cos_ref[...], sin_ref[...]
    for h in range(RET_HEADS):
        dk = slice(h * RET_DK, (h + 1) * RET_DK)
        dv = slice(h * RET_DV, (h + 1) * RET_DV)
        o_intra, q_state, k_state, vb, finish = _ret_common(
            q_ref[:, dk], k_ref[:, dk], v_ref[:, dv], g_ref[:, dv], cosf, sinf, dm_ref[h], qd_ref[h], kd_ref[h],
            nw_ref[h])
        s = s_ref[0, h]
        a_ref[:, dv] = finish(o_intra + _dot(q_state, s.astype(BF16)))
        s_ref[0, h] = s * cd_ref[h] + _dot_tn(k_state.astype(BF16), vb)


def _ret_sample_kernel(q_ref, k_ref, v_ref, g_ref, cos_ref, sin_ref, dm_ref, qd_ref, kd_ref, cd_ref, nw_ref,
                       bm_ref, bmt_ref, s0_ref, a_ref, s_ref):
    o_intra, q_state, k_state, vb, finish = _ret_common(
        q_ref[...], k_ref[...], v_ref[...], g_ref[...], cos_ref[...], sin_ref[...], dm_ref[0], qd_ref[0], kd_ref[0],
        nw_ref[0])
    nb = SAMPLE_PER_TILE
    s0 = s0_ref[:, 0]
    q_bd = jnp.concatenate([q_state] * nb, axis=1) * bm_ref[...]
    a_ref[...] = finish(o_intra + _dot(q_bd, s0.reshape(nb * RET_DK, RET_DV).astype(BF16)))
    k_t = k_state.T.astype(BF16)
    k_bd_t = jnp.concatenate([k_t] * nb, axis=0) * bmt_ref[...]
    ds = _dot(k_bd_t, vb).reshape(nb, RET_DK, RET_DV)
    s_ref[:, 0] = s0 * cd_ref[0] + ds


def _ret_specs(row_block):
    def at(width, off):
        return lambda *ids: (row_block(*ids)[0], off // width + row_block(*ids)[1])

    def head(*shape):
        return pl.BlockSpec((1,) + shape, lambda *ids: (row_block(*ids)[1],) + (0,) * len(shape))

    return [
        pl.BlockSpec((ROWS, RET_DK), at(RET_DK, _ColsA.Q)),
        pl.BlockSpec((ROWS, RET_DK), at(RET_DK, _ColsA.K)),
        pl.BlockSpec((ROWS, RET_DV), at(RET_DV, _ColsA.V)),
        pl.BlockSpec((ROWS, RET_DV), at(RET_DV, _ColsA.GRET)),
    ], head


def _retention_sample_call(proj, row0, bsz, seq, state, ret_norm):
    rb0 = row0 // ROWS
    dm, qd, kd, cd = _ret_tables(seq)
    pos = PAST_LEN + jnp.arange(seq, dtype=jnp.int32)
    cosf, sinf = _rope_tables(jnp.tile(pos, SAMPLE_PER_TILE))
    rb = lambda t, h: (rb0 + t, h)
    specs, head = _ret_specs(rb)
    full = lambda *shape: pl.BlockSpec(shape, lambda t, h: (0,) * len(shape))
    specs += [
        full(ROWS, RET_DK), full(ROWS, RET_DK),
        head(ROWS, ROWS), head(ROWS, LANES), head(ROWS, LANES), head(1, RET_DV), head(1, RET_DV),
        full(ROWS, SAMPLE_PER_TILE * LANES), full(SAMPLE_PER_TILE * LANES, ROWS),
        pl.BlockSpec((SAMPLE_PER_TILE, 1, RET_DK, RET_DV), lambda t, h: (t, h, 0, 0)),
    ]
    bm = _block_mask(BF16)
    operands = (proj, proj, proj, proj, cosf, sinf, dm, qd, kd, cd, ret_norm.reshape(RET_HEADS, 1, RET_DV),
                bm, bm.T, state)
    out_shapes = (jax.ShapeDtypeStruct((bsz * seq, RET_V), BF16), jax.ShapeDtypeStruct(state.shape, F32))
    out_specs = (pl.BlockSpec((ROWS, RET_DV), lambda t, h: (t, h)),
                 pl.BlockSpec((SAMPLE_PER_TILE, 1, RET_DK, RET_DV), lambda t, h: (t, h, 0, 0)))
    return specs, operands, out_shapes, out_specs


def _softplus(x):
    return jnp.maximum(x, 0.0) + jnp.log1p(jnp.exp(-jnp.abs(x)))


def _conv_piece(x, pred, w_ref, b_ref):
    width = x.shape[1]
    x3 = x.reshape(ROWS // SUBLANES, SUBLANES, width)
    t8 = lax.broadcasted_iota(jnp.int32, x3.shape, 1)
    acc = b_ref[...].reshape(1, 1, width)
    for i in range(CONV_WIDTH):
        s = CONV_WIDTH - 1 - i
        tap = x3 if s == 0 else pltpu.roll(jnp.where(t8 >= SUBLANES - s, pred, x3), s, 1)
        acc = acc + tap * w_ref[pl.ds(i, 1), :].reshape(1, 1, width)
    return _silu(acc).reshape(ROWS, width)


def _ssd_decay_terms(c, dt_pre, a_log, ltri):
    dt = _softplus(dt_pre)
    d_a = dt * (-jnp.exp(a_log))
    hi = d_a.astype(BF16)
    r1 = d_a - hi.astype(F32)
    mid = r1.astype(BF16)
    lo = (r1 - mid.astype(F32)).astype(BF16)
    cum = _dot(ltri, hi) + _dot(ltri, mid) + _dot(ltri, lo)
    cum3 = cum.reshape(ROWS // c, c, LANES)
    c_last = jnp.broadcast_to(cum3[:, c - 1:c, :], cum3.shape).reshape(ROWS, LANES)
    to_end = jnp.exp(c_last - cum) * dt
    e_cum = jnp.exp(cum)
    return cum, to_end, e_cum, cum.T, dt.T


def _ssd_group(c, terms, head0, z_ref, dsk_ref, nrm_ref, xs, bm, cm, y_ref, state_io, zero=None):
    cum, to_end, e_cum, cum_t, dt_t = terms
    b_b, c_b = bm.astype(BF16), cm.astype(BF16)
    cb = _dot_nt(c_b, b_b)
    ri = lax.broadcasted_iota(jnp.int32, (ROWS, ROWS), 0)
    ci = lax.broadcasted_iota(jnp.int32, (ROWS, ROWS), 1)
    shift = c.bit_length() - 1
    causal = (ci <= ri) & (jnp.right_shift(ri, shift) == jnp.right_shift(ci, shift))
    lo_half = lax.broadcasted_iota(jnp.int32, (ROWS, LANES), 1) < SSM_HEADDIM

    def col(a, h):
        return jnp.broadcast_to(a[:, h:h + 1], (ROWS, LANES))

    def weights(h):
        seg = col(cum, h) - jnp.broadcast_to(cum_t[h:h + 1, :], (ROWS, ROWS))
        decay = jnp.exp(jnp.where(causal, seg, -jnp.inf))
        return (cb * decay * jnp.broadcast_to(dt_t[h:h + 1, :], (ROWS, ROWS))).astype(BF16)

    pieces = []
    for m in range(HEADS_PER_GROUP // 2):
        ha, hb = head0 + 2 * m, head0 + 2 * m + 1
        xp = xs[:, m * LANES:(m + 1) * LANES]
        w2 = jnp.concatenate([weights(ha), weights(hb)], axis=1)
        x2 = jnp.concatenate([jnp.where(lo_half, xp, 0.0), jnp.where(lo_half, 0.0, xp)], axis=0).astype(BF16)
        y = _dot(w2, x2)
        xw = (xp * jnp.where(lo_half, col(to_end, ha), col(to_end, hb))).astype(BF16)
        y_state = state_io(m, ha, c_b, xw, b_b, e_cum)
        y = y + y_state * jnp.where(lo_half, col(e_cum, ha), col(e_cum, hb))
        pieces.append(y + xp * dsk_ref[:, m * LANES:(m + 1) * LANES])
    if zero is not None:
        pieces[0] = jnp.concatenate([pieces[0][:SUBLANES] + zero, pieces[0][SUBLANES:]], axis=0)
    yg = jnp.concatenate(pieces, axis=1) * _silu(z_ref[...])
    ms = jnp.mean(yg * yg, axis=-1, keepdims=True)
    y_ref[...] = (yg * lax.rsqrt(ms + EPS) * nrm_ref[...]).astype(BF16)


def _pair_decay(e_cum, row, ha):
    top = lax.broadcasted_iota(jnp.int32, (2 * SSM_HEADDIM, SSM_DSTATE), 0) < SSM_HEADDIM
    ea = jnp.broadcast_to(e_cum[row:row + 1, ha:ha + 1], top.shape)
    eb = jnp.broadcast_to(e_cum[row:row + 1, ha + 1:ha + 2], top.shape)
    return jnp.where(top, ea, eb)


def _mixer_prompt_kernel(x0_ref, xn_ref, nw_ref, w_ref, cwx_ref, cbx_ref, cwb_ref, cbb_ref, cwc_ref, cbc_ref,
                         dtb_ref, alog_ref, dsk_ref, nrm_ref, ltri_ref, *rest):
    ret_in, (y_ref, hs_ref, px_ref, pb_ref, pc_ref, a_ref, s_ref, proj_ref) = rest[:11], rest[11:]
    c = pl.program_id(1)

    def normed(x_ref):
        x = x_ref[...]
        ms = jnp.mean(x * x, axis=-1, keepdims=True)
        return (x * lax.rsqrt(ms + EPS) * nw_ref[...]).astype(BF16)

    def project(x_ref, slot):
        proj_ref[slot] = _dot(normed(x_ref), w_ref[...])

    @pl.when(c == 0)
    def _():
        hs_ref[...] = jnp.zeros_like(hs_ref)
        px_ref[...] = jnp.zeros_like(px_ref)
        pb_ref[...] = jnp.zeros_like(pb_ref)
        pc_ref[...] = jnp.zeros_like(pc_ref)
        s_ref[...] = jnp.zeros_like(s_ref)
        project(x0_ref, 0)

    def conv(x_ref, p_ref, w_ref, bias_ref):
        x = x_ref[...]
        x3 = x.reshape(ROWS // SUBLANES, SUBLANES, x.shape[1])
        pred = jnp.concatenate([p_ref[...], x3[:-1]], axis=0)
        out = _conv_piece(x, pred, w_ref, bias_ref)
        p_ref[0] = x3[-1]
        return out

    def step(cur_slot, next_slot):
        h_next = normed(xn_ref)
        bounds = [0, 5 * 256, 10 * 256, 15 * 256, _ColsB.TOTAL]

        def project_slice(g):
            lo, hi = bounds[g], bounds[g + 1]
            part = _dot(h_next, w_ref[:, lo:hi])
            proj_ref[next_slot, :, lo:hi] = part
            bits = pltpu.bitcast(part[ROWS - SUBLANES:, hi - lo - LANES:], jnp.uint32)
            sixteen = jnp.uint32(16)
            return pltpu.bitcast(lax.shift_right_logical(lax.shift_right_logical(bits, sixteen), sixteen), F32)

        _ret_prompt_step(*ret_in, a_ref, s_ref)
        cur = proj_ref.at[cur_slot]
        gn = SSM_GROUPS * SSM_DSTATE
        z_ref = cur.at[:, _ColsB.Z:_ColsB.Z + SSM_INNER]
        xs_ref = cur.at[:, _ColsB.XS:_ColsB.XS + SSM_INNER]
        b_ref = cur.at[:, _ColsB.B:_ColsB.B + gn]
        c_ref = cur.at[:, _ColsB.C:_ColsB.C + gn]
        dt_ref = cur.at[:, _ColsB.DT:_ColsB.DT + LANES]
        terms = _ssd_decay_terms(CHUNK, dt_ref[...] + dtb_ref[...], alog_ref[...], ltri_ref[...])
        for g in range(SSM_GROUPS):
            def cols(ref, width, g=g):
                return ref.at[..., g * width:(g + 1) * width]

            gi, n = GROUP_INNER, SSM_DSTATE
            xs = conv(cols(xs_ref, gi), cols(px_ref, gi), cols(cwx_ref, gi), cols(cbx_ref, gi))
            bm = conv(cols(b_ref, n), cols(pb_ref, n), cols(cwb_ref, n), cols(cbb_ref, n))
            cm = conv(cols(c_ref, n), cols(pc_ref, n), cols(cwc_ref, n), cols(cbc_ref, n))
            hs_g = hs_ref.at[0, g * gi:(g + 1) * gi, :]

            def state_io(m, ha, c_b, xw, b_b, e_cum, hs_g=hs_g):
                rows = pl.ds(m * 2 * SSM_HEADDIM, 2 * SSM_HEADDIM)
                h = hs_g[rows, :]
                hs_g[rows, :] = h * _pair_decay(e_cum, ROWS - 1, ha) + _dot_tn(xw, b_b)
                return _dot_nt(c_b, h.astype(BF16))

            _ssd_group(CHUNK, terms, g * HEADS_PER_GROUP, cols(z_ref, gi), cols(dsk_ref, gi), cols(nrm_ref, gi),
                       xs, bm, cm, cols(y_ref, gi), state_io, zero=project_slice(g))

    @pl.when(c % 2 == 0)
    def _():
        step(0, 1)

    @pl.when(c % 2 == 1)
    def _():
        step(1, 0)


def _ssd_sample_kernel(seq, z_ref, xs_ref, b_ref, c_ref, dt_ref, cwx_ref, cbx_ref, cwb_ref, cbb_ref, cwc_ref,
                       cbc_ref, dtb_ref, alog_ref, dsk_ref, nrm_ref, ltri_ref, bmask_ref, px_ref, pb_ref, pc_ref,
                       h0_ref, y_ref, hs_ref, csx_ref, csb_ref, csc_ref):
    nb = SAMPLE_PER_TILE

    def conv(x_ref, p_ref, w_ref, bias_ref, cs_ref):
        x = x_ref[...]
        cs_ref[...] = x.reshape(nb, seq, x.shape[1])
        return _conv_piece(x, p_ref[...], w_ref, bias_ref)

    xs = conv(xs_ref, px_ref, cwx_ref, cbx_ref, csx_ref)
    bm = conv(b_ref, pb_ref, cwb_ref, cbb_ref, csb_ref)
    cm = conv(c_ref, pc_ref, cwc_ref, cbc_ref, csc_ref)
    bmask = bmask_ref[...]

    def state_io(m, ha, c_b, xw, b_b, e_cum):
        rows = pl.ds(m * 2 * SSM_HEADDIM, 2 * SSM_HEADDIM)
        c_bd = jnp.concatenate([c_b] * nb, axis=1) * bmask
        b_bd = jnp.concatenate([b_b] * nb, axis=1) * bmask
        hs = [h0_ref[i, rows, :] for i in range(nb)]
        h_cat = jnp.concatenate(hs, axis=1).astype(BF16)
        dh = _dot_tn(xw, b_bd)
        for i in range(nb):
            decay = _pair_decay(e_cum, i * seq + seq - 1, ha)
            hs_ref[i, rows, :] = hs[i] * decay + dh[:, i * SSM_DSTATE:(i + 1) * SSM_DSTATE]
        return _dot_nt(c_bd, h_cat)

    shift = jnp.bitwise_and(LANES - HEADS_PER_GROUP * pl.program_id(1), LANES - 1)
    dt_pre = pltpu.roll(dt_ref[...] + dtb_ref[...], shift, 1)
    a_log = pltpu.roll(jnp.broadcast_to(alog_ref[...], (SUBLANES, LANES)), shift, 1)[:1]
    terms = _ssd_decay_terms(seq, dt_pre, a_log, ltri_ref[...])
    _ssd_group(seq, terms, 0, z_ref, dsk_ref, nrm_ref, xs, bm, cm, y_ref, state_io)


def _ssd_tables(c):
    r = jnp.arange(ROWS)
    ltri = ((r[:, None] >= r[None, :]) & ((r[:, None] // c) == (r[None, :] // c))).astype(BF16)
    return ltri


def _ssd_specs(row_block):
    def at(width, off):
        return lambda *ids: (row_block(*ids)[0], off // width + row_block(*ids)[1])

    def grp(rows, width):
        return pl.BlockSpec((rows, width), lambda *ids: (0, row_block(*ids)[1]))

    n = SSM_DSTATE
    return [
        pl.BlockSpec((ROWS, GROUP_INNER), at(GROUP_INNER, _ColsB.Z)),
        pl.BlockSpec((ROWS, GROUP_INNER), at(GROUP_INNER, _ColsB.XS)),
        pl.BlockSpec((ROWS, n), at(n, _ColsB.B)),
        pl.BlockSpec((ROWS, n), at(n, _ColsB.C)),
        pl.BlockSpec((ROWS, LANES), lambda *ids: (row_block(*ids)[0], _ColsB.DT // LANES)),
        grp(CONV_WIDTH, GROUP_INNER), grp(1, GROUP_INNER),
        grp(CONV_WIDTH, n), grp(1, n), grp(CONV_WIDTH, n), grp(1, n),
        pl.BlockSpec((1, LANES), lambda *ids: (0, 0)), pl.BlockSpec((1, LANES), lambda *ids: (0, 0)),
        grp(1, GROUP_INNER), grp(1, GROUP_INNER),
    ]


def _ssd_params(conv_w, conv_b, dt_bias, a_log, d_skip, ssm_norm):
    gn = SSM_GROUPS * SSM_DSTATE
    cwx, cwb, cwc = conv_w[:, :SSM_INNER], conv_w[:, SSM_INNER:SSM_INNER + gn], conv_w[:, SSM_INNER + gn:]
    cb = conv_b.reshape(1, CONV_DIM)
    cbx, cbb, cbc = cb[:, :SSM_INNER], cb[:, SSM_INNER:SSM_INNER + gn], cb[:, SSM_INNER + gn:]

    def head_lanes(v):
        return jnp.pad(v, (0, LANES - SSM_HEADS)).reshape(1, LANES)

    dsk = jnp.repeat(d_skip, SSM_HEADDIM).reshape(1, SSM_INNER)
    return (cwx, cbx, cwb, cbb, cwc, cbc, head_lanes(dt_bias), head_lanes(a_log), dsk,
            ssm_norm.reshape(1, SSM_INNER))


def _conv_tail(csx, csb, csc):
    keep = SUBLANES - (CONV_WIDTH - 1)
    return jnp.concatenate([csx[:, keep:], csb[:, keep:], csc[:, keep:]], axis=-1)


def _mixer_prompt(x, norm_w, w_b, proj, bsz, seq, params, ret_norm):
    nch = seq // ROWS
    gn = SSM_GROUPS * SSM_DSTATE
    full = lambda a: pl.BlockSpec(a.shape, lambda b, c: (0,) * a.ndim)
    row = lambda width, off: pl.BlockSpec((ROWS, width), lambda b, c: (b * nch + c, off // width))
    ltri = _ssd_tables(CHUNK)
    dm, qd, kd, cd = _ret_tables(CHUNK)
    cosf, sinf = _rope_tables(jnp.arange(seq, dtype=jnp.int32))
    rnw = ret_norm.reshape(RET_HEADS, 1, RET_DV)
    specs = [pl.BlockSpec((ROWS, D_MODEL), lambda b, c: (b * nch, 0)),
             pl.BlockSpec((ROWS, D_MODEL), lambda b, c: (b * nch + jnp.minimum(c + 1, nch - 1), 0)),
             full(norm_w), full(w_b), *[full(p) for p in params], full(ltri),
             row(RET_QK, _ColsA.Q), row(RET_QK, _ColsA.K), row(RET_V, _ColsA.V), row(RET_V, _ColsA.GRET),
             pl.BlockSpec((ROWS, RET_DK), lambda b, c: (c, 0)), pl.BlockSpec((ROWS, RET_DK), lambda b, c: (c, 0)),
             full(dm), full(qd), full(kd), full(cd), full(rnw)]
    tail = lambda width: pl.BlockSpec((1, SUBLANES, width), lambda b, c: (b, 0, 0))
    y, hs, csx, csb, csc, a, s = pl.pallas_call(
        _mixer_prompt_kernel,
        out_shape=(jax.ShapeDtypeStruct((bsz * seq, SSM_INNER), BF16),
                   jax.ShapeDtypeStruct((bsz, SSM_HEADS * SSM_HEADDIM, SSM_DSTATE), F32),
                   jax.ShapeDtypeStruct((bsz, SUBLANES, SSM_INNER), F32),
                   jax.ShapeDtypeStruct((bsz, SUBLANES, gn), F32),
                   jax.ShapeDtypeStruct((bsz, SUBLANES, gn), F32),
                   jax.ShapeDtypeStruct((bsz * seq, RET_V), BF16),
                   jax.ShapeDtypeStruct((bsz, RET_HEADS, RET_DK, RET_DV), F32)),
        grid=(bsz, nch),
        in_specs=specs,
        out_specs=(pl.BlockSpec((ROWS, SSM_INNER), lambda b, c: (b * nch + c, 0)),
                   pl.BlockSpec((1, SSM_HEADS * SSM_HEADDIM, SSM_DSTATE), lambda b, c: (b, 0, 0)),
                   tail(SSM_INNER), tail(gn), tail(gn),
                   pl.BlockSpec((ROWS, RET_V), lambda b, c: (b * nch + c, 0)),
                   pl.BlockSpec((1, RET_HEADS, RET_DK, RET_DV), lambda b, c: (b, 0, 0, 0))),
        scratch_shapes=[pltpu.VMEM((2, ROWS, _ColsB.TOTAL), F32)],
        compiler_params=_params(("parallel", "arbitrary")),
        name="mixer_prompt",
    )(x, x, norm_w, w_b, *params, ltri, proj, proj, proj, proj, cosf, sinf, dm, qd, kd, cd, rnw)
    return y, hs, _conv_tail(csx, csb, csc), a, s


def _mixer_sample_kernel(seq, n_ssd_in, n_ret_in, *refs):
    ssd_in, refs = refs[:n_ssd_in], refs[n_ssd_in:]
    ret_in, refs = refs[:n_ret_in], refs[n_ret_in:]
    _ssd_sample_kernel(seq, *ssd_in, *refs[:5])
    _ret_sample_kernel(*ret_in, *refs[5:])


def _mixer_sample(proj_a, proj_b, row0, bsz, seq, params, state, conv_state, ret_state, ret_norm):
    assert SSM_GROUPS == RET_HEADS
    ntile = bsz // SAMPLE_PER_TILE
    gn = SSM_GROUPS * SSM_DSTATE
    proj = proj_b
    specs = _ssd_specs(lambda t, g: (t, g))
    gh = HEADS_PER_GROUP * SSM_HEADDIM
    cs = jnp.pad(conv_state, ((0, 0), (SUBLANES - (CONV_WIDTH - 1), 0), (0, 0)))
    csx, csb, csc = cs[:, :, :SSM_INNER], cs[:, :, SSM_INNER:SSM_INNER + gn], cs[:, :, SSM_INNER + gn:]
    pred = lambda width: pl.BlockSpec((SAMPLE_PER_TILE, SUBLANES, width), lambda t, g: (t, 0, g))
    state_spec = pl.BlockSpec((SAMPLE_PER_TILE, gh, SSM_DSTATE), lambda t, g: (t, g, 0))
    specs += [
        pl.BlockSpec((ROWS, ROWS), lambda t, g: (0, 0)),
        pl.BlockSpec((ROWS, SAMPLE_PER_TILE * LANES), lambda t, g: (0, 0)),
        pred(GROUP_INNER), pred(SSM_DSTATE), pred(SSM_DSTATE),
        state_spec,
    ]
    state2 = state.reshape(bsz, SSM_HEADS * SSM_HEADDIM, SSM_DSTATE)
    assert seq == SUBLANES
    ssd_operands = (proj, proj, proj, proj, proj, *params, _ssd_tables(seq), _block_mask(BF16), csx, csb, csc,
                    state2)
    r_specs, r_operands, r_shapes, r_out_specs = _retention_sample_call(proj_a, row0, bsz, seq, ret_state, ret_norm)
    y, hs, csx, csb, csc, a, s = pl.pallas_call(
        functools.partial(_mixer_sample_kernel, seq, len(ssd_operands), len(r_operands)),
        out_shape=(jax.ShapeDtypeStruct((bsz * seq, SSM_INNER), BF16), jax.ShapeDtypeStruct(state2.shape, F32),
                   jax.ShapeDtypeStruct((bsz, seq, SSM_INNER), F32),
                   jax.ShapeDtypeStruct((bsz, seq, gn), F32),
                   jax.ShapeDtypeStruct((bsz, seq, gn), F32), *r_shapes),
        grid=(ntile, SSM_GROUPS),
        in_specs=specs + r_specs,
        out_specs=(pl.BlockSpec((ROWS, GROUP_INNER), lambda t, g: (t, g)), state_spec,
                   pred(GROUP_INNER), pred(SSM_DSTATE), pred(SSM_DSTATE), *r_out_specs),
        compiler_params=_params(("parallel", "parallel")),
        name="mixer_sample",
    )(*ssd_operands, *r_operands)
    return y, hs, _conv_tail(csx, csb, csc), a, s


def _post_kernel(n_prompt_tiles, ap_ref, as_ref, yp_ref, ys_ref, ga_ref, gb_ref, xp_ref, xs_ref, wr_ref, ws_ref,
                 wo_ref, nw_ref, wrt_ref, brt_ref, x1_ref, h2_ref, lg_ref):
    is_prompt = pl.program_id(0) < n_prompt_tiles
    half = x1_ref.shape[0] // 2
    for r in (pl.ds(0, half), pl.ds(half, half)):
        branch_a = _dot(jnp.where(is_prompt, ap_ref[r, :], as_ref[r, :]), wr_ref[...])
        branch_b = _dot(jnp.where(is_prompt, yp_ref[r, :], ys_ref[r, :]), ws_ref[...])
        merged = jax.nn.sigmoid(ga_ref[r, :]) * branch_a + jax.nn.sigmoid(gb_ref[r, :]) * branch_b
        x1 = jnp.where(is_prompt, xp_ref[r, :], xs_ref[r, :]) + _dot(merged.astype(BF16), wo_ref[...])
        x1_ref[r, :] = x1
        ms = jnp.mean(x1 * x1, axis=-1, keepdims=True)
        h2 = x1 * lax.rsqrt(ms + EPS) * nw_ref[...]
        _store_row_tiles(h2_ref.at[pl.ds(r.start * ROW_TILES, half * ROW_TILES), :], h2)
        lg_ref[r, :] = _dot(h2.astype(BF16), wrt_ref[...]) + brt_ref[...]


def _post(a_p, a_s, y_p, y_s, proj, xp, xs, w_ret, w_ssm, w_out, norm_ffn, w_router, b_router):
    t = xp.shape[0] + xs.shape[0]
    tm = _pick(math.gcd(xp.shape[0], xs.shape[0]), 512)
    npt = xp.shape[0] // tm
    row = lambda width, blk: pl.BlockSpec((tm, width), lambda i: (i, blk))
    full = lambda *shape: pl.BlockSpec(shape, lambda i: (0,) * len(shape))
    return pl.pallas_call(
        functools.partial(_post_kernel, npt),
        out_shape=(jax.ShapeDtypeStruct((t, D_MODEL), F32), jax.ShapeDtypeStruct((t * ROW_TILES, LANES), F32),
                   jax.ShapeDtypeStruct((t, LANES), F32)),
        grid=(t // tm,),
        in_specs=[*_split_rows(tm, npt, RET_V), *_split_rows(tm, npt, SSM_INNER),
                  row(D_MODEL, _ColsA.GA // D_MODEL), row(D_MODEL, _ColsA.GB // D_MODEL),
                  *_split_rows(tm, npt, D_MODEL),
                  full(RET_V, D_MODEL), full(SSM_INNER, D_MODEL), full(D_MODEL, D_MODEL), full(1, D_MODEL),
                  full(D_MODEL, LANES), full(1, LANES)],
        out_specs=(row(D_MODEL, 0), pl.BlockSpec((tm * ROW_TILES, LANES), lambda i: (i, 0)), row(LANES, 0)),
        compiler_params=_params(("parallel",)),
        name="post_mixer",
    )(a_p, a_s, y_p, y_s, proj, proj, xp, xs, w_ret, w_ssm, w_out, norm_ffn, w_router, b_router)


def _route_kernel(lg_ref, lstrict_ref, srow_ref, gate_ref, tmeta_ref, cnt_ref):
    @pl.when(pl.program_id(0) == 0)
    def _():
        cnt_ref[...] = jnp.zeros_like(cnt_ref)

    tm = lg_ref.shape[0]
    lane = lax.broadcasted_iota(jnp.int32, (tm, LANES), 1)
    lane_f = lane.astype(F32)
    cur = jnp.where(lane < N_EXPERTS, lg_ref[...], -jnp.inf)
    vals, hots = [], []
    for _ in range(TOP_K):
        m = jnp.max(cur, axis=1, keepdims=True)
        idx = jnp.min(jnp.where(cur == m, lane_f, float(LANES)), axis=1, keepdims=True)
        hot = lane_f == idx
        vals.append(m)
        hots.append(hot)
        cur = jnp.where(hot, -jnp.inf, cur)
    exps = [jnp.exp(v - vals[0]) for v in vals]
    denom = exps[0] + exps[1] + exps[2] + exps[3]
    sel = hots[0] | hots[1] | hots[2] | hots[3]
    self32 = sel.astype(F32)
    base = cnt_ref[...]
    num = jnp.sum(self32, axis=0, keepdims=True)
    rank = _dot(lstrict_ref[...], self32.astype(BF16))
    cnt_ref[...] = base + num
    lane8 = lax.broadcasted_iota(jnp.int32, (SUBLANES, LANES), 1)
    incl = jnp.broadcast_to(num, (SUBLANES, LANES))
    shift = 1
    while shift < N_EXPERTS:
        incl = incl + jnp.where(lane8 >= shift, pltpu.roll(incl, shift, 1), 0.0)
        shift *= 2
    off = incl[:1] - num
    slot = rank + off
    meta = jnp.zeros((tm, LANES), F32)
    for k in range(TOP_K):
        s_k = jnp.sum(jnp.where(hots[k], slot, 0.0), axis=1, keepdims=True)
        meta = jnp.where(lane == k, s_k, meta)
        meta = jnp.where(lane == SUBLANES + k, exps[k] / denom, meta)
    meta_t = meta.T
    srow_ref[0] = (meta_t[:SUBLANES] * float(ROW_TILES)).astype(jnp.int32)
    gate_ref[0] = meta_t[SUBLANES:2 * SUBLANES]
    row8 = lax.broadcasted_iota(jnp.int32, (SUBLANES, LANES), 0)
    tmeta_ref[0] = jnp.where(row8 == 0, base, jnp.where(row8 == 1, num, jnp.where(row8 == 2, off, 0.0)))


def _route(logits, tm):
    t = logits.shape[0]
    r = jnp.arange(tm)
    lstrict = (r[:, None] > r[None, :]).astype(BF16)
    return pl.pallas_call(
        _route_kernel,
        out_shape=(jax.ShapeDtypeStruct((t // tm, SUBLANES, tm), jnp.int32),
                   jax.ShapeDtypeStruct((t // tm, SUBLANES, tm), F32),
                   jax.ShapeDtypeStruct((t // tm, SUBLANES, LANES), F32),
                   jax.ShapeDtypeStruct((1, LANES), F32)),
        grid=(t // tm,),
        in_specs=[pl.BlockSpec((tm, LANES), lambda i: (i, 0)), pl.BlockSpec((tm, tm), lambda i: (0, 0))],
        out_specs=(pl.BlockSpec((1, SUBLANES, tm), lambda i: (i, 0, 0)),
                   pl.BlockSpec((1, SUBLANES, tm), lambda i: (i, 0, 0)),
                   pl.BlockSpec((1, SUBLANES, LANES), lambda i: (i, 0, 0)),
                   pl.BlockSpec((1, LANES), lambda i: (0, 0))),
        compiler_params=_params(("arbitrary",)),
        name="route",
    )(logits, lstrict)


def _for_each_strip(tmeta_ref, tm, make_copy, action):
    del tm

    def per_expert(e, carry):
        n, hbm0, tile0 = tmeta_ref[0, 1, e], tmeta_ref[0, 0, e], tmeta_ref[0, 2, e]
        n_chunks = lax.shift_right_logical(n, STRIP_BITS)

        def chunk(c, inner):
            action(make_copy(hbm0 + c * STRIP, tile0 + c * STRIP, STRIP))
            return inner

        lax.fori_loop(0, n_chunks, chunk, 0)
        for bit in reversed(range(STRIP_BITS)):
            size = 1 << bit

            @pl.when(jnp.bitwise_and(n, size) != 0)
            def _():
                done = jnp.bitwise_and(n, ~(2 * size - 1))
                action(make_copy(hbm0 + done, tile0 + done, size))
        return carry

    lax.fori_loop(0, N_EXPERTS, per_expert, 0)


def _rows(ref, row, n):
    return ref.at[pl.ds(pl.multiple_of(row * ROW_TILES, ROW_TILES), n * ROW_TILES), :]


def _dispatch_kernel(pstart_ref, pend_ref, srow_ref, tmeta_ref, h2_ref, xs_ref, stage_ref, zero_ref, sem):
    tm = h2_ref.shape[0] // ROW_TILES
    i = pl.program_id(0)
    slot = i % 2

    used = pend_ref[N_EXPERTS - 1] // EXPERT_BLOCK
    total = xs_ref.shape[0] // (EXPERT_BLOCK * ROW_TILES)

    def clear(action):
        def block(row):
            return pltpu.make_async_copy(zero_ref, _rows(xs_ref, row, EXPERT_BLOCK), sem.at[2])

        def last_of_segment(e, carry):
            @pl.when(pend_ref[e] > pstart_ref[e])
            def _():
                action(block(pend_ref[e] - EXPERT_BLOCK))
            return carry

        lax.fori_loop(0, N_EXPERTS, last_of_segment, 0)
        lax.fori_loop(used, total, lambda j, c: (action(block(j * EXPERT_BLOCK)), c)[1], 0)

    @pl.when(i == 0)
    def _():
        zero_ref[...] = jnp.zeros_like(zero_ref)
        clear(lambda copy: copy.start())

    def place(t, carry):
        row = h2_ref[pl.ds(pl.multiple_of(t * ROW_TILES, ROW_TILES), ROW_TILES), :]
        for k in range(TOP_K):
            stage_ref[slot, pl.ds(pl.multiple_of(srow_ref[k * tm + t], ROW_TILES), ROW_TILES), :] = row
        return carry

    lax.fori_loop(0, tm, place, 0, unroll=ISSUE_UNROLL)

    @pl.when(i == 0)
    def _():
        clear(lambda copy: copy.wait())

    def strips(meta_ref, s):
        return functools.partial(
            _for_each_strip, meta_ref, tm,
            lambda hbm_row, tile_row, n: pltpu.make_async_copy(_rows(stage_ref.at[s], tile_row, n),
                                                               _rows(xs_ref, hbm_row, n), sem.at[s]))

    strips(tmeta_ref, slot)(lambda copy: copy.start())

    def wait_all(s):
        pltpu.make_async_copy(stage_ref.at[s], _rows(xs_ref, 0, TOP_K * tm), sem.at[s]).wait()

    @pl.when(i > 0)
    def _():
        wait_all(1 - slot)

    @pl.when(i == pl.num_programs(0) - 1)
    def _():
        wait_all(slot)


def _dispatch(pstart, pends, srow, tmeta, h2, n_rows, tm):
    t = h2.shape[0] // ROW_TILES
    smem = lambda shape, imap: pl.BlockSpec(shape, imap, memory_space=pltpu.SMEM)
    return pl.pallas_call(
        _dispatch_kernel,
        out_shape=jax.ShapeDtypeStruct((n_rows * ROW_TILES, LANES), F32),
        grid_spec=pltpu.PrefetchScalarGridSpec(
            num_scalar_prefetch=2,
            grid=(t // tm,),
            in_specs=[smem((SUBLANES * tm,), lambda i, ps, pe: (i,)),
                      smem((1, SUBLANES, LANES), lambda i, ps, pe: (i, 0, 0)),
                      pl.BlockSpec((tm * ROW_TILES, LANES), lambda i, ps, pe: (i, 0))],
            out_specs=pl.BlockSpec(memory_space=pl.ANY),
            scratch_shapes=[pltpu.VMEM((2, TOP_K * tm * ROW_TILES, LANES), F32),
                            pltpu.VMEM((EXPERT_BLOCK * ROW_TILES, LANES), F32),
                            pltpu.SemaphoreType.DMA((3,))],
        ),
        compiler_params=_params(("arbitrary",)),
        name="dispatch",
    )(pstart, pends, srow, tmeta, h2)


def _combine_kernel(n_prompt_tiles, srow_ref, gate_ref, tmeta_ref, tmeta_next_ref, x1_ref, nw_ref, yb_ref,
                    op_ref, os_ref, buf_ref, acc_ref, sem):
    tm = x1_ref.shape[0]
    i = pl.program_id(0)
    slot = i % 2

    def strips(meta_ref, s):
        return functools.partial(
            _for_each_strip, meta_ref, tm,
            lambda hbm_row, tile_row, n: pltpu.make_async_copy(_rows(yb_ref, hbm_row, n),
                                                               _rows(buf_ref.at[s], tile_row, n), sem.at[s]))

    @pl.when(i == 0)
    def _():
        strips(tmeta_ref, 0)(lambda copy: copy.start())

    @pl.when(i + 1 < pl.num_programs(0))
    def _():
        strips(tmeta_next_ref, 1 - slot)(lambda copy: copy.start())

    pltpu.make_async_copy(_rows(yb_ref, 0, TOP_K * tm), buf_ref.at[slot], sem.at[slot]).wait()

    def gather(t, carry):
        def term(k):
            row = buf_ref[slot, pl.ds(pl.multiple_of(srow_ref[k * tm + t], ROW_TILES), ROW_TILES), :]
            return row * gate_ref[k * tm + t]

        moe = term(0)
        for k in range(1, TOP_K):
            moe = moe + term(k)
        acc_ref[pl.ds(pl.multiple_of(t * ROW_TILES, ROW_TILES), ROW_TILES), :] = moe
        return carry

    lax.fori_loop(0, tm, gather, 0, unroll=ISSUE_UNROLL)
    x2 = x1_ref[...] + _load_row_tiles(acc_ref, tm)
    ms = jnp.mean(x2 * x2, axis=-1, keepdims=True)
    out = x2 * lax.rsqrt(ms + EPS) * nw_ref[...]

    @pl.when(i < n_prompt_tiles)
    def _():
        op_ref[...] = out

    @pl.when(i >= n_prompt_tiles)
    def _():
        os_ref[...] = out


def _combine(srow, gates, tmeta, x1, norm_final, yb, n_prompt, tm):
    t = x1.shape[0]
    nt = t // tm
    npt = n_prompt // tm
    smem = lambda shape, imap: pl.BlockSpec(shape, imap, memory_space=pltpu.SMEM)
    return pl.pallas_call(
        functools.partial(_combine_kernel, npt),
        out_shape=(jax.ShapeDtypeStruct((n_prompt, D_MODEL), F32),
                   jax.ShapeDtypeStruct((t - n_prompt, D_MODEL), F32)),
        grid=(nt,),
        in_specs=[smem((SUBLANES * tm,), lambda i: (i,)),
                  smem((SUBLANES * tm,), lambda i: (i,)),
                  smem((1, SUBLANES, LANES), lambda i: (i, 0, 0)),
                  smem((1, SUBLANES, LANES), lambda i: (jnp.minimum(i + 1, nt - 1), 0, 0)),
                  pl.BlockSpec((tm, D_MODEL), lambda i: (i, 0)),
                  pl.BlockSpec((1, D_MODEL), lambda i: (0, 0)),
                  pl.BlockSpec(memory_space=pl.ANY)],
        out_specs=_split_rows(tm, npt, D_MODEL),
        scratch_shapes=[pltpu.VMEM((2, TOP_K * tm * ROW_TILES, LANES), F32),
                        pltpu.VMEM((tm * ROW_TILES, LANES), F32),
                        pltpu.SemaphoreType.DMA((2,))],
        compiler_params=_params(("arbitrary",)),
        name="combine",
    )(srow, gates, tmeta, tmeta, x1, norm_final, yb)


def _expert_kernel(be_ref, nv_ref, next_ref, xs_ref, bgu_ref, bd_ref, wgu_hbm, wd_hbm, yb_ref,
                   wgu_f, wd_f, wgu_b, wd_b, sem):
    i = pl.program_id(0)

    def fetch(e):
        return (pltpu.make_async_copy(wgu_hbm.at[e], wgu_f, sem.at[0]),
                pltpu.make_async_copy(wd_hbm.at[e], wd_f, sem.at[1]))

    @pl.when(i == 0)
    def _():
        for copy in fetch(be_ref[0]):
            copy.start()

    first = jnp.logical_or(i == 0, be_ref[i] != be_ref[jnp.maximum(i - 1, 0)])

    @pl.when(jnp.logical_and(first, i < nv_ref[0]))
    def _():
        for copy in fetch(be_ref[i]):
            copy.wait()
        wgu_b[...] = wgu_f[...].astype(BF16)
        wd_b[...] = wd_f[...].astype(BF16)

        @pl.when(next_ref[i] >= 0)
        def _():
            for copy in fetch(next_ref[i]):
                copy.start()

    @pl.when(i < nv_ref[0])
    def _():
        gu = _dot(_load_row_tiles(xs_ref, EXPERT_BLOCK).astype(BF16), wgu_b[...]) + bgu_ref[0]
        glu = jnp.minimum(gu[:, :D_FF], SWIGLU_LIMIT)
        lin = jnp.clip(gu[:, D_FF:], -SWIGLU_LIMIT, SWIGLU_LIMIT)
        act = glu * jax.nn.sigmoid(SWIGLU_ALPHA * glu) * (lin + 1.0)
        _store_row_tiles(yb_ref, _dot(act.astype(BF16), wd_b[...]) + bd_ref[0])

    @pl.when(i >= nv_ref[0])
    def _():
        yb_ref[...] = jnp.zeros_like(yb_ref)


def _experts(block_expert, n_valid, next_expert, xs, w_gu, b_gu, w_d, b_d):
    rows = EXPERT_BLOCK * ROW_TILES
    nb = xs.shape[0] // rows
    return pl.pallas_call(
        _expert_kernel,
        out_shape=jax.ShapeDtypeStruct(xs.shape, F32),
        grid_spec=pltpu.PrefetchScalarGridSpec(
            num_scalar_prefetch=3,
            grid=(nb,),
            in_specs=[pl.BlockSpec((rows, LANES), lambda i, be, nv, nx: (jnp.minimum(i, nv[0] - 1), 0)),
                      pl.BlockSpec((1, 1, 2 * D_FF), lambda i, be, nv, nx: (be[i], 0, 0)),
                      pl.BlockSpec((1, 1, D_MODEL), lambda i, be, nv, nx: (be[i], 0, 0)),
                      pl.BlockSpec(memory_space=pl.ANY),
                      pl.BlockSpec(memory_space=pl.ANY)],
            out_specs=pl.BlockSpec((rows, LANES), lambda i, be, nv, nx: (i, 0)),
            scratch_shapes=[pltpu.VMEM((D_MODEL, 2 * D_FF), F32), pltpu.VMEM((D_FF, D_MODEL), F32),
                            pltpu.VMEM((D_MODEL, 2 * D_FF), BF16), pltpu.VMEM((D_FF, D_MODEL), BF16),
                            pltpu.SemaphoreType.DMA((2,))],
        ),
        compiler_params=_params(("arbitrary",)),
        name="experts",
    )(block_expert, n_valid, next_expert, xs, b_gu, b_d, w_gu, w_d)


def _expert_layout(counts, n_blocks):
    counts = counts.astype(jnp.int32)
    padded = (counts + EXPERT_BLOCK - 1) // EXPERT_BLOCK * EXPERT_BLOCK
    pends = jnp.cumsum(padded)
    pstart = pends - padded
    block_start = jnp.arange(n_blocks, dtype=jnp.int32) * EXPERT_BLOCK
    block_expert = jnp.minimum(jnp.sum(pends[None, :] <= block_start[:, None], axis=1), N_EXPERTS - 1)
    n_valid = (pends[-1:] // EXPERT_BLOCK).astype(jnp.int32)
    seg_end = jnp.sum(jnp.where(block_expert[:, None] == jnp.arange(N_EXPERTS), pends[None, :], 0), axis=1)
    after = seg_end // EXPERT_BLOCK
    next_expert = jnp.where(after < n_valid[0], block_expert[jnp.minimum(after, n_blocks - 1)], -1)
    return (pstart.astype(jnp.int32), pends.astype(jnp.int32), block_expert.astype(jnp.int32), n_valid,
            next_expert.astype(jnp.int32))


def _permute_w_in(w_in):
    sizes = [RET_QK, RET_QK, RET_V, RET_V, SSM_INNER, CONV_DIM, SSM_HEADS, D_MODEL, D_MODEL]
    offs = [0]
    for s in sizes:
        offs.append(offs[-1] + s)
    q, k, v, g_ret, z, xbc, dt, g_a, g_b = [w_in[:, offs[i]:offs[i + 1]] for i in range(len(sizes))]
    gn = SSM_GROUPS * SSM_DSTATE
    xs, bm, cm = xbc[:, :SSM_INNER], xbc[:, SSM_INNER:SSM_INNER + gn], xbc[:, SSM_INNER + gn:]
    dt_pad = jnp.pad(dt, ((0, 0), (0, _ColsB.TOTAL - _ColsB.DT - SSM_HEADS)))
    w_a = jnp.concatenate([v, g_ret, g_a, g_b, q, k], axis=1).astype(BF16)
    w_b = jnp.concatenate([z, xs, bm, cm, dt_pad], axis=1).astype(BF16)
    return w_a, w_b


def _forward(x_prompt, x_sample, state_ret, state_ssm, state_conv, norm_mix, w_in, ret_norm, w_out_ret,
             conv_w, conv_b, dt_bias, a_log, d_skip, ssm_norm, w_out_ssm, w_out, norm_ffn,
             w_router, b_router, w_gate_up, b_gate_up, w_down, b_down, norm_final):
    bp, lp, _ = x_prompt.shape
    bs, ls, _ = x_sample.shape
    assert lp % ROWS == 0 and bs % SAMPLE_PER_TILE == 0 and ls * SAMPLE_PER_TILE == ROWS
    tp, ts = bp * lp, bs * ls
    t_all = tp + ts
    xp, xs_in = x_prompt.reshape(tp, D_MODEL), x_sample.reshape(ts, D_MODEL)

    w_a, w_b = _permute_w_in(w_in[0])
    norm_w = norm_mix[0].reshape(1, D_MODEL)
    proj = _inproj(xp, xs_in, norm_w, w_a)
    proj_ssd_s = _inproj(None, xs_in, norm_w, w_b)


    sp = _ssd_params(conv_w[0], conv_b[0], dt_bias[0], a_log[0], d_skip[0], ssm_norm[0])
    y_p, ssm_p, conv_p, a_p, ret_p = _mixer_prompt(xp, norm_w, w_b, proj, bp, lp, sp, ret_norm[0])
    y_s, ssm_s, conv_s, a_s, ret_s = _mixer_sample(proj, proj_ssd_s, tp, bs, ls, sp, state_ssm[0], state_conv[0],
                                                   state_ret[0], ret_norm[0])

    w_router_pad = jnp.pad(w_router[0], ((0, 0), (0, LANES - N_EXPERTS))).astype(BF16)
    b_router_pad = jnp.pad(b_router[0], (0, LANES - N_EXPERTS)).reshape(1, LANES)
    x1, h2, logits = _post(a_p, a_s, y_p, y_s, proj, xp, xs_in, w_out_ret[0].astype(BF16), w_out_ssm[0].astype(BF16),
                           w_out[0].astype(BF16), norm_ffn[0].reshape(1, D_MODEL), w_router_pad, b_router_pad)

    tm_moe = _pick(math.gcd(tp, ts), 512)
    srow, gates, tmeta, counts = _route(logits, tm_moe)
    n_blocks = -(-(t_all * TOP_K + N_EXPERTS * (EXPERT_BLOCK - 1)) // EXPERT_BLOCK)
    pstart, pends, block_expert, n_valid, next_expert = _expert_layout(counts[0, :N_EXPERTS], n_blocks)
    tmeta = tmeta.astype(jnp.int32)
    run_start = tmeta[:, :1, :] + jnp.pad(pstart, (0, LANES - N_EXPERTS))
    tmeta = jnp.concatenate([run_start, tmeta[:, 1:, :]], axis=1)
    srow, gates = srow.reshape(-1), gates.reshape(-1)
    xs = _dispatch(pstart, pends, srow, tmeta, h2, n_blocks * EXPERT_BLOCK, tm_moe)
    yb = _experts(block_expert, n_valid, next_expert, xs, w_gate_up[0],
                  b_gate_up[0].reshape(N_EXPERTS, 1, 2 * D_FF), w_down[0], b_down[0].reshape(N_EXPERTS, 1, D_MODEL))
    out_p, out_s = _combine(srow, gates, tmeta, x1, norm_final.reshape(1, D_MODEL), yb, tp, tm_moe)

    shape_s = (1, bs, SSM_HEADS, SSM_HEADDIM, SSM_DSTATE)
    shape_p = (1, bp, SSM_HEADS, SSM_HEADDIM, SSM_DSTATE)
    return (out_p.reshape(bp, lp, D_MODEL), out_s.reshape(bs, ls, D_MODEL),
            ret_p[None], ret_s[None], ssm_p.reshape(shape_p), ssm_s.reshape(shape_s),
            conv_p[None], conv_s[None])


def kernel(x_prompt, x_sample, state_ret, state_ssm, state_conv, norm_mix, w_in, ret_norm, w_out_ret, conv_w, conv_b, dt_bias, a_log, d_skip, ssm_norm, w_out_ssm, w_out, norm_ffn, w_router, b_router, w_gate_up, b_gate_up, w_down, b_down, norm_final):
    return _forward(x_prompt, x_sample, state_ret, state_ssm, state_conv, norm_mix, w_in, ret_norm, w_out_ret,
                    conv_w, conv_b, dt_bias, a_log, d_skip, ssm_norm, w_out_ssm, w_out, norm_ffn,
                    w_router, b_router, w_gate_up, b_gate_up, w_down, b_down, norm_final)
```

```python
import functools
import math

import jax
import jax.numpy as jnp
from jax import lax
from jax.experimental import pallas as pl
from jax.experimental.pallas import tpu as pltpu

F32 = jnp.float32
BF16 = jnp.bfloat16

D_MODEL = 1024
PAST_LEN = 16384
RET_HEADS = 4
RET_DK = 128
RET_DV = 256
RET_QK = RET_HEADS * RET_DK
RET_V = RET_HEADS * RET_DV
ROPE_BASE = 10000.0
SSM_INNER = 2 * D_MODEL
SSM_HEADDIM = 64
SSM_HEADS = SSM_INNER // SSM_HEADDIM
SSM_GROUPS = 4
SSM_DSTATE = 128
HEADS_PER_GROUP = SSM_HEADS // SSM_GROUPS
GROUP_INNER = SSM_INNER // SSM_GROUPS
CONV_WIDTH = 4
CONV_DIM = SSM_INNER + 2 * SSM_GROUPS * SSM_DSTATE
CHUNK = 128
N_EXPERTS = 32
TOP_K = 4
D_FF = D_MODEL
SWIGLU_LIMIT = 7.0
SWIGLU_ALPHA = 1.702
EPS = 1e-6

LANES = 128
SUBLANES = 8
ROWS = 128
SAMPLE_PER_TILE = 16
EXPERT_BLOCK = 512
ROW_TILES = D_MODEL // LANES
ISSUE_UNROLL = 8
STRIP_BITS = 5
STRIP = 1 << STRIP_BITS
VMEM_LIMIT = 56 * 1024 * 1024


class _ColsA:
    V = 0
    GRET = V + RET_V
    GA = GRET + RET_V
    GB = GA + D_MODEL
    Q = GB + D_MODEL
    K = Q + RET_QK
    TOTAL = K + RET_QK


class _ColsB:
    Z = 0
    XS = Z + SSM_INNER
    B = XS + SSM_INNER
    C = B + SSM_GROUPS * SSM_DSTATE
    DT = C + SSM_GROUPS * SSM_DSTATE
    TOTAL = DT + 2 * LANES


def _pick(n, target):
    best = None
    for c in range(LANES, target + 1, LANES):
        if n % c == 0:
            best = c
    assert best is not None, (n, target)
    return best


def _params(sem, **kw):
    return pltpu.CompilerParams(dimension_semantics=sem, vmem_limit_bytes=VMEM_LIMIT, **kw)


def _dot(a, b):
    return jnp.dot(a, b, preferred_element_type=F32)


def _dot_nt(a, b):
    return lax.dot_general(a, b, (((1,), (1,)), ((), ())), preferred_element_type=F32)


def _dot_tn(a, b):
    return lax.dot_general(a, b, (((0,), (0,)), ((), ())), preferred_element_type=F32)


def _silu(x):
    return x * jax.nn.sigmoid(x)


def _store_row_tiles(ref, value, *lead):
    n = value.shape[0]
    for s in range(ROW_TILES):
        ref[(*lead, pl.ds(s, n, stride=ROW_TILES), slice(None))] = value[:, s * LANES:(s + 1) * LANES]


def _load_row_tiles(ref, n, *lead):
    return jnp.concatenate([ref[(*lead, pl.ds(s, n, stride=ROW_TILES), slice(None))] for s in range(ROW_TILES)],
                           axis=1)


def _inproj_kernel(n_prompt_tiles, xp_ref, xs_ref, nw_ref, w_ref, o_ref, h_ref):
    @pl.when(pl.program_id(1) == 0)
    def _():
        x = jnp.where(pl.program_id(0) < n_prompt_tiles, xp_ref[...], xs_ref[...])
        ms = jnp.mean(x * x, axis=-1, keepdims=True)
        h_ref[...] = (x * lax.rsqrt(ms + EPS) * nw_ref[...]).astype(BF16)

    o_ref[...] = _dot(h_ref[...], w_ref[...])


def _split_rows(tm, npt, width):
    prompt = pl.BlockSpec((tm, width), lambda i, *_: (jnp.minimum(i, npt - 1), 0))
    sample = pl.BlockSpec((tm, width), lambda i, *_: (jnp.maximum(i - npt, 0), 0))
    return prompt, sample


def _inproj(xp, xs, norm_w, w):
    total = w.shape[1]
    t = xs.shape[0] + (0 if xp is None else xp.shape[0])
    tm = _pick(xs.shape[0] if xp is None else math.gcd(xp.shape[0], xs.shape[0]), 1024)
    npt = 0 if xp is None else xp.shape[0] // tm
    tn = _pick(total, 2560)
    assert tn % (2 * LANES) == 0
    x_specs = list(_split_rows(tm, npt, D_MODEL)) if npt else [pl.BlockSpec((tm, D_MODEL), lambda i, j: (i, 0))] * 2
    return pl.pallas_call(
        functools.partial(_inproj_kernel, npt),
        out_shape=jax.ShapeDtypeStruct((t, total), F32),
        grid=(t // tm, total // tn),
        in_specs=[
            *x_specs,
            pl.BlockSpec((1, D_MODEL), lambda i, j: (0, 0)),
            pl.BlockSpec((D_MODEL, tn), lambda i, j: (0, j)),
        ],
        out_specs=pl.BlockSpec((tm, tn), lambda i, j: (i, j)),
        scratch_shapes=[pltpu.VMEM((tm, D_MODEL), BF16)],
        compiler_params=_params(("parallel", "arbitrary")),
        name="in_proj",
    )(xs if xp is None else xp, xs, norm_w, w)


def _ret_log_decay():
    return jnp.log(1.0 - 2.0 ** (-5.0 - jnp.arange(RET_HEADS, dtype=F32)))


def _ret_tables(c):
    lg = _ret_log_decay()
    r = jnp.arange(ROWS)
    t = (r % c).astype(F32)
    seg = r // c
    diff = t[:, None] - t[None, :]
    ok = (seg[:, None] == seg[None, :]) & (diff >= 0)
    dm = jnp.where(ok[None], jnp.exp(lg[:, None, None] * jnp.maximum(diff, 0.0)[None]), 0.0)
    qd = jnp.exp(lg[:, None] * (t[None, :] + 1.0))
    kd = jnp.exp(lg[:, None] * (c - 1.0 - t[None, :]))
    cd = jnp.exp(lg * c)
    qd = jnp.broadcast_to(qd[:, :, None], (RET_HEADS, ROWS, LANES))
    kd = jnp.broadcast_to(kd[:, :, None], (RET_HEADS, ROWS, LANES))
    cd = jnp.broadcast_to(cd[:, None, None], (RET_HEADS, 1, RET_DV))
    return dm, qd, kd, cd


def _rope_tables(pos):
    half = RET_DK // 2
    inv_freq = 1.0 / (ROPE_BASE ** jnp.linspace(0.0, 1.0, half, dtype=F32))
    ang = pos.astype(F32)[:, None] * inv_freq[None, :]
    cos, sin = jnp.cos(ang), jnp.sin(ang)
    return jnp.concatenate([cos, cos], -1), jnp.concatenate([-sin, sin], -1)


def _block_mask(dtype):
    r = jnp.arange(ROWS) // (ROWS // SAMPLE_PER_TILE)
    b = jnp.arange(SAMPLE_PER_TILE * LANES) // LANES
    return (r[:, None] == b[None, :]).astype(dtype)


def _ret_common(q, k, v, g, cosf, sinf, dm, qd, kd, nw):
    qr = q * cosf + pltpu.roll(q, RET_DK // 2, 1) * sinf
    kr = (k * cosf + pltpu.roll(k, RET_DK // 2, 1) * sinf) * (RET_DK ** -0.5)
    vb = v.astype(BF16)
    scores = _dot_nt(qr.astype(BF16), kr.astype(BF16)) * dm
    o_intra = _dot(scores.astype(BF16), vb)
    q_state = (qr * qd).astype(BF16)
    k_state = kr * kd

    def finish(o):
        ms = jnp.mean(o * o, axis=-1, keepdims=True)
        on = o * lax.rsqrt(ms + EPS) * nw
        return (_silu(g) * on).astype(BF16)

    return o_intra, q_state, k_state, vb, finish


def _ret_prompt_step(q_ref, k_ref, v_ref, g_ref, cos_ref, sin_ref, dm_ref, qd_ref, kd_ref, cd_ref, nw_ref,
                     a_ref, s_ref):
    cosf, sinf = cos_ref[...], sin_ref[...]
    for h in range(RET_HEADS):
        dk = slice(h * RET_DK, (h + 1) * RET_DK)
        dv = slice(h * RET_DV, (h + 1) * RET_DV)
        o_intra, q_state, k_state, vb, finish = _ret_common(
            q_ref[:, dk], k_ref[:, dk], v_ref[:, dv], g_ref[:, dv], cosf, sinf, dm_ref[h], qd_ref[h], kd_ref[h],
            nw_ref[h])
        s = s_ref[0, h]
        a_ref[:, dv] = finish(o_intra + _dot(q_state, s.astype(BF16)))
        s_ref[0, h] = s * cd_ref[h] + _dot_tn(k_state.astype(BF16), vb)


def _ret_sample_kernel(q_ref, k_ref, v_ref, g_ref, cos_ref, sin_ref, dm_ref, qd_ref, kd_ref, cd_ref, nw_ref,
                       bm_ref, bmt_ref, s0_ref, a_ref, s_ref):
    o_intra, q_state, k_state, vb, finish = _ret_common(
        q_ref[...], k_ref[...], v_ref[...], g_ref[...], cos_ref[...], sin_ref[...], dm_ref[0], qd_ref[0], kd_ref[0],
        nw_ref[0])
    nb = SAMPLE_PER_TILE
    s0 = s0_ref[:, 0]
    q_bd = jnp.concatenate([q_state] * nb, axis=1) * bm_ref[...]
    a_ref[...] = finish(o_intra + _dot(q_bd, s0.reshape(nb * RET_DK, RET_DV).astype(BF16)))
    k_t = k_state.T.astype(BF16)
    k_bd_t = jnp.concatenate([k_t] * nb, axis=0) * bmt_ref[...]
    ds = _dot(k_bd_t, vb).reshape(nb, RET_DK, RET_DV)
    s_ref[:, 0] = s0 * cd_ref[0] + ds


def _ret_specs(row_block):
    def at(width, off):
        return lambda *ids: (row_block(*ids)[0], off // width + row_block(*ids)[1])

    def head(*shape):
        return pl.BlockSpec((1,) + shape, lambda *ids: (row_block(*ids)[1],) + (0,) * len(shape))

    return [
        pl.BlockSpec((ROWS, RET_DK), at(RET_DK, _ColsA.Q)),
        pl.BlockSpec((ROWS, RET_DK), at(RET_DK, _ColsA.K)),
        pl.BlockSpec((ROWS, RET_DV), at(RET_DV, _ColsA.V)),
        pl.BlockSpec((ROWS, RET_DV), at(RET_DV, _ColsA.GRET)),
    ], head


def _retention_sample_call(proj, row0, bsz, seq, state, ret_norm):
    rb0 = row0 // ROWS
    dm, qd, kd, cd = _ret_tables(seq)
    pos = PAST_LEN + jnp.arange(seq, dtype=jnp.int32)
    cosf, sinf = _rope_tables(jnp.tile(pos, SAMPLE_PER_TILE))
    rb = lambda t, h: (rb0 + t, h)
    specs, head = _ret_specs(rb)
    full = lambda *shape: pl.BlockSpec(shape, lambda t, h: (0,) * len(shape))
    specs += [
        full(ROWS, RET_DK), full(ROWS, RET_DK),
        head(ROWS, ROWS), head(ROWS, LANES), head(ROWS, LANES), head(1, RET_DV), head(1, RET_DV),
        full(ROWS, SAMPLE_PER_TILE * LANES), full(SAMPLE_PER_TILE * LANES, ROWS),
        pl.BlockSpec((SAMPLE_PER_TILE, 1, RET_DK, RET_DV), lambda t, h: (t, h, 0, 0)),
    ]
    bm = _block_mask(BF16)
    operands = (proj, proj, proj, proj, cosf, sinf, dm, qd, kd, cd, ret_norm.reshape(RET_HEADS, 1, RET_DV),
                bm, bm.T, state)
    out_shapes = (jax.ShapeDtypeStruct((bsz * seq, RET_V), BF16), jax.ShapeDtypeStruct(state.shape, F32))
    out_specs = (pl.BlockSpec((ROWS, RET_DV), lambda t, h: (t, h)),
                 pl.BlockSpec((SAMPLE_PER_TILE, 1, RET_DK, RET_DV), lambda t, h: (t, h, 0, 0)))
    return specs, operands, out_shapes, out_specs


def _softplus(x):
    return jnp.maximum(x, 0.0) + jnp.log1p(jnp.exp(-jnp.abs(x)))


def _conv_piece(x, pred, w_ref, b_ref):
    width = x.shape[1]
    x3 = x.reshape(ROWS // SUBLANES, SUBLANES, width)
    t8 = lax.broadcasted_iota(jnp.int32, x3.shape, 1)
    acc = b_ref[...].reshape(1, 1, width)
    for i in range(CONV_WIDTH):
        s = CONV_WIDTH - 1 - i
        tap = x3 if s == 0 else pltpu.roll(jnp.where(t8 >= SUBLANES - s, pred, x3), s, 1)
        acc = acc + tap * w_ref[pl.ds(i, 1), :].reshape(1, 1, width)
    return _silu(acc).reshape(ROWS, width)


def _ssd_decay_terms(c, dt_pre, a_log, ltri):
    dt = _softplus(dt_pre)
    d_a = dt * (-jnp.exp(a_log))
    hi = d_a.astype(BF16)
    r1 = d_a - hi.astype(F32)
    mid = r1.astype(BF16)
    lo = (r1 - mid.astype(F32)).astype(BF16)
    cum = _dot(ltri, hi) + _dot(ltri, mid) + _dot(ltri, lo)
    cum3 = cum.reshape(ROWS // c, c, LANES)
    c_last = jnp.broadcast_to(cum3[:, c - 1:c, :], cum3.shape).reshape(ROWS, LANES)
    to_end = jnp.exp(c_last - cum) * dt
    e_cum = jnp.exp(cum)
    return cum, to_end, e_cum, cum.T, dt.T


def _ssd_group(c, terms, head0, z_ref, dsk_ref, nrm_ref, xs, bm, cm, y_ref, state_io, zero=None):
    cum, to_end, e_cum, cum_t, dt_t = terms
    b_b, c_b = bm.astype(BF16), cm.astype(BF16)
    cb = _dot_nt(c_b, b_b)
    ri = lax.broadcasted_iota(jnp.int32, (ROWS, ROWS), 0)
    ci = lax.broadcasted_iota(jnp.int32, (ROWS, ROWS), 1)
    shift = c.bit_length() - 1
    causal = (ci <= ri) & (jnp.right_shift(ri, shift) == jnp.right_shift(ci, shift))
    lo_half = lax.broadcasted_iota(jnp.int32, (ROWS, LANES), 1) < SSM_HEADDIM

    def col(a, h):
        return jnp.broadcast_to(a[:, h:h + 1], (ROWS, LANES))

    def weights(h):
        seg = col(cum, h) - jnp.broadcast_to(cum_t[h:h + 1, :], (ROWS, ROWS))
        decay = jnp.exp(jnp.where(causal, seg, -jnp.inf))
        return (cb * decay * jnp.broadcast_to(dt_t[h:h + 1, :], (ROWS, ROWS))).astype(BF16)

    pieces = []
    for m in range(HEADS_PER_GROUP // 2):
        ha, hb = head0 + 2 * m, head0 + 2 * m + 1
        xp = xs[:, m * LANES:(m + 1) * LANES]
        w2 = jnp.concatenate([weights(ha), weights(hb)], axis=1)
        x2 = jnp.concatenate([jnp.where(lo_half, xp, 0.0), jnp.where(lo_half, 0.0, xp)], axis=0).astype(BF16)
        y = _dot(w2, x2)
        xw = (xp * jnp.where(lo_half, col(to_end, ha), col(to_end, hb))).astype(BF16)
        y_state = state_io(m, ha, c_b, xw, b_b, e_cum)
        y = y + y_state * jnp.where(lo_half, col(e_cum, ha), col(e_cum, hb))
        pieces.append(y + xp * dsk_ref[:, m * LANES:(m + 1) * LANES])
    if zero is not None:
        pieces[0] = jnp.concatenate([pieces[0][:SUBLANES] + zero, pieces[0][SUBLANES:]], axis=0)
    yg = jnp.concatenate(pieces, axis=1) * _silu(z_ref[...])
    ms = jnp.mean(yg * yg, axis=-1, keepdims=True)
    y_ref[...] = (yg * lax.rsqrt(ms + EPS) * nrm_ref[...]).astype(BF16)


def _pair_decay(e_cum, row, ha):
    top = lax.broadcasted_iota(jnp.int32, (2 * SSM_HEADDIM, SSM_DSTATE), 0) < SSM_HEADDIM
    ea = jnp.broadcast_to(e_cum[row:row + 1, ha:ha + 1], top.shape)
    eb = jnp.broadcast_to(e_cum[row:row + 1, ha + 1:ha + 2], top.shape)
    return jnp.where(top, ea, eb)


def _mixer_prompt_kernel(x0_ref, xn_ref, nw_ref, w_ref, cwx_ref, cbx_ref, cwb_ref, cbb_ref, cwc_ref, cbc_ref,
                         dtb_ref, alog_ref, dsk_ref, nrm_ref, ltri_ref, *rest):
    ret_in, (y_ref, hs_ref, px_ref, pb_ref, pc_ref, a_ref, s_ref, proj_ref) = rest[:11], rest[11:]
    c = pl.program_id(1)

    def normed(x_ref):
        x = x_ref[...]
        ms = jnp.mean(x * x, axis=-1, keepdims=True)
        return (x * lax.rsqrt(ms + EPS) * nw_ref[...]).astype(BF16)

    def project(x_ref, slot):
        proj_ref[slot] = _dot(normed(x_ref), w_ref[...])

    @pl.when(c == 0)
    def _():
        hs_ref[...] = jnp.zeros_like(hs_ref)
        px_ref[...] = jnp.zeros_like(px_ref)
        pb_ref[...] = jnp.zeros_like(pb_ref)
        pc_ref[...] = jnp.zeros_like(pc_ref)
        s_ref[...] = jnp.zeros_like(s_ref)
        project(x0_ref, 0)

    def conv(x_ref, p_ref, w_ref, bias_ref):
        x = x_ref[...]
        x3 = x.reshape(ROWS // SUBLANES, SUBLANES, x.shape[1])
        pred = jnp.concatenate([p_ref[...], x3[:-1]], axis=0)
        out = _conv_piece(x, pred, w_ref, bias_ref)
        p_ref[0] = x3[-1]
        return out

    def step(cur_slot, next_slot):
        h_next = normed(xn_ref)
        bounds = [0, 5 * 256, 10 * 256, 15 * 256, _ColsB.TOTAL]

        def project_slice(g):
            lo, hi = bounds[g], bounds[g + 1]
            part = _dot(h_next, w_ref[:, lo:hi])
            proj_ref[next_slot, :, lo:hi] = part
            bits = pltpu.bitcast(part[ROWS - SUBLANES:, hi - lo - LANES:], jnp.uint32)
            sixteen = jnp.uint32(16)
            return pltpu.bitcast(lax.shift_right_logical(lax.shift_right_logical(bits, sixteen), sixteen), F32)

        _ret_prompt_step(*ret_in, a_ref, s_ref)
        cur = proj_ref.at[cur_slot]
        gn = SSM_GROUPS * SSM_DSTATE
        z_ref = cur.at[:, _ColsB.Z:_ColsB.Z + SSM_INNER]
        xs_ref = cur.at[:, _ColsB.XS:_ColsB.XS + SSM_INNER]
        b_ref = cur.at[:, _ColsB.B:_ColsB.B + gn]
        c_ref = cur.at[:, _ColsB.C:_ColsB.C + gn]
        dt_ref = cur.at[:, _ColsB.DT:_ColsB.DT + LANES]
        terms = _ssd_decay_terms(CHUNK, dt_ref[...] + dtb_ref[...], alog_ref[...], ltri_ref[...])
        for g in range(SSM_GROUPS):
            def cols(ref, width, g=g):
                return ref.at[..., g * width:(g + 1) * width]

            gi, n = GROUP_INNER, SSM_DSTATE
            xs = conv(cols(xs_ref, gi), cols(px_ref, gi), cols(cwx_ref, gi), cols(cbx_ref, gi))
            bm = conv(cols(b_ref, n), cols(pb_ref, n), cols(cwb_ref, n), cols(cbb_ref, n))
            cm = conv(cols(c_ref, n), cols(pc_ref, n), cols(cwc_ref, n), cols(cbc_ref, n))
            hs_g = hs_ref.at[0, g * gi:(g + 1) * gi, :]

            def state_io(m, ha, c_b, xw, b_b, e_cum, hs_g=hs_g):
                rows = pl.ds(m * 2 * SSM_HEADDIM, 2 * SSM_HEADDIM)
                h = hs_g[rows, :]
                hs_g[rows, :] = h * _pair_decay(e_cum, ROWS - 1, ha) + _dot_tn(xw, b_b)
                return _dot_nt(c_b, h.astype(BF16))

            _ssd_group(CHUNK, terms, g * HEADS_PER_GROUP, cols(z_ref, gi), cols(dsk_ref, gi), cols(nrm_ref, gi),
                       xs, bm, cm, cols(y_ref, gi), state_io, zero=project_slice(g))

    @pl.when(c % 2 == 0)
    def _():
        step(0, 1)

    @pl.when(c % 2 == 1)
    def _():
        step(1, 0)


def _ssd_sample_kernel(seq, z_ref, xs_ref, b_ref, c_ref, dt_ref, cwx_ref, cbx_ref, cwb_ref, cbb_ref, cwc_ref,
                       cbc_ref, dtb_ref, alog_ref, dsk_ref, nrm_ref, ltri_ref, bmask_ref, px_ref, pb_ref, pc_ref,
                       h0_ref, y_ref, hs_ref, csx_ref, csb_ref, csc_ref):
    nb = SAMPLE_PER_TILE

    def conv(x_ref, p_ref, w_ref, bias_ref, cs_ref):
        x = x_ref[...]
        cs_ref[...] = x.reshape(nb, seq, x.shape[1])
        return _conv_piece(x, p_ref[...], w_ref, bias_ref)

    xs = conv(xs_ref, px_ref, cwx_ref, cbx_ref, csx_ref)
    bm = conv(b_ref, pb_ref, cwb_ref, cbb_ref, csb_ref)
    cm = conv(c_ref, pc_ref, cwc_ref, cbc_ref, csc_ref)
    bmask = bmask_ref[...]

    def state_io(m, ha, c_b, xw, b_b, e_cum):
        rows = pl.ds(m * 2 * SSM_HEADDIM, 2 * SSM_HEADDIM)
        c_bd = jnp.concatenate([c_b] * nb, axis=1) * bmask
        b_bd = jnp.concatenate([b_b] * nb, axis=1) * bmask
        hs = [h0_ref[i, rows, :] for i in range(nb)]
        h_cat = jnp.concatenate(hs, axis=1).astype(BF16)
        dh = _dot_tn(xw, b_bd)
        for i in range(nb):
            decay = _pair_decay(e_cum, i * seq + seq - 1, ha)
            hs_ref[i, rows, :] = hs[i] * decay + dh[:, i * SSM_DSTATE:(i + 1) * SSM_DSTATE]
        return _dot_nt(c_bd, h_cat)

    shift = jnp.bitwise_and(LANES - HEADS_PER_GROUP * pl.program_id(1), LANES - 1)
    dt_pre = pltpu.roll(dt_ref[...] + dtb_ref[...], shift, 1)
    a_log = pltpu.roll(jnp.broadcast_to(alog_ref[...], (SUBLANES, LANES)), shift, 1)[:1]
    terms = _ssd_decay_terms(seq, dt_pre, a_log, ltri_ref[...])
    _ssd_group(seq, terms, 0, z_ref, dsk_ref, nrm_ref, xs, bm, cm, y_ref, state_io)


def _ssd_tables(c):
    r = jnp.arange(ROWS)
    ltri = ((r[:, None] >= r[None, :]) & ((r[:, None] // c) == (r[None, :] // c))).astype(BF16)
    return ltri


def _ssd_specs(row_block):
    def at(width, off):
        return lambda *ids: (row_block(*ids)[0], off // width + row_block(*ids)[1])

    def grp(rows, width):
        return pl.BlockSpec((rows, width), lambda *ids: (0, row_block(*ids)[1]))

    n = SSM_DSTATE
    return [
        pl.BlockSpec((ROWS, GROUP_INNER), at(GROUP_INNER, _ColsB.Z)),
        pl.BlockSpec((ROWS, GROUP_INNER), at(GROUP_INNER, _ColsB.XS)),
        pl.BlockSpec((ROWS, n), at(n, _ColsB.B)),
        pl.BlockSpec((ROWS, n), at(n, _ColsB.C)),
        pl.BlockSpec((ROWS, LANES), lambda *ids: (row_block(*ids)[0], _ColsB.DT // LANES)),
        grp(CONV_WIDTH, GROUP_INNER), grp(1, GROUP_INNER),
        grp(CONV_WIDTH, n), grp(1, n), grp(CONV_WIDTH, n), grp(1, n),
        pl.BlockSpec((1, LANES), lambda *ids: (0, 0)), pl.BlockSpec((1, LANES), lambda *ids: (0, 0)),
        grp(1, GROUP_INNER), grp(1, GROUP_INNER),
    ]


def _ssd_params(conv_w, conv_b, dt_bias, a_log, d_skip, ssm_norm):
    gn = SSM_GROUPS * SSM_DSTATE
    cwx, cwb, cwc = conv_w[:, :SSM_INNER], conv_w[:, SSM_INNER:SSM_INNER + gn], conv_w[:, SSM_INNER + gn:]
    cb = conv_b.reshape(1, CONV_DIM)
    cbx, cbb, cbc = cb[:, :SSM_INNER], cb[:, SSM_INNER:SSM_INNER + gn], cb[:, SSM_INNER + gn:]

    def head_lanes(v):
        return jnp.pad(v, (0, LANES - SSM_HEADS)).reshape(1, LANES)

    dsk = jnp.repeat(d_skip, SSM_HEADDIM).reshape(1, SSM_INNER)
    return (cwx, cbx, cwb, cbb, cwc, cbc, head_lanes(dt_bias), head_lanes(a_log), dsk,
            ssm_norm.reshape(1, SSM_INNER))


def _conv_tail(csx, csb, csc):
    keep = SUBLANES - (CONV_WIDTH - 1)
    return jnp.concatenate([csx[:, keep:], csb[:, keep:], csc[:, keep:]], axis=-1)


def _mixer_prompt(x, norm_w, w_b, proj, bsz, seq, params, ret_norm):
    nch = seq // ROWS
    gn = SSM_GROUPS * SSM_DSTATE
    full = lambda a: pl.BlockSpec(a.shape, lambda b, c: (0,) * a.ndim)
    row = lambda width, off: pl.BlockSpec((ROWS, width), lambda b, c: (b * nch + c, off // width))
    ltri = _ssd_tables(CHUNK)
    dm, qd, kd, cd = _ret_tables(CHUNK)
    cosf, sinf = _rope_tables(jnp.arange(seq, dtype=jnp.int32))
    rnw = ret_norm.reshape(RET_HEADS, 1, RET_DV)
    specs = [pl.BlockSpec((ROWS, D_MODEL), lambda b, c: (b * nch, 0)),
             pl.BlockSpec((ROWS, D_MODEL), lambda b, c: (b * nch + jnp.minimum(c + 1, nch - 1), 0)),
             full(norm_w), full(w_b), *[full(p) for p in params], full(ltri),
             row(RET_QK, _ColsA.Q), row(RET_QK, _ColsA.K), row(RET_V, _ColsA.V), row(RET_V, _ColsA.GRET),
             pl.BlockSpec((ROWS, RET_DK), lambda b, c: (c, 0)), pl.BlockSpec((ROWS, RET_DK), lambda b, c: (c, 0)),
             full(dm), full(qd), full(kd), full(cd), full(rnw)]
    tail = lambda width: pl.BlockSpec((1, SUBLANES, width), lambda b, c: (b, 0, 0))
    y, hs, csx, csb, csc, a, s = pl.pallas_call(
        _mixer_prompt_kernel,
        out_shape=(jax.ShapeDtypeStruct((bsz * seq, SSM_INNER), BF16),
                   jax.ShapeDtypeStruct((bsz, SSM_HEADS * SSM_HEADDIM, SSM_DSTATE), F32),
                   jax.ShapeDtypeStruct((bsz, SUBLANES, SSM_INNER), F32),
                   jax.ShapeDtypeStruct((bsz, SUBLANES, gn), F32),
                   jax.ShapeDtypeStruct((bsz, SUBLANES, gn), F32),
                   jax.ShapeDtypeStruct((bsz * seq, RET_V), BF16),
                   jax.ShapeDtypeStruct((bsz, RET_HEADS, RET_DK, RET_DV), F32)),
        grid=(bsz, nch),
        in_specs=specs,
        out_specs=(pl.BlockSpec((ROWS, SSM_INNER), lambda b, c: (b * nch + c, 0)),
                   pl.BlockSpec((1, SSM_HEADS * SSM_HEADDIM, SSM_DSTATE), lambda b, c: (b, 0, 0)),
                   tail(SSM_INNER), tail(gn), tail(gn),
                   pl.BlockSpec((ROWS, RET_V), lambda b, c: (b * nch + c, 0)),
                   pl.BlockSpec((1, RET_HEADS, RET_DK, RET_DV), lambda b, c: (b, 0, 0, 0))),
        scratch_shapes=[pltpu.VMEM((2, ROWS, _ColsB.TOTAL), F32)],
        compiler_params=_params(("parallel", "arbitrary")),
        name="mixer_prompt",
    )(x, x, norm_w, w_b, *params, ltri, proj, proj, proj, proj, cosf, sinf, dm, qd, kd, cd, rnw)
    return y, hs, _conv_tail(csx, csb, csc), a, s


def _mixer_sample_kernel(seq, n_ssd_in, n_ret_in, *refs):
    ssd_in, refs = refs[:n_ssd_in], refs[n_ssd_in:]
    ret_in, refs = refs[:n_ret_in], refs[n_ret_in:]
    _ssd_sample_kernel(seq, *ssd_in, *refs[:5])
    _ret_sample_kernel(*ret_in, *refs[5:])


def _mixer_sample(proj_a, proj_b, row0, bsz, seq, params, state, conv_state, ret_state, ret_norm):
    assert SSM_GROUPS == RET_HEADS
    ntile = bsz // SAMPLE_PER_TILE
    gn = SSM_GROUPS * SSM_DSTATE
    proj = proj_b
    specs = _ssd_specs(lambda t, g: (t, g))
    gh = HEADS_PER_GROUP * SSM_HEADDIM
    cs = jnp.pad(conv_state, ((0, 0), (SUBLANES - (CONV_WIDTH - 1), 0), (0, 0)))
    csx, csb, csc = cs[:, :, :SSM_INNER], cs[:, :, SSM_INNER:SSM_INNER + gn], cs[:, :, SSM_INNER + gn:]
    pred = lambda width: pl.BlockSpec((SAMPLE_PER_TILE, SUBLANES, width), lambda t, g: (t, 0, g))
    state_spec = pl.BlockSpec((SAMPLE_PER_TILE, gh, SSM_DSTATE), lambda t, g: (t, g, 0))
    specs += [
        pl.BlockSpec((ROWS, ROWS), lambda t, g: (0, 0)),
        pl.BlockSpec((ROWS, SAMPLE_PER_TILE * LANES), lambda t, g: (0, 0)),
        pred(GROUP_INNER), pred(SSM_DSTATE), pred(SSM_DSTATE),
        state_spec,
    ]
    state2 = state.reshape(bsz, SSM_HEADS * SSM_HEADDIM, SSM_DSTATE)
    assert seq == SUBLANES
    ssd_operands = (proj, proj, proj, proj, proj, *params, _ssd_tables(seq), _block_mask(BF16), csx, csb, csc,
                    state2)
    r_specs, r_operands, r_shapes, r_out_specs = _retention_sample_call(proj_a, row0, bsz, seq, ret_state, ret_norm)
    y, hs, csx, csb, csc, a, s = pl.pallas_call(
        functools.partial(_mixer_sample_kernel, seq, len(ssd_operands), len(r_operands)),
        out_shape=(jax.ShapeDtypeStruct((bsz * seq, SSM_INNER), BF16), jax.ShapeDtypeStruct(state2.shape, F32),
                   jax.ShapeDtypeStruct((bsz, seq, SSM_INNER), F32),
                   jax.ShapeDtypeStruct((bsz, seq, gn), F32),
                   jax.ShapeDtypeStruct((bsz, seq, gn), F32), *r_shapes),
        grid=(ntile, SSM_GROUPS),
        in_specs=specs + r_specs,
        out_specs=(pl.BlockSpec((ROWS, GROUP_INNER), lambda t, g: (t, g)), state_spec,
                   pred(GROUP_INNER), pred(SSM_DSTATE), pred(SSM_DSTATE), *r_out_specs),
        compiler_params=_params(("parallel", "parallel")),
        name="mixer_sample",
    )(*ssd_operands, *r_operands)
    return y, hs, _conv_tail(csx, csb, csc), a, s


def _post_kernel(n_prompt_tiles, ap_ref, as_ref, yp_ref, ys_ref, ga_ref, gb_ref, xp_ref, xs_ref, wr_ref, ws_ref,
                 wo_ref, nw_ref, wrt_ref, brt_ref, x1_ref, h2_ref, lg_ref):
    is_prompt = pl.program_id(0) < n_prompt_tiles
    half = x1_ref.shape[0] // 2
    for r in (pl.ds(0, half), pl.ds(half, half)):
        branch_a = _dot(jnp.where(is_prompt, ap_ref[r, :], as_ref[r, :]), wr_ref[...])
        branch_b = _dot(jnp.where(is_prompt, yp_ref[r, :], ys_ref[r, :]), ws_ref[...])
        merged = jax.nn.sigmoid(ga_ref[r, :]) * branch_a + jax.nn.sigmoid(gb_ref[r, :]) * branch_b
        x1 = jnp.where(is_prompt, xp_ref[r, :], xs_ref[r, :]) + _dot(merged.astype(BF16), wo_ref[...])
        x1_ref[r, :] = x1
        ms = jnp.mean(x1 * x1, axis=-1, keepdims=True)
        h2 = x1 * lax.rsqrt(ms + EPS) * nw_ref[...]
        _store_row_tiles(h2_ref.at[pl.ds(r.start * ROW_TILES, half * ROW_TILES), :], h2)
        lg_ref[r, :] = _dot(h2.astype(BF16), wrt_ref[...]) + brt_ref[...]


def _post(a_p, a_s, y_p, y_s, proj, xp, xs, w_ret, w_ssm, w_out, norm_ffn, w_router, b_router):
    t = xp.shape[0] + xs.shape[0]
    tm = _pick(math.gcd(xp.shape[0], xs.shape[0]), 512)
    npt = xp.shape[0] // tm
    row = lambda width, blk: pl.BlockSpec((tm, width), lambda i: (i, blk))
    full = lambda *shape: pl.BlockSpec(shape, lambda i: (0,) * len(shape))
    return pl.pallas_call(
        functools.partial(_post_kernel, npt),
        out_shape=(jax.ShapeDtypeStruct((t, D_MODEL), F32), jax.ShapeDtypeStruct((t * ROW_TILES, LANES), F32),
                   jax.ShapeDtypeStruct((t, LANES), F32)),
        grid=(t // tm,),
        in_specs=[*_split_rows(tm, npt, RET_V), *_split_rows(tm, npt, SSM_INNER),
                  row(D_MODEL, _ColsA.GA // D_MODEL), row(D_MODEL, _ColsA.GB // D_MODEL),
                  *_split_rows(tm, npt, D_MODEL),
                  full(RET_V, D_MODEL), full(SSM_INNER, D_MODEL), full(D_MODEL, D_MODEL), full(1, D_MODEL),
                  full(D_MODEL, LANES), full(1, LANES)],
        out_specs=(row(D_MODEL, 0), pl.BlockSpec((tm * ROW_TILES, LANES), lambda i: (i, 0)), row(LANES, 0)),
        compiler_params=_params(("parallel",)),
        name="post_mixer",
    )(a_p, a_s, y_p, y_s, proj, proj, xp, xs, w_ret, w_ssm, w_out, norm_ffn, w_router, b_router)


def _route_kernel(lg_ref, lstrict_ref, srow_ref, gate_ref, tmeta_ref, cnt_ref):
    @pl.when(pl.program_id(0) == 0)
    def _():
        cnt_ref[...] = jnp.zeros_like(cnt_ref)

    tm = lg_ref.shape[0]
    lane = lax.broadcasted_iota(jnp.int32, (tm, LANES), 1)
    lane_f = lane.astype(F32)
    cur = jnp.where(lane < N_EXPERTS, lg_ref[...], -jnp.inf)
    vals, hots = [], []
    for _ in range(TOP_K):
        m = jnp.max(cur, axis=1, keepdims=True)
        idx = jnp.min(jnp.where(cur == m, lane_f, float(LANES)), axis=1, keepdims=True)
        hot = lane_f == idx
        vals.append(m)
        hots.append(hot)
        cur = jnp.where(hot, -jnp.inf, cur)
    exps = [jnp.exp(v - vals[0]) for v in vals]
    denom = exps[0] + exps[1] + exps[2] + exps[3]
    sel = hots[0] | hots[1] | hots[2] | hots[3]
    self32 = sel.astype(F32)
    base = cnt_ref[...]
    num = jnp.sum(self32, axis=0, keepdims=True)
    rank = _dot(lstrict_ref[...], self32.astype(BF16))
    cnt_ref[...] = base + num
    lane8 = lax.broadcasted_iota(jnp.int32, (SUBLANES, LANES), 1)
    incl = jnp.broadcast_to(num, (SUBLANES, LANES))
    shift = 1
    while shift < N_EXPERTS:
        incl = incl + jnp.where(lane8 >= shift, pltpu.roll(incl, shift, 1), 0.0)
        shift *= 2
    off = incl[:1] - num
    slot = rank + off
    meta = jnp.zeros((tm, LANES), F32)
    for k in range(TOP_K):
        s_k = jnp.sum(jnp.where(hots[k], slot, 0.0), axis=1, keepdims=True)
        meta = jnp.where(lane == k, s_k, meta)
        meta = jnp.where(lane == SUBLANES + k, exps[k] / denom, meta)
    meta_t = meta.T
    srow_ref[0] = (meta_t[:SUBLANES] * float(ROW_TILES)).astype(jnp.int32)
    gate_ref[0] = meta_t[SUBLANES:2 * SUBLANES]
    row8 = lax.broadcasted_iota(jnp.int32, (SUBLANES, LANES), 0)
    tmeta_ref[0] = jnp.where(row8 == 0, base, jnp.where(row8 == 1, num, jnp.where(row8 == 2, off, 0.0)))


def _route(logits, tm):
    t = logits.shape[0]
    r = jnp.arange(tm)
    lstrict = (r[:, None] > r[None, :]).astype(BF16)
    return pl.pallas_call(
        _route_kernel,
        out_shape=(jax.ShapeDtypeStruct((t // tm, SUBLANES, tm), jnp.int32),
                   jax.ShapeDtypeStruct((t // tm, SUBLANES, tm), F32),
                   jax.ShapeDtypeStruct((t // tm, SUBLANES, LANES), F32),
                   jax.ShapeDtypeStruct((1, LANES), F32)),
        grid=(t // tm,),
        in_specs=[pl.BlockSpec((tm, LANES), lambda i: (i, 0)), pl.BlockSpec((tm, tm), lambda i: (0, 0))],
        out_specs=(pl.BlockSpec((1, SUBLANES, tm), lambda i: (i, 0, 0)),
                   pl.BlockSpec((1, SUBLANES, tm), lambda i: (i, 0, 0)),
                   pl.BlockSpec((1, SUBLANES, LANES), lambda i: (i, 0, 0)),
                   pl.BlockSpec((1, LANES), lambda i: (0, 0))),
        compiler_params=_params(("arbitrary",)),
        name="route",
    )(logits, lstrict)


def _for_each_strip(tmeta_ref, tm, make_copy, action):
    del tm

    def per_expert(e, carry):
        n, hbm0, tile0 = tmeta_ref[0, 1, e], tmeta_ref[0, 0, e], tmeta_ref[0, 2, e]
        n_chunks = lax.shift_right_logical(n, STRIP_BITS)

        def chunk(c, inner):
            action(make_copy(hbm0 + c * STRIP, tile0 + c * STRIP, STRIP))
            return inner

        lax.fori_loop(0, n_chunks, chunk, 0)
        for bit in reversed(range(STRIP_BITS)):
            size = 1 << bit

            @pl.when(jnp.bitwise_and(n, size) != 0)
            def _():
                done = jnp.bitwise_and(n, ~(2 * size - 1))
                action(make_copy(hbm0 + done, tile0 + done, size))
        return carry

    lax.fori_loop(0, N_EXPERTS, per_expert, 0)


def _rows(ref, row, n):
    return ref.at[pl.ds(pl.multiple_of(row * ROW_TILES, ROW_TILES), n * ROW_TILES), :]


def _dispatch_kernel(pstart_ref, pend_ref, srow_ref, tmeta_ref, h2_ref, xs_ref, stage_ref, zero_ref, sem):
    tm = h2_ref.shape[0] // ROW_TILES
    i = pl.program_id(0)
    slot = i % 2

    used = pend_ref[N_EXPERTS - 1] // EXPERT_BLOCK
    total = xs_ref.shape[0] // (EXPERT_BLOCK * ROW_TILES)

    def clear(action):
        def block(row):
            return pltpu.make_async_copy(zero_ref, _rows(xs_ref, row, EXPERT_BLOCK), sem.at[2])

        def last_of_segment(e, carry):
            @pl.when(pend_ref[e] > pstart_ref[e])
            def _():
                action(block(pend_ref[e] - EXPERT_BLOCK))
            return carry

        lax.fori_loop(0, N_EXPERTS, last_of_segment, 0)
        lax.fori_loop(used, total, lambda j, c: (action(block(j * EXPERT_BLOCK)), c)[1], 0)

    @pl.when(i == 0)
    def _():
        zero_ref[...] = jnp.zeros_like(zero_ref)
        clear(lambda copy: copy.start())

    def place(t, carry):
        row = h2_ref[pl.ds(pl.multiple_of(t * ROW_TILES, ROW_TILES), ROW_TILES), :]
        for k in range(TOP_K):
            stage_ref[slot, pl.ds(pl.multiple_of(srow_ref[k * tm + t], ROW_TILES), ROW_TILES), :] = row
        return carry

    lax.fori_loop(0, tm, place, 0, unroll=ISSUE_UNROLL)

    @pl.when(i == 0)
    def _():
        clear(lambda copy: copy.wait())

    def strips(meta_ref, s):
        return functools.partial(
            _for_each_strip, meta_ref, tm,
            lambda hbm_row, tile_row, n: pltpu.make_async_copy(_rows(stage_ref.at[s], tile_row, n),
                                                               _rows(xs_ref, hbm_row, n), sem.at[s]))

    strips(tmeta_ref, slot)(lambda copy: copy.start())

    def wait_all(s):
        pltpu.make_async_copy(stage_ref.at[s], _rows(xs_ref, 0, TOP_K * tm), sem.at[s]).wait()

    @pl.when(i > 0)
    def _():
        wait_all(1 - slot)

    @pl.when(i == pl.num_programs(0) - 1)
    def _():
        wait_all(slot)


def _dispatch(pstart, pends, srow, tmeta, h2, n_rows, tm):
    t = h2.shape[0] // ROW_TILES
    smem = lambda shape, imap: pl.BlockSpec(shape, imap, memory_space=pltpu.SMEM)
    return pl.pallas_call(
        _dispatch_kernel,
        out_shape=jax.ShapeDtypeStruct((n_rows * ROW_TILES, LANES), F32),
        grid_spec=pltpu.PrefetchScalarGridSpec(
            num_scalar_prefetch=2,
            grid=(t // tm,),
            in_specs=[smem((SUBLANES * tm,), lambda i, ps, pe: (i,)),
                      smem((1, SUBLANES, LANES), lambda i, ps, pe: (i, 0, 0)),
                      pl.BlockSpec((tm * ROW_TILES, LANES), lambda i, ps, pe: (i, 0))],
            out_specs=pl.BlockSpec(memory_space=pl.ANY),
            scratch_shapes=[pltpu.VMEM((2, TOP_K * tm * ROW_TILES, LANES), F32),
                            pltpu.VMEM((EXPERT_BLOCK * ROW_TILES, LANES), F32),
                            pltpu.SemaphoreType.DMA((3,))],
        ),
        compiler_params=_params(("arbitrary",)),
        name="dispatch",
    )(pstart, pends, srow, tmeta, h2)


def _combine_kernel(n_prompt_tiles, srow_ref, gate_ref, tmeta_ref, tmeta_next_ref, x1_ref, nw_ref, yb_ref,
                    op_ref, os_ref, buf_ref, acc_ref, sem):
    tm = x1_ref.shape[0]
    i = pl.program_id(0)
    slot = i % 2

    def strips(meta_ref, s):
        return functools.partial(
            _for_each_strip, meta_ref, tm,
            lambda hbm_row, tile_row, n: pltpu.make_async_copy(_rows(yb_ref, hbm_row, n),
                                                               _rows(buf_ref.at[s], tile_row, n), sem.at[s]))

    @pl.when(i == 0)
    def _():
        strips(tmeta_ref, 0)(lambda copy: copy.start())

    @pl.when(i + 1 < pl.num_programs(0))
    def _():
        strips(tmeta_next_ref, 1 - slot)(lambda copy: copy.start())

    pltpu.make_async_copy(_rows(yb_ref, 0, TOP_K * tm), buf_ref.at[slot], sem.at[slot]).wait()

    def gather(t, carry):
        def term(k):
            row = buf_ref[slot, pl.ds(pl.multiple_of(srow_ref[k * tm + t], ROW_TILES), ROW_TILES), :]
            return row * gate_ref[k * tm + t]

        moe = term(0)
        for k in range(1, TOP_K):
            moe = moe + term(k)
        acc_ref[pl.ds(pl.multiple_of(t * ROW_TILES, ROW_TILES), ROW_TILES), :] = moe
        return carry

    lax.fori_loop(0, tm, gather, 0, unroll=ISSUE_UNROLL)
    x2 = x1_ref[...] + _load_row_tiles(acc_ref, tm)
    ms = jnp.mean(x2 * x2, axis=-1, keepdims=True)
    out = x2 * lax.rsqrt(ms + EPS) * nw_ref[...]

    @pl.when(i < n_prompt_tiles)
    def _():
        op_ref[...] = out

    @pl.when(i >= n_prompt_tiles)
    def _():
        os_ref[...] = out


def _combine(srow, gates, tmeta, x1, norm_final, yb, n_prompt, tm):
    t = x1.shape[0]
    nt = t // tm
    npt = n_prompt // tm
    smem = lambda shape, imap: pl.BlockSpec(shape, imap, memory_space=pltpu.SMEM)
    return pl.pallas_call(
        functools.partial(_combine_kernel, npt),
        out_shape=(jax.ShapeDtypeStruct((n_prompt, D_MODEL), F32),
                   jax.ShapeDtypeStruct((t - n_prompt, D_MODEL), F32)),
        grid=(nt,),
        in_specs=[smem((SUBLANES * tm,), lambda i: (i,)),
                  smem((SUBLANES * tm,), lambda i: (i,)),
                  smem((1, SUBLANES, LANES), lambda i: (i, 0, 0)),
                  smem((1, SUBLANES, LANES), lambda i: (jnp.minimum(i + 1, nt - 1), 0, 0)),
                  pl.BlockSpec((tm, D_MODEL), lambda i: (i, 0)),
                  pl.BlockSpec((1, D_MODEL), lambda i: (0, 0)),
                  pl.BlockSpec(memory_space=pl.ANY)],
        out_specs=_split_rows(tm, npt, D_MODEL),
        scratch_shapes=[pltpu.VMEM((2, TOP_K * tm * ROW_TILES, LANES), F32),
                        pltpu.VMEM((tm * ROW_TILES, LANES), F32),
                        pltpu.SemaphoreType.DMA((2,))],
        compiler_params=_params(("arbitrary",)),
        name="combine",
    )(srow, gates, tmeta, tmeta, x1, norm_final, yb)


def _expert_kernel(be_ref, nv_ref, next_ref, xs_ref, bgu_ref, bd_ref, wgu_hbm, wd_hbm, yb_ref,
                   wgu_f, wd_f, wgu_b, wd_b, sem):
    i = pl.program_id(0)

    def fetch(e):
        return (pltpu.make_async_copy(wgu_hbm.at[e], wgu_f, sem.at[0]),
                pltpu.make_async_copy(wd_hbm.at[e], wd_f, sem.at[1]))

    @pl.when(i == 0)
    def _():
        for copy in fetch(be_ref[0]):
            copy.start()

    first = jnp.logical_or(i == 0, be_ref[i] != be_ref[jnp.maximum(i - 1, 0)])

    @pl.when(jnp.logical_and(first, i < nv_ref[0]))
    def _():
        for copy in fetch(be_ref[i]):
            copy.wait()
        wgu_b[...] = wgu_f[...].astype(BF16)
        wd_b[...] = wd_f[...].astype(BF16)

        @pl.when(next_ref[i] >= 0)
        def _():
            for copy in fetch(next_ref[i]):
                copy.start()

    @pl.when(i < nv_ref[0])
    def _():
        gu = _dot(_load_row_tiles(xs_ref, EXPERT_BLOCK).astype(BF16), wgu_b[...]) + bgu_ref[0]
        glu = jnp.minimum(gu[:, :D_FF], SWIGLU_LIMIT)
        lin = jnp.clip(gu[:, D_FF:], -SWIGLU_LIMIT, SWIGLU_LIMIT)
        act = glu * jax.nn.sigmoid(SWIGLU_ALPHA * glu) * (lin + 1.0)
        _store_row_tiles(yb_ref, _dot(act.astype(BF16), wd_b[...]) + bd_ref[0])

    @pl.when(i >= nv_ref[0])
    def _():
        yb_ref[...] = jnp.zeros_like(yb_ref)


def _experts(block_expert, n_valid, next_expert, xs, w_gu, b_gu, w_d, b_d):
    rows = EXPERT_BLOCK * ROW_TILES
    nb = xs.shape[0] // rows
    return pl.pallas_call(
        _expert_kernel,
        out_shape=jax.ShapeDtypeStruct(xs.shape, F32),
        grid_spec=pltpu.PrefetchScalarGridSpec(
            num_scalar_prefetch=3,
            grid=(nb,),
            in_specs=[pl.BlockSpec((rows, LANES), lambda i, be, nv, nx: (jnp.minimum(i, nv[0] - 1), 0)),
                      pl.BlockSpec((1, 1, 2 * D_FF), lambda i, be, nv, nx: (be[i], 0, 0)),
                      pl.BlockSpec((1, 1, D_MODEL), lambda i, be, nv, nx: (be[i], 0, 0)),
                      pl.BlockSpec(memory_space=pl.ANY),
                      pl.BlockSpec(memory_space=pl.ANY)],
            out_specs=pl.BlockSpec((rows, LANES), lambda i, be, nv, nx: (i, 0)),
            scratch_shapes=[pltpu.VMEM((D_MODEL, 2 * D_FF), F32), pltpu.VMEM((D_FF, D_MODEL), F32),
                            pltpu.VMEM((D_MODEL, 2 * D_FF), BF16), pltpu.VMEM((D_FF, D_MODEL), BF16),
                            pltpu.SemaphoreType.DMA((2,))],
        ),
        compiler_params=_params(("arbitrary",)),
        name="experts",
    )(block_expert, n_valid, next_expert, xs, b_gu, b_d, w_gu, w_d)


def _expert_layout(counts, n_blocks):
    counts = counts.astype(jnp.int32)
    padded = (counts + EXPERT_BLOCK - 1) // EXPERT_BLOCK * EXPERT_BLOCK
    pends = jnp.cumsum(padded)
    pstart = pends - padded
    block_start = jnp.arange(n_blocks, dtype=jnp.int32) * EXPERT_BLOCK
    block_expert = jnp.minimum(jnp.sum(pends[None, :] <= block_start[:, None], axis=1), N_EXPERTS - 1)
    n_valid = (pends[-1:] // EXPERT_BLOCK).astype(jnp.int32)
    seg_end = jnp.sum(jnp.where(block_expert[:, None] == jnp.arange(N_EXPERTS), pends[None, :], 0), axis=1)
    after = seg_end // EXPERT_BLOCK
    next_expert = jnp.where(after < n_valid[0], block_expert[jnp.minimum(after, n_blocks - 1)], -1)
    return (pstart.astype(jnp.int32), pends.astype(jnp.int32), block_expert.astype(jnp.int32), n_valid,
            next_expert.astype(jnp.int32))


def _permute_w_in(w_in):
    sizes = [RET_QK, RET_QK, RET_V, RET_V, SSM_INNER, CONV_DIM, SSM_HEADS, D_MODEL, D_MODEL]
    offs = [0]
    for s in sizes:
        offs.append(offs[-1] + s)
    q, k, v, g_ret, z, xbc, dt, g_a, g_b = [w_in[:, offs[i]:offs[i + 1]] for i in range(len(sizes))]
    gn = SSM_GROUPS * SSM_DSTATE
    xs, bm, cm = xbc[:, :SSM_INNER], xbc[:, SSM_INNER:SSM_INNER + gn], xbc[:, SSM_INNER + gn:]
    dt_pad = jnp.pad(dt, ((0, 0), (0, _ColsB.TOTAL - _ColsB.DT - SSM_HEADS)))
    w_a = jnp.concatenate([v, g_ret, g_a, g_b, q, k], axis=1).astype(BF16)
    w_b = jnp.concatenate([z, xs, bm, cm, dt_pad], axis=1).astype(BF16)
    return w_a, w_b


def _forward(x_prompt, x_sample, state_ret, state_ssm, state_conv, norm_mix, w_in, ret_norm, w_out_ret,
             conv_w, conv_b, dt_bias, a_log, d_skip, ssm_norm, w_out_ssm, w_out, norm_ffn,
             w_router, b_router, w_gate_up, b_gate_up, w_down, b_down, norm_final):
    bp, lp, _ = x_prompt.shape
    bs, ls, _ = x_sample.shape
    assert lp % ROWS == 0 and bs % SAMPLE_PER_TILE == 0 and ls * SAMPLE_PER_TILE == ROWS
    tp, ts = bp * lp, bs * ls
    t_all = tp + ts
    xp, xs_in = x_prompt.reshape(tp, D_MODEL), x_sample.reshape(ts, D_MODEL)

    w_a, w_b = _permute_w_in(w_in[0])
    norm_w = norm_mix[0].reshape(1, D_MODEL)
    proj = _inproj(xp, xs_in, norm_w, w_a)
    proj_ssd_s = _inproj(None, xs_in, norm_w, w_b)


    sp = _ssd_params(conv_w[0], conv_b[0], dt_bias[0], a_log[0], d_skip[0], ssm_norm[0])
    y_p, ssm_p, conv_p, a_p, ret_p = _mixer_prompt(xp, norm_w, w_b, proj, bp, lp, sp, ret_norm[0])
    y_s, ssm_s, conv_s, a_s, ret_s = _mixer_sample(proj, proj_ssd_s, tp, bs, ls, sp, state_ssm[0], state_conv[0],
                                                   state_ret[0], ret_norm[0])

    w_router_pad = jnp.pad(w_router[0], ((0, 0), (0, LANES - N_EXPERTS))).astype(BF16)
    b_router_pad = jnp.pad(b_router[0], (0, LANES - N_EXPERTS)).reshape(1, LANES)
    x1, h2, logits = _post(a_p, a_s, y_p, y_s, proj, xp, xs_in, w_out_ret[0].astype(BF16), w_out_ssm[0].astype(BF16),
                           w_out[0].astype(BF16), norm_ffn[0].reshape(1, D_MODEL), w_router_pad, b_router_pad)

    tm_moe = _pick(math.gcd(tp, ts), 512)
    srow, gates, tmeta, counts = _route(logits, tm_moe)
    n_blocks = -(-(t_all * TOP_K + N_EXPERTS * (EXPERT_BLOCK - 1)) // EXPERT_BLOCK)
    pstart, pends, block_expert, n_valid, next_expert = _expert_layout(counts[0, :N_EXPERTS], n_blocks)
    tmeta = tmeta.astype(jnp.int32)
    run_start = tmeta[:, :1, :] + jnp.pad(pstart, (0, LANES - N_EXPERTS))
    tmeta = jnp.concatenate([run_start, tmeta[:, 1:, :]], axis=1)
    srow, gates = srow.reshape(-1), gates.reshape(-1)
    xs = _dispatch(pstart, pends, srow, tmeta, h2, n_blocks * EXPERT_BLOCK, tm_moe)
    yb = _experts(block_expert, n_valid, next_expert, xs, w_gate_up[0],
                  b_gate_up[0].reshape(N_EXPERTS, 1, 2 * D_FF), w_down[0], b_down[0].reshape(N_EXPERTS, 1, D_MODEL))
    out_p, out_s = _combine(srow, gates, tmeta, x1, norm_final.reshape(1, D_MODEL), yb, tp, tm_moe)

    shape_s = (1, bs, SSM_HEADS, SSM_HEADDIM, SSM_DSTATE)
    shape_p = (1, bp, SSM_HEADS, SSM_HEADDIM, SSM_DSTATE)
    return (out_p.reshape(bp, lp, D_MODEL), out_s.reshape(bs, ls, D_MODEL),
            ret_p[None], ret_s[None], ssm_p.reshape(shape_p), ssm_s.reshape(shape_s),
            conv_p[None], conv_s[None])


def kernel(x_prompt, x_sample, state_ret, state_ssm, state_conv, norm_mix, w_in, ret_norm, w_out_ret, conv_w, conv_b, dt_bias, a_log, d_skip, ssm_norm, w_out_ssm, w_out, norm_ffn, w_router, b_router, w_gate_up, b_gate_up, w_down, b_down, norm_final):
    return _forward(x_prompt, x_sample, state_ret, state_ssm, state_conv, norm_mix, w_in, ret_norm, w_out_ret,
                    conv_w, conv_b, dt_bias, a_log, d_skip, ssm_norm, w_out_ssm, w_out, norm_ffn,
                    w_router, b_router, w_gate_up, b_gate_up, w_down, b_down, norm_final)
```

```python
import functools
import math

import jax
import jax.numpy as jnp
from jax import lax
from jax.experimental import pallas as pl
from jax.experimental.pallas import tpu as pltpu

F32 = jnp.float32
BF16 = jnp.bfloat16

D_MODEL = 1024
PAST_LEN = 16384
RET_HEADS = 4
RET_DK = 128
RET_DV = 256
RET_QK = RET_HEADS * RET_DK
RET_V = RET_HEADS * RET_DV
ROPE_BASE = 10000.0
SSM_INNER = 2 * D_MODEL
SSM_HEADDIM = 64
SSM_HEADS = SSM_INNER // SSM_HEADDIM
SSM_GROUPS = 4
SSM_DSTATE = 128
HEADS_PER_GROUP = SSM_HEADS // SSM_GROUPS
GROUP_INNER = SSM_INNER // SSM_GROUPS
CONV_WIDTH = 4
CONV_DIM = SSM_INNER + 2 * SSM_GROUPS * SSM_DSTATE
CHUNK = 128
N_EXPERTS = 32
TOP_K = 4
D_FF = D_MODEL
SWIGLU_LIMIT = 7.0
SWIGLU_ALPHA = 1.702
EPS = 1e-6

LANES = 128
SUBLANES = 8
ROWS = 128
SAMPLE_PER_TILE = 16
EXPERT_BLOCK = 512
ROW_TILES = D_MODEL // LANES
ISSUE_UNROLL = 8
STRIP_BITS = 5
STRIP = 1 << STRIP_BITS
VMEM_LIMIT = 56 * 1024 * 1024


class _ColsA:
    GA = 0
    GB = GA + D_MODEL
    TOTAL = GB + D_MODEL


class _ColsB:
    Z = 0
    XS = Z + SSM_INNER
    B = XS + SSM_INNER
    C = B + SSM_GROUPS * SSM_DSTATE
    DT = C + SSM_GROUPS * SSM_DSTATE
    V = DT + 2 * LANES
    GRET = V + RET_V
    Q = GRET + RET_V
    K = Q + RET_QK
    TOTAL = K + RET_QK


def _pick(n, target, step=LANES):
    best = None
    for c in range(step, target + 1, step):
        if n % c == 0:
            best = c
    assert best is not None, (n, target)
    return best


def _params(sem, **kw):
    return pltpu.CompilerParams(dimension_semantics=sem, vmem_limit_bytes=VMEM_LIMIT, **kw)


def _dot(a, b):
    return jnp.dot(a, b, preferred_element_type=F32)


def _dot_nt(a, b):
    return lax.dot_general(a, b, (((1,), (1,)), ((), ())), preferred_element_type=F32)


def _dot_tn(a, b):
    return lax.dot_general(a, b, (((0,), (0,)), ((), ())), preferred_element_type=F32)


def _silu(x):
    return x * jax.nn.sigmoid(x)


def _store_row_tiles(ref, value, *lead):
    n = value.shape[0]
    for s in range(ROW_TILES):
        ref[(*lead, pl.ds(s, n, stride=ROW_TILES), slice(None))] = value[:, s * LANES:(s + 1) * LANES]


def _load_row_tiles(ref, n, *lead):
    return jnp.concatenate([ref[(*lead, pl.ds(s, n, stride=ROW_TILES), slice(None))] for s in range(ROW_TILES)],
                           axis=1)


def _inproj_kernel(n_prompt_tiles, xp_ref, xs_ref, nw_ref, w_ref, o_ref, h_ref):
    @pl.when(pl.program_id(1) == 0)
    def _():
        x = jnp.where(pl.program_id(0) < n_prompt_tiles, xp_ref[...], xs_ref[...])
        ms = jnp.mean(x * x, axis=-1, keepdims=True)
        h_ref[...] = (x * lax.rsqrt(ms + EPS) * nw_ref[...]).astype(BF16)

    o_ref[...] = _dot(h_ref[...], w_ref[...])


def _split_rows(tm, npt, width):
    prompt = pl.BlockSpec((tm, width), lambda i, *_: (jnp.minimum(i, npt - 1), 0))
    sample = pl.BlockSpec((tm, width), lambda i, *_: (jnp.maximum(i - npt, 0), 0))
    return prompt, sample


def _inproj(xp, xs, norm_w, w):
    total = w.shape[1]
    t = xs.shape[0] + (0 if xp is None else xp.shape[0])
    tm = _pick(xs.shape[0] if xp is None else math.gcd(xp.shape[0], xs.shape[0]), 1024)
    npt = 0 if xp is None else xp.shape[0] // tm
    tn = _pick(total, 2560, step=2 * LANES)
    x_specs = list(_split_rows(tm, npt, D_MODEL)) if npt else [pl.BlockSpec((tm, D_MODEL), lambda i, j: (i, 0))] * 2
    return pl.pallas_call(
        functools.partial(_inproj_kernel, npt),
        out_shape=jax.ShapeDtypeStruct((t, total), F32),
        grid=(t // tm, total // tn),
        in_specs=[
            *x_specs,
            pl.BlockSpec((1, D_MODEL), lambda i, j: (0, 0)),
            pl.BlockSpec((D_MODEL, tn), lambda i, j: (0, j)),
        ],
        out_specs=pl.BlockSpec((tm, tn), lambda i, j: (i, j)),
        scratch_shapes=[pltpu.VMEM((tm, D_MODEL), BF16)],
        compiler_params=_params(("parallel", "arbitrary")),
        name="in_proj",
    )(xs if xp is None else xp, xs, norm_w, w)


def _ret_log_decay():
    return jnp.log(1.0 - 2.0 ** (-5.0 - jnp.arange(RET_HEADS, dtype=F32)))


def _ret_tables(c):
    lg = _ret_log_decay()
    r = jnp.arange(ROWS)
    t = (r % c).astype(F32)
    seg = r // c
    diff = t[:, None] - t[None, :]
    ok = (seg[:, None] == seg[None, :]) & (diff >= 0)
    dm = jnp.where(ok[None], jnp.exp(lg[:, None, None] * jnp.maximum(diff, 0.0)[None]), 0.0)
    qd = jnp.exp(lg[:, None] * (t[None, :] + 1.0))
    kd = jnp.exp(lg[:, None] * (c - 1.0 - t[None, :]))
    cd = jnp.exp(lg * c)
    qd = jnp.broadcast_to(qd[:, :, None], (RET_HEADS, ROWS, LANES))
    kd = jnp.broadcast_to(kd[:, :, None], (RET_HEADS, ROWS, LANES))
    cd = jnp.broadcast_to(cd[:, None, None], (RET_HEADS, 1, RET_DV))
    return dm, qd, kd, cd


def _rope_tables(pos):
    half = RET_DK // 2
    inv_freq = 1.0 / (ROPE_BASE ** jnp.linspace(0.0, 1.0, half, dtype=F32))
    ang = pos.astype(F32)[:, None] * inv_freq[None, :]
    cos, sin = jnp.cos(ang), jnp.sin(ang)
    return jnp.concatenate([cos, cos], -1), jnp.concatenate([-sin, sin], -1)


def _block_mask(dtype):
    r = jnp.arange(ROWS) // (ROWS // SAMPLE_PER_TILE)
    b = jnp.arange(SAMPLE_PER_TILE * LANES) // LANES
    return (r[:, None] == b[None, :]).astype(dtype)


def _ret_common(q, k, v, g, cosf, sinf, dm, qd, kd, nw):
    qr = q * cosf + pltpu.roll(q, RET_DK // 2, 1) * sinf
    kr = (k * cosf + pltpu.roll(k, RET_DK // 2, 1) * sinf) * (RET_DK ** -0.5)
    vb = v.astype(BF16)
    scores = _dot_nt(qr.astype(BF16), kr.astype(BF16)) * dm
    o_intra = _dot(scores.astype(BF16), vb)
    q_state = (qr * qd).astype(BF16)
    k_state = kr * kd

    def finish(o):
        ms = jnp.mean(o * o, axis=-1, keepdims=True)
        on = o * lax.rsqrt(ms + EPS) * nw
        return (_silu(g) * on).astype(BF16)

    return o_intra, q_state, k_state, vb, finish


def _ret_prompt_step(q_ref, k_ref, v_ref, g_ref, cos_ref, sin_ref, dm_ref, qd_ref, kd_ref, cd_ref, nw_ref,
                     a_ref, s_ref):
    cosf, sinf = cos_ref[...], sin_ref[...]
    for h in range(RET_HEADS):
        dk = slice(h * RET_DK, (h + 1) * RET_DK)
        dv = slice(h * RET_DV, (h + 1) * RET_DV)
        o_intra, q_state, k_state, vb, finish = _ret_common(
            q_ref[:, dk], k_ref[:, dk], v_ref[:, dv], g_ref[:, dv], cosf, sinf, dm_ref[h], qd_ref[h], kd_ref[h],
            nw_ref[h])
        s = s_ref[0, h]
        a_ref[:, dv] = finish(o_intra + _dot(q_state, s.astype(BF16)))
        s_ref[0, h] = s * cd_ref[h] + _dot_tn(k_state.astype(BF16), vb)


def _ret_sample_kernel(q_ref, k_ref, v_ref, g_ref, cos_ref, sin_ref, dm_ref, qd_ref, kd_ref, cd_ref, nw_ref,
                       bm_ref, bmt_ref, s0_ref, a_ref, s_ref):
    o_intra, q_state, k_state, vb, finish = _ret_common(
        q_ref[...], k_ref[...], v_ref[...], g_ref[...], cos_ref[...], sin_ref[...], dm_ref[0], qd_ref[0], kd_ref[0],
        nw_ref[0])
    nb = SAMPLE_PER_TILE
    s0 = s0_ref[:, 0]
    q_bd = jnp.concatenate([q_state] * nb, axis=1) * bm_ref[...]
    a_ref[...] = finish(o_intra + _dot(q_bd, s0.reshape(nb * RET_DK, RET_DV).astype(BF16)))
    k_t = k_state.T.astype(BF16)
    k_bd_t = jnp.concatenate([k_t] * nb, axis=0) * bmt_ref[...]
    ds = _dot(k_bd_t, vb).reshape(nb, RET_DK, RET_DV)
    s_ref[:, 0] = s0 * cd_ref[0] + ds


def _ret_specs(row_block):
    def at(width, off):
        return lambda *ids: (row_block(*ids)[0], off // width + row_block(*ids)[1])

    def head(*shape):
        return pl.BlockSpec((1,) + shape, lambda *ids: (row_block(*ids)[1],) + (0,) * len(shape))

    return [
        pl.BlockSpec((ROWS, RET_DK), at(RET_DK, _ColsB.Q)),
        pl.BlockSpec((ROWS, RET_DK), at(RET_DK, _ColsB.K)),
        pl.BlockSpec((ROWS, RET_DV), at(RET_DV, _ColsB.V)),
        pl.BlockSpec((ROWS, RET_DV), at(RET_DV, _ColsB.GRET)),
    ], head


def _retention_sample_call(proj, row0, bsz, seq, state, ret_norm):
    rb0 = row0 // ROWS
    dm, qd, kd, cd = _ret_tables(seq)
    pos = PAST_LEN + jnp.arange(seq, dtype=jnp.int32)
    cosf, sinf = _rope_tables(jnp.tile(pos, SAMPLE_PER_TILE))
    rb = lambda t, h: (rb0 + t, h)
    specs, head = _ret_specs(rb)
    full = lambda *shape: pl.BlockSpec(shape, lambda t, h: (0,) * len(shape))
    specs += [
        full(ROWS, RET_DK), full(ROWS, RET_DK),
        head(ROWS, ROWS), head(ROWS, LANES), head(ROWS, LANES), head(1, RET_DV), head(1, RET_DV),
        full(ROWS, SAMPLE_PER_TILE * LANES), full(SAMPLE_PER_TILE * LANES, ROWS),
        pl.BlockSpec((SAMPLE_PER_TILE, 1, RET_DK, RET_DV), lambda t, h: (t, h, 0, 0)),
    ]
    bm = _block_mask(BF16)
    operands = (proj, proj, proj, proj, cosf, sinf, dm, qd, kd, cd, ret_norm.reshape(RET_HEADS, 1, RET_DV),
                bm, bm.T, state)
    out_shapes = (jax.ShapeDtypeStruct((bsz * seq, RET_V), BF16), jax.ShapeDtypeStruct(state.shape, F32))
    out_specs = (pl.BlockSpec((ROWS, RET_DV), lambda t, h: (t, h)),
                 pl.BlockSpec((SAMPLE_PER_TILE, 1, RET_DK, RET_DV), lambda t, h: (t, h, 0, 0)))
    return specs, operands, out_shapes, out_specs


def _softplus(x):
    return jnp.maximum(x, 0.0) + jnp.log1p(jnp.exp(-jnp.abs(x)))


def _conv_piece(x, pred, w_ref, b_ref):
    width = x.shape[1]
    x3 = x.reshape(ROWS // SUBLANES, SUBLANES, width)
    t8 = lax.broadcasted_iota(jnp.int32, x3.shape, 1)
    acc = b_ref[...].reshape(1, 1, width)
    for i in range(CONV_WIDTH):
        s = CONV_WIDTH - 1 - i
        tap = x3 if s == 0 else pltpu.roll(jnp.where(t8 >= SUBLANES - s, pred, x3), s, 1)
        acc = acc + tap * w_ref[pl.ds(i, 1), :].reshape(1, 1, width)
    return _silu(acc).reshape(ROWS, width)


def _ssd_decay_terms(c, dt_pre, a_log, ltri):
    dt = _softplus(dt_pre)
    d_a = dt * (-jnp.exp(a_log))
    hi = d_a.astype(BF16)
    r1 = d_a - hi.astype(F32)
    mid = r1.astype(BF16)
    lo = (r1 - mid.astype(F32)).astype(BF16)
    cum = _dot(ltri, hi) + _dot(ltri, mid) + _dot(ltri, lo)
    cum3 = cum.reshape(ROWS // c, c, LANES)
    c_last = jnp.broadcast_to(cum3[:, c - 1:c, :], cum3.shape).reshape(ROWS, LANES)
    to_end = jnp.exp(c_last - cum) * dt
    e_cum = jnp.exp(cum)
    return cum, to_end, e_cum, cum.T, dt.T


def _ssd_group(c, terms, head0, z_ref, dsk_ref, nrm_ref, xs, bm, cm, y_ref, state_io, zero=None):
    cum, to_end, e_cum, cum_t, dt_t = terms
    b_b, c_b = bm.astype(BF16), cm.astype(BF16)
    cb = _dot_nt(c_b, b_b)
    ri = lax.broadcasted_iota(jnp.int32, (ROWS, ROWS), 0)
    ci = lax.broadcasted_iota(jnp.int32, (ROWS, ROWS), 1)
    shift = c.bit_length() - 1
    causal = (ci <= ri) & (jnp.right_shift(ri, shift) == jnp.right_shift(ci, shift))
    lo_half = lax.broadcasted_iota(jnp.int32, (ROWS, LANES), 1) < SSM_HEADDIM

    def col(a, h):
        return jnp.broadcast_to(a[:, h:h + 1], (ROWS, LANES))

    def weights(h):
        seg = col(cum, h) - jnp.broadcast_to(cum_t[h:h + 1, :], (ROWS, ROWS))
        decay = jnp.exp(jnp.where(causal, seg, -jnp.inf))
        return (cb * decay * jnp.broadcast_to(dt_t[h:h + 1, :], (ROWS, ROWS))).astype(BF16)

    pieces = []
    for m in range(HEADS_PER_GROUP // 2):
        ha, hb = head0 + 2 * m, head0 + 2 * m + 1
        xp = xs[:, m * LANES:(m + 1) * LANES]
        w2 = jnp.concatenate([weights(ha), weights(hb)], axis=1)
        x2 = jnp.concatenate([jnp.where(lo_half, xp, 0.0), jnp.where(lo_half, 0.0, xp)], axis=0).astype(BF16)
        y = _dot(w2, x2)
        xw = (xp * jnp.where(lo_half, col(to_end, ha), col(to_end, hb))).astype(BF16)
        y_state = state_io(m, ha, c_b, xw, b_b, e_cum)
        y = y + y_state * jnp.where(lo_half, col(e_cum, ha), col(e_cum, hb))
        pieces.append(y + xp * dsk_ref[:, m * LANES:(m + 1) * LANES])
    if zero is not None:
        pieces[0] = jnp.concatenate([pieces[0][:SUBLANES] + zero, pieces[0][SUBLANES:]], axis=0)
    yg = jnp.concatenate(pieces, axis=1) * _silu(z_ref[...])
    ms = jnp.mean(yg * yg, axis=-1, keepdims=True)
    y_ref[...] = (yg * lax.rsqrt(ms + EPS) * nrm_ref[...]).astype(BF16)


def _pair_decay(e_cum, row, ha):
    top = lax.broadcasted_iota(jnp.int32, (2 * SSM_HEADDIM, SSM_DSTATE), 0) < SSM_HEADDIM
    ea = jnp.broadcast_to(e_cum[row:row + 1, ha:ha + 1], top.shape)
    eb = jnp.broadcast_to(e_cum[row:row + 1, ha + 1:ha + 2], top.shape)
    return jnp.where(top, ea, eb)


def _mixer_prompt_kernel(x0_ref, xn_ref, nw_ref, w_ref, cwx_ref, cbx_ref, cwb_ref, cbb_ref, cwc_ref, cbc_ref,
                         dtb_ref, alog_ref, dsk_ref, nrm_ref, ltri_ref, *rest):
    ret_in, (y_ref, hs_ref, px_ref, pb_ref, pc_ref, a_ref, s_ref, proj_ref) = rest[:7], rest[7:]
    c = pl.program_id(1)

    def normed(x_ref):
        x = x_ref[...]
        ms = jnp.mean(x * x, axis=-1, keepdims=True)
        return (x * lax.rsqrt(ms + EPS) * nw_ref[...]).astype(BF16)

    def project(x_ref, slot):
        proj_ref[slot] = _dot(normed(x_ref), w_ref[...])

    @pl.when(c == 0)
    def _():
        hs_ref[...] = jnp.zeros_like(hs_ref)
        px_ref[...] = jnp.zeros_like(px_ref)
        pb_ref[...] = jnp.zeros_like(pb_ref)
        pc_ref[...] = jnp.zeros_like(pc_ref)
        s_ref[...] = jnp.zeros_like(s_ref)
        project(x0_ref, 0)

    def conv(x_ref, p_ref, w_ref, bias_ref):
        x = x_ref[...]
        x3 = x.reshape(ROWS // SUBLANES, SUBLANES, x.shape[1])
        pred = jnp.concatenate([p_ref[...], x3[:-1]], axis=0)
        out = _conv_piece(x, pred, w_ref, bias_ref)
        p_ref[0] = x3[-1]
        return out

    def step(cur_slot, next_slot):
        h_next = normed(xn_ref)
        bounds = [0, 8 * 256, 16 * 256, 24 * 256, _ColsB.TOTAL]

        def project_slice(g):
            lo, hi = bounds[g], bounds[g + 1]
            part = _dot(h_next, w_ref[:, lo:hi])
            proj_ref[next_slot, :, lo:hi] = part
            bits = pltpu.bitcast(part[ROWS - SUBLANES:, hi - lo - LANES:], jnp.uint32)
            sixteen = jnp.uint32(16)
            return pltpu.bitcast(lax.shift_right_logical(lax.shift_right_logical(bits, sixteen), sixteen), F32)

        cur = proj_ref.at[cur_slot]
        gn = SSM_GROUPS * SSM_DSTATE
        _ret_prompt_step(cur.at[:, _ColsB.Q:_ColsB.Q + RET_QK], cur.at[:, _ColsB.K:_ColsB.K + RET_QK],
                         cur.at[:, _ColsB.V:_ColsB.V + RET_V], cur.at[:, _ColsB.GRET:_ColsB.GRET + RET_V],
                         *ret_in, a_ref, s_ref)
        z_ref = cur.at[:, _ColsB.Z:_ColsB.Z + SSM_INNER]
        xs_ref = cur.at[:, _ColsB.XS:_ColsB.XS + SSM_INNER]
        b_ref = cur.at[:, _ColsB.B:_ColsB.B + gn]
        c_ref = cur.at[:, _ColsB.C:_ColsB.C + gn]
        dt_ref = cur.at[:, _ColsB.DT:_ColsB.DT + LANES]
        terms = _ssd_decay_terms(CHUNK, dt_ref[...] + dtb_ref[...], alog_ref[...], ltri_ref[...])
        for g in range(SSM_GROUPS):
            def cols(ref, width, g=g):
                return ref.at[..., g * width:(g + 1) * width]

            gi, n = GROUP_INNER, SSM_DSTATE
            xs = conv(cols(xs_ref, gi), cols(px_ref, gi), cols(cwx_ref, gi), cols(cbx_ref, gi))
            bm = conv(cols(b_ref, n), cols(pb_ref, n), cols(cwb_ref, n), cols(cbb_ref, n))
            cm = conv(cols(c_ref, n), cols(pc_ref, n), cols(cwc_ref, n), cols(cbc_ref, n))
            hs_g = hs_ref.at[0, g * gi:(g + 1) * gi, :]

            def state_io(m, ha, c_b, xw, b_b, e_cum, hs_g=hs_g):
                rows = pl.ds(m * 2 * SSM_HEADDIM, 2 * SSM_HEADDIM)
                h = hs_g[rows, :]
                hs_g[rows, :] = h * _pair_decay(e_cum, ROWS - 1, ha) + _dot_tn(xw, b_b)
                return _dot_nt(c_b, h.astype(BF16))

            _ssd_group(CHUNK, terms, g * HEADS_PER_GROUP, cols(z_ref, gi), cols(dsk_ref, gi), cols(nrm_ref, gi),
                       xs, bm, cm, cols(y_ref, gi), state_io, zero=project_slice(g))

    @pl.when(c % 2 == 0)
    def _():
        step(0, 1)

    @pl.when(c % 2 == 1)
    def _():
        step(1, 0)


def _ssd_sample_kernel(seq, z_ref, xs_ref, b_ref, c_ref, dt_ref, cwx_ref, cbx_ref, cwb_ref, cbb_ref, cwc_ref,
                       cbc_ref, dtb_ref, alog_ref, dsk_ref, nrm_ref, ltri_ref, bmask_ref, px_ref, pb_ref, pc_ref,
                       h0_ref, y_ref, hs_ref, csx_ref, csb_ref, csc_ref):
    nb = SAMPLE_PER_TILE

    def conv(x_ref, p_ref, w_ref, bias_ref, cs_ref):
        x = x_ref[...]
        cs_ref[...] = x.reshape(nb, seq, x.shape[1])
        return _conv_piece(x, p_ref[...], w_ref, bias_ref)

    xs = conv(xs_ref, px_ref, cwx_ref, cbx_ref, csx_ref)
    bm = conv(b_ref, pb_ref, cwb_ref, cbb_ref, csb_ref)
    cm = conv(c_ref, pc_ref, cwc_ref, cbc_ref, csc_ref)
    bmask = bmask_ref[...]

    def state_io(m, ha, c_b, xw, b_b, e_cum):
        rows = pl.ds(m * 2 * SSM_HEADDIM, 2 * SSM_HEADDIM)
        c_bd = jnp.concatenate([c_b] * nb, axis=1) * bmask
        b_bd = jnp.concatenate([b_b] * nb, axis=1) * bmask
        hs = [h0_ref[i, rows, :] for i in range(nb)]
        h_cat = jnp.concatenate(hs, axis=1).astype(BF16)
        dh = _dot_tn(xw, b_bd)
        for i in range(nb):
            decay = _pair_decay(e_cum, i * seq + seq - 1, ha)
            hs_ref[i, rows, :] = hs[i] * decay + dh[:, i * SSM_DSTATE:(i + 1) * SSM_DSTATE]
        return _dot_nt(c_bd, h_cat)

    shift = jnp.bitwise_and(LANES - HEADS_PER_GROUP * pl.program_id(1), LANES - 1)
    dt_pre = pltpu.roll(dt_ref[...] + dtb_ref[...], shift, 1)
    a_log = pltpu.roll(jnp.broadcast_to(alog_ref[...], (SUBLANES, LANES)), shift, 1)[:1]
    terms = _ssd_decay_terms(seq, dt_pre, a_log, ltri_ref[...])
    _ssd_group(seq, terms, 0, z_ref, dsk_ref, nrm_ref, xs, bm, cm, y_ref, state_io)


def _ssd_tables(c):
    r = jnp.arange(ROWS)
    ltri = ((r[:, None] >= r[None, :]) & ((r[:, None] // c) == (r[None, :] // c))).astype(BF16)
    return ltri


def _ssd_specs(row_block):
    def at(width, off):
        return lambda *ids: (row_block(*ids)[0], off // width + row_block(*ids)[1])

    def grp(rows, width):
        return pl.BlockSpec((rows, width), lambda *ids: (0, row_block(*ids)[1]))

    n = SSM_DSTATE
    return [
        pl.BlockSpec((ROWS, GROUP_INNER), at(GROUP_INNER, _ColsB.Z)),
        pl.BlockSpec((ROWS, GROUP_INNER), at(GROUP_INNER, _ColsB.XS)),
        pl.BlockSpec((ROWS, n), at(n, _ColsB.B)),
        pl.BlockSpec((ROWS, n), at(n, _ColsB.C)),
        pl.BlockSpec((ROWS, LANES), lambda *ids: (row_block(*ids)[0], _ColsB.DT // LANES)),
        grp(CONV_WIDTH, GROUP_INNER), grp(1, GROUP_INNER),
        grp(CONV_WIDTH, n), grp(1, n), grp(CONV_WIDTH, n), grp(1, n),
        pl.BlockSpec((1, LANES), lambda *ids: (0, 0)), pl.BlockSpec((1, LANES), lambda *ids: (0, 0)),
        grp(1, GROUP_INNER), grp(1, GROUP_INNER),
    ]


def _ssd_params(conv_w, conv_b, dt_bias, a_log, d_skip, ssm_norm):
    gn = SSM_GROUPS * SSM_DSTATE
    cwx, cwb, cwc = conv_w[:, :SSM_INNER], conv_w[:, SSM_INNER:SSM_INNER + gn], conv_w[:, SSM_INNER + gn:]
    cb = conv_b.reshape(1, CONV_DIM)
    cbx, cbb, cbc = cb[:, :SSM_INNER], cb[:, SSM_INNER:SSM_INNER + gn], cb[:, SSM_INNER + gn:]

    def head_lanes(v):
        return jnp.pad(v, (0, LANES - SSM_HEADS)).reshape(1, LANES)

    dsk = jnp.repeat(d_skip, SSM_HEADDIM).reshape(1, SSM_INNER)
    return (cwx, cbx, cwb, cbb, cwc, cbc, head_lanes(dt_bias), head_lanes(a_log), dsk,
            ssm_norm.reshape(1, SSM_INNER))


def _conv_tail(csx, csb, csc):
    keep = SUBLANES - (CONV_WIDTH - 1)
    return jnp.concatenate([csx[:, keep:], csb[:, keep:], csc[:, keep:]], axis=-1)


def _mixer_prompt(x, norm_w, w_b, bsz, seq, params, ret_norm):
    nch = seq // ROWS
    gn = SSM_GROUPS * SSM_DSTATE
    full = lambda a: pl.BlockSpec(a.shape, lambda b, c: (0,) * a.ndim)
    ltri = _ssd_tables(CHUNK)
    dm, qd, kd, cd = _ret_tables(CHUNK)
    cosf, sinf = _rope_tables(jnp.arange(seq, dtype=jnp.int32))
    rnw = ret_norm.reshape(RET_HEADS, 1, RET_DV)
    specs = [pl.BlockSpec((ROWS, D_MODEL), lambda b, c: (b * nch, 0)),
             pl.BlockSpec((ROWS, D_MODEL), lambda b, c: (b * nch + jnp.minimum(c + 1, nch - 1), 0)),
             full(norm_w), full(w_b), *[full(p) for p in params], full(ltri),
             pl.BlockSpec((ROWS, RET_DK), lambda b, c: (c, 0)), pl.BlockSpec((ROWS, RET_DK), lambda b, c: (c, 0)),
             full(dm), full(qd), full(kd), full(cd), full(rnw)]
    tail = lambda width: pl.BlockSpec((1, SUBLANES, width), lambda b, c: (b, 0, 0))
    y, hs, csx, csb, csc, a, s = pl.pallas_call(
        _mixer_prompt_kernel,
        out_shape=(jax.ShapeDtypeStruct((bsz * seq, SSM_INNER), BF16),
                   jax.ShapeDtypeStruct((bsz, SSM_HEADS * SSM_HEADDIM, SSM_DSTATE), F32),
                   jax.ShapeDtypeStruct((bsz, SUBLANES, SSM_INNER), F32),
                   jax.ShapeDtypeStruct((bsz, SUBLANES, gn), F32),
                   jax.ShapeDtypeStruct((bsz, SUBLANES, gn), F32),
                   jax.ShapeDtypeStruct((bsz * seq, RET_V), BF16),
                   jax.ShapeDtypeStruct((bsz, RET_HEADS, RET_DK, RET_DV), F32)),
        grid=(bsz, nch),
        in_specs=specs,
        out_specs=(pl.BlockSpec((ROWS, SSM_INNER), lambda b, c: (b * nch + c, 0)),
                   pl.BlockSpec((1, SSM_HEADS * SSM_HEADDIM, SSM_DSTATE), lambda b, c: (b, 0, 0)),
                   tail(SSM_INNER), tail(gn), tail(gn),
                   pl.BlockSpec((ROWS, RET_V), lambda b, c: (b * nch + c, 0)),
                   pl.BlockSpec((1, RET_HEADS, RET_DK, RET_DV), lambda b, c: (b, 0, 0, 0))),
        scratch_shapes=[pltpu.VMEM((2, ROWS, _ColsB.TOTAL), F32)],
        compiler_params=_params(("parallel", "arbitrary")),
        name="mixer_prompt",
    )(x, x, norm_w, w_b, *params, ltri, cosf, sinf, dm, qd, kd, cd, rnw)
    return y, hs, _conv_tail(csx, csb, csc), a, s


def _mixer_sample_kernel(seq, n_ssd_in, n_ret_in, *refs):
    ssd_in, refs = refs[:n_ssd_in], refs[n_ssd_in:]
    ret_in, refs = refs[:n_ret_in], refs[n_ret_in:]
    _ssd_sample_kernel(seq, *ssd_in, *refs[:5])
    _ret_sample_kernel(*ret_in, *refs[5:])


def _mixer_sample(proj, bsz, seq, params, state, conv_state, ret_state, ret_norm):
    assert SSM_GROUPS == RET_HEADS
    ntile = bsz // SAMPLE_PER_TILE
    gn = SSM_GROUPS * SSM_DSTATE
    specs = _ssd_specs(lambda t, g: (t, g))
    gh = HEADS_PER_GROUP * SSM_HEADDIM
    cs = jnp.pad(conv_state, ((0, 0), (SUBLANES - (CONV_WIDTH - 1), 0), (0, 0)))
    csx, csb, csc = cs[:, :, :SSM_INNER], cs[:, :, SSM_INNER:SSM_INNER + gn], cs[:, :, SSM_INNER + gn:]
    pred = lambda width: pl.BlockSpec((SAMPLE_PER_TILE, SUBLANES, width), lambda t, g: (t, 0, g))
    state_spec = pl.BlockSpec((SAMPLE_PER_TILE, gh, SSM_DSTATE), lambda t, g: (t, g, 0))
    specs += [
        pl.BlockSpec((ROWS, ROWS), lambda t, g: (0, 0)),
        pl.BlockSpec((ROWS, SAMPLE_PER_TILE * LANES), lambda t, g: (0, 0)),
        pred(GROUP_INNER), pred(SSM_DSTATE), pred(SSM_DSTATE),
        state_spec,
    ]
    state2 = state.reshape(bsz, SSM_HEADS * SSM_HEADDIM, SSM_DSTATE)
    assert seq == SUBLANES
    ssd_operands = (proj, proj, proj, proj, proj, *params, _ssd_tables(seq), _block_mask(BF16), csx, csb, csc,
                    state2)
    r_specs, r_operands, r_shapes, r_out_specs = _retention_sample_call(proj, 0, bsz, seq, ret_state, ret_norm)
    y, hs, csx, csb, csc, a, s = pl.pallas_call(
        functools.partial(_mixer_sample_kernel, seq, len(ssd_operands), len(r_operands)),
        out_shape=(jax.ShapeDtypeStruct((bsz * seq, SSM_INNER), BF16), jax.ShapeDtypeStruct(state2.shape, F32),
                   jax.ShapeDtypeStruct((bsz, seq, SSM_INNER), F32),
                   jax.ShapeDtypeStruct((bsz, seq, gn), F32),
                   jax.ShapeDtypeStruct((bsz, seq, gn), F32), *r_shapes),
        grid=(ntile, SSM_GROUPS),
        in_specs=specs + r_specs,
        out_specs=(pl.BlockSpec((ROWS, GROUP_INNER), lambda t, g: (t, g)), state_spec,
                   pred(GROUP_INNER), pred(SSM_DSTATE), pred(SSM_DSTATE), *r_out_specs),
        compiler_params=_params(("parallel", "parallel")),
        name="mixer_sample",
    )(*ssd_operands, *r_operands)
    return y, hs, _conv_tail(csx, csb, csc), a, s


def _post_kernel(n_prompt_tiles, ap_ref, as_ref, yp_ref, ys_ref, ga_ref, gb_ref, xp_ref, xs_ref, wr_ref, ws_ref,
                 wo_ref, nw_ref, wrt_ref, brt_ref, x1_ref, h2_ref, lg_ref):
    is_prompt = pl.program_id(0) < n_prompt_tiles
    half = x1_ref.shape[0] // 2
    for r in (pl.ds(0, half), pl.ds(half, half)):
        branch_a = _dot(jnp.where(is_prompt, ap_ref[r, :], as_ref[r, :]), wr_ref[...])
        branch_b = _dot(jnp.where(is_prompt, yp_ref[r, :], ys_ref[r, :]), ws_ref[...])
        merged = jax.nn.sigmoid(ga_ref[r, :]) * branch_a + jax.nn.sigmoid(gb_ref[r, :]) * branch_b
        x1 = jnp.where(is_prompt, xp_ref[r, :], xs_ref[r, :]) + _dot(merged.astype(BF16), wo_ref[...])
        x1_ref[r, :] = x1
        ms = jnp.mean(x1 * x1, axis=-1, keepdims=True)
        h2 = x1 * lax.rsqrt(ms + EPS) * nw_ref[...]
        _store_row_tiles(h2_ref.at[pl.ds(r.start * ROW_TILES, half * ROW_TILES), :], h2)
        lg_ref[r, :] = _dot(h2.astype(BF16), wrt_ref[...]) + brt_ref[...]


def _post(a_p, a_s, y_p, y_s, proj, xp, xs, w_ret, w_ssm, w_out, norm_ffn, w_router, b_router):
    t = xp.shape[0] + xs.shape[0]
    tm = _pick(math.gcd(xp.shape[0], xs.shape[0]), 512)
    npt = xp.shape[0] // tm
    row = lambda width, blk: pl.BlockSpec((tm, width), lambda i: (i, blk))
    full = lambda *shape: pl.BlockSpec(shape, lambda i: (0,) * len(shape))
    return pl.pallas_call(
        functools.partial(_post_kernel, npt),
        out_shape=(jax.ShapeDtypeStruct((t, D_MODEL), F32), jax.ShapeDtypeStruct((t * ROW_TILES, LANES), F32),
                   jax.ShapeDtypeStruct((t, LANES), F32)),
        grid=(t // tm,),
        in_specs=[*_split_rows(tm, npt, RET_V), *_split_rows(tm, npt, SSM_INNER),
                  row(D_MODEL, _ColsA.GA // D_MODEL), row(D_MODEL, _ColsA.GB // D_MODEL),
                  *_split_rows(tm, npt, D_MODEL),
                  full(RET_V, D_MODEL), full(SSM_INNER, D_MODEL), full(D_MODEL, D_MODEL), full(1, D_MODEL),
                  full(D_MODEL, LANES), full(1, LANES)],
        out_specs=(row(D_MODEL, 0), pl.BlockSpec((tm * ROW_TILES, LANES), lambda i: (i, 0)), row(LANES, 0)),
        compiler_params=_params(("parallel",)),
        name="post_mixer",
    )(a_p, a_s, y_p, y_s, proj, proj, xp, xs, w_ret, w_ssm, w_out, norm_ffn, w_router, b_router)


def _route_kernel(lg_ref, lstrict_ref, srow_ref, gate_ref, tmeta_ref, cnt_ref):
    @pl.when(pl.program_id(0) == 0)
    def _():
        cnt_ref[...] = jnp.zeros_like(cnt_ref)

    tm = lg_ref.shape[0]
    lane = lax.broadcasted_iota(jnp.int32, (tm, LANES), 1)
    lane_f = lane.astype(F32)
    cur = jnp.where(lane < N_EXPERTS, lg_ref[...], -jnp.inf)
    vals, hots = [], []
    for _ in range(TOP_K):
        m = jnp.max(cur, axis=1, keepdims=True)
        idx = jnp.min(jnp.where(cur == m, lane_f, float(LANES)), axis=1, keepdims=True)
        hot = lane_f == idx
        vals.append(m)
        hots.append(hot)
        cur = jnp.where(hot, -jnp.inf, cur)
    exps = [jnp.exp(v - vals[0]) for v in vals]
    denom = exps[0] + exps[1] + exps[2] + exps[3]
    sel = hots[0] | hots[1] | hots[2] | hots[3]
    self32 = sel.astype(F32)
    base = cnt_ref[...]
    num = jnp.sum(self32, axis=0, keepdims=True)
    rank = _dot(lstrict_ref[...], self32.astype(BF16))
    cnt_ref[...] = base + num
    lane8 = lax.broadcasted_iota(jnp.int32, (SUBLANES, LANES), 1)
    incl = jnp.broadcast_to(num, (SUBLANES, LANES))
    shift = 1
    while shift < N_EXPERTS:
        incl = incl + jnp.where(lane8 >= shift, pltpu.roll(incl, shift, 1), 0.0)
        shift *= 2
    off = incl[:1] - num
    slot = rank + off
    meta = jnp.zeros((tm, LANES), F32)
    for k in range(TOP_K):
        s_k = jnp.sum(jnp.where(hots[k], slot, 0.0), axis=1, keepdims=True)
        meta = jnp.where(lane == k, s_k, meta)
        meta = jnp.where(lane == SUBLANES + k, exps[k] / denom, meta)
    meta_t = meta.T
    srow_ref[0] = (meta_t[:SUBLANES] * float(ROW_TILES)).astype(jnp.int32)
    gate_ref[0] = meta_t[SUBLANES:2 * SUBLANES]
    row8 = lax.broadcasted_iota(jnp.int32, (SUBLANES, LANES), 0)
    tmeta_ref[0] = jnp.where(row8 == 0, base, jnp.where(row8 == 1, num, jnp.where(row8 == 2, off, 0.0)))


def _route(logits, tm):
    t = logits.shape[0]
    r = jnp.arange(tm)
    lstrict = (r[:, None] > r[None, :]).astype(BF16)
    return pl.pallas_call(
        _route_kernel,
        out_shape=(jax.ShapeDtypeStruct((t // tm, SUBLANES, tm), jnp.int32),
                   jax.ShapeDtypeStruct((t // tm, SUBLANES, tm), F32),
                   jax.ShapeDtypeStruct((t // tm, SUBLANES, LANES), F32),
                   jax.ShapeDtypeStruct((1, LANES), F32)),
        grid=(t // tm,),
        in_specs=[pl.BlockSpec((tm, LANES), lambda i: (i, 0)), pl.BlockSpec((tm, tm), lambda i: (0, 0))],
        out_specs=(pl.BlockSpec((1, SUBLANES, tm), lambda i: (i, 0, 0)),
                   pl.BlockSpec((1, SUBLANES, tm), lambda i: (i, 0, 0)),
                   pl.BlockSpec((1, SUBLANES, LANES), lambda i: (i, 0, 0)),
                   pl.BlockSpec((1, LANES), lambda i: (0, 0))),
        compiler_params=_params(("arbitrary",)),
        name="route",
    )(logits, lstrict)


def _for_each_strip(tmeta_ref, tm, make_copy, action):
    del tm

    def per_expert(e, carry):
        n, hbm0, tile0 = tmeta_ref[0, 1, e], tmeta_ref[0, 0, e], tmeta_ref[0, 2, e]
        n_chunks = lax.shift_right_logical(n, STRIP_BITS)

        def chunk(c, inner):
            action(make_copy(hbm0 + c * STRIP, tile0 + c * STRIP, STRIP))
            return inner

        lax.fori_loop(0, n_chunks, chunk, 0)
        for bit in reversed(range(STRIP_BITS)):
            size = 1 << bit

            @pl.when(jnp.bitwise_and(n, size) != 0)
            def _():
                done = jnp.bitwise_and(n, ~(2 * size - 1))
                action(make_copy(hbm0 + done, tile0 + done, size))
        return carry

    lax.fori_loop(0, N_EXPERTS, per_expert, 0)


def _rows(ref, row, n):
    return ref.at[pl.ds(pl.multiple_of(row * ROW_TILES, ROW_TILES), n * ROW_TILES), :]


def _dispatch_kernel(pstart_ref, pend_ref, srow_ref, tmeta_ref, h2_ref, xs_ref, stage_ref, zero_ref, sem):
    tm = h2_ref.shape[0] // ROW_TILES
    i = pl.program_id(0)
    slot = i % 2

    used = pend_ref[N_EXPERTS - 1] // EXPERT_BLOCK
    total = xs_ref.shape[0] // (EXPERT_BLOCK * ROW_TILES)

    def clear(action):
        def block(row):
            return pltpu.make_async_copy(zero_ref, _rows(xs_ref, row, EXPERT_BLOCK), sem.at[2])

        def last_of_segment(e, carry):
            @pl.when(pend_ref[e] > pstart_ref[e])
            def _():
                action(block(pend_ref[e] - EXPERT_BLOCK))
            return carry

        lax.fori_loop(0, N_EXPERTS, last_of_segment, 0)
        lax.fori_loop(used, total, lambda j, c: (action(block(j * EXPERT_BLOCK)), c)[1], 0)

    @pl.when(i == 0)
    def _():
        zero_ref[...] = jnp.zeros_like(zero_ref)
        clear(lambda copy: copy.start())

    def place(t, carry):
        row = h2_ref[pl.ds(pl.multiple_of(t * ROW_TILES, ROW_TILES), ROW_TILES), :]
        for k in range(TOP_K):
            stage_ref[slot, pl.ds(pl.multiple_of(srow_ref[k * tm + t], ROW_TILES), ROW_TILES), :] = row
        return carry

    lax.fori_loop(0, tm, place, 0, unroll=ISSUE_UNROLL)

    @pl.when(i == 0)
    def _():
        clear(lambda copy: copy.wait())

    def strips(meta_ref, s):
        return functools.partial(
            _for_each_strip, meta_ref, tm,
            lambda hbm_row, tile_row, n: pltpu.make_async_copy(_rows(stage_ref.at[s], tile_row, n),
                                                               _rows(xs_ref, hbm_row, n), sem.at[s]))

    strips(tmeta_ref, slot)(lambda copy: copy.start())

    def wait_all(s):
        pltpu.make_async_copy(stage_ref.at[s], _rows(xs_ref, 0, TOP_K * tm), sem.at[s]).wait()

    @pl.when(i > 0)
    def _():
        wait_all(1 - slot)

    @pl.when(i == pl.num_programs(0) - 1)
    def _():
        wait_all(slot)


def _dispatch(pstart, pends, srow, tmeta, h2, n_rows, tm):
    t = h2.shape[0] // ROW_TILES
    smem = lambda shape, imap: pl.BlockSpec(shape, imap, memory_space=pltpu.SMEM)
    return pl.pallas_call(
        _dispatch_kernel,
        out_shape=jax.ShapeDtypeStruct((n_rows * ROW_TILES, LANES), F32),
        grid_spec=pltpu.PrefetchScalarGridSpec(
            num_scalar_prefetch=2,
            grid=(t // tm,),
            in_specs=[smem((SUBLANES * tm,), lambda i, ps, pe: (i,)),
                      smem((1, SUBLANES, LANES), lambda i, ps, pe: (i, 0, 0)),
                      pl.BlockSpec((tm * ROW_TILES, LANES), lambda i, ps, pe: (i, 0))],
            out_specs=pl.BlockSpec(memory_space=pl.ANY),
            scratch_shapes=[pltpu.VMEM((2, TOP_K * tm * ROW_TILES, LANES), F32),
                            pltpu.VMEM((EXPERT_BLOCK * ROW_TILES, LANES), F32),
                            pltpu.SemaphoreType.DMA((3,))],
        ),
        compiler_params=_params(("arbitrary",)),
        name="dispatch",
    )(pstart, pends, srow, tmeta, h2)


def _combine_kernel(n_prompt_tiles, srow_ref, gate_ref, tmeta_ref, tmeta_next_ref, x1_ref, nw_ref, yb_ref,
                    op_ref, os_ref, buf_ref, acc_ref, sem):
    tm = x1_ref.shape[0]
    i = pl.program_id(0)
    slot = i % 2

    def strips(meta_ref, s):
        return functools.partial(
            _for_each_strip, meta_ref, tm,
            lambda hbm_row, tile_row, n: pltpu.make_async_copy(_rows(yb_ref, hbm_row, n),
                                                               _rows(buf_ref.at[s], tile_row, n), sem.at[s]))

    @pl.when(i == 0)
    def _():
        strips(tmeta_ref, 0)(lambda copy: copy.start())

    @pl.when(i + 1 < pl.num_programs(0))
    def _():
        strips(tmeta_next_ref, 1 - slot)(lambda copy: copy.start())

    pltpu.make_async_copy(_rows(yb_ref, 0, TOP_K * tm), buf_ref.at[slot], sem.at[slot]).wait()

    def gather(t, carry):
        def term(k):
            row = buf_ref[slot, pl.ds(pl.multiple_of(srow_ref[k * tm + t], ROW_TILES), ROW_TILES), :]
            return row * gate_ref[k * tm + t]

        moe = term(0)
        for k in range(1, TOP_K):
            moe = moe + term(k)
        acc_ref[pl.ds(pl.multiple_of(t * ROW_TILES, ROW_TILES), ROW_TILES), :] = moe
        return carry

    lax.fori_loop(0, tm, gather, 0, unroll=ISSUE_UNROLL)
    x2 = x1_ref[...] + _load_row_tiles(acc_ref, tm)
    ms = jnp.mean(x2 * x2, axis=-1, keepdims=True)
    out = x2 * lax.rsqrt(ms + EPS) * nw_ref[...]

    @pl.when(i < n_prompt_tiles)
    def _():
        op_ref[...] = out

    @pl.when(i >= n_prompt_tiles)
    def _():
        os_ref[...] = out


def _combine(srow, gates, tmeta, x1, norm_final, yb, n_prompt, tm):
    t = x1.shape[0]
    nt = t // tm
    npt = n_prompt // tm
    smem = lambda shape, imap: pl.BlockSpec(shape, imap, memory_space=pltpu.SMEM)
    return pl.pallas_call(
        functools.partial(_combine_kernel, npt),
        out_shape=(jax.ShapeDtypeStruct((n_prompt, D_MODEL), F32),
                   jax.ShapeDtypeStruct((t - n_prompt, D_MODEL), F32)),
        grid=(nt,),
        in_specs=[smem((SUBLANES * tm,), lambda i: (i,)),
                  smem((SUBLANES * tm,), lambda i: (i,)),
                  smem((1, SUBLANES, LANES), lambda i: (i, 0, 0)),
                  smem((1, SUBLANES, LANES), lambda i: (jnp.minimum(i + 1, nt - 1), 0, 0)),
                  pl.BlockSpec((tm, D_MODEL), lambda i: (i, 0)),
                  pl.BlockSpec((1, D_MODEL), lambda i: (0, 0)),
                  pl.BlockSpec(memory_space=pl.ANY)],
        out_specs=_split_rows(tm, npt, D_MODEL),
        scratch_shapes=[pltpu.VMEM((2, TOP_K * tm * ROW_TILES, LANES), F32),
                        pltpu.VMEM((tm * ROW_TILES, LANES), F32),
                        pltpu.SemaphoreType.DMA((2,))],
        compiler_params=_params(("arbitrary",)),
        name="combine",
    )(srow, gates, tmeta, tmeta, x1, norm_final, yb)


def _expert_kernel(be_ref, nv_ref, next_ref, xs_ref, bgu_ref, bd_ref, wgu_hbm, wd_hbm, yb_ref,
                   wgu_f, wd_f, wgu_b, wd_b, sem):
    i = pl.program_id(0)

    def fetch(e):
        return (pltpu.make_async_copy(wgu_hbm.at[e], wgu_f, sem.at[0]),
                pltpu.make_async_copy(wd_hbm.at[e], wd_f, sem.at[1]))

    @pl.when(i == 0)
    def _():
        for copy in fetch(be_ref[0]):
            copy.start()

    first = jnp.logical_or(i == 0, be_ref[i] != be_ref[jnp.maximum(i - 1, 0)])

    @pl.when(jnp.logical_and(first, i < nv_ref[0]))
    def _():
        for copy in fetch(be_ref[i]):
            copy.wait()
        wgu_b[...] = wgu_f[...].astype(BF16)
        wd_b[...] = wd_f[...].astype(BF16)

        @pl.when(next_ref[i] >= 0)
        def _():
            for copy in fetch(next_ref[i]):
                copy.start()

    @pl.when(i < nv_ref[0])
    def _():
        gu = _dot(_load_row_tiles(xs_ref, EXPERT_BLOCK).astype(BF16), wgu_b[...]) + bgu_ref[0]
        glu = jnp.minimum(gu[:, :D_FF], SWIGLU_LIMIT)
        lin = jnp.clip(gu[:, D_FF:], -SWIGLU_LIMIT, SWIGLU_LIMIT)
        act = glu * jax.nn.sigmoid(SWIGLU_ALPHA * glu) * (lin + 1.0)
        _store_row_tiles(yb_ref, _dot(act.astype(BF16), wd_b[...]) + bd_ref[0])

    @pl.when(i >= nv_ref[0])
    def _():
        yb_ref[...] = jnp.zeros_like(yb_ref)


def _experts(block_expert, n_valid, next_expert, xs, w_gu, b_gu, w_d, b_d):
    rows = EXPERT_BLOCK * ROW_TILES
    nb = xs.shape[0] // rows
    return pl.pallas_call(
        _expert_kernel,
        out_shape=jax.ShapeDtypeStruct(xs.shape, F32),
        grid_spec=pltpu.PrefetchScalarGridSpec(
            num_scalar_prefetch=3,
            grid=(nb,),
            in_specs=[pl.BlockSpec((rows, LANES), lambda i, be, nv, nx: (jnp.minimum(i, nv[0] - 1), 0)),
                      pl.BlockSpec((1, 1, 2 * D_FF), lambda i, be, nv, nx: (be[i], 0, 0)),
                      pl.BlockSpec((1, 1, D_MODEL), lambda i, be, nv, nx: (be[i], 0, 0)),
                      pl.BlockSpec(memory_space=pl.ANY),
                      pl.BlockSpec(memory_space=pl.ANY)],
            out_specs=pl.BlockSpec((rows, LANES), lambda i, be, nv, nx: (i, 0)),
            scratch_shapes=[pltpu.VMEM((D_MODEL, 2 * D_FF), F32), pltpu.VMEM((D_FF, D_MODEL), F32),
                            pltpu.VMEM((D_MODEL, 2 * D_FF), BF16), pltpu.VMEM((D_FF, D_MODEL), BF16),
                            pltpu.SemaphoreType.DMA((2,))],
        ),
        compiler_params=_params(("arbitrary",)),
        name="experts",
    )(block_expert, n_valid, next_expert, xs, b_gu, b_d, w_gu, w_d)


def _expert_layout(counts, n_blocks):
    counts = counts.astype(jnp.int32)
    padded = (counts + EXPERT_BLOCK - 1) // EXPERT_BLOCK * EXPERT_BLOCK
    pends = jnp.cumsum(padded)
    pstart = pends - padded
    block_start = jnp.arange(n_blocks, dtype=jnp.int32) * EXPERT_BLOCK
    block_expert = jnp.minimum(jnp.sum(pends[None, :] <= block_start[:, None], axis=1), N_EXPERTS - 1)
    n_valid = (pends[-1:] // EXPERT_BLOCK).astype(jnp.int32)
    seg_end = jnp.sum(jnp.where(block_expert[:, None] == jnp.arange(N_EXPERTS), pends[None, :], 0), axis=1)
    after = seg_end // EXPERT_BLOCK
    next_expert = jnp.where(after < n_valid[0], block_expert[jnp.minimum(after, n_blocks - 1)], -1)
    return (pstart.astype(jnp.int32), pends.astype(jnp.int32), block_expert.astype(jnp.int32), n_valid,
            next_expert.astype(jnp.int32))


def _permute_w_in(w_in):
    sizes = [RET_QK, RET_QK, RET_V, RET_V, SSM_INNER, CONV_DIM, SSM_HEADS, D_MODEL, D_MODEL]
    offs = [0]
    for s in sizes:
        offs.append(offs[-1] + s)
    q, k, v, g_ret, z, xbc, dt, g_a, g_b = [w_in[:, offs[i]:offs[i + 1]] for i in range(len(sizes))]
    gn = SSM_GROUPS * SSM_DSTATE
    xs, bm, cm = xbc[:, :SSM_INNER], xbc[:, SSM_INNER:SSM_INNER + gn], xbc[:, SSM_INNER + gn:]
    dt_pad = jnp.pad(dt, ((0, 0), (0, _ColsB.V - _ColsB.DT - SSM_HEADS)))
    w_a = jnp.concatenate([g_a, g_b], axis=1).astype(BF16)
    w_b = jnp.concatenate([z, xs, bm, cm, dt_pad, v, g_ret, q, k], axis=1).astype(BF16)
    return w_a, w_b


def _forward(x_prompt, x_sample, state_ret, state_ssm, state_conv, norm_mix, w_in, ret_norm, w_out_ret,
             conv_w, conv_b, dt_bias, a_log, d_skip, ssm_norm, w_out_ssm, w_out, norm_ffn,
             w_router, b_router, w_gate_up, b_gate_up, w_down, b_down, norm_final):
    bp, lp, _ = x_prompt.shape
    bs, ls, _ = x_sample.shape
    assert lp % ROWS == 0 and bs % SAMPLE_PER_TILE == 0 and ls * SAMPLE_PER_TILE == ROWS
    tp, ts = bp * lp, bs * ls
    t_all = tp + ts
    xp, xs_in = x_prompt.reshape(tp, D_MODEL), x_sample.reshape(ts, D_MODEL)

    w_a, w_b = _permute_w_in(w_in[0])
    norm_w = norm_mix[0].reshape(1, D_MODEL)
    proj = _inproj(xp, xs_in, norm_w, w_a)
    proj_mix_s = _inproj(None, xs_in, norm_w, w_b)

    sp = _ssd_params(conv_w[0], conv_b[0], dt_bias[0], a_log[0], d_skip[0], ssm_norm[0])
    y_p, ssm_p, conv_p, a_p, ret_p = _mixer_prompt(xp, norm_w, w_b, bp, lp, sp, ret_norm[0])
    y_s, ssm_s, conv_s, a_s, ret_s = _mixer_sample(proj_mix_s, bs, ls, sp, state_ssm[0], state_conv[0],
                                                   state_ret[0], ret_norm[0])

    w_router_pad = jnp.pad(w_router[0], ((0, 0), (0, LANES - N_EXPERTS))).astype(BF16)
    b_router_pad = jnp.pad(b_router[0], (0, LANES - N_EXPERTS)).reshape(1, LANES)
    x1, h2, logits = _post(a_p, a_s, y_p, y_s, proj, xp, xs_in, w_out_ret[0].astype(BF16), w_out_ssm[0].astype(BF16),
                           w_out[0].astype(BF16), norm_ffn[0].reshape(1, D_MODEL), w_router_pad, b_router_pad)

    tm_moe = _pick(math.gcd(tp, ts), 512)
    srow, gates, tmeta, counts = _route(logits, tm_moe)
    n_blocks = -(-(t_all * TOP_K + N_EXPERTS * (EXPERT_BLOCK - 1)) // EXPERT_BLOCK)
    pstart, pends, block_expert, n_valid, next_expert = _expert_layout(counts[0, :N_EXPERTS], n_blocks)
    tmeta = tmeta.astype(jnp.int32)
    run_start = tmeta[:, :1, :] + jnp.pad(pstart, (0, LANES - N_EXPERTS))
    tmeta = jnp.concatenate([run_start, tmeta[:, 1:, :]], axis=1)
    srow, gates = srow.reshape(-1), gates.reshape(-1)
    xs = _dispatch(pstart, pends, srow, tmeta, h2, n_blocks * EXPERT_BLOCK, tm_moe)
    yb = _experts(block_expert, n_valid, next_expert, xs, w_gate_up[0],
                  b_gate_up[0].reshape(N_EXPERTS, 1, 2 * D_FF), w_down[0], b_down[0].reshape(N_EXPERTS, 1, D_MODEL))
    out_p, out_s = _combine(srow, gates, tmeta, x1, norm_final.reshape(1, D_MODEL), yb, tp, tm_moe)

    shape_s = (1, bs, SSM_HEADS, SSM_HEADDIM, SSM_DSTATE)
    shape_p = (1, bp, SSM_HEADS, SSM_HEADDIM, SSM_DSTATE)
    return (out_p.reshape(bp, lp, D_MODEL), out_s.reshape(bs, ls, D_MODEL),
            ret_p[None], ret_s[None], ssm_p.reshape(shape_p), ssm_s.reshape(shape_s),
            conv_p[None], conv_s[None])


def kernel(x_prompt, x_sample, state_ret, state_ssm, state_conv, norm_mix, w_in, ret_norm, w_out_ret, conv_w, conv_b, dt_bias, a_log, d_skip, ssm_norm, w_out_ssm, w_out, norm_ffn, w_router, b_router, w_gate_up, b_gate_up, w_down, b_down, norm_final):
    return _forward(x_prompt, x_sample, state_ret, state_ssm, state_conv, norm_mix, w_in, ret_norm, w_out_ret,
                    conv_w, conv_b, dt_bias, a_log, d_skip, ssm_norm, w_out_ssm, w_out, norm_ffn,
                    w_router, b_router, w_gate_up, b_gate_up, w_down, b_down, norm_final)
```

```python
import functools
import math

import jax
import jax.numpy as jnp
from jax import lax
from jax.experimental import pallas as pl
from jax.experimental.pallas import tpu as pltpu

F32 = jnp.float32
BF16 = jnp.bfloat16

D_MODEL = 1024
PAST_LEN = 16384
RET_HEADS = 4
RET_DK = 128
RET_DV = 256
RET_QK = RET_HEADS * RET_DK
RET_V = RET_HEADS * RET_DV
ROPE_BASE = 10000.0
SSM_INNER = 2 * D_MODEL
SSM_HEADDIM = 64
SSM_HEADS = SSM_INNER // SSM_HEADDIM
SSM_GROUPS = 4
SSM_DSTATE = 128
HEADS_PER_GROUP = SSM_HEADS // SSM_GROUPS
GROUP_INNER = SSM_INNER // SSM_GROUPS
CONV_WIDTH = 4
CONV_DIM = SSM_INNER + 2 * SSM_GROUPS * SSM_DSTATE
CHUNK = 128
N_EXPERTS = 32
TOP_K = 4
D_FF = D_MODEL
SWIGLU_LIMIT = 7.0
SWIGLU_ALPHA = 1.702
EPS = 1e-6

LANES = 128
SUBLANES = 8
ROWS = 128
SAMPLE_PER_TILE = 16
EXPERT_BLOCK = 512
ROW_TILES = D_MODEL // LANES
ISSUE_UNROLL = 8
STRIP_BITS = 5
STRIP = 1 << STRIP_BITS
VMEM_LIMIT = 56 * 1024 * 1024


class _ColsB:
    GA = 0
    GB = GA + D_MODEL
    Z = GB + D_MODEL
    XS = Z + SSM_INNER
    B = XS + SSM_INNER
    C = B + SSM_GROUPS * SSM_DSTATE
    DT = C + SSM_GROUPS * SSM_DSTATE
    V = DT + 2 * LANES
    GRET = V + RET_V
    Q = GRET + RET_V
    K = Q + RET_QK
    TOTAL = K + RET_QK
    PADDED = -(-TOTAL // (6 * LANES)) * 6 * LANES


def _pick(n, target, step=LANES):
    best = None
    for c in range(step, target + 1, step):
        if n % c == 0:
            best = c
    assert best is not None, (n, target)
    return best


def _params(sem, **kw):
    return pltpu.CompilerParams(dimension_semantics=sem, vmem_limit_bytes=VMEM_LIMIT, **kw)


def _dot(a, b):
    return jnp.dot(a, b, preferred_element_type=F32)


def _dot_nt(a, b):
    return lax.dot_general(a, b, (((1,), (1,)), ((), ())), preferred_element_type=F32)


def _dot_tn(a, b):
    return lax.dot_general(a, b, (((0,), (0,)), ((), ())), preferred_element_type=F32)


def _silu(x):
    return x * jax.nn.sigmoid(x)


def _store_row_tiles(ref, value, *lead):
    n = value.shape[0]
    for s in range(ROW_TILES):
        ref[(*lead, pl.ds(s, n, stride=ROW_TILES), slice(None))] = value[:, s * LANES:(s + 1) * LANES]


def _load_row_tiles(ref, n, *lead):
    return jnp.concatenate([ref[(*lead, pl.ds(s, n, stride=ROW_TILES), slice(None))] for s in range(ROW_TILES)],
                           axis=1)


def _inproj_kernel(x_ref, nw_ref, w_ref, o_ref, h_ref):
    @pl.when(pl.program_id(1) == 0)
    def _():
        x = x_ref[...]
        ms = jnp.mean(x * x, axis=-1, keepdims=True)
        h_ref[...] = (x * lax.rsqrt(ms + EPS) * nw_ref[...]).astype(BF16)

    o_ref[...] = _dot(h_ref[...], w_ref[...])


def _split_rows(tm, npt, width, sample_col=0):
    prompt = pl.BlockSpec((tm, width), lambda i, *_: (jnp.minimum(i, npt - 1), 0))
    sample = pl.BlockSpec((tm, width), lambda i, *_: (jnp.maximum(i - npt, 0), sample_col))
    return prompt, sample


def _inproj(x, norm_w, w):
    t, total = x.shape[0], w.shape[1]
    tm = _pick(t, 1024)
    tn = _pick(total, 2560, step=2 * LANES)
    return pl.pallas_call(
        _inproj_kernel,
        out_shape=jax.ShapeDtypeStruct((t, total), F32),
        grid=(t // tm, total // tn),
        in_specs=[
            pl.BlockSpec((tm, D_MODEL), lambda i, j: (i, 0)),
            pl.BlockSpec((1, D_MODEL), lambda i, j: (0, 0)),
            pl.BlockSpec((D_MODEL, tn), lambda i, j: (0, j)),
        ],
        out_specs=pl.BlockSpec((tm, tn), lambda i, j: (i, j)),
        scratch_shapes=[pltpu.VMEM((tm, D_MODEL), BF16)],
        compiler_params=_params(("parallel", "arbitrary")),
        name="in_proj",
    )(x, norm_w, w)


def _ret_log_decay():
    return jnp.log(1.0 - 2.0 ** (-5.0 - jnp.arange(RET_HEADS, dtype=F32)))


def _ret_tables(c):
    lg = _ret_log_decay()
    r = jnp.arange(ROWS)
    t = (r % c).astype(F32)
    seg = r // c
    diff = t[:, None] - t[None, :]
    ok = (seg[:, None] == seg[None, :]) & (diff >= 0)
    dm = jnp.where(ok[None], jnp.exp(lg[:, None, None] * jnp.maximum(diff, 0.0)[None]), 0.0)
    qd = jnp.exp(lg[:, None] * (t[None, :] + 1.0))
    kd = jnp.exp(lg[:, None] * (c - 1.0 - t[None, :]))
    cd = jnp.exp(lg * c)
    qd = jnp.broadcast_to(qd[:, :, None], (RET_HEADS, ROWS, LANES))
    kd = jnp.broadcast_to(kd[:, :, None], (RET_HEADS, ROWS, LANES))
    cd = jnp.broadcast_to(cd[:, None, None], (RET_HEADS, 1, RET_DV))
    return dm, qd, kd, cd


def _rope_tables(pos):
    half = RET_DK // 2
    inv_freq = 1.0 / (ROPE_BASE ** jnp.linspace(0.0, 1.0, half, dtype=F32))
    ang = pos.astype(F32)[:, None] * inv_freq[None, :]
    cos, sin = jnp.cos(ang), jnp.sin(ang)
    return jnp.concatenate([cos, cos], -1), jnp.concatenate([-sin, sin], -1)


def _block_mask(dtype):
    r = jnp.arange(ROWS) // (ROWS // SAMPLE_PER_TILE)
    b = jnp.arange(SAMPLE_PER_TILE * LANES) // LANES
    return (r[:, None] == b[None, :]).astype(dtype)


def _ret_common(q, k, v, g, cosf, sinf, dm, qd, kd, nw):
    qr = q * cosf + pltpu.roll(q, RET_DK // 2, 1) * sinf
    kr = (k * cosf + pltpu.roll(k, RET_DK // 2, 1) * sinf) * (RET_DK ** -0.5)
    vb = v.astype(BF16)
    scores = _dot_nt(qr.astype(BF16), kr.astype(BF16)) * dm
    o_intra = _dot(scores.astype(BF16), vb)
    q_state = (qr * qd).astype(BF16)
    k_state = kr * kd

    def finish(o):
        ms = jnp.mean(o * o, axis=-1, keepdims=True)
        on = o * lax.rsqrt(ms + EPS) * nw
        return (_silu(g) * on).astype(BF16)

    return o_intra, q_state, k_state, vb, finish


def _ret_prompt_step(q_ref, k_ref, v_ref, g_ref, cos_ref, sin_ref, dm_ref, qd_ref, kd_ref, cd_ref, nw_ref,
                     a_ref, s_ref):
    cosf, sinf = cos_ref[...], sin_ref[...]
    for h in range(RET_HEADS):
        dk = slice(h * RET_DK, (h + 1) * RET_DK)
        dv = slice(h * RET_DV, (h + 1) * RET_DV)
        o_intra, q_state, k_state, vb, finish = _ret_common(
            q_ref[:, dk], k_ref[:, dk], v_ref[:, dv], g_ref[:, dv], cosf, sinf, dm_ref[h], qd_ref[h], kd_ref[h],
            nw_ref[h])
        s = s_ref[0, h]
        a_ref[:, dv] = finish(o_intra + _dot(q_state, s.astype(BF16)))
        s_ref[0, h] = s * cd_ref[h] + _dot_tn(k_state.astype(BF16), vb)


def _ret_sample_kernel(q_ref, k_ref, v_ref, g_ref, cos_ref, sin_ref, dm_ref, qd_ref, kd_ref, cd_ref, nw_ref,
                       bm_ref, bmt_ref, s0_ref, a_ref, s_ref):
    o_intra, q_state, k_state, vb, finish = _ret_common(
        q_ref[...], k_ref[...], v_ref[...], g_ref[...], cos_ref[...], sin_ref[...], dm_ref[0], qd_ref[0], kd_ref[0],
        nw_ref[0])
    nb = SAMPLE_PER_TILE
    s0 = s0_ref[:, 0]
    q_bd = jnp.concatenate([q_state] * nb, axis=1) * bm_ref[...]
    a_ref[...] = finish(o_intra + _dot(q_bd, s0.reshape(nb * RET_DK, RET_DV).astype(BF16)))
    k_t = k_state.T.astype(BF16)
    k_bd_t = jnp.concatenate([k_t] * nb, axis=0) * bmt_ref[...]
    ds = _dot(k_bd_t, vb).reshape(nb, RET_DK, RET_DV)
    s_ref[:, 0] = s0 * cd_ref[0] + ds


def _ret_specs(row_block):
    def at(width, off):
        return lambda *ids: (row_block(*ids)[0], off // width + row_block(*ids)[1])

    def head(*shape):
        return pl.BlockSpec((1,) + shape, lambda *ids: (row_block(*ids)[1],) + (0,) * len(shape))

    return [
        pl.BlockSpec((ROWS, RET_DK), at(RET_DK, _ColsB.Q)),
        pl.BlockSpec((ROWS, RET_DK), at(RET_DK, _ColsB.K)),
        pl.BlockSpec((ROWS, RET_DV), at(RET_DV, _ColsB.V)),
        pl.BlockSpec((ROWS, RET_DV), at(RET_DV, _ColsB.GRET)),
    ], head


def _retention_sample_call(proj, row0, bsz, seq, state, ret_norm):
    rb0 = row0 // ROWS
    dm, qd, kd, cd = _ret_tables(seq)
    pos = PAST_LEN + jnp.arange(seq, dtype=jnp.int32)
    cosf, sinf = _rope_tables(jnp.tile(pos, SAMPLE_PER_TILE))
    rb = lambda t, h: (rb0 + t, h)
    specs, head = _ret_specs(rb)
    full = lambda *shape: pl.BlockSpec(shape, lambda t, h: (0,) * len(shape))
    specs += [
        full(ROWS, RET_DK), full(ROWS, RET_DK),
        head(ROWS, ROWS), head(ROWS, LANES), head(ROWS, LANES), head(1, RET_DV), head(1, RET_DV),
        full(ROWS, SAMPLE_PER_TILE * LANES), full(SAMPLE_PER_TILE * LANES, ROWS),
        pl.BlockSpec((SAMPLE_PER_TILE, 1, RET_DK, RET_DV), lambda t, h: (t, h, 0, 0)),
    ]
    bm = _block_mask(BF16)
    operands = (proj, proj, proj, proj, cosf, sinf, dm, qd, kd, cd, ret_norm.reshape(RET_HEADS, 1, RET_DV),
                bm, bm.T, state)
    out_shapes = (jax.ShapeDtypeStruct((bsz * seq, RET_V), BF16), jax.ShapeDtypeStruct(state.shape, F32))
    out_specs = (pl.BlockSpec((ROWS, RET_DV), lambda t, h: (t, h)),
                 pl.BlockSpec((SAMPLE_PER_TILE, 1, RET_DK, RET_DV), lambda t, h: (t, h, 0, 0)))
    return specs, operands, out_shapes, out_specs


def _softplus(x):
    return jnp.maximum(x, 0.0) + jnp.log1p(jnp.exp(-jnp.abs(x)))


def _conv_piece(x, pred, w_ref, b_ref):
    width = x.shape[1]
    x3 = x.reshape(ROWS // SUBLANES, SUBLANES, width)
    t8 = lax.broadcasted_iota(jnp.int32, x3.shape, 1)
    acc = b_ref[...].reshape(1, 1, width)
    for i in range(CONV_WIDTH):
        s = CONV_WIDTH - 1 - i
        tap = x3 if s == 0 else pltpu.roll(jnp.where(t8 >= SUBLANES - s, pred, x3), s, 1)
        acc = acc + tap * w_ref[pl.ds(i, 1), :].reshape(1, 1, width)
    return _silu(acc).reshape(ROWS, width)


def _ssd_decay_terms(c, dt_pre, a_log, ltri):
    dt = _softplus(dt_pre)
    d_a = dt * (-jnp.exp(a_log))
    hi = d_a.astype(BF16)
    r1 = d_a - hi.astype(F32)
    mid = r1.astype(BF16)
    lo = (r1 - mid.astype(F32)).astype(BF16)
    cum = _dot(ltri, hi) + _dot(ltri, mid) + _dot(ltri, lo)
    cum3 = cum.reshape(ROWS // c, c, LANES)
    c_last = jnp.broadcast_to(cum3[:, c - 1:c, :], cum3.shape).reshape(ROWS, LANES)
    to_end = jnp.exp(c_last - cum) * dt
    e_cum = jnp.exp(cum)
    return cum, to_end, e_cum, cum.T, dt.T


def _ssd_group(c, terms, head0, z_ref, dsk_ref, nrm_ref, xs, bm, cm, y_ref, state_io, zero=None):
    cum, to_end, e_cum, cum_t, dt_t = terms
    b_b, c_b = bm.astype(BF16), cm.astype(BF16)
    cb = _dot_nt(c_b, b_b)
    ri = lax.broadcasted_iota(jnp.int32, (ROWS, ROWS), 0)
    ci = lax.broadcasted_iota(jnp.int32, (ROWS, ROWS), 1)
    shift = c.bit_length() - 1
    causal = (ci <= ri) & (jnp.right_shift(ri, shift) == jnp.right_shift(ci, shift))
    lo_half = lax.broadcasted_iota(jnp.int32, (ROWS, LANES), 1) < SSM_HEADDIM

    def col(a, h):
        return jnp.broadcast_to(a[:, h:h + 1], (ROWS, LANES))

    def weights(h):
        seg = col(cum, h) - jnp.broadcast_to(cum_t[h:h + 1, :], (ROWS, ROWS))
        decay = jnp.exp(jnp.where(causal, seg, -jnp.inf))
        return (cb * decay * jnp.broadcast_to(dt_t[h:h + 1, :], (ROWS, ROWS))).astype(BF16)

    pieces = []
    for m in range(HEADS_PER_GROUP // 2):
        ha, hb = head0 + 2 * m, head0 + 2 * m + 1
        xp = xs[:, m * LANES:(m + 1) * LANES]
        w2 = jnp.concatenate([weights(ha), weights(hb)], axis=1)
        x2 = jnp.concatenate([jnp.where(lo_half, xp, 0.0), jnp.where(lo_half, 0.0, xp)], axis=0).astype(BF16)
        y = _dot(w2, x2)
        xw = (xp * jnp.where(lo_half, col(to_end, ha), col(to_end, hb))).astype(BF16)
        y_state = state_io(m, ha, c_b, xw, b_b, e_cum)
        y = y + y_state * jnp.where(lo_half, col(e_cum, ha), col(e_cum, hb))
        pieces.append(y + xp * dsk_ref[:, m * LANES:(m + 1) * LANES])
    if zero is not None:
        pieces[0] = jnp.concatenate([pieces[0][:SUBLANES] + zero, pieces[0][SUBLANES:]], axis=0)
    yg = jnp.concatenate(pieces, axis=1) * _silu(z_ref[...])
    ms = jnp.mean(yg * yg, axis=-1, keepdims=True)
    y_ref[...] = (yg * lax.rsqrt(ms + EPS) * nrm_ref[...]).astype(BF16)


def _pair_decay(e_cum, row, ha):
    top = lax.broadcasted_iota(jnp.int32, (2 * SSM_HEADDIM, SSM_DSTATE), 0) < SSM_HEADDIM
    ea = jnp.broadcast_to(e_cum[row:row + 1, ha:ha + 1], top.shape)
    eb = jnp.broadcast_to(e_cum[row:row + 1, ha + 1:ha + 2], top.shape)
    return jnp.where(top, ea, eb)


def _mixer_prompt_kernel(x0_ref, xn_ref, nw_ref, w_ref, cwx_ref, cbx_ref, cwb_ref, cbb_ref, cwc_ref, cbc_ref,
                         dtb_ref, alog_ref, dsk_ref, nrm_ref, ltri_ref, *rest):
    ret_in, (y_ref, hs_ref, px_ref, pb_ref, pc_ref, a_ref, s_ref, ga_ref, gb_ref, proj_ref) = rest[:7], rest[7:]
    c = pl.program_id(1)

    def normed(x_ref):
        x = x_ref[...]
        ms = jnp.mean(x * x, axis=-1, keepdims=True)
        return (x * lax.rsqrt(ms + EPS) * nw_ref[...]).astype(BF16)

    def project(x_ref, slot):
        proj_ref[slot] = _dot(normed(x_ref), w_ref[:, :_ColsB.TOTAL])

    @pl.when(c == 0)
    def _():
        hs_ref[...] = jnp.zeros_like(hs_ref)
        px_ref[...] = jnp.zeros_like(px_ref)
        pb_ref[...] = jnp.zeros_like(pb_ref)
        pc_ref[...] = jnp.zeros_like(pc_ref)
        s_ref[...] = jnp.zeros_like(s_ref)
        project(x0_ref, 0)

    def conv(x_ref, p_ref, w_ref, bias_ref):
        x = x_ref[...]
        x3 = x.reshape(ROWS // SUBLANES, SUBLANES, x.shape[1])
        pred = jnp.concatenate([p_ref[...], x3[:-1]], axis=0)
        out = _conv_piece(x, pred, w_ref, bias_ref)
        p_ref[0] = x3[-1]
        return out

    def step(cur_slot, next_slot):
        h_next = normed(xn_ref)
        bounds = [0, 10 * 256, 20 * 256, 30 * 256, _ColsB.TOTAL]

        def project_slice(g):
            lo, hi = bounds[g], bounds[g + 1]
            part = _dot(h_next, w_ref[:, lo:hi])
            proj_ref[next_slot, :, lo:hi] = part
            bits = pltpu.bitcast(part[ROWS - SUBLANES:, hi - lo - LANES:], jnp.uint32)
            sixteen = jnp.uint32(16)
            return pltpu.bitcast(lax.shift_right_logical(lax.shift_right_logical(bits, sixteen), sixteen), F32)

        cur = proj_ref.at[cur_slot]
        gn = SSM_GROUPS * SSM_DSTATE
        _ret_prompt_step(cur.at[:, _ColsB.Q:_ColsB.Q + RET_QK], cur.at[:, _ColsB.K:_ColsB.K + RET_QK],
                         cur.at[:, _ColsB.V:_ColsB.V + RET_V], cur.at[:, _ColsB.GRET:_ColsB.GRET + RET_V],
                         *ret_in, a_ref, s_ref)
        z_ref = cur.at[:, _ColsB.Z:_ColsB.Z + SSM_INNER]
        xs_ref = cur.at[:, _ColsB.XS:_ColsB.XS + SSM_INNER]
        b_ref = cur.at[:, _ColsB.B:_ColsB.B + gn]
        c_ref = cur.at[:, _ColsB.C:_ColsB.C + gn]
        dt_ref = cur.at[:, _ColsB.DT:_ColsB.DT + LANES]
        ga_ref[...] = cur[:, _ColsB.GA:_ColsB.GA + D_MODEL]
        gb_ref[...] = cur[:, _ColsB.GB:_ColsB.GB + D_MODEL]
        terms = _ssd_decay_terms(CHUNK, dt_ref[...] + dtb_ref[...], alog_ref[...], ltri_ref[...])
        for g in range(SSM_GROUPS):
            def cols(ref, width, g=g):
                return ref.at[..., g * width:(g + 1) * width]

            gi, n = GROUP_INNER, SSM_DSTATE
            xs = conv(cols(xs_ref, gi), cols(px_ref, gi), cols(cwx_ref, gi), cols(cbx_ref, gi))
            bm = conv(cols(b_ref, n), cols(pb_ref, n), cols(cwb_ref, n), cols(cbb_ref, n))
            cm = conv(cols(c_ref, n), cols(pc_ref, n), cols(cwc_ref, n), cols(cbc_ref, n))
            hs_g = hs_ref.at[0, g * gi:(g + 1) * gi, :]

            def state_io(m, ha, c_b, xw, b_b, e_cum, hs_g=hs_g):
                rows = pl.ds(m * 2 * SSM_HEADDIM, 2 * SSM_HEADDIM)
                h = hs_g[rows, :]
                hs_g[rows, :] = h * _pair_decay(e_cum, ROWS - 1, ha) + _dot_tn(xw, b_b)
                return _dot_nt(c_b, h.astype(BF16))

            _ssd_group(CHUNK, terms, g * HEADS_PER_GROUP, cols(z_ref, gi), cols(dsk_ref, gi), cols(nrm_ref, gi),
                       xs, bm, cm, cols(y_ref, gi), state_io, zero=project_slice(g))

    @pl.when(c % 2 == 0)
    def _():
        step(0, 1)

    @pl.when(c % 2 == 1)
    def _():
        step(1, 0)


def _ssd_sample_kernel(seq, z_ref, xs_ref, b_ref, c_ref, dt_ref, cwx_ref, cbx_ref, cwb_ref, cbb_ref, cwc_ref,
                       cbc_ref, dtb_ref, alog_ref, dsk_ref, nrm_ref, ltri_ref, bmask_ref, px_ref, pb_ref, pc_ref,
                       h0_ref, y_ref, hs_ref, csx_ref, csb_ref, csc_ref):
    nb = SAMPLE_PER_TILE

    def conv(x_ref, p_ref, w_ref, bias_ref, cs_ref):
        x = x_ref[...]
        cs_ref[...] = x.reshape(nb, seq, x.shape[1])
        return _conv_piece(x, p_ref[...], w_ref, bias_ref)

    xs = conv(xs_ref, px_ref, cwx_ref, cbx_ref, csx_ref)
    bm = conv(b_ref, pb_ref, cwb_ref, cbb_ref, csb_ref)
    cm = conv(c_ref, pc_ref, cwc_ref, cbc_ref, csc_ref)
    bmask = bmask_ref[...]

    def state_io(m, ha, c_b, xw, b_b, e_cum):
        rows = pl.ds(m * 2 * SSM_HEADDIM, 2 * SSM_HEADDIM)
        c_bd = jnp.concatenate([c_b] * nb, axis=1) * bmask
        b_bd = jnp.concatenate([b_b] * nb, axis=1) * bmask
        hs = [h0_ref[i, rows, :] for i in range(nb)]
        h_cat = jnp.concatenate(hs, axis=1).astype(BF16)
        dh = _dot_tn(xw, b_bd)
        for i in range(nb):
            decay = _pair_decay(e_cum, i * seq + seq - 1, ha)
            hs_ref[i, rows, :] = hs[i] * decay + dh[:, i * SSM_DSTATE:(i + 1) * SSM_DSTATE]
        return _dot_nt(c_bd, h_cat)

    shift = jnp.bitwise_and(LANES - HEADS_PER_GROUP * pl.program_id(1), LANES - 1)
    dt_pre = pltpu.roll(dt_ref[...] + dtb_ref[...], shift, 1)
    a_log = pltpu.roll(jnp.broadcast_to(alog_ref[...], (SUBLANES, LANES)), shift, 1)[:1]
    terms = _ssd_decay_terms(seq, dt_pre, a_log, ltri_ref[...])
    _ssd_group(seq, terms, 0, z_ref, dsk_ref, nrm_ref, xs, bm, cm, y_ref, state_io)


def _ssd_tables(c):
    r = jnp.arange(ROWS)
    ltri = ((r[:, None] >= r[None, :]) & ((r[:, None] // c) == (r[None, :] // c))).astype(BF16)
    return ltri


def _ssd_specs(row_block):
    def at(width, off):
        return lambda *ids: (row_block(*ids)[0], off // width + row_block(*ids)[1])

    def grp(rows, width):
        return pl.BlockSpec((rows, width), lambda *ids: (0, row_block(*ids)[1]))

    n = SSM_DSTATE
    return [
        pl.BlockSpec((ROWS, GROUP_INNER), at(GROUP_INNER, _ColsB.Z)),
        pl.BlockSpec((ROWS, GROUP_INNER), at(GROUP_INNER, _ColsB.XS)),
        pl.BlockSpec((ROWS, n), at(n, _ColsB.B)),
        pl.BlockSpec((ROWS, n), at(n, _ColsB.C)),
        pl.BlockSpec((ROWS, LANES), lambda *ids: (row_block(*ids)[0], _ColsB.DT // LANES)),
        grp(CONV_WIDTH, GROUP_INNER), grp(1, GROUP_INNER),
        grp(CONV_WIDTH, n), grp(1, n), grp(CONV_WIDTH, n), grp(1, n),
        pl.BlockSpec((1, LANES), lambda *ids: (0, 0)), pl.BlockSpec((1, LANES), lambda *ids: (0, 0)),
        grp(1, GROUP_INNER), grp(1, GROUP_INNER),
    ]


def _ssd_params(conv_w, conv_b, dt_bias, a_log, d_skip, ssm_norm):
    gn = SSM_GROUPS * SSM_DSTATE
    cwx, cwb, cwc = conv_w[:, :SSM_INNER], conv_w[:, SSM_INNER:SSM_INNER + gn], conv_w[:, SSM_INNER + gn:]
    cb = conv_b.reshape(1, CONV_DIM)
    cbx, cbb, cbc = cb[:, :SSM_INNER], cb[:, SSM_INNER:SSM_INNER + gn], cb[:, SSM_INNER + gn:]

    def head_lanes(v):
        return jnp.pad(v, (0, LANES - SSM_HEADS)).reshape(1, LANES)

    dsk = jnp.repeat(d_skip, SSM_HEADDIM).reshape(1, SSM_INNER)
    return (cwx, cbx, cwb, cbb, cwc, cbc, head_lanes(dt_bias), head_lanes(a_log), dsk,
            ssm_norm.reshape(1, SSM_INNER))


def _conv_tail(csx, csb, csc):
    keep = SUBLANES - (CONV_WIDTH - 1)
    return jnp.concatenate([csx[:, keep:], csb[:, keep:], csc[:, keep:]], axis=-1)


def _mixer_prompt(x, norm_w, w_b, bsz, seq, params, ret_norm):
    nch = seq // ROWS
    gn = SSM_GROUPS * SSM_DSTATE
    full = lambda a: pl.BlockSpec(a.shape, lambda b, c: (0,) * a.ndim)
    ltri = _ssd_tables(CHUNK)
    dm, qd, kd, cd = _ret_tables(CHUNK)
    cosf, sinf = _rope_tables(jnp.arange(seq, dtype=jnp.int32))
    rnw = ret_norm.reshape(RET_HEADS, 1, RET_DV)
    specs = [pl.BlockSpec((ROWS, D_MODEL), lambda b, c: (b * nch, 0)),
             pl.BlockSpec((ROWS, D_MODEL), lambda b, c: (b * nch + jnp.minimum(c + 1, nch - 1), 0)),
             full(norm_w), full(w_b), *[full(p) for p in params], full(ltri),
             pl.BlockSpec((ROWS, RET_DK), lambda b, c: (c, 0)), pl.BlockSpec((ROWS, RET_DK), lambda b, c: (c, 0)),
             full(dm), full(qd), full(kd), full(cd), full(rnw)]
    tail = lambda width: pl.BlockSpec((1, SUBLANES, width), lambda b, c: (b, 0, 0))
    y, hs, csx, csb, csc, a, s, ga, gb = pl.pallas_call(
        _mixer_prompt_kernel,
        out_shape=(jax.ShapeDtypeStruct((bsz * seq, SSM_INNER), BF16),
                   jax.ShapeDtypeStruct((bsz, SSM_HEADS * SSM_HEADDIM, SSM_DSTATE), F32),
                   jax.ShapeDtypeStruct((bsz, SUBLANES, SSM_INNER), F32),
                   jax.ShapeDtypeStruct((bsz, SUBLANES, gn), F32),
                   jax.ShapeDtypeStruct((bsz, SUBLANES, gn), F32),
                   jax.ShapeDtypeStruct((bsz * seq, RET_V), BF16),
                   jax.ShapeDtypeStruct((bsz, RET_HEADS, RET_DK, RET_DV), F32),
                   jax.ShapeDtypeStruct((bsz * seq, D_MODEL), F32),
                   jax.ShapeDtypeStruct((bsz * seq, D_MODEL), F32)),
        grid=(bsz, nch),
        in_specs=specs,
        out_specs=(pl.BlockSpec((ROWS, SSM_INNER), lambda b, c: (b * nch + c, 0)),
                   pl.BlockSpec((1, SSM_HEADS * SSM_HEADDIM, SSM_DSTATE), lambda b, c: (b, 0, 0)),
                   tail(SSM_INNER), tail(gn), tail(gn),
                   pl.BlockSpec((ROWS, RET_V), lambda b, c: (b * nch + c, 0)),
                   pl.BlockSpec((1, RET_HEADS, RET_DK, RET_DV), lambda b, c: (b, 0, 0, 0)),
                   pl.BlockSpec((ROWS, D_MODEL), lambda b, c: (b * nch + c, 0)),
                   pl.BlockSpec((ROWS, D_MODEL), lambda b, c: (b * nch + c, 0))),
        scratch_shapes=[pltpu.VMEM((2, ROWS, _ColsB.TOTAL), F32)],
        compiler_params=_params(("parallel", "arbitrary")),
        name="mixer_prompt",
    )(x, x, norm_w, w_b, *params, ltri, cosf, sinf, dm, qd, kd, cd, rnw)
    return y, hs, _conv_tail(csx, csb, csc), a, s, ga, gb


def _mixer_sample_kernel(seq, n_ssd_in, n_ret_in, *refs):
    ssd_in, refs = refs[:n_ssd_in], refs[n_ssd_in:]
    ret_in, refs = refs[:n_ret_in], refs[n_ret_in:]
    _ssd_sample_kernel(seq, *ssd_in, *refs[:5])
    _ret_sample_kernel(*ret_in, *refs[5:])


def _mixer_sample(proj, bsz, seq, params, state, conv_state, ret_state, ret_norm):
    assert SSM_GROUPS == RET_HEADS
    ntile = bsz // SAMPLE_PER_TILE
    gn = SSM_GROUPS * SSM_DSTATE
    specs = _ssd_specs(lambda t, g: (t, g))
    gh = HEADS_PER_GROUP * SSM_HEADDIM
    cs = jnp.pad(conv_state, ((0, 0), (SUBLANES - (CONV_WIDTH - 1), 0), (0, 0)))
    csx, csb, csc = cs[:, :, :SSM_INNER], cs[:, :, SSM_INNER:SSM_INNER + gn], cs[:, :, SSM_INNER + gn:]
    pred = lambda width: pl.BlockSpec((SAMPLE_PER_TILE, SUBLANES, width), lambda t, g: (t, 0, g))
    state_spec = pl.BlockSpec((SAMPLE_PER_TILE, gh, SSM_DSTATE), lambda t, g: (t, g, 0))
    specs += [
        pl.BlockSpec((ROWS, ROWS), lambda t, g: (0, 0)),
        pl.BlockSpec((ROWS, SAMPLE_PER_TILE * LANES), lambda t, g: (0, 0)),
        pred(GROUP_INNER), pred(SSM_DSTATE), pred(SSM_DSTATE),
        state_spec,
    ]
    state2 = state.reshape(bsz, SSM_HEADS * SSM_HEADDIM, SSM_DSTATE)
    assert seq == SUBLANES
    ssd_operands = (proj, proj, proj, proj, proj, *params, _ssd_tables(seq), _block_mask(BF16), csx, csb, csc,
                    state2)
    r_specs, r_operands, r_shapes, r_out_specs = _retention_sample_call(proj, 0, bsz, seq, ret_state, ret_norm)
    y, hs, csx, csb, csc, a, s = pl.pallas_call(
        functools.partial(_mixer_sample_kernel, seq, len(ssd_operands), len(r_operands)),
        out_shape=(jax.ShapeDtypeStruct((bsz * seq, SSM_INNER), BF16), jax.ShapeDtypeStruct(state2.shape, F32),
                   jax.ShapeDtypeStruct((bsz, seq, SSM_INNER), F32),
                   jax.ShapeDtypeStruct((bsz, seq, gn), F32),
                   jax.ShapeDtypeStruct((bsz, seq, gn), F32), *r_shapes),
        grid=(ntile, SSM_GROUPS),
        in_specs=specs + r_specs,
        out_specs=(pl.BlockSpec((ROWS, GROUP_INNER), lambda t, g: (t, g)), state_spec,
                   pred(GROUP_INNER), pred(SSM_DSTATE), pred(SSM_DSTATE), *r_out_specs),
        compiler_params=_params(("parallel", "parallel")),
        name="mixer_sample",
    )(*ssd_operands, *r_operands)
    return y, hs, _conv_tail(csx, csb, csc), a, s


def _post_kernel(n_prompt_tiles, ap_ref, as_ref, yp_ref, ys_ref, gap_ref, gas_ref, gbp_ref, gbs_ref, xp_ref, xs_ref,
                 wr_ref, ws_ref, wo_ref, nw_ref, wrt_ref, brt_ref, x1_ref, h2_ref, lg_ref):
    is_prompt = pl.program_id(0) < n_prompt_tiles
    half = x1_ref.shape[0] // 2
    for r in (pl.ds(0, half), pl.ds(half, half)):
        branch_a = _dot(jnp.where(is_prompt, ap_ref[r, :], as_ref[r, :]), wr_ref[...])
        branch_b = _dot(jnp.where(is_prompt, yp_ref[r, :], ys_ref[r, :]), ws_ref[...])
        g_a = jnp.where(is_prompt, gap_ref[r, :], gas_ref[r, :])
        g_b = jnp.where(is_prompt, gbp_ref[r, :], gbs_ref[r, :])
        merged = jax.nn.sigmoid(g_a) * branch_a + jax.nn.sigmoid(g_b) * branch_b
        x1 = jnp.where(is_prompt, xp_ref[r, :], xs_ref[r, :]) + _dot(merged.astype(BF16), wo_ref[...])
        x1_ref[r, :] = x1
        ms = jnp.mean(x1 * x1, axis=-1, keepdims=True)
        h2 = x1 * lax.rsqrt(ms + EPS) * nw_ref[...]
        _store_row_tiles(h2_ref.at[pl.ds(r.start * ROW_TILES, half * ROW_TILES), :], h2)
        lg_ref[r, :] = _dot(h2.astype(BF16), wrt_ref[...]) + brt_ref[...]


def _post(a_p, a_s, y_p, y_s, ga_p, gb_p, proj_s, xp, xs, w_ret, w_ssm, w_out, norm_ffn, w_router, b_router):
    t = xp.shape[0] + xs.shape[0]
    tm = _pick(math.gcd(xp.shape[0], xs.shape[0]), 512)
    npt = xp.shape[0] // tm
    row = lambda width, blk: pl.BlockSpec((tm, width), lambda i: (i, blk))
    full = lambda *shape: pl.BlockSpec(shape, lambda i: (0,) * len(shape))
    return pl.pallas_call(
        functools.partial(_post_kernel, npt),
        out_shape=(jax.ShapeDtypeStruct((t, D_MODEL), F32), jax.ShapeDtypeStruct((t * ROW_TILES, LANES), F32),
                   jax.ShapeDtypeStruct((t, LANES), F32)),
        grid=(t // tm,),
        in_specs=[*_split_rows(tm, npt, RET_V), *_split_rows(tm, npt, SSM_INNER),
                  *_split_rows(tm, npt, D_MODEL, _ColsB.GA // D_MODEL),
                  *_split_rows(tm, npt, D_MODEL, _ColsB.GB // D_MODEL),
                  *_split_rows(tm, npt, D_MODEL),
                  full(RET_V, D_MODEL), full(SSM_INNER, D_MODEL), full(D_MODEL, D_MODEL), full(1, D_MODEL),
                  full(D_MODEL, LANES), full(1, LANES)],
        out_specs=(row(D_MODEL, 0), pl.BlockSpec((tm * ROW_TILES, LANES), lambda i: (i, 0)), row(LANES, 0)),
        compiler_params=_params(("parallel",)),
        name="post_mixer",
    )(a_p, a_s, y_p, y_s, ga_p, proj_s, gb_p, proj_s, xp, xs, w_ret, w_ssm, w_out, norm_ffn, w_router, b_router)


def _route_kernel(lg_ref, lstrict_ref, srow_ref, gate_ref, tmeta_ref, cnt_ref):
    @pl.when(pl.program_id(0) == 0)
    def _():
        cnt_ref[...] = jnp.zeros_like(cnt_ref)

    tm = lg_ref.shape[0]
    lane = lax.broadcasted_iota(jnp.int32, (tm, LANES), 1)
    lane_f = lane.astype(F32)
    cur = jnp.where(lane < N_EXPERTS, lg_ref[...], -jnp.inf)
    vals, hots = [], []
    for _ in range(TOP_K):
        m = jnp.max(cur, axis=1, keepdims=True)
        idx = jnp.min(jnp.where(cur == m, lane_f, float(LANES)), axis=1, keepdims=True)
        hot = lane_f == idx
        vals.append(m)
        hots.append(hot)
        cur = jnp.where(hot, -jnp.inf, cur)
    exps = [jnp.exp(v - vals[0]) for v in vals]
    denom = exps[0] + exps[1] + exps[2] + exps[3]
    sel = hots[0] | hots[1] | hots[2] | hots[3]
    self32 = sel.astype(F32)
    base = cnt_ref[...]
    num = jnp.sum(self32, axis=0, keepdims=True)
    rank = _dot(lstrict_ref[...], self32.astype(BF16))
    cnt_ref[...] = base + num
    lane8 = lax.broadcasted_iota(jnp.int32, (SUBLANES, LANES), 1)
    incl = jnp.broadcast_to(num, (SUBLANES, LANES))
    shift = 1
    while shift < N_EXPERTS:
        incl = incl + jnp.where(lane8 >= shift, pltpu.roll(incl, shift, 1), 0.0)
        shift *= 2
    off = incl[:1] - num
    slot = rank + off
    meta = jnp.zeros((tm, LANES), F32)
    for k in range(TOP_K):
        s_k = jnp.sum(jnp.where(hots[k], slot, 0.0), axis=1, keepdims=True)
        meta = jnp.where(lane == k, s_k, meta)
        meta = jnp.where(lane == SUBLANES + k, exps[k] / denom, meta)
    meta_t = meta.T
    srow_ref[0] = (meta_t[:SUBLANES] * float(ROW_TILES)).astype(jnp.int32)
    gate_ref[0] = meta_t[SUBLANES:2 * SUBLANES]
    row8 = lax.broadcasted_iota(jnp.int32, (SUBLANES, LANES), 0)
    tmeta_ref[0] = jnp.where(row8 == 0, base, jnp.where(row8 == 1, num, jnp.where(row8 == 2, off, 0.0)))


def _route(logits, tm):
    t = logits.shape[0]
    r = jnp.arange(tm)
    lstrict = (r[:, None] > r[None, :]).astype(BF16)
    return pl.pallas_call(
        _route_kernel,
        out_shape=(jax.ShapeDtypeStruct((t // tm, SUBLANES, tm), jnp.int32),
                   jax.ShapeDtypeStruct((t // tm, SUBLANES, tm), F32),
                   jax.ShapeDtypeStruct((t // tm, SUBLANES, LANES), F32),
                   jax.ShapeDtypeStruct((1, LANES), F32)),
        grid=(t // tm,),
        in_specs=[pl.BlockSpec((tm, LANES), lambda i: (i, 0)), pl.BlockSpec((tm, tm), lambda i: (0, 0))],
        out_specs=(pl.BlockSpec((1, SUBLANES, tm), lambda i: (i, 0, 0)),
                   pl.BlockSpec((1, SUBLANES, tm), lambda i: (i, 0, 0)),
                   pl.BlockSpec((1, SUBLANES, LANES), lambda i: (i, 0, 0)),
                   pl.BlockSpec((1, LANES), lambda i: (0, 0))),
        compiler_params=_params(("arbitrary",)),
        name="route",
    )(logits, lstrict)


def _for_each_strip(tmeta_ref, tm, make_copy, action):
    del tm

    def per_expert(e, carry):
        n, hbm0, tile0 = tmeta_ref[0, 1, e], tmeta_ref[0, 0, e], tmeta_ref[0, 2, e]
        n_chunks = lax.shift_right_logical(n, STRIP_BITS)

        def chunk(c, inner):
            action(make_copy(hbm0 + c * STRIP, tile0 + c * STRIP, STRIP))
            return inner

        lax.fori_loop(0, n_chunks, chunk, 0)
        for bit in reversed(range(STRIP_BITS)):
            size = 1 << bit

            @pl.when(jnp.bitwise_and(n, size) != 0)
            def _():
                done = jnp.bitwise_and(n, ~(2 * size - 1))
                action(make_copy(hbm0 + done, tile0 + done, size))
        return carry

    lax.fori_loop(0, N_EXPERTS, per_expert, 0)


def _rows(ref, row, n):
    return ref.at[pl.ds(pl.multiple_of(row * ROW_TILES, ROW_TILES), n * ROW_TILES), :]


def _dispatch_kernel(pstart_ref, pend_ref, srow_ref, tmeta_ref, h2_ref, xs_ref, stage_ref, zero_ref, sem):
    tm = h2_ref.shape[0] // ROW_TILES
    i = pl.program_id(0)
    slot = i % 2

    used = pend_ref[N_EXPERTS - 1] // EXPERT_BLOCK
    total = xs_ref.shape[0] // (EXPERT_BLOCK * ROW_TILES)

    def clear(action):
        def block(row):
            return pltpu.make_async_copy(zero_ref, _rows(xs_ref, row, EXPERT_BLOCK), sem.at[2])

        def last_of_segment(e, carry):
            @pl.when(pend_ref[e] > pstart_ref[e])
            def _():
                action(block(pend_ref[e] - EXPERT_BLOCK))
            return carry

        lax.fori_loop(0, N_EXPERTS, last_of_segment, 0)
        lax.fori_loop(used, total, lambda j, c: (action(block(j * EXPERT_BLOCK)), c)[1], 0)

    @pl.when(i == 0)
    def _():
        zero_ref[...] = jnp.zeros_like(zero_ref)
        clear(lambda copy: copy.start())

    def place(t, carry):
        row = h2_ref[pl.ds(pl.multiple_of(t * ROW_TILES, ROW_TILES), ROW_TILES), :]
        for k in range(TOP_K):
            stage_ref[slot, pl.ds(pl.multiple_of(srow_ref[k * tm + t], ROW_TILES), ROW_TILES), :] = row
        return carry

    lax.fori_loop(0, tm, place, 0, unroll=ISSUE_UNROLL)

    @pl.when(i == 0)
    def _():
        clear(lambda copy: copy.wait())

    def strips(meta_ref, s):
        return functools.partial(
            _for_each_strip, meta_ref, tm,
            lambda hbm_row, tile_row, n: pltpu.make_async_copy(_rows(stage_ref.at[s], tile_row, n),
                                                               _rows(xs_ref, hbm_row, n), sem.at[s]))

    strips(tmeta_ref, slot)(lambda copy: copy.start())

    def wait_all(s):
        pltpu.make_async_copy(stage_ref.at[s], _rows(xs_ref, 0, TOP_K * tm), sem.at[s]).wait()

    @pl.when(i > 0)
    def _():
        wait_all(1 - slot)

    @pl.when(i == pl.num_programs(0) - 1)
    def _():
        wait_all(slot)


def _dispatch(pstart, pends, srow, tmeta, h2, n_rows, tm):
    t = h2.shape[0] // ROW_TILES
    smem = lambda shape, imap: pl.BlockSpec(shape, imap, memory_space=pltpu.SMEM)
    return pl.pallas_call(
        _dispatch_kernel,
        out_shape=jax.ShapeDtypeStruct((n_rows * ROW_TILES, LANES), F32),
        grid_spec=pltpu.PrefetchScalarGridSpec(
            num_scalar_prefetch=2,
            grid=(t // tm,),
            in_specs=[smem((SUBLANES * tm,), lambda i, ps, pe: (i,)),
                      smem((1, SUBLANES, LANES), lambda i, ps, pe: (i, 0, 0)),
                      pl.BlockSpec((tm * ROW_TILES, LANES), lambda i, ps, pe: (i, 0))],
            out_specs=pl.BlockSpec(memory_space=pl.ANY),
            scratch_shapes=[pltpu.VMEM((2, TOP_K * tm * ROW_TILES, LANES), F32),
                            pltpu.VMEM((EXPERT_BLOCK * ROW_TILES, LANES), F32),
                            pltpu.SemaphoreType.DMA((3,))],
        ),
        compiler_params=_params(("arbitrary",)),
        name="dispatch",
    )(pstart, pends, srow, tmeta, h2)


def _combine_kernel(n_prompt_tiles, srow_ref, gate_ref, tmeta_ref, tmeta_next_ref, x1_ref, nw_ref, yb_ref,
                    op_ref, os_ref, buf_ref, acc_ref, sem):
    tm = x1_ref.shape[0]
    i = pl.program_id(0)
    slot = i % 2

    def strips(meta_ref, s):
        return functools.partial(
            _for_each_strip, meta_ref, tm,
            lambda hbm_row, tile_row, n: pltpu.make_async_copy(_rows(yb_ref, hbm_row, n),
                                                               _rows(buf_ref.at[s], tile_row, n), sem.at[s]))

    @pl.when(i == 0)
    def _():
        strips(tmeta_ref, 0)(lambda copy: copy.start())

    @pl.when(i + 1 < pl.num_programs(0))
    def _():
        strips(tmeta_next_ref, 1 - slot)(lambda copy: copy.start())

    pltpu.make_async_copy(_rows(yb_ref, 0, TOP_K * tm), buf_ref.at[slot], sem.at[slot]).wait()

    def gather(t, carry):
        def term(k):
            row = buf_ref[slot, pl.ds(pl.multiple_of(srow_ref[k * tm + t], ROW_TILES), ROW_TILES), :]
            return row * gate_ref[k * tm + t]

        moe = term(0)
        for k in range(1, TOP_K):
            moe = moe + term(k)
        acc_ref[pl.ds(pl.multiple_of(t * ROW_TILES, ROW_TILES), ROW_TILES), :] = moe
        return carry

    lax.fori_loop(0, tm, gather, 0, unroll=ISSUE_UNROLL)
    x2 = x1_ref[...] + _load_row_tiles(acc_ref, tm)
    ms = jnp.mean(x2 * x2, axis=-1, keepdims=True)
    out = x2 * lax.rsqrt(ms + EPS) * nw_ref[...]

    @pl.when(i < n_prompt_tiles)
    def _():
        op_ref[...] = out

    @pl.when(i >= n_prompt_tiles)
    def _():
        os_ref[...] = out


def _combine(srow, gates, tmeta, x1, norm_final, yb, n_prompt, tm):
    t = x1.shape[0]
    nt = t // tm
    npt = n_prompt // tm
    smem = lambda shape, imap: pl.BlockSpec(shape, imap, memory_space=pltpu.SMEM)
    return pl.pallas_call(
        functools.partial(_combine_kernel, npt),
        out_shape=(jax.ShapeDtypeStruct((n_prompt, D_MODEL), F32),
                   jax.ShapeDtypeStruct((t - n_prompt, D_MODEL), F32)),
        grid=(nt,),
        in_specs=[smem((SUBLANES * tm,), lambda i: (i,)),
                  smem((SUBLANES * tm,), lambda i: (i,)),
                  smem((1, SUBLANES, LANES), lambda i: (i, 0, 0)),
                  smem((1, SUBLANES, LANES), lambda i: (jnp.minimum(i + 1, nt - 1), 0, 0)),
                  pl.BlockSpec((tm, D_MODEL), lambda i: (i, 0)),
                  pl.BlockSpec((1, D_MODEL), lambda i: (0, 0)),
                  pl.BlockSpec(memory_space=pl.ANY)],
        out_specs=_split_rows(tm, npt, D_MODEL),
        scratch_shapes=[pltpu.VMEM((2, TOP_K * tm * ROW_TILES, LANES), F32),
                        pltpu.VMEM((tm * ROW_TILES, LANES), F32),
                        pltpu.SemaphoreType.DMA((2,))],
        compiler_params=_params(("arbitrary",)),
        name="combine",
    )(srow, gates, tmeta, tmeta, x1, norm_final, yb)


def _expert_kernel(be_ref, nv_ref, next_ref, xs_ref, bgu_ref, bd_ref, wgu_hbm, wd_hbm, yb_ref,
                   wgu_f, wd_f, wgu_b, wd_b, sem):
    i = pl.program_id(0)

    def fetch(e):
        return (pltpu.make_async_copy(wgu_hbm.at[e], wgu_f, sem.at[0]),
                pltpu.make_async_copy(wd_hbm.at[e], wd_f, sem.at[1]))

    @pl.when(i == 0)
    def _():
        for copy in fetch(be_ref[0]):
            copy.start()

    first = jnp.logical_or(i == 0, be_ref[i] != be_ref[jnp.maximum(i - 1, 0)])

    @pl.when(jnp.logical_and(first, i < nv_ref[0]))
    def _():
        for copy in fetch(be_ref[i]):
            copy.wait()
        wgu_b[...] = wgu_f[...].astype(BF16)
        wd_b[...] = wd_f[...].astype(BF16)

        @pl.when(next_ref[i] >= 0)
        def _():
            for copy in fetch(next_ref[i]):
                copy.start()

    @pl.when(i < nv_ref[0])
    def _():
        gu = _dot(_load_row_tiles(xs_ref, EXPERT_BLOCK).astype(BF16), wgu_b[...]) + bgu_ref[0]
        glu = jnp.minimum(gu[:, :D_FF], SWIGLU_LIMIT)
        lin = jnp.clip(gu[:, D_FF:], -SWIGLU_LIMIT, SWIGLU_LIMIT)
        act = glu * jax.nn.sigmoid(SWIGLU_ALPHA * glu) * (lin + 1.0)
        _store_row_tiles(yb_ref, _dot(act.astype(BF16), wd_b[...]) + bd_ref[0])

    @pl.when(i >= nv_ref[0])
    def _():
        yb_ref[...] = jnp.zeros_like(yb_ref)


def _experts(block_expert, n_valid, next_expert, xs, w_gu, b_gu, w_d, b_d):
    rows = EXPERT_BLOCK * ROW_TILES
    nb = xs.shape[0] // rows
    return pl.pallas_call(
        _expert_kernel,
        out_shape=jax.ShapeDtypeStruct(xs.shape, F32),
        grid_spec=pltpu.PrefetchScalarGridSpec(
            num_scalar_prefetch=3,
            grid=(nb,),
            in_specs=[pl.BlockSpec((rows, LANES), lambda i, be, nv, nx: (jnp.minimum(i, nv[0] - 1), 0)),
                      pl.BlockSpec((1, 1, 2 * D_FF), lambda i, be, nv, nx: (be[i], 0, 0)),
                      pl.BlockSpec((1, 1, D_MODEL), lambda i, be, nv, nx: (be[i], 0, 0)),
                      pl.BlockSpec(memory_space=pl.ANY),
                      pl.BlockSpec(memory_space=pl.ANY)],
            out_specs=pl.BlockSpec((rows, LANES), lambda i, be, nv, nx: (i, 0)),
            scratch_shapes=[pltpu.VMEM((D_MODEL, 2 * D_FF), F32), pltpu.VMEM((D_FF, D_MODEL), F32),
                            pltpu.VMEM((D_MODEL, 2 * D_FF), BF16), pltpu.VMEM((D_FF, D_MODEL), BF16),
                            pltpu.SemaphoreType.DMA((2,))],
        ),
        compiler_params=_params(("arbitrary",)),
        name="experts",
    )(block_expert, n_valid, next_expert, xs, b_gu, b_d, w_gu, w_d)


def _expert_layout(counts, n_blocks):
    counts = counts.astype(jnp.int32)
    padded = (counts + EXPERT_BLOCK - 1) // EXPERT_BLOCK * EXPERT_BLOCK
    pends = jnp.cumsum(padded)
    pstart = pends - padded
    block_start = jnp.arange(n_blocks, dtype=jnp.int32) * EXPERT_BLOCK
    block_expert = jnp.minimum(jnp.sum(pends[None, :] <= block_start[:, None], axis=1), N_EXPERTS - 1)
    n_valid = (pends[-1:] // EXPERT_BLOCK).astype(jnp.int32)
    seg_end = jnp.sum(jnp.where(block_expert[:, None] == jnp.arange(N_EXPERTS), pends[None, :], 0), axis=1)
    after = seg_end // EXPERT_BLOCK
    next_expert = jnp.where(after < n_valid[0], block_expert[jnp.minimum(after, n_blocks - 1)], -1)
    return (pstart.astype(jnp.int32), pends.astype(jnp.int32), block_expert.astype(jnp.int32), n_valid,
            next_expert.astype(jnp.int32))


def _permute_w_in(w_in):
    sizes = [RET_QK, RET_QK, RET_V, RET_V, SSM_INNER, CONV_DIM, SSM_HEADS, D_MODEL, D_MODEL]
    offs = [0]
    for s in sizes:
        offs.append(offs[-1] + s)
    q, k, v, g_ret, z, xbc, dt, g_a, g_b = [w_in[:, offs[i]:offs[i + 1]] for i in range(len(sizes))]
    gn = SSM_GROUPS * SSM_DSTATE
    xs, bm, cm = xbc[:, :SSM_INNER], xbc[:, SSM_INNER:SSM_INNER + gn], xbc[:, SSM_INNER + gn:]
    dt_pad = jnp.pad(dt, ((0, 0), (0, _ColsB.V - _ColsB.DT - SSM_HEADS)))
    tail = jnp.zeros((D_MODEL, _ColsB.PADDED - _ColsB.TOTAL), w_in.dtype)
    return jnp.concatenate([g_a, g_b, z, xs, bm, cm, dt_pad, v, g_ret, q, k, tail], axis=1).astype(BF16)


def _forward(x_prompt, x_sample, state_ret, state_ssm, state_conv, norm_mix, w_in, ret_norm, w_out_ret,
             conv_w, conv_b, dt_bias, a_log, d_skip, ssm_norm, w_out_ssm, w_out, norm_ffn,
             w_router, b_router, w_gate_up, b_gate_up, w_down, b_down, norm_final):
    bp, lp, _ = x_prompt.shape
    bs, ls, _ = x_sample.shape
    assert lp % ROWS == 0 and bs % SAMPLE_PER_TILE == 0 and ls * SAMPLE_PER_TILE == ROWS
    tp, ts = bp * lp, bs * ls
    t_all = tp + ts
    xp, xs_in = x_prompt.reshape(tp, D_MODEL), x_sample.reshape(ts, D_MODEL)

    w_b = _permute_w_in(w_in[0])
    norm_w = norm_mix[0].reshape(1, D_MODEL)
    proj_s = _inproj(xs_in, norm_w, w_b)

    sp = _ssd_params(conv_w[0], conv_b[0], dt_bias[0], a_log[0], d_skip[0], ssm_norm[0])
    y_p, ssm_p, conv_p, a_p, ret_p, ga_p, gb_p = _mixer_prompt(xp, norm_w, w_b, bp, lp, sp, ret_norm[0])
    y_s, ssm_s, conv_s, a_s, ret_s = _mixer_sample(proj_s, bs, ls, sp, state_ssm[0], state_conv[0],
                                                   state_ret[0], ret_norm[0])

    w_router_pad = jnp.pad(w_router[0], ((0, 0), (0, LANES - N_EXPERTS))).astype(BF16)
    b_router_pad = jnp.pad(b_router[0], (0, LANES - N_EXPERTS)).reshape(1, LANES)
    x1, h2, logits = _post(a_p, a_s, y_p, y_s, ga_p, gb_p, proj_s, xp, xs_in, w_out_ret[0].astype(BF16), w_out_ssm[0].astype(BF16),
                           w_out[0].astype(BF16), norm_ffn[0].reshape(1, D_MODEL), w_router_pad, b_router_pad)

    tm_moe = _pick(math.gcd(tp, ts), 512)
    srow, gates, tmeta, counts = _route(logits, tm_moe)
    n_blocks = -(-(t_all * TOP_K + N_EXPERTS * (EXPERT_BLOCK - 1)) // EXPERT_BLOCK)
    pstart, pends, block_expert, n_valid, next_expert = _expert_layout(counts[0, :N_EXPERTS], n_blocks)
    tmeta = tmeta.astype(jnp.int32)
    run_start = tmeta[:, :1, :] + jnp.pad(pstart, (0, LANES - N_EXPERTS))
    tmeta = jnp.concatenate([run_start, tmeta[:, 1:, :]], axis=1)
    srow, gates = srow.reshape(-1), gates.reshape(-1)
    xs = _dispatch(pstart, pends, srow, tmeta, h2, n_blocks * EXPERT_BLOCK, tm_moe)
    yb = _experts(block_expert, n_valid, next_expert, xs, w_gate_up[0],
                  b_gate_up[0].reshape(N_EXPERTS, 1, 2 * D_FF), w_down[0], b_down[0].reshape(N_EXPERTS, 1, D_MODEL))
    out_p, out_s = _combine(srow, gates, tmeta, x1, norm_final.reshape(1, D_MODEL), yb, tp, tm_moe)

    shape_s = (1, bs, SSM_HEADS, SSM_HEADDIM, SSM_DSTATE)
    shape_p = (1, bp, SSM_HEADS, SSM_HEADDIM, SSM_DSTATE)
    return (out_p.reshape(bp, lp, D_MODEL), out_s.reshape(bs, ls, D_MODEL),
            ret_p[None], ret_s[None], ssm_p.reshape(shape_p), ssm_s.reshape(shape_s),
            conv_p[None], conv_s[None])


def kernel(x_prompt, x_sample, state_ret, state_ssm, state_conv, norm_mix, w_in, ret_norm, w_out_ret, conv_w, conv_b, dt_bias, a_log, d_skip, ssm_norm, w_out_ssm, w_out, norm_ffn, w_router, b_router, w_gate_up, b_gate_up, w_down, b_down, norm_final):
    return _forward(x_prompt, x_sample, state_ret, state_ssm, state_conv, norm_mix, w_in, ret_norm, w_out_ret,
                    conv_w, conv_b, dt_bias, a_log, d_skip, ssm_norm, w_out_ssm, w_out, norm_ffn,
                    w_router, b_router, w_gate_up, b_gate_up, w_down, b_down, norm_final)
```

```python
import functools
import math

import jax
import jax.numpy as jnp
from jax import lax
from jax.experimental import pallas as pl
from jax.experimental.pallas import tpu as pltpu

F32 = jnp.float32
BF16 = jnp.bfloat16

D_MODEL = 1024
PAST_LEN = 16384
RET_HEADS = 4
RET_DK = 128
RET_DV = 256
RET_QK = RET_HEADS * RET_DK
RET_V = RET_HEADS * RET_DV
ROPE_BASE = 10000.0
SSM_INNER = 2 * D_MODEL
SSM_HEADDIM = 64
SSM_HEADS = SSM_INNER // SSM_HEADDIM
SSM_GROUPS = 4
SSM_DSTATE = 128
HEADS_PER_GROUP = SSM_HEADS // SSM_GROUPS
GROUP_INNER = SSM_INNER // SSM_GROUPS
CONV_WIDTH = 4
CONV_DIM = SSM_INNER + 2 * SSM_GROUPS * SSM_DSTATE
CHUNK = 128
N_EXPERTS = 32
TOP_K = 4
D_FF = D_MODEL
SWIGLU_LIMIT = 7.0
SWIGLU_ALPHA = 1.702
EPS = 1e-6

LANES = 128
SUBLANES = 8
ROWS = 128
SAMPLE_PER_TILE = 16
EXPERT_BLOCK = 512
ROW_TILES = D_MODEL // LANES
ISSUE_UNROLL = 8
STRIP_BITS = 5
STRIP = 1 << STRIP_BITS
VMEM_LIMIT = 56 * 1024 * 1024


class _ColsB:
    GA = 0
    GB = GA + D_MODEL
    Z = GB + D_MODEL
    XS = Z + SSM_INNER
    B = XS + SSM_INNER
    C = B + SSM_GROUPS * SSM_DSTATE
    DT = C + SSM_GROUPS * SSM_DSTATE
    V = DT + 2 * LANES
    GRET = V + RET_V
    Q = GRET + RET_V
    K = Q + RET_QK
    TOTAL = K + RET_QK
    PADDED = -(-TOTAL // (6 * LANES)) * 6 * LANES


def _pick(n, target, step=LANES):
    best = None
    for c in range(step, target + 1, step):
        if n % c == 0:
            best = c
    assert best is not None, (n, target)
    return best


def _params(sem, **kw):
    return pltpu.CompilerParams(dimension_semantics=sem, vmem_limit_bytes=VMEM_LIMIT, **kw)


def _dot(a, b):
    return jnp.dot(a, b, preferred_element_type=F32)


def _dot_nt(a, b):
    return lax.dot_general(a, b, (((1,), (1,)), ((), ())), preferred_element_type=F32)


def _dot_tn(a, b):
    return lax.dot_general(a, b, (((0,), (0,)), ((), ())), preferred_element_type=F32)


def _silu(x):
    return x * jax.nn.sigmoid(x)


def _store_row_tiles(ref, value, *lead):
    n = value.shape[0]
    for s in range(ROW_TILES):
        ref[(*lead, pl.ds(s, n, stride=ROW_TILES), slice(None))] = value[:, s * LANES:(s + 1) * LANES]


def _load_row_tiles(ref, n, *lead):
    return jnp.concatenate([ref[(*lead, pl.ds(s, n, stride=ROW_TILES), slice(None))] for s in range(ROW_TILES)],
                           axis=1)


def _inproj_kernel(x_ref, nw_ref, w_ref, o_ref, h_ref):
    @pl.when(pl.program_id(1) == 0)
    def _():
        x = x_ref[...]
        ms = jnp.mean(x * x, axis=-1, keepdims=True)
        h_ref[...] = (x * lax.rsqrt(ms + EPS) * nw_ref[...]).astype(BF16)

    o_ref[...] = _dot(h_ref[...], w_ref[...])


def _split_rows(tm, npt, width, sample_col=0):
    prompt = pl.BlockSpec((tm, width), lambda i, *_: (jnp.minimum(i, npt - 1), 0))
    sample = pl.BlockSpec((tm, width), lambda i, *_: (jnp.maximum(i - npt, 0), sample_col))
    return prompt, sample


def _inproj(x, norm_w, w):
    t, total = x.shape[0], w.shape[1]
    tm = _pick(t, 1024)
    tn = _pick(total, 2560, step=2 * LANES)
    return pl.pallas_call(
        _inproj_kernel,
        out_shape=jax.ShapeDtypeStruct((t, total), F32),
        grid=(t // tm, total // tn),
        in_specs=[
            pl.BlockSpec((tm, D_MODEL), lambda i, j: (i, 0)),
            pl.BlockSpec((1, D_MODEL), lambda i, j: (0, 0)),
            pl.BlockSpec((D_MODEL, tn), lambda i, j: (0, j)),
        ],
        out_specs=pl.BlockSpec((tm, tn), lambda i, j: (i, j)),
        scratch_shapes=[pltpu.VMEM((tm, D_MODEL), BF16)],
        compiler_params=_params(("parallel", "arbitrary")),
        name="in_proj",
    )(x, norm_w, w)


def _ret_log_decay():
    return jnp.log(1.0 - 2.0 ** (-5.0 - jnp.arange(RET_HEADS, dtype=F32)))


def _ret_tables(c):
    lg = _ret_log_decay()
    r = jnp.arange(ROWS)
    t = (r % c).astype(F32)
    seg = r // c
    diff = t[:, None] - t[None, :]
    ok = (seg[:, None] == seg[None, :]) & (diff >= 0)
    dm = jnp.where(ok[None], jnp.exp(lg[:, None, None] * jnp.maximum(diff, 0.0)[None]), 0.0)
    qd = jnp.exp(lg[:, None] * (t[None, :] + 1.0))
    kd = jnp.exp(lg[:, None] * (c - 1.0 - t[None, :]))
    cd = jnp.exp(lg * c)
    qd = jnp.broadcast_to(qd[:, :, None], (RET_HEADS, ROWS, LANES))
    kd = jnp.broadcast_to(kd[:, :, None], (RET_HEADS, ROWS, LANES))
    cd = jnp.broadcast_to(cd[:, None, None], (RET_HEADS, 1, RET_DV))
    return dm, qd, kd, cd


def _rope_tables(pos):
    half = RET_DK // 2
    inv_freq = 1.0 / (ROPE_BASE ** jnp.linspace(0.0, 1.0, half, dtype=F32))
    ang = pos.astype(F32)[:, None] * inv_freq[None, :]
    cos, sin = jnp.cos(ang), jnp.sin(ang)
    return jnp.concatenate([cos, cos], -1), jnp.concatenate([-sin, sin], -1)


def _block_mask(dtype):
    r = jnp.arange(ROWS) // (ROWS // SAMPLE_PER_TILE)
    b = jnp.arange(SAMPLE_PER_TILE * LANES) // LANES
    return (r[:, None] == b[None, :]).astype(dtype)


def _ret_common(q, k, v, g, cosf, sinf, dm, qd, kd, nw):
    qr = q * cosf + pltpu.roll(q, RET_DK // 2, 1) * sinf
    kr = (k * cosf + pltpu.roll(k, RET_DK // 2, 1) * sinf) * (RET_DK ** -0.5)
    vb = v.astype(BF16)
    scores = _dot_nt(qr.astype(BF16), kr.astype(BF16)) * dm
    o_intra = _dot(scores.astype(BF16), vb)
    q_state = (qr * qd).astype(BF16)
    k_state = kr * kd

    def finish(o):
        ms = jnp.mean(o * o, axis=-1, keepdims=True)
        on = o * lax.rsqrt(ms + EPS) * nw
        return (_silu(g) * on).astype(BF16)

    return o_intra, q_state, k_state, vb, finish


def _ret_prompt_step(q_ref, k_ref, v_ref, g_ref, cos_ref, sin_ref, dm_ref, qd_ref, kd_ref, cd_ref, nw_ref,
                     a_ref, s_ref):
    cosf, sinf = cos_ref[...], sin_ref[...]
    for h in range(RET_HEADS):
        dk = slice(h * RET_DK, (h + 1) * RET_DK)
        dv = slice(h * RET_DV, (h + 1) * RET_DV)
        o_intra, q_state, k_state, vb, finish = _ret_common(
            q_ref[:, dk], k_ref[:, dk], v_ref[:, dv], g_ref[:, dv], cosf, sinf, dm_ref[h], qd_ref[h], kd_ref[h],
            nw_ref[h])
        s = s_ref[0, h]
        a_ref[:, dv] = finish(o_intra + _dot(q_state, s.astype(BF16)))
        s_ref[0, h] = s * cd_ref[h] + _dot_tn(k_state.astype(BF16), vb)


def _ret_sample_kernel(q_ref, k_ref, v_ref, g_ref, cos_ref, sin_ref, dm_ref, qd_ref, kd_ref, cd_ref, nw_ref,
                       bm_ref, bmt_ref, s0_ref, a_ref, s_ref):
    o_intra, q_state, k_state, vb, finish = _ret_common(
        q_ref[...], k_ref[...], v_ref[...], g_ref[...], cos_ref[...], sin_ref[...], dm_ref[0], qd_ref[0], kd_ref[0],
        nw_ref[0])
    nb = SAMPLE_PER_TILE
    s0 = s0_ref[:, 0]
    q_bd = jnp.concatenate([q_state] * nb, axis=1) * bm_ref[...]
    a_ref[...] = finish(o_intra + _dot(q_bd, s0.reshape(nb * RET_DK, RET_DV).astype(BF16)))
    k_t = k_state.T.astype(BF16)
    k_bd_t = jnp.concatenate([k_t] * nb, axis=0) * bmt_ref[...]
    ds = _dot(k_bd_t, vb).reshape(nb, RET_DK, RET_DV)
    s_ref[:, 0] = s0 * cd_ref[0] + ds


def _ret_specs(row_block):
    def at(width, off):
        return lambda *ids: (row_block(*ids)[0], off // width + row_block(*ids)[1])

    def head(*shape):
        return pl.BlockSpec((1,) + shape, lambda *ids: (row_block(*ids)[1],) + (0,) * len(shape))

    return [
        pl.BlockSpec((ROWS, RET_DK), at(RET_DK, _ColsB.Q)),
        pl.BlockSpec((ROWS, RET_DK), at(RET_DK, _ColsB.K)),
        pl.BlockSpec((ROWS, RET_DV), at(RET_DV, _ColsB.V)),
        pl.BlockSpec((ROWS, RET_DV), at(RET_DV, _ColsB.GRET)),
    ], head


def _retention_sample_call(proj, row0, bsz, seq, state, ret_norm):
    rb0 = row0 // ROWS
    dm, qd, kd, cd = _ret_tables(seq)
    pos = PAST_LEN + jnp.arange(seq, dtype=jnp.int32)
    cosf, sinf = _rope_tables(jnp.tile(pos, SAMPLE_PER_TILE))
    rb = lambda t, h: (rb0 + t, h)
    specs, head = _ret_specs(rb)
    full = lambda *shape: pl.BlockSpec(shape, lambda t, h: (0,) * len(shape))
    specs += [
        full(ROWS, RET_DK), full(ROWS, RET_DK),
        head(ROWS, ROWS), head(ROWS, LANES), head(ROWS, LANES), head(1, RET_DV), head(1, RET_DV),
        full(ROWS, SAMPLE_PER_TILE * LANES), full(SAMPLE_PER_TILE * LANES, ROWS),
        pl.BlockSpec((SAMPLE_PER_TILE, 1, RET_DK, RET_DV), lambda t, h: (t, h, 0, 0)),
    ]
    bm = _block_mask(BF16)
    operands = (proj, proj, proj, proj, cosf, sinf, dm, qd, kd, cd, ret_norm.reshape(RET_HEADS, 1, RET_DV),
                bm, bm.T, state)
    out_shapes = (jax.ShapeDtypeStruct((bsz * seq, RET_V), BF16), jax.ShapeDtypeStruct(state.shape, F32))
    out_specs = (pl.BlockSpec((ROWS, RET_DV), lambda t, h: (t, h)),
                 pl.BlockSpec((SAMPLE_PER_TILE, 1, RET_DK, RET_DV), lambda t, h: (t, h, 0, 0)))
    return specs, operands, out_shapes, out_specs


def _softplus(x):
    return jnp.maximum(x, 0.0) + jnp.log1p(jnp.exp(-jnp.abs(x)))


def _conv_piece(x, pred, w_ref, b_ref):
    width = x.shape[1]
    x3 = x.reshape(ROWS // SUBLANES, SUBLANES, width)
    t8 = lax.broadcasted_iota(jnp.int32, x3.shape, 1)
    acc = b_ref[...].reshape(1, 1, width)
    for i in range(CONV_WIDTH):
        s = CONV_WIDTH - 1 - i
        tap = x3 if s == 0 else pltpu.roll(jnp.where(t8 >= SUBLANES - s, pred, x3), s, 1)
        acc = acc + tap * w_ref[pl.ds(i, 1), :].reshape(1, 1, width)
    return _silu(acc).reshape(ROWS, width)


def _ssd_decay_terms(c, dt_pre, a_log, ltri):
    dt = _softplus(dt_pre)
    d_a = dt * (-jnp.exp(a_log))
    hi = d_a.astype(BF16)
    r1 = d_a - hi.astype(F32)
    mid = r1.astype(BF16)
    lo = (r1 - mid.astype(F32)).astype(BF16)
    cum = _dot(ltri, hi) + _dot(ltri, mid) + _dot(ltri, lo)
    cum3 = cum.reshape(ROWS // c, c, LANES)
    c_last = jnp.broadcast_to(cum3[:, c - 1:c, :], cum3.shape).reshape(ROWS, LANES)
    to_end = jnp.exp(c_last - cum) * dt
    e_cum = jnp.exp(cum)
    return cum, to_end, e_cum, cum.T, dt.T


def _ssd_group(c, terms, head0, z_ref, dsk_ref, nrm_ref, xs, bm, cm, y_ref, state_io, zero=None):
    cum, to_end, e_cum, cum_t, dt_t = terms
    b_b, c_b = bm.astype(BF16), cm.astype(BF16)
    cb = _dot_nt(c_b, b_b)
    ri = lax.broadcasted_iota(jnp.int32, (ROWS, ROWS), 0)
    ci = lax.broadcasted_iota(jnp.int32, (ROWS, ROWS), 1)
    shift = c.bit_length() - 1
    causal = (ci <= ri) & (jnp.right_shift(ri, shift) == jnp.right_shift(ci, shift))
    lo_half = lax.broadcasted_iota(jnp.int32, (ROWS, LANES), 1) < SSM_HEADDIM

    def col(a, h):
        return jnp.broadcast_to(a[:, h:h + 1], (ROWS, LANES))

    def weights(h):
        seg = col(cum, h) - jnp.broadcast_to(cum_t[h:h + 1, :], (ROWS, ROWS))
        decay = jnp.exp(jnp.where(causal, seg, -jnp.inf))
        return (cb * decay * jnp.broadcast_to(dt_t[h:h + 1, :], (ROWS, ROWS))).astype(BF16)

    pieces = []
    for m in range(HEADS_PER_GROUP // 2):
        ha, hb = head0 + 2 * m, head0 + 2 * m + 1
        xp = xs[:, m * LANES:(m + 1) * LANES]
        w2 = jnp.concatenate([weights(ha), weights(hb)], axis=1)
        x2 = jnp.concatenate([jnp.where(lo_half, xp, 0.0), jnp.where(lo_half, 0.0, xp)], axis=0).astype(BF16)
        y = _dot(w2, x2)
        xw = (xp * jnp.where(lo_half, col(to_end, ha), col(to_end, hb))).astype(BF16)
        y_state = state_io(m, ha, c_b, xw, b_b, e_cum)
        y = y + y_state * jnp.where(lo_half, col(e_cum, ha), col(e_cum, hb))
        pieces.append(y + xp * dsk_ref[:, m * LANES:(m + 1) * LANES])
    if zero is not None:
        pieces[0] = jnp.concatenate([pieces[0][:SUBLANES] + zero, pieces[0][SUBLANES:]], axis=0)
    yg = jnp.concatenate(pieces, axis=1) * _silu(z_ref[...])
    ms = jnp.mean(yg * yg, axis=-1, keepdims=True)
    y_ref[...] = (yg * lax.rsqrt(ms + EPS) * nrm_ref[...]).astype(BF16)


def _pair_decay(e_cum, row, ha):
    top = lax.broadcasted_iota(jnp.int32, (2 * SSM_HEADDIM, SSM_DSTATE), 0) < SSM_HEADDIM
    ea = jnp.broadcast_to(e_cum[row:row + 1, ha:ha + 1], top.shape)
    eb = jnp.broadcast_to(e_cum[row:row + 1, ha + 1:ha + 2], top.shape)
    return jnp.where(top, ea, eb)


def _mixer_prompt_kernel(x0_ref, xn_ref, nw_ref, w_ref, cwx_ref, cbx_ref, cwb_ref, cbb_ref, cwc_ref, cbc_ref,
                         dtb_ref, alog_ref, dsk_ref, nrm_ref, ltri_ref, *rest):
    ret_in, (y_ref, hs_ref, px_ref, pb_ref, pc_ref, a_ref, s_ref, ga_ref, gb_ref, proj_ref) = rest[:7], rest[7:]
    c = pl.program_id(1)
    step_no = pl.program_id(0) * pl.num_programs(1) + c

    def normed(x_ref):
        x = x_ref[...]
        ms = jnp.mean(x * x, axis=-1, keepdims=True)
        return (x * lax.rsqrt(ms + EPS) * nw_ref[...]).astype(BF16)

    def project(x_ref, slot):
        proj_ref[slot] = _dot(normed(x_ref), w_ref[:, :_ColsB.TOTAL])

    @pl.when(c == 0)
    def _():
        hs_ref[...] = jnp.zeros_like(hs_ref)
        px_ref[...] = jnp.zeros_like(px_ref)
        pb_ref[...] = jnp.zeros_like(pb_ref)
        pc_ref[...] = jnp.zeros_like(pc_ref)
        s_ref[...] = jnp.zeros_like(s_ref)

    @pl.when(step_no == 0)
    def _():
        project(x0_ref, 0)

    def conv(x_ref, p_ref, w_ref, bias_ref):
        x = x_ref[...]
        x3 = x.reshape(ROWS // SUBLANES, SUBLANES, x.shape[1])
        pred = jnp.concatenate([p_ref[...], x3[:-1]], axis=0)
        out = _conv_piece(x, pred, w_ref, bias_ref)
        p_ref[0] = x3[-1]
        return out

    def step(cur_slot, next_slot):
        h_next = normed(xn_ref)
        bounds = [0, 10 * 256, 20 * 256, 30 * 256, _ColsB.TOTAL]

        def project_slice(g):
            lo, hi = bounds[g], bounds[g + 1]
            part = _dot(h_next, w_ref[:, lo:hi])
            proj_ref[next_slot, :, lo:hi] = part
            bits = pltpu.bitcast(part[ROWS - SUBLANES:, hi - lo - LANES:], jnp.uint32)
            sixteen = jnp.uint32(16)
            return pltpu.bitcast(lax.shift_right_logical(lax.shift_right_logical(bits, sixteen), sixteen), F32)

        cur = proj_ref.at[cur_slot]
        gn = SSM_GROUPS * SSM_DSTATE
        _ret_prompt_step(cur.at[:, _ColsB.Q:_ColsB.Q + RET_QK], cur.at[:, _ColsB.K:_ColsB.K + RET_QK],
                         cur.at[:, _ColsB.V:_ColsB.V + RET_V], cur.at[:, _ColsB.GRET:_ColsB.GRET + RET_V],
                         *ret_in, a_ref, s_ref)
        z_ref = cur.at[:, _ColsB.Z:_ColsB.Z + SSM_INNER]
        xs_ref = cur.at[:, _ColsB.XS:_ColsB.XS + SSM_INNER]
        b_ref = cur.at[:, _ColsB.B:_ColsB.B + gn]
        c_ref = cur.at[:, _ColsB.C:_ColsB.C + gn]
        dt_ref = cur.at[:, _ColsB.DT:_ColsB.DT + LANES]
        ga_ref[...] = cur[:, _ColsB.GA:_ColsB.GA + D_MODEL]
        gb_ref[...] = cur[:, _ColsB.GB:_ColsB.GB + D_MODEL]
        terms = _ssd_decay_terms(CHUNK, dt_ref[...] + dtb_ref[...], alog_ref[...], ltri_ref[...])
        for g in range(SSM_GROUPS):
            def cols(ref, width, g=g):
                return ref.at[..., g * width:(g + 1) * width]

            gi, n = GROUP_INNER, SSM_DSTATE
            xs = conv(cols(xs_ref, gi), cols(px_ref, gi), cols(cwx_ref, gi), cols(cbx_ref, gi))
            bm = conv(cols(b_ref, n), cols(pb_ref, n), cols(cwb_ref, n), cols(cbb_ref, n))
            cm = conv(cols(c_ref, n), cols(pc_ref, n), cols(cwc_ref, n), cols(cbc_ref, n))
            hs_g = hs_ref.at[0, g * gi:(g + 1) * gi, :]

            def state_io(m, ha, c_b, xw, b_b, e_cum, hs_g=hs_g):
                rows = pl.ds(m * 2 * SSM_HEADDIM, 2 * SSM_HEADDIM)
                h = hs_g[rows, :]
                hs_g[rows, :] = h * _pair_decay(e_cum, ROWS - 1, ha) + _dot_tn(xw, b_b)
                return _dot_nt(c_b, h.astype(BF16))

            _ssd_group(CHUNK, terms, g * HEADS_PER_GROUP, cols(z_ref, gi), cols(dsk_ref, gi), cols(nrm_ref, gi),
                       xs, bm, cm, cols(y_ref, gi), state_io, zero=project_slice(g))

    @pl.when(step_no % 2 == 0)
    def _():
        step(0, 1)

    @pl.when(step_no % 2 == 1)
    def _():
        step(1, 0)


def _ssd_sample_kernel(seq, z_ref, xs_ref, b_ref, c_ref, dt_ref, cwx_ref, cbx_ref, cwb_ref, cbb_ref, cwc_ref,
                       cbc_ref, dtb_ref, alog_ref, dsk_ref, nrm_ref, ltri_ref, bmask_ref, px_ref, pb_ref, pc_ref,
                       h0_ref, y_ref, hs_ref, csx_ref, csb_ref, csc_ref):
    nb = SAMPLE_PER_TILE

    def conv(x_ref, p_ref, w_ref, bias_ref, cs_ref):
        x = x_ref[...]
        cs_ref[...] = x.reshape(nb, seq, x.shape[1])
        return _conv_piece(x, p_ref[...], w_ref, bias_ref)

    xs = conv(xs_ref, px_ref, cwx_ref, cbx_ref, csx_ref)
    bm = conv(b_ref, pb_ref, cwb_ref, cbb_ref, csb_ref)
    cm = conv(c_ref, pc_ref, cwc_ref, cbc_ref, csc_ref)
    bmask = bmask_ref[...]

    def state_io(m, ha, c_b, xw, b_b, e_cum):
        rows = pl.ds(m * 2 * SSM_HEADDIM, 2 * SSM_HEADDIM)
        c_bd = jnp.concatenate([c_b] * nb, axis=1) * bmask
        b_bd = jnp.concatenate([b_b] * nb, axis=1) * bmask
        hs = [h0_ref[i, rows, :] for i in range(nb)]
        h_cat = jnp.concatenate(hs, axis=1).astype(BF16)
        dh = _dot_tn(xw, b_bd)
        for i in range(nb):
            decay = _pair_decay(e_cum, i * seq + seq - 1, ha)
            hs_ref[i, rows, :] = hs[i] * decay + dh[:, i * SSM_DSTATE:(i + 1) * SSM_DSTATE]
        return _dot_nt(c_bd, h_cat)

    shift = jnp.bitwise_and(LANES - HEADS_PER_GROUP * pl.program_id(1), LANES - 1)
    dt_pre = pltpu.roll(dt_ref[...] + dtb_ref[...], shift, 1)
    a_log = pltpu.roll(jnp.broadcast_to(alog_ref[...], (SUBLANES, LANES)), shift, 1)[:1]
    terms = _ssd_decay_terms(seq, dt_pre, a_log, ltri_ref[...])
    _ssd_group(seq, terms, 0, z_ref, dsk_ref, nrm_ref, xs, bm, cm, y_ref, state_io)


def _ssd_tables(c):
    r = jnp.arange(ROWS)
    ltri = ((r[:, None] >= r[None, :]) & ((r[:, None] // c) == (r[None, :] // c))).astype(BF16)
    return ltri


def _ssd_specs(row_block):
    def at(width, off):
        return lambda *ids: (row_block(*ids)[0], off // width + row_block(*ids)[1])

    def grp(rows, width):
        return pl.BlockSpec((rows, width), lambda *ids: (0, row_block(*ids)[1]))

    n = SSM_DSTATE
    return [
        pl.BlockSpec((ROWS, GROUP_INNER), at(GROUP_INNER, _ColsB.Z)),
        pl.BlockSpec((ROWS, GROUP_INNER), at(GROUP_INNER, _ColsB.XS)),
        pl.BlockSpec((ROWS, n), at(n, _ColsB.B)),
        pl.BlockSpec((ROWS, n), at(n, _ColsB.C)),
        pl.BlockSpec((ROWS, LANES), lambda *ids: (row_block(*ids)[0], _ColsB.DT // LANES)),
        grp(CONV_WIDTH, GROUP_INNER), grp(1, GROUP_INNER),
        grp(CONV_WIDTH, n), grp(1, n), grp(CONV_WIDTH, n), grp(1, n),
        pl.BlockSpec((1, LANES), lambda *ids: (0, 0)), pl.BlockSpec((1, LANES), lambda *ids: (0, 0)),
        grp(1, GROUP_INNER), grp(1, GROUP_INNER),
    ]


def _ssd_params(conv_w, conv_b, dt_bias, a_log, d_skip, ssm_norm):
    gn = SSM_GROUPS * SSM_DSTATE
    cwx, cwb, cwc = conv_w[:, :SSM_INNER], conv_w[:, SSM_INNER:SSM_INNER + gn], conv_w[:, SSM_INNER + gn:]
    cb = conv_b.reshape(1, CONV_DIM)
    cbx, cbb, cbc = cb[:, :SSM_INNER], cb[:, SSM_INNER:SSM_INNER + gn], cb[:, SSM_INNER + gn:]

    def head_lanes(v):
        return jnp.pad(v, (0, LANES - SSM_HEADS)).reshape(1, LANES)

    dsk = jnp.repeat(d_skip, SSM_HEADDIM).reshape(1, SSM_INNER)
    return (cwx, cbx, cwb, cbb, cwc, cbc, head_lanes(dt_bias), head_lanes(a_log), dsk,
            ssm_norm.reshape(1, SSM_INNER))


def _conv_tail(csx, csb, csc):
    keep = SUBLANES - (CONV_WIDTH - 1)
    return jnp.concatenate([csx[:, keep:], csb[:, keep:], csc[:, keep:]], axis=-1)


def _mixer_prompt(x, norm_w, w_b, bsz, seq, params, ret_norm):
    nch = seq // ROWS
    gn = SSM_GROUPS * SSM_DSTATE
    full = lambda a: pl.BlockSpec(a.shape, lambda b, c: (0,) * a.ndim)
    ltri = _ssd_tables(CHUNK)
    dm, qd, kd, cd = _ret_tables(CHUNK)
    cosf, sinf = _rope_tables(jnp.arange(seq, dtype=jnp.int32))
    rnw = ret_norm.reshape(RET_HEADS, 1, RET_DV)
    specs = [pl.BlockSpec((ROWS, D_MODEL), lambda b, c: (0, 0)),
             pl.BlockSpec((ROWS, D_MODEL), lambda b, c: (jnp.minimum(b * nch + c + 1, bsz * nch - 1), 0)),
             full(norm_w), full(w_b), *[full(p) for p in params], full(ltri),
             pl.BlockSpec((ROWS, RET_DK), lambda b, c: (c, 0)), pl.BlockSpec((ROWS, RET_DK), lambda b, c: (c, 0)),
             full(dm), full(qd), full(kd), full(cd), full(rnw)]
    tail = lambda width: pl.BlockSpec((1, SUBLANES, width), lambda b, c: (b, 0, 0))
    y, hs, csx, csb, csc, a, s, ga, gb = pl.pallas_call(
        _mixer_prompt_kernel,
        out_shape=(jax.ShapeDtypeStruct((bsz * seq, SSM_INNER), BF16),
                   jax.ShapeDtypeStruct((bsz, SSM_HEADS * SSM_HEADDIM, SSM_DSTATE), F32),
                   jax.ShapeDtypeStruct((bsz, SUBLANES, SSM_INNER), F32),
                   jax.ShapeDtypeStruct((bsz, SUBLANES, gn), F32),
                   jax.ShapeDtypeStruct((bsz, SUBLANES, gn), F32),
                   jax.ShapeDtypeStruct((bsz * seq, RET_V), BF16),
                   jax.ShapeDtypeStruct((bsz, RET_HEADS, RET_DK, RET_DV), F32),
                   jax.ShapeDtypeStruct((bsz * seq, D_MODEL), F32),
                   jax.ShapeDtypeStruct((bsz * seq, D_MODEL), F32)),
        grid=(bsz, nch),
        in_specs=specs,
        out_specs=(pl.BlockSpec((ROWS, SSM_INNER), lambda b, c: (b * nch + c, 0)),
                   pl.BlockSpec((1, SSM_HEADS * SSM_HEADDIM, SSM_DSTATE), lambda b, c: (b, 0, 0)),
                   tail(SSM_INNER), tail(gn), tail(gn),
                   pl.BlockSpec((ROWS, RET_V), lambda b, c: (b * nch + c, 0)),
                   pl.BlockSpec((1, RET_HEADS, RET_DK, RET_DV), lambda b, c: (b, 0, 0, 0)),
                   pl.BlockSpec((ROWS, D_MODEL), lambda b, c: (b * nch + c, 0)),
                   pl.BlockSpec((ROWS, D_MODEL), lambda b, c: (b * nch + c, 0))),
        scratch_shapes=[pltpu.VMEM((2, ROWS, _ColsB.TOTAL), F32)],
        compiler_params=_params(("arbitrary", "arbitrary")),
        name="mixer_prompt",
    )(x, x, norm_w, w_b, *params, ltri, cosf, sinf, dm, qd, kd, cd, rnw)
    return y, hs, _conv_tail(csx, csb, csc), a, s, ga, gb


def _mixer_sample_kernel(seq, n_ssd_in, n_ret_in, *refs):
    ssd_in, refs = refs[:n_ssd_in], refs[n_ssd_in:]
    ret_in, refs = refs[:n_ret_in], refs[n_ret_in:]
    _ssd_sample_kernel(seq, *ssd_in, *refs[:5])
    _ret_sample_kernel(*ret_in, *refs[5:])


def _mixer_sample(proj, bsz, seq, params, state, conv_state, ret_state, ret_norm):
    assert SSM_GROUPS == RET_HEADS
    ntile = bsz // SAMPLE_PER_TILE
    gn = SSM_GROUPS * SSM_DSTATE
    specs = _ssd_specs(lambda t, g: (t, g))
    gh = HEADS_PER_GROUP * SSM_HEADDIM
    cs = jnp.pad(conv_state, ((0, 0), (SUBLANES - (CONV_WIDTH - 1), 0), (0, 0)))
    csx, csb, csc = cs[:, :, :SSM_INNER], cs[:, :, SSM_INNER:SSM_INNER + gn], cs[:, :, SSM_INNER + gn:]
    pred = lambda width: pl.BlockSpec((SAMPLE_PER_TILE, SUBLANES, width), lambda t, g: (t, 0, g))
    state_spec = pl.BlockSpec((SAMPLE_PER_TILE, gh, SSM_DSTATE), lambda t, g: (t, g, 0))
    specs += [
        pl.BlockSpec((ROWS, ROWS), lambda t, g: (0, 0)),
        pl.BlockSpec((ROWS, SAMPLE_PER_TILE * LANES), lambda t, g: (0, 0)),
        pred(GROUP_INNER), pred(SSM_DSTATE), pred(SSM_DSTATE),
        state_spec,
    ]
    state2 = state.reshape(bsz, SSM_HEADS * SSM_HEADDIM, SSM_DSTATE)
    assert seq == SUBLANES
    ssd_operands = (proj, proj, proj, proj, proj, *params, _ssd_tables(seq), _block_mask(BF16), csx, csb, csc,
                    state2)
    r_specs, r_operands, r_shapes, r_out_specs = _retention_sample_call(proj, 0, bsz, seq, ret_state, ret_norm)
    y, hs, csx, csb, csc, a, s = pl.pallas_call(
        functools.partial(_mixer_sample_kernel, seq, len(ssd_operands), len(r_operands)),
        out_shape=(jax.ShapeDtypeStruct((bsz * seq, SSM_INNER), BF16), jax.ShapeDtypeStruct(state2.shape, F32),
                   jax.ShapeDtypeStruct((bsz, seq, SSM_INNER), F32),
                   jax.ShapeDtypeStruct((bsz, seq, gn), F32),
                   jax.ShapeDtypeStruct((bsz, seq, gn), F32), *r_shapes),
        grid=(ntile, SSM_GROUPS),
        in_specs=specs + r_specs,
        out_specs=(pl.BlockSpec((ROWS, GROUP_INNER), lambda t, g: (t, g)), state_spec,
                   pred(GROUP_INNER), pred(SSM_DSTATE), pred(SSM_DSTATE), *r_out_specs),
        compiler_params=_params(("parallel", "parallel")),
        name="mixer_sample",
    )(*ssd_operands, *r_operands)
    return y, hs, _conv_tail(csx, csb, csc), a, s


def _post_kernel(n_prompt_tiles, ap_ref, as_ref, yp_ref, ys_ref, gap_ref, gas_ref, gbp_ref, gbs_ref, xp_ref, xs_ref,
                 wr_ref, ws_ref, wo_ref, nw_ref, wrt_ref, brt_ref, x1_ref, h2_ref, lg_ref):
    is_prompt = pl.program_id(0) < n_prompt_tiles
    half = x1_ref.shape[0] // 2
    for r in (pl.ds(0, half), pl.ds(half, half)):
        branch_a = _dot(jnp.where(is_prompt, ap_ref[r, :], as_ref[r, :]), wr_ref[...])
        branch_b = _dot(jnp.where(is_prompt, yp_ref[r, :], ys_ref[r, :]), ws_ref[...])
        g_a = jnp.where(is_prompt, gap_ref[r, :], gas_ref[r, :])
        g_b = jnp.where(is_prompt, gbp_ref[r, :], gbs_ref[r, :])
        merged = jax.nn.sigmoid(g_a) * branch_a + jax.nn.sigmoid(g_b) * branch_b
        x1 = jnp.where(is_prompt, xp_ref[r, :], xs_ref[r, :]) + _dot(merged.astype(BF16), wo_ref[...])
        x1_ref[r, :] = x1
        ms = jnp.mean(x1 * x1, axis=-1, keepdims=True)
        h2 = x1 * lax.rsqrt(ms + EPS) * nw_ref[...]
        _store_row_tiles(h2_ref.at[pl.ds(r.start * ROW_TILES, half * ROW_TILES), :], h2)
        lg_ref[r, :] = _dot(h2.astype(BF16), wrt_ref[...]) + brt_ref[...]


def _post(a_p, a_s, y_p, y_s, ga_p, gb_p, proj_s, xp, xs, w_ret, w_ssm, w_out, norm_ffn, w_router, b_router):
    t = xp.shape[0] + xs.shape[0]
    tm = _pick(math.gcd(xp.shape[0], xs.shape[0]), 512)
    npt = xp.shape[0] // tm
    row = lambda width, blk: pl.BlockSpec((tm, width), lambda i: (i, blk))
    full = lambda *shape: pl.BlockSpec(shape, lambda i: (0,) * len(shape))
    return pl.pallas_call(
        functools.partial(_post_kernel, npt),
        out_shape=(jax.ShapeDtypeStruct((t, D_MODEL), F32), jax.ShapeDtypeStruct((t * ROW_TILES, LANES), F32),
                   jax.ShapeDtypeStruct((t, LANES), F32)),
        grid=(t // tm,),
        in_specs=[*_split_rows(tm, npt, RET_V), *_split_rows(tm, npt, SSM_INNER),
                  *_split_rows(tm, npt, D_MODEL, _ColsB.GA // D_MODEL),
                  *_split_rows(tm, npt, D_MODEL, _ColsB.GB // D_MODEL),
                  *_split_rows(tm, npt, D_MODEL),
                  full(RET_V, D_MODEL), full(SSM_INNER, D_MODEL), full(D_MODEL, D_MODEL), full(1, D_MODEL),
                  full(D_MODEL, LANES), full(1, LANES)],
        out_specs=(row(D_MODEL, 0), pl.BlockSpec((tm * ROW_TILES, LANES), lambda i: (i, 0)), row(LANES, 0)),
        compiler_params=_params(("parallel",)),
        name="post_mixer",
    )(a_p, a_s, y_p, y_s, ga_p, proj_s, gb_p, proj_s, xp, xs, w_ret, w_ssm, w_out, norm_ffn, w_router, b_router)


def _route_kernel(lg_ref, lstrict_ref, srow_ref, gate_ref, tmeta_ref, cnt_ref):
    @pl.when(pl.program_id(0) == 0)
    def _():
        cnt_ref[...] = jnp.zeros_like(cnt_ref)

    tm = lg_ref.shape[0]
    lane = lax.broadcasted_iota(jnp.int32, (tm, LANES), 1)
    lane_f = lane.astype(F32)
    cur = jnp.where(lane < N_EXPERTS, lg_ref[...], -jnp.inf)
    vals, hots = [], []
    for _ in range(TOP_K):
        m = jnp.max(cur, axis=1, keepdims=True)
        idx = jnp.min(jnp.where(cur == m, lane_f, float(LANES)), axis=1, keepdims=True)
        hot = lane_f == idx
        vals.append(m)
        hots.append(hot)
        cur = jnp.where(hot, -jnp.inf, cur)
    exps = [jnp.exp(v - vals[0]) for v in vals]
    denom = exps[0] + exps[1] + exps[2] + exps[3]
    sel = hots[0] | hots[1] | hots[2] | hots[3]
    self32 = sel.astype(F32)
    base = cnt_ref[...]
    num = jnp.sum(self32, axis=0, keepdims=True)
    rank = _dot(lstrict_ref[...], self32.astype(BF16))
    cnt_ref[...] = base + num
    lane8 = lax.broadcasted_iota(jnp.int32, (SUBLANES, LANES), 1)
    incl = jnp.broadcast_to(num, (SUBLANES, LANES))
    shift = 1
    while shift < N_EXPERTS:
        incl = incl + jnp.where(lane8 >= shift, pltpu.roll(incl, shift, 1), 0.0)
        shift *= 2
    off = incl[:1] - num
    slot = rank + off
    meta = jnp.zeros((tm, LANES), F32)
    for k in range(TOP_K):
        s_k = jnp.sum(jnp.where(hots[k], slot, 0.0), axis=1, keepdims=True)
        meta = jnp.where(lane == k, s_k, meta)
        meta = jnp.where(lane == SUBLANES + k, exps[k] / denom, meta)
    meta_t = meta.T
    srow_ref[0] = (meta_t[:SUBLANES] * float(ROW_TILES)).astype(jnp.int32)
    gate_ref[0] = meta_t[SUBLANES:2 * SUBLANES]
    row8 = lax.broadcasted_iota(jnp.int32, (SUBLANES, LANES), 0)
    tmeta_ref[0] = jnp.where(row8 == 0, base, jnp.where(row8 == 1, num, jnp.where(row8 == 2, off, 0.0)))


def _route(logits, tm):
    t = logits.shape[0]
    r = jnp.arange(tm)
    lstrict = (r[:, None] > r[None, :]).astype(BF16)
    return pl.pallas_call(
        _route_kernel,
        out_shape=(jax.ShapeDtypeStruct((t // tm, SUBLANES, tm), jnp.int32),
                   jax.ShapeDtypeStruct((t // tm, SUBLANES, tm), F32),
                   jax.ShapeDtypeStruct((t // tm, SUBLANES, LANES), F32),
                   jax.ShapeDtypeStruct((1, LANES), F32)),
        grid=(t // tm,),
        in_specs=[pl.BlockSpec((tm, LANES), lambda i: (i, 0)), pl.BlockSpec((tm, tm), lambda i: (0, 0))],
        out_specs=(pl.BlockSpec((1, SUBLANES, tm), lambda i: (i, 0, 0)),
                   pl.BlockSpec((1, SUBLANES, tm), lambda i: (i, 0, 0)),
                   pl.BlockSpec((1, SUBLANES, LANES), lambda i: (i, 0, 0)),
                   pl.BlockSpec((1, LANES), lambda i: (0, 0))),
        compiler_params=_params(("arbitrary",)),
        name="route",
    )(logits, lstrict)


def _for_each_strip(tmeta_ref, tm, make_copy, action):
    del tm

    def per_expert(e, carry):
        n, hbm0, tile0 = tmeta_ref[0, 1, e], tmeta_ref[0, 0, e], tmeta_ref[0, 2, e]
        n_chunks = lax.shift_right_logical(n, STRIP_BITS)

        def chunk(c, inner):
            action(make_copy(hbm0 + c * STRIP, tile0 + c * STRIP, STRIP))
            return inner

        lax.fori_loop(0, n_chunks, chunk, 0)
        for bit in reversed(range(STRIP_BITS)):
            size = 1 << bit

            @pl.when(jnp.bitwise_and(n, size) != 0)
            def _():
                done = jnp.bitwise_and(n, ~(2 * size - 1))
                action(make_copy(hbm0 + done, tile0 + done, size))
        return carry

    lax.fori_loop(0, N_EXPERTS, per_expert, 0)


def _rows(ref, row, n):
    return ref.at[pl.ds(pl.multiple_of(row * ROW_TILES, ROW_TILES), n * ROW_TILES), :]


def _dispatch_kernel(pstart_ref, pend_ref, srow_ref, tmeta_ref, h2_ref, xs_ref, stage_ref, zero_ref, sem):
    tm = h2_ref.shape[0] // ROW_TILES
    i = pl.program_id(0)
    slot = i % 2

    used = pend_ref[N_EXPERTS - 1] // EXPERT_BLOCK
    total = xs_ref.shape[0] // (EXPERT_BLOCK * ROW_TILES)

    def clear(action):
        def block(row):
            return pltpu.make_async_copy(zero_ref, _rows(xs_ref, row, EXPERT_BLOCK), sem.at[2])

        def last_of_segment(e, carry):
            @pl.when(pend_ref[e] > pstart_ref[e])
            def _():
                action(block(pend_ref[e] - EXPERT_BLOCK))
            return carry

        lax.fori_loop(0, N_EXPERTS, last_of_segment, 0)
        lax.fori_loop(used, total, lambda j, c: (action(block(j * EXPERT_BLOCK)), c)[1], 0)

    @pl.when(i == 0)
    def _():
        zero_ref[...] = jnp.zeros_like(zero_ref)
        clear(lambda copy: copy.start())

    def place(t, carry):
        row = h2_ref[pl.ds(pl.multiple_of(t * ROW_TILES, ROW_TILES), ROW_TILES), :]
        for k in range(TOP_K):
            stage_ref[slot, pl.ds(pl.multiple_of(srow_ref[k * tm + t], ROW_TILES), ROW_TILES), :] = row
        return carry

    lax.fori_loop(0, tm, place, 0, unroll=ISSUE_UNROLL)

    @pl.when(i == 0)
    def _():
        clear(lambda copy: copy.wait())

    def strips(meta_ref, s):
        return functools.partial(
            _for_each_strip, meta_ref, tm,
            lambda hbm_row, tile_row, n: pltpu.make_async_copy(_rows(stage_ref.at[s], tile_row, n),
                                                               _rows(xs_ref, hbm_row, n), sem.at[s]))

    strips(tmeta_ref, slot)(lambda copy: copy.start())

    def wait_all(s):
        pltpu.make_async_copy(stage_ref.at[s], _rows(xs_ref, 0, TOP_K * tm), sem.at[s]).wait()

    @pl.when(i > 0)
    def _():
        wait_all(1 - slot)

    @pl.when(i == pl.num_programs(0) - 1)
    def _():
        wait_all(slot)


def _dispatch(pstart, pends, srow, tmeta, h2, n_rows, tm):
    t = h2.shape[0] // ROW_TILES
    smem = lambda shape, imap: pl.BlockSpec(shape, imap, memory_space=pltpu.SMEM)
    return pl.pallas_call(
        _dispatch_kernel,
        out_shape=jax.ShapeDtypeStruct((n_rows * ROW_TILES, LANES), F32),
        grid_spec=pltpu.PrefetchScalarGridSpec(
            num_scalar_prefetch=2,
            grid=(t // tm,),
            in_specs=[smem((SUBLANES * tm,), lambda i, ps, pe: (i,)),
                      smem((1, SUBLANES, LANES), lambda i, ps, pe: (i, 0, 0)),
                      pl.BlockSpec((tm * ROW_TILES, LANES), lambda i, ps, pe: (i, 0))],
            out_specs=pl.BlockSpec(memory_space=pl.ANY),
            scratch_shapes=[pltpu.VMEM((2, TOP_K * tm * ROW_TILES, LANES), F32),
                            pltpu.VMEM((EXPERT_BLOCK * ROW_TILES, LANES), F32),
                            pltpu.SemaphoreType.DMA((3,))],
        ),
        compiler_params=_params(("arbitrary",)),
        name="dispatch",
    )(pstart, pends, srow, tmeta, h2)


def _combine_kernel(n_prompt_tiles, srow_ref, gate_ref, tmeta_ref, tmeta_next_ref, x1_ref, nw_ref, yb_ref,
                    op_ref, os_ref, buf_ref, acc_ref, sem):
    tm = x1_ref.shape[0]
    i = pl.program_id(0)
    slot = i % 2

    def strips(meta_ref, s):
        return functools.partial(
            _for_each_strip, meta_ref, tm,
            lambda hbm_row, tile_row, n: pltpu.make_async_copy(_rows(yb_ref, hbm_row, n),
                                                               _rows(buf_ref.at[s], tile_row, n), sem.at[s]))

    @pl.when(i == 0)
    def _():
        strips(tmeta_ref, 0)(lambda copy: copy.start())

    @pl.when(i + 1 < pl.num_programs(0))
    def _():
        strips(tmeta_next_ref, 1 - slot)(lambda copy: copy.start())

    pltpu.make_async_copy(_rows(yb_ref, 0, TOP_K * tm), buf_ref.at[slot], sem.at[slot]).wait()

    def gather(t, carry):
        def term(k):
            row = buf_ref[slot, pl.ds(pl.multiple_of(srow_ref[k * tm + t], ROW_TILES), ROW_TILES), :]
            return row * gate_ref[k * tm + t]

        moe = term(0)
        for k in range(1, TOP_K):
            moe = moe + term(k)
        acc_ref[pl.ds(pl.multiple_of(t * ROW_TILES, ROW_TILES), ROW_TILES), :] = moe
        return carry

    lax.fori_loop(0, tm, gather, 0, unroll=ISSUE_UNROLL)
    x2 = x1_ref[...] + _load_row_tiles(acc_ref, tm)
    ms = jnp.mean(x2 * x2, axis=-1, keepdims=True)
    out = x2 * lax.rsqrt(ms + EPS) * nw_ref[...]

    @pl.when(i < n_prompt_tiles)
    def _():
        op_ref[...] = out

    @pl.when(i >= n_prompt_tiles)
    def _():
        os_ref[...] = out


def _combine(srow, gates, tmeta, x1, norm_final, yb, n_prompt, tm):
    t = x1.shape[0]
    nt = t // tm
    npt = n_prompt // tm
    smem = lambda shape, imap: pl.BlockSpec(shape, imap, memory_space=pltpu.SMEM)
    return pl.pallas_call(
        functools.partial(_combine_kernel, npt),
        out_shape=(jax.ShapeDtypeStruct((n_prompt, D_MODEL), F32),
                   jax.ShapeDtypeStruct((t - n_prompt, D_MODEL), F32)),
        grid=(nt,),
        in_specs=[smem((SUBLANES * tm,), lambda i: (i,)),
                  smem((SUBLANES * tm,), lambda i: (i,)),
                  smem((1, SUBLANES, LANES), lambda i: (i, 0, 0)),
                  smem((1, SUBLANES, LANES), lambda i: (jnp.minimum(i + 1, nt - 1), 0, 0)),
                  pl.BlockSpec((tm, D_MODEL), lambda i: (i, 0)),
                  pl.BlockSpec((1, D_MODEL), lambda i: (0, 0)),
                  pl.BlockSpec(memory_space=pl.ANY)],
        out_specs=_split_rows(tm, npt, D_MODEL),
        scratch_shapes=[pltpu.VMEM((2, TOP_K * tm * ROW_TILES, LANES), F32),
                        pltpu.VMEM((tm * ROW_TILES, LANES), F32),
                        pltpu.SemaphoreType.DMA((2,))],
        compiler_params=_params(("arbitrary",)),
        name="combine",
    )(srow, gates, tmeta, tmeta, x1, norm_final, yb)


def _expert_kernel(be_ref, nv_ref, next_ref, xs_ref, bgu_ref, bd_ref, wgu_hbm, wd_hbm, yb_ref,
                   wgu_f, wd_f, wgu_b, wd_b, sem):
    i = pl.program_id(0)

    def fetch(e):
        return (pltpu.make_async_copy(wgu_hbm.at[e], wgu_f, sem.at[0]),
                pltpu.make_async_copy(wd_hbm.at[e], wd_f, sem.at[1]))

    @pl.when(i == 0)
    def _():
        for copy in fetch(be_ref[0]):
            copy.start()

    first = jnp.logical_or(i == 0, be_ref[i] != be_ref[jnp.maximum(i - 1, 0)])

    @pl.when(jnp.logical_and(first, i < nv_ref[0]))
    def _():
        for copy in fetch(be_ref[i]):
            copy.wait()
        wgu_b[...] = wgu_f[...].astype(BF16)
        wd_b[...] = wd_f[...].astype(BF16)

        @pl.when(next_ref[i] >= 0)
        def _():
            for copy in fetch(next_ref[i]):
                copy.start()

    @pl.when(i < nv_ref[0])
    def _():
        gu = _dot(_load_row_tiles(xs_ref, EXPERT_BLOCK).astype(BF16), wgu_b[...]) + bgu_ref[0]
        glu = jnp.minimum(gu[:, :D_FF], SWIGLU_LIMIT)
        lin = jnp.clip(gu[:, D_FF:], -SWIGLU_LIMIT, SWIGLU_LIMIT)
        act = glu * jax.nn.sigmoid(SWIGLU_ALPHA * glu) * (lin + 1.0)
        _store_row_tiles(yb_ref, _dot(act.astype(BF16), wd_b[...]) + bd_ref[0])

    @pl.when(i >= nv_ref[0])
    def _():
        yb_ref[...] = jnp.zeros_like(yb_ref)


def _experts(block_expert, n_valid, next_expert, xs, w_gu, b_gu, w_d, b_d):
    rows = EXPERT_BLOCK * ROW_TILES
    nb = xs.shape[0] // rows
    return pl.pallas_call(
        _expert_kernel,
        out_shape=jax.ShapeDtypeStruct(xs.shape, F32),
        grid_spec=pltpu.PrefetchScalarGridSpec(
            num_scalar_prefetch=3,
            grid=(nb,),
            in_specs=[pl.BlockSpec((rows, LANES), lambda i, be, nv, nx: (jnp.minimum(i, nv[0] - 1), 0)),
                      pl.BlockSpec((1, 1, 2 * D_FF), lambda i, be, nv, nx: (be[i], 0, 0)),
                      pl.BlockSpec((1, 1, D_MODEL), lambda i, be, nv, nx: (be[i], 0, 0)),
                      pl.BlockSpec(memory_space=pl.ANY),
                      pl.BlockSpec(memory_space=pl.ANY)],
            out_specs=pl.BlockSpec((rows, LANES), lambda i, be, nv, nx: (i, 0)),
            scratch_shapes=[pltpu.VMEM((D_MODEL, 2 * D_FF), F32), pltpu.VMEM((D_FF, D_MODEL), F32),
                            pltpu.VMEM((D_MODEL, 2 * D_FF), BF16), pltpu.VMEM((D_FF, D_MODEL), BF16),
                            pltpu.SemaphoreType.DMA((2,))],
        ),
        compiler_params=_params(("arbitrary",)),
        name="experts",
    )(block_expert, n_valid, next_expert, xs, b_gu, b_d, w_gu, w_d)


def _expert_layout(counts, n_blocks):
    counts = counts.astype(jnp.int32)
    padded = (counts + EXPERT_BLOCK - 1) // EXPERT_BLOCK * EXPERT_BLOCK
    pends = jnp.cumsum(padded)
    pstart = pends - padded
    block_start = jnp.arange(n_blocks, dtype=jnp.int32) * EXPERT_BLOCK
    block_expert = jnp.minimum(jnp.sum(pends[None, :] <= block_start[:, None], axis=1), N_EXPERTS - 1)
    n_valid = (pends[-1:] // EXPERT_BLOCK).astype(jnp.int32)
    seg_end = jnp.sum(jnp.where(block_expert[:, None] == jnp.arange(N_EXPERTS), pends[None, :], 0), axis=1)
    after = seg_end // EXPERT_BLOCK
    next_expert = jnp.where(after < n_valid[0], block_expert[jnp.minimum(after, n_blocks - 1)], -1)
    return (pstart.astype(jnp.int32), pends.astype(jnp.int32), block_expert.astype(jnp.int32), n_valid,
            next_expert.astype(jnp.int32))


def _permute_w_in(w_in):
    sizes = [RET_QK, RET_QK, RET_V, RET_V, SSM_INNER, CONV_DIM, SSM_HEADS, D_MODEL, D_MODEL]
    offs = [0]
    for s in sizes:
        offs.append(offs[-1] + s)
    q, k, v, g_ret, z, xbc, dt, g_a, g_b = [w_in[:, offs[i]:offs[i + 1]] for i in range(len(sizes))]
    gn = SSM_GROUPS * SSM_DSTATE
    xs, bm, cm = xbc[:, :SSM_INNER], xbc[:, SSM_INNER:SSM_INNER + gn], xbc[:, SSM_INNER + gn:]
    dt_pad = jnp.pad(dt, ((0, 0), (0, _ColsB.V - _ColsB.DT - SSM_HEADS)))
    tail = jnp.zeros((D_MODEL, _ColsB.PADDED - _ColsB.TOTAL), w_in.dtype)
    return jnp.concatenate([g_a, g_b, z, xs, bm, cm, dt_pad, v, g_ret, q, k, tail], axis=1).astype(BF16)


def _forward(x_prompt, x_sample, state_ret, state_ssm, state_conv, norm_mix, w_in, ret_norm, w_out_ret,
             conv_w, conv_b, dt_bias, a_log, d_skip, ssm_norm, w_out_ssm, w_out, norm_ffn,
             w_router, b_router, w_gate_up, b_gate_up, w_down, b_down, norm_final):
    bp, lp, _ = x_prompt.shape
    bs, ls, _ = x_sample.shape
    assert lp % ROWS == 0 and bs % SAMPLE_PER_TILE == 0 and ls * SAMPLE_PER_TILE == ROWS
    tp, ts = bp * lp, bs * ls
    t_all = tp + ts
    xp, xs_in = x_prompt.reshape(tp, D_MODEL), x_sample.reshape(ts, D_MODEL)

    w_b = _permute_w_in(w_in[0])
    norm_w = norm_mix[0].reshape(1, D_MODEL)
    proj_s = _inproj(xs_in, norm_w, w_b)

    sp = _ssd_params(conv_w[0], conv_b[0], dt_bias[0], a_log[0], d_skip[0], ssm_norm[0])
    y_p, ssm_p, conv_p, a_p, ret_p, ga_p, gb_p = _mixer_prompt(xp, norm_w, w_b, bp, lp, sp, ret_norm[0])
    y_s, ssm_s, conv_s, a_s, ret_s = _mixer_sample(proj_s, bs, ls, sp, state_ssm[0], state_conv[0],
                                                   state_ret[0], ret_norm[0])

    w_router_pad = jnp.pad(w_router[0], ((0, 0), (0, LANES - N_EXPERTS))).astype(BF16)
    b_router_pad = jnp.pad(b_router[0], (0, LANES - N_EXPERTS)).reshape(1, LANES)
    x1, h2, logits = _post(a_p, a_s, y_p, y_s, ga_p, gb_p, proj_s, xp, xs_in, w_out_ret[0].astype(BF16), w_out_ssm[0].astype(BF16),
                           w_out[0].astype(BF16), norm_ffn[0].reshape(1, D_MODEL), w_router_pad, b_router_pad)

    tm_moe = _pick(math.gcd(tp, ts), 512)
    srow, gates, tmeta, counts = _route(logits, tm_moe)
    n_blocks = -(-(t_all * TOP_K + N_EXPERTS * (EXPERT_BLOCK - 1)) // EXPERT_BLOCK)
    pstart, pends, block_expert, n_valid, next_expert = _expert_layout(counts[0, :N_EXPERTS], n_blocks)
    tmeta = tmeta.astype(jnp.int32)
    run_start = tmeta[:, :1, :] + jnp.pad(pstart, (0, LANES - N_EXPERTS))
    tmeta = jnp.concatenate([run_start, tmeta[:, 1:, :]], axis=1)
    srow, gates = srow.reshape(-1), gates.reshape(-1)
    xs = _dispatch(pstart, pends, srow, tmeta, h2, n_blocks * EXPERT_BLOCK, tm_moe)
    yb = _experts(block_expert, n_valid, next_expert, xs, w_gate_up[0],
                  b_gate_up[0].reshape(N_EXPERTS, 1, 2 * D_FF), w_down[0], b_down[0].reshape(N_EXPERTS, 1, D_MODEL))
    out_p, out_s = _combine(srow, gates, tmeta, x1, norm_final.reshape(1, D_MODEL), yb, tp, tm_moe)

    shape_s = (1, bs, SSM_HEADS, SSM_HEADDIM, SSM_DSTATE)
    shape_p = (1, bp, SSM_HEADS, SSM_HEADDIM, SSM_DSTATE)
    return (out_p.reshape(bp, lp, D_MODEL), out_s.reshape(bs, ls, D_MODEL),
            ret_p[None], ret_s[None], ssm_p.reshape(shape_p), ssm_s.reshape(shape_s),
            conv_p[None], conv_s[None])


def kernel(x_prompt, x_sample, state_ret, state_ssm, state_conv, norm_mix, w_in, ret_norm, w_out_ret, conv_w, conv_b, dt_bias, a_log, d_skip, ssm_norm, w_out_ssm, w_out, norm_ffn, w_router, b_router, w_gate_up, b_gate_up, w_down, b_down, norm_final):
    return _forward(x_prompt, x_sample, state_ret, state_ssm, state_conv, norm_mix, w_in, ret_norm, w_out_ret,
                    conv_w, conv_b, dt_bias, a_log, d_skip, ssm_norm, w_out_ssm, w_out, norm_ffn,
                    w_router, b_router, w_gate_up, b_gate_up, w_down, b_down, norm_final)
```

```python
import functools
import math

import jax
import jax.numpy as jnp
from jax import lax
from jax.experimental import pallas as pl
from jax.experimental.pallas import tpu as pltpu

F32 = jnp.float32
BF16 = jnp.bfloat16

D_MODEL = 1024
PAST_LEN = 16384
RET_HEADS = 4
RET_DK = 128
RET_DV = 256
RET_QK = RET_HEADS * RET_DK
RET_V = RET_HEADS * RET_DV
ROPE_BASE = 10000.0
SSM_INNER = 2 * D_MODEL
SSM_HEADDIM = 64
SSM_HEADS = SSM_INNER // SSM_HEADDIM
SSM_GROUPS = 4
SSM_DSTATE = 128
HEADS_PER_GROUP = SSM_HEADS // SSM_GROUPS
GROUP_INNER = SSM_INNER // SSM_GROUPS
CONV_WIDTH = 4
CONV_DIM = SSM_INNER + 2 * SSM_GROUPS * SSM_DSTATE
CHUNK = 128
N_EXPERTS = 32
TOP_K = 4
D_FF = D_MODEL
SWIGLU_LIMIT = 7.0
SWIGLU_ALPHA = 1.702
EPS = 1e-6

LANES = 128
SUBLANES = 8
ROWS = 128
SAMPLE_PER_TILE = 16
EXPERT_BLOCK = 512
ROW_TILES = D_MODEL // LANES
ISSUE_UNROLL = 8
STRIP_BITS = 5
STRIP = 1 << STRIP_BITS
VMEM_LIMIT = 56 * 1024 * 1024


class _ColsB:
    GA = 0
    GB = GA + D_MODEL
    Z = GB + D_MODEL
    XS = Z + SSM_INNER
    B = XS + SSM_INNER
    C = B + SSM_GROUPS * SSM_DSTATE
    DT = C + SSM_GROUPS * SSM_DSTATE
    V = DT + 2 * LANES
    GRET = V + RET_V
    Q = GRET + RET_V
    K = Q + RET_QK
    TOTAL = K + RET_QK
    PADDED = -(-TOTAL // (6 * LANES)) * 6 * LANES


def _pick(n, target, step=LANES):
    best = None
    for c in range(step, target + 1, step):
        if n % c == 0:
            best = c
    assert best is not None, (n, target)
    return best


def _params(sem, **kw):
    return pltpu.CompilerParams(dimension_semantics=sem, vmem_limit_bytes=VMEM_LIMIT, **kw)


def _dot(a, b):
    return jnp.dot(a, b, preferred_element_type=F32)


def _dot_nt(a, b):
    return lax.dot_general(a, b, (((1,), (1,)), ((), ())), preferred_element_type=F32)


def _dot_tn(a, b):
    return lax.dot_general(a, b, (((0,), (0,)), ((), ())), preferred_element_type=F32)


def _silu(x):
    return x * jax.nn.sigmoid(x)


def _store_row_tiles(ref, value, *lead):
    n = value.shape[0]
    for s in range(ROW_TILES):
        ref[(*lead, pl.ds(s, n, stride=ROW_TILES), slice(None))] = value[:, s * LANES:(s + 1) * LANES]


def _load_row_tiles(ref, n, *lead):
    return jnp.concatenate([ref[(*lead, pl.ds(s, n, stride=ROW_TILES), slice(None))] for s in range(ROW_TILES)],
                           axis=1)


def _inproj_kernel(x_ref, nw_ref, w_ref, o_ref, h_ref):
    @pl.when(pl.program_id(1) == 0)
    def _():
        x = x_ref[...]
        ms = jnp.mean(x * x, axis=-1, keepdims=True)
        h_ref[...] = (x * lax.rsqrt(ms + EPS) * nw_ref[...]).astype(BF16)

    o_ref[...] = _dot(h_ref[...], w_ref[...])


def _split_rows(tm, npt, width, sample_col=0):
    prompt = pl.BlockSpec((tm, width), lambda i, *_: (jnp.minimum(i, npt - 1), 0))
    sample = pl.BlockSpec((tm, width), lambda i, *_: (jnp.maximum(i - npt, 0), sample_col))
    return prompt, sample


def _inproj(x, norm_w, w):
    t, total = x.shape[0], w.shape[1]
    tm = _pick(t, 1024)
    tn = _pick(total, 2560, step=2 * LANES)
    return pl.pallas_call(
        _inproj_kernel,
        out_shape=jax.ShapeDtypeStruct((t, total), F32),
        grid=(t // tm, total // tn),
        in_specs=[
            pl.BlockSpec((tm, D_MODEL), lambda i, j: (i, 0)),
            pl.BlockSpec((1, D_MODEL), lambda i, j: (0, 0)),
            pl.BlockSpec((D_MODEL, tn), lambda i, j: (0, j)),
        ],
        out_specs=pl.BlockSpec((tm, tn), lambda i, j: (i, j)),
        scratch_shapes=[pltpu.VMEM((tm, D_MODEL), BF16)],
        compiler_params=_params(("parallel", "arbitrary")),
        name="in_proj",
    )(x, norm_w, w)


def _ret_log_decay():
    return jnp.log(1.0 - 2.0 ** (-5.0 - jnp.arange(RET_HEADS, dtype=F32)))


def _ret_tables(c):
    lg = _ret_log_decay()
    r = jnp.arange(ROWS)
    t = (r % c).astype(F32)
    seg = r // c
    diff = t[:, None] - t[None, :]
    ok = (seg[:, None] == seg[None, :]) & (diff >= 0)
    dm = jnp.where(ok[None], jnp.exp(lg[:, None, None] * jnp.maximum(diff, 0.0)[None]), 0.0)
    qd = jnp.exp(lg[:, None] * (t[None, :] + 1.0))
    kd = jnp.exp(lg[:, None] * (c - 1.0 - t[None, :]))
    cd = jnp.exp(lg * c)
    qd = jnp.broadcast_to(qd[:, :, None], (RET_HEADS, ROWS, LANES))
    kd = jnp.broadcast_to(kd[:, :, None], (RET_HEADS, ROWS, LANES))
    cd = jnp.broadcast_to(cd[:, None, None], (RET_HEADS, 1, RET_DV))
    return dm, qd, kd, cd


def _rope_tables(pos):
    half = RET_DK // 2
    inv_freq = 1.0 / (ROPE_BASE ** jnp.linspace(0.0, 1.0, half, dtype=F32))
    ang = pos.astype(F32)[:, None] * inv_freq[None, :]
    cos, sin = jnp.cos(ang), jnp.sin(ang)
    return jnp.concatenate([cos, cos], -1), jnp.concatenate([-sin, sin], -1)


def _block_mask(dtype):
    r = jnp.arange(ROWS) // (ROWS // SAMPLE_PER_TILE)
    b = jnp.arange(SAMPLE_PER_TILE * LANES) // LANES
    return (r[:, None] == b[None, :]).astype(dtype)


def _ret_common(q, k, v, g, cosf, sinf, dm, qd, kd, nw):
    qr = q * cosf + pltpu.roll(q, RET_DK // 2, 1) * sinf
    kr = (k * cosf + pltpu.roll(k, RET_DK // 2, 1) * sinf) * (RET_DK ** -0.5)
    vb = v.astype(BF16)
    scores = _dot_nt(qr.astype(BF16), kr.astype(BF16)) * dm
    o_intra = _dot(scores.astype(BF16), vb)
    q_state = (qr * qd).astype(BF16)
    k_state = kr * kd

    def finish(o):
        ms = jnp.mean(o * o, axis=-1, keepdims=True)
        on = o * lax.rsqrt(ms + EPS) * nw
        return (_silu(g) * on).astype(BF16)

    return o_intra, q_state, k_state, vb, finish


def _ret_prompt_step(q_ref, k_ref, v_ref, g_ref, cos_ref, sin_ref, dm_ref, qd_ref, kd_ref, cd_ref, nw_ref,
                     a_ref, s_ref):
    cosf, sinf = cos_ref[...], sin_ref[...]
    for h in range(RET_HEADS):
        dk = slice(h * RET_DK, (h + 1) * RET_DK)
        dv = slice(h * RET_DV, (h + 1) * RET_DV)
        o_intra, q_state, k_state, vb, finish = _ret_common(
            q_ref[:, dk], k_ref[:, dk], v_ref[:, dv], g_ref[:, dv], cosf, sinf, dm_ref[h], qd_ref[h], kd_ref[h],
            nw_ref[h])
        s = s_ref[0, h]
        a_ref[:, dv] = finish(o_intra + _dot(q_state, s.astype(BF16)))
        s_ref[0, h] = s * cd_ref[h] + _dot_tn(k_state.astype(BF16), vb)


def _ret_sample_kernel(q_ref, k_ref, v_ref, g_ref, cos_ref, sin_ref, dm_ref, qd_ref, kd_ref, cd_ref, nw_ref,
                       bm_ref, bmt_ref, s0_ref, a_ref, s_ref):
    o_intra, q_state, k_state, vb, finish = _ret_common(
        q_ref[...], k_ref[...], v_ref[...], g_ref[...], cos_ref[...], sin_ref[...], dm_ref[0], qd_ref[0], kd_ref[0],
        nw_ref[0])
    nb = SAMPLE_PER_TILE
    s0 = s0_ref[:, 0]
    q_bd = jnp.concatenate([q_state] * nb, axis=1) * bm_ref[...]
    a_ref[...] = finish(o_intra + _dot(q_bd, s0.reshape(nb * RET_DK, RET_DV).astype(BF16)))
    k_t = k_state.T.astype(BF16)
    k_bd_t = jnp.concatenate([k_t] * nb, axis=0) * bmt_ref[...]
    ds = _dot(k_bd_t, vb).reshape(nb, RET_DK, RET_DV)
    s_ref[:, 0] = s0 * cd_ref[0] + ds


def _ret_specs(row_block):
    def at(width, off):
        return lambda *ids: (row_block(*ids)[0], off // width + row_block(*ids)[1])

    def head(*shape):
        return pl.BlockSpec((1,) + shape, lambda *ids: (row_block(*ids)[1],) + (0,) * len(shape))

    return [
        pl.BlockSpec((ROWS, RET_DK), at(RET_DK, _ColsB.Q)),
        pl.BlockSpec((ROWS, RET_DK), at(RET_DK, _ColsB.K)),
        pl.BlockSpec((ROWS, RET_DV), at(RET_DV, _ColsB.V)),
        pl.BlockSpec((ROWS, RET_DV), at(RET_DV, _ColsB.GRET)),
    ], head


def _retention_sample_call(proj, row0, bsz, seq, state, ret_norm):
    rb0 = row0 // ROWS
    dm, qd, kd, cd = _ret_tables(seq)
    pos = PAST_LEN + jnp.arange(seq, dtype=jnp.int32)
    cosf, sinf = _rope_tables(jnp.tile(pos, SAMPLE_PER_TILE))
    rb = lambda t, h: (rb0 + t, h)
    specs, head = _ret_specs(rb)
    full = lambda *shape: pl.BlockSpec(shape, lambda t, h: (0,) * len(shape))
    specs += [
        full(ROWS, RET_DK), full(ROWS, RET_DK),
        head(ROWS, ROWS), head(ROWS, LANES), head(ROWS, LANES), head(1, RET_DV), head(1, RET_DV),
        full(ROWS, SAMPLE_PER_TILE * LANES), full(SAMPLE_PER_TILE * LANES, ROWS),
        pl.BlockSpec((SAMPLE_PER_TILE, 1, RET_DK, RET_DV), lambda t, h: (t, h, 0, 0)),
    ]
    bm = _block_mask(BF16)
    operands = (proj, proj, proj, proj, cosf, sinf, dm, qd, kd, cd, ret_norm.reshape(RET_HEADS, 1, RET_DV),
                bm, bm.T, state)
    out_shapes = (jax.ShapeDtypeStruct((bsz * seq, RET_V), BF16), jax.ShapeDtypeStruct(state.shape, F32))
    out_specs = (pl.BlockSpec((ROWS, RET_DV), lambda t, h: (t, h)),
                 pl.BlockSpec((SAMPLE_PER_TILE, 1, RET_DK, RET_DV), lambda t, h: (t, h, 0, 0)))
    return specs, operands, out_shapes, out_specs


def _softplus(x):
    return jnp.maximum(x, 0.0) + jnp.log1p(jnp.exp(-jnp.abs(x)))


def _conv_piece(x, pred, w_ref, b_ref):
    width = x.shape[1]
    x3 = x.reshape(ROWS // SUBLANES, SUBLANES, width)
    t8 = lax.broadcasted_iota(jnp.int32, x3.shape, 1)
    acc = b_ref[...].reshape(1, 1, width)
    for i in range(CONV_WIDTH):
        s = CONV_WIDTH - 1 - i
        tap = x3 if s == 0 else pltpu.roll(jnp.where(t8 >= SUBLANES - s, pred, x3), s, 1)
        acc = acc + tap * w_ref[pl.ds(i, 1), :].reshape(1, 1, width)
    return _silu(acc).reshape(ROWS, width)


def _ssd_decay_terms(c, dt_pre, a_log, ltri):
    dt = _softplus(dt_pre)
    d_a = dt * (-jnp.exp(a_log))
    hi = d_a.astype(BF16)
    r1 = d_a - hi.astype(F32)
    mid = r1.astype(BF16)
    lo = (r1 - mid.astype(F32)).astype(BF16)
    cum = _dot(ltri, hi) + _dot(ltri, mid) + _dot(ltri, lo)
    cum3 = cum.reshape(ROWS // c, c, LANES)
    c_last = jnp.broadcast_to(cum3[:, c - 1:c, :], cum3.shape).reshape(ROWS, LANES)
    to_end = jnp.exp(c_last - cum) * dt
    e_cum = jnp.exp(cum)
    return cum, to_end, e_cum, cum.T, dt.T


def _ssd_group(c, terms, head0, z_ref, dsk_ref, nrm_ref, xs, bm, cm, y_ref, state_io, zero=None):
    cum, to_end, e_cum, cum_t, dt_t = terms
    b_b, c_b = bm.astype(BF16), cm.astype(BF16)
    cb = _dot_nt(c_b, b_b)
    ri = lax.broadcasted_iota(jnp.int32, (ROWS, ROWS), 0)
    ci = lax.broadcasted_iota(jnp.int32, (ROWS, ROWS), 1)
    shift = c.bit_length() - 1
    causal = (ci <= ri) & (jnp.right_shift(ri, shift) == jnp.right_shift(ci, shift))
    lo_half = lax.broadcasted_iota(jnp.int32, (ROWS, LANES), 1) < SSM_HEADDIM

    def col(a, h):
        return jnp.broadcast_to(a[:, h:h + 1], (ROWS, LANES))

    def weights(h):
        seg = col(cum, h) - jnp.broadcast_to(cum_t[h:h + 1, :], (ROWS, ROWS))
        decay = jnp.exp(jnp.where(causal, seg, -jnp.inf))
        return (cb * decay * jnp.broadcast_to(dt_t[h:h + 1, :], (ROWS, ROWS))).astype(BF16)

    pieces = []
    for m in range(HEADS_PER_GROUP // 2):
        ha, hb = head0 + 2 * m, head0 + 2 * m + 1
        xp = xs[:, m * LANES:(m + 1) * LANES]
        w2 = jnp.concatenate([weights(ha), weights(hb)], axis=1)
        x2 = jnp.concatenate([jnp.where(lo_half, xp, 0.0), jnp.where(lo_half, 0.0, xp)], axis=0).astype(BF16)
        y = _dot(w2, x2)
        xw = (xp * jnp.where(lo_half, col(to_end, ha), col(to_end, hb))).astype(BF16)
        y_state = state_io(m, ha, c_b, xw, b_b, e_cum)
        y = y + y_state * jnp.where(lo_half, col(e_cum, ha), col(e_cum, hb))
        pieces.append(y + xp * dsk_ref[:, m * LANES:(m + 1) * LANES])
    if zero is not None:
        pieces[0] = jnp.concatenate([pieces[0][:SUBLANES] + zero, pieces[0][SUBLANES:]], axis=0)
    yg = jnp.concatenate(pieces, axis=1) * _silu(z_ref[...])
    ms = jnp.mean(yg * yg, axis=-1, keepdims=True)
    y_ref[...] = (yg * lax.rsqrt(ms + EPS) * nrm_ref[...]).astype(BF16)


def _pair_decay(e_cum, row, ha):
    top = lax.broadcasted_iota(jnp.int32, (2 * SSM_HEADDIM, SSM_DSTATE), 0) < SSM_HEADDIM
    ea = jnp.broadcast_to(e_cum[row:row + 1, ha:ha + 1], top.shape)
    eb = jnp.broadcast_to(e_cum[row:row + 1, ha + 1:ha + 2], top.shape)
    return jnp.where(top, ea, eb)


def _mixer_prompt_kernel(x0_ref, xn_ref, nw_ref, w_ref, cwx_ref, cbx_ref, cwb_ref, cbb_ref, cwc_ref, cbc_ref,
                         dtb_ref, alog_ref, dsk_ref, nrm_ref, ltri_ref, *rest):
    ret_in, (y_ref, hs_ref, px_ref, pb_ref, pc_ref, a_ref, s_ref, ga_ref, gb_ref, proj_ref) = rest[:7], rest[7:]
    c = pl.program_id(1)
    step_no = pl.program_id(0) * pl.num_programs(1) + c

    def normed(x_ref):
        x = x_ref[...]
        ms = jnp.mean(x * x, axis=-1, keepdims=True)
        return (x * lax.rsqrt(ms + EPS) * nw_ref[...]).astype(BF16)

    def project(x_ref, slot):
        proj_ref[slot] = _dot(normed(x_ref), w_ref[:, :_ColsB.TOTAL])

    @pl.when(c == 0)
    def _():
        hs_ref[...] = jnp.zeros_like(hs_ref)
        px_ref[...] = jnp.zeros_like(px_ref)
        pb_ref[...] = jnp.zeros_like(pb_ref)
        pc_ref[...] = jnp.zeros_like(pc_ref)
        s_ref[...] = jnp.zeros_like(s_ref)

    @pl.when(step_no == 0)
    def _():
        project(x0_ref, 0)

    def conv(x_ref, p_ref, w_ref, bias_ref):
        x = x_ref[...]
        x3 = x.reshape(ROWS // SUBLANES, SUBLANES, x.shape[1])
        pred = jnp.concatenate([p_ref[...], x3[:-1]], axis=0)
        out = _conv_piece(x, pred, w_ref, bias_ref)
        p_ref[0] = x3[-1]
        return out

    def step(cur_slot, next_slot):
        h_next = normed(xn_ref)
        bounds = [0, 10 * 256, 20 * 256, 30 * 256, _ColsB.TOTAL]

        def project_slice(g):
            lo, hi = bounds[g], bounds[g + 1]
            part = _dot(h_next, w_ref[:, lo:hi])
            proj_ref[next_slot, :, lo:hi] = part
            bits = pltpu.bitcast(part[ROWS - SUBLANES:, hi - lo - LANES:], jnp.uint32)
            sixteen = jnp.uint32(16)
            return pltpu.bitcast(lax.shift_right_logical(lax.shift_right_logical(bits, sixteen), sixteen), F32)

        cur = proj_ref.at[cur_slot]
        gn = SSM_GROUPS * SSM_DSTATE
        _ret_prompt_step(cur.at[:, _ColsB.Q:_ColsB.Q + RET_QK], cur.at[:, _ColsB.K:_ColsB.K + RET_QK],
                         cur.at[:, _ColsB.V:_ColsB.V + RET_V], cur.at[:, _ColsB.GRET:_ColsB.GRET + RET_V],
                         *ret_in, a_ref, s_ref)
        z_ref = cur.at[:, _ColsB.Z:_ColsB.Z + SSM_INNER]
        xs_ref = cur.at[:, _ColsB.XS:_ColsB.XS + SSM_INNER]
        b_ref = cur.at[:, _ColsB.B:_ColsB.B + gn]
        c_ref = cur.at[:, _ColsB.C:_ColsB.C + gn]
        dt_ref = cur.at[:, _ColsB.DT:_ColsB.DT + LANES]
        ga_ref[...] = cur[:, _ColsB.GA:_ColsB.GA + D_MODEL]
        gb_ref[...] = cur[:, _ColsB.GB:_ColsB.GB + D_MODEL]
        terms = _ssd_decay_terms(CHUNK, dt_ref[...] + dtb_ref[...], alog_ref[...], ltri_ref[...])
        for g in range(SSM_GROUPS):
            def cols(ref, width, g=g):
                return ref.at[..., g * width:(g + 1) * width]

            gi, n = GROUP_INNER, SSM_DSTATE
            xs = conv(cols(xs_ref, gi), cols(px_ref, gi), cols(cwx_ref, gi), cols(cbx_ref, gi))
            bm = conv(cols(b_ref, n), cols(pb_ref, n), cols(cwb_ref, n), cols(cbb_ref, n))
            cm = conv(cols(c_ref, n), cols(pc_ref, n), cols(cwc_ref, n), cols(cbc_ref, n))
            hs_g = hs_ref.at[0, g * gi:(g + 1) * gi, :]

            def state_io(m, ha, c_b, xw, b_b, e_cum, hs_g=hs_g):
                rows = pl.ds(m * 2 * SSM_HEADDIM, 2 * SSM_HEADDIM)
                h = hs_g[rows, :]
                hs_g[rows, :] = h * _pair_decay(e_cum, ROWS - 1, ha) + _dot_tn(xw, b_b)
                return _dot_nt(c_b, h.astype(BF16))

            _ssd_group(CHUNK, terms, g * HEADS_PER_GROUP, cols(z_ref, gi), cols(dsk_ref, gi), cols(nrm_ref, gi),
                       xs, bm, cm, cols(y_ref, gi), state_io, zero=project_slice(g))

    @pl.when(step_no % 2 == 0)
    def _():
        step(0, 1)

    @pl.when(step_no % 2 == 1)
    def _():
        step(1, 0)


def _ssd_sample_kernel(seq, z_ref, xs_ref, b_ref, c_ref, dt_ref, cwx_ref, cbx_ref, cwb_ref, cbb_ref, cwc_ref,
                       cbc_ref, dtb_ref, alog_ref, dsk_ref, nrm_ref, ltri_ref, bmask_ref, px_ref, pb_ref, pc_ref,
                       h0_ref, y_ref, hs_ref, csx_ref, csb_ref, csc_ref):
    nb = SAMPLE_PER_TILE

    def conv(x_ref, p_ref, w_ref, bias_ref, cs_ref):
        x = x_ref[...]
        cs_ref[...] = x.reshape(nb, seq, x.shape[1])
        return _conv_piece(x, p_ref[...], w_ref, bias_ref)

    xs = conv(xs_ref, px_ref, cwx_ref, cbx_ref, csx_ref)
    bm = conv(b_ref, pb_ref, cwb_ref, cbb_ref, csb_ref)
    cm = conv(c_ref, pc_ref, cwc_ref, cbc_ref, csc_ref)
    bmask = bmask_ref[...]

    def state_io(m, ha, c_b, xw, b_b, e_cum):
        rows = pl.ds(m * 2 * SSM_HEADDIM, 2 * SSM_HEADDIM)
        c_bd = jnp.concatenate([c_b] * nb, axis=1) * bmask
        b_bd = jnp.concatenate([b_b] * nb, axis=1) * bmask
        hs = [h0_ref[i, rows, :] for i in range(nb)]
        h_cat = jnp.concatenate(hs, axis=1).astype(BF16)
        dh = _dot_tn(xw, b_bd)
        for i in range(nb):
            decay = _pair_decay(e_cum, i * seq + seq - 1, ha)
            hs_ref[i, rows, :] = hs[i] * decay + dh[:, i * SSM_DSTATE:(i + 1) * SSM_DSTATE]
        return _dot_nt(c_bd, h_cat)

    shift = jnp.bitwise_and(LANES - HEADS_PER_GROUP * pl.program_id(1), LANES - 1)
    dt_pre = pltpu.roll(dt_ref[...] + dtb_ref[...], shift, 1)
    a_log = pltpu.roll(jnp.broadcast_to(alog_ref[...], (SUBLANES, LANES)), shift, 1)[:1]
    terms = _ssd_decay_terms(seq, dt_pre, a_log, ltri_ref[...])
    _ssd_group(seq, terms, 0, z_ref, dsk_ref, nrm_ref, xs, bm, cm, y_ref, state_io)


def _ssd_tables(c):
    r = jnp.arange(ROWS)
    ltri = ((r[:, None] >= r[None, :]) & ((r[:, None] // c) == (r[None, :] // c))).astype(BF16)
    return ltri


def _ssd_specs(row_block):
    def at(width, off):
        return lambda *ids: (row_block(*ids)[0], off // width + row_block(*ids)[1])

    def grp(rows, width):
        return pl.BlockSpec((rows, width), lambda *ids: (0, row_block(*ids)[1]))

    n = SSM_DSTATE
    return [
        pl.BlockSpec((ROWS, GROUP_INNER), at(GROUP_INNER, _ColsB.Z)),
        pl.BlockSpec((ROWS, GROUP_INNER), at(GROUP_INNER, _ColsB.XS)),
        pl.BlockSpec((ROWS, n), at(n, _ColsB.B)),
        pl.BlockSpec((ROWS, n), at(n, _ColsB.C)),
        pl.BlockSpec((ROWS, LANES), lambda *ids: (row_block(*ids)[0], _ColsB.DT // LANES)),
        grp(CONV_WIDTH, GROUP_INNER), grp(1, GROUP_INNER),
        grp(CONV_WIDTH, n), grp(1, n), grp(CONV_WIDTH, n), grp(1, n),
        pl.BlockSpec((1, LANES), lambda *ids: (0, 0)), pl.BlockSpec((1, LANES), lambda *ids: (0, 0)),
        grp(1, GROUP_INNER), grp(1, GROUP_INNER),
    ]


def _ssd_params(conv_w, conv_b, dt_bias, a_log, d_skip, ssm_norm):
    gn = SSM_GROUPS * SSM_DSTATE
    cwx, cwb, cwc = conv_w[:, :SSM_INNER], conv_w[:, SSM_INNER:SSM_INNER + gn], conv_w[:, SSM_INNER + gn:]
    cb = conv_b.reshape(1, CONV_DIM)
    cbx, cbb, cbc = cb[:, :SSM_INNER], cb[:, SSM_INNER:SSM_INNER + gn], cb[:, SSM_INNER + gn:]

    def head_lanes(v):
        return jnp.pad(v, (0, LANES - SSM_HEADS)).reshape(1, LANES)

    dsk = jnp.repeat(d_skip, SSM_HEADDIM).reshape(1, SSM_INNER)
    return (cwx, cbx, cwb, cbb, cwc, cbc, head_lanes(dt_bias), head_lanes(a_log), dsk,
            ssm_norm.reshape(1, SSM_INNER))


def _conv_tail(csx, csb, csc):
    keep = SUBLANES - (CONV_WIDTH - 1)
    return jnp.concatenate([csx[:, keep:], csb[:, keep:], csc[:, keep:]], axis=-1)


def _mixer_prompt(x, norm_w, w_b, bsz, seq, params, ret_norm):
    nch = seq // ROWS
    gn = SSM_GROUPS * SSM_DSTATE
    full = lambda a: pl.BlockSpec(a.shape, lambda b, c: (0,) * a.ndim)
    ltri = _ssd_tables(CHUNK)
    dm, qd, kd, cd = _ret_tables(CHUNK)
    cosf, sinf = _rope_tables(jnp.arange(seq, dtype=jnp.int32))
    rnw = ret_norm.reshape(RET_HEADS, 1, RET_DV)
    specs = [pl.BlockSpec((ROWS, D_MODEL), lambda b, c: (0, 0)),
             pl.BlockSpec((ROWS, D_MODEL), lambda b, c: (jnp.minimum(b * nch + c + 1, bsz * nch - 1), 0)),
             full(norm_w), full(w_b), *[full(p) for p in params], full(ltri),
             pl.BlockSpec((ROWS, RET_DK), lambda b, c: (c, 0)), pl.BlockSpec((ROWS, RET_DK), lambda b, c: (c, 0)),
             full(dm), full(qd), full(kd), full(cd), full(rnw)]
    tail = lambda width: pl.BlockSpec((1, SUBLANES, width), lambda b, c: (b, 0, 0))
    y, hs, csx, csb, csc, a, s, ga, gb = pl.pallas_call(
        _mixer_prompt_kernel,
        out_shape=(jax.ShapeDtypeStruct((bsz * seq, SSM_INNER), BF16),
                   jax.ShapeDtypeStruct((bsz, SSM_HEADS * SSM_HEADDIM, SSM_DSTATE), F32),
                   jax.ShapeDtypeStruct((bsz, SUBLANES, SSM_INNER), F32),
                   jax.ShapeDtypeStruct((bsz, SUBLANES, gn), F32),
                   jax.ShapeDtypeStruct((bsz, SUBLANES, gn), F32),
                   jax.ShapeDtypeStruct((bsz * seq, RET_V), BF16),
                   jax.ShapeDtypeStruct((bsz, RET_HEADS, RET_DK, RET_DV), F32),
                   jax.ShapeDtypeStruct((bsz * seq, D_MODEL), F32),
                   jax.ShapeDtypeStruct((bsz * seq, D_MODEL), F32)),
        grid=(bsz, nch),
        in_specs=specs,
        out_specs=(pl.BlockSpec((ROWS, SSM_INNER), lambda b, c: (b * nch + c, 0)),
                   pl.BlockSpec((1, SSM_HEADS * SSM_HEADDIM, SSM_DSTATE), lambda b, c: (b, 0, 0)),
                   tail(SSM_INNER), tail(gn), tail(gn),
                   pl.BlockSpec((ROWS, RET_V), lambda b, c: (b * nch + c, 0)),
                   pl.BlockSpec((1, RET_HEADS, RET_DK, RET_DV), lambda b, c: (b, 0, 0, 0)),
                   pl.BlockSpec((ROWS, D_MODEL), lambda b, c: (b * nch + c, 0)),
                   pl.BlockSpec((ROWS, D_MODEL), lambda b, c: (b * nch + c, 0))),
        scratch_shapes=[pltpu.VMEM((2, ROWS, _ColsB.TOTAL), F32)],
        compiler_params=_params(("arbitrary", "arbitrary")),
        name="mixer_prompt",
    )(x, x, norm_w, w_b, *params, ltri, cosf, sinf, dm, qd, kd, cd, rnw)
    return y, hs, _conv_tail(csx, csb, csc), a, s, ga, gb


def _mixer_sample_kernel(seq, n_ssd_in, n_ret_in, *refs):
    ssd_in, refs = refs[:n_ssd_in], refs[n_ssd_in:]
    ret_in, refs = refs[:n_ret_in], refs[n_ret_in:]
    _ssd_sample_kernel(seq, *ssd_in, *refs[:5])
    _ret_sample_kernel(*ret_in, *refs[5:])


def _mixer_sample(proj, bsz, seq, params, state, conv_state, ret_state, ret_norm):
    assert SSM_GROUPS == RET_HEADS
    ntile = bsz // SAMPLE_PER_TILE
    gn = SSM_GROUPS * SSM_DSTATE
    specs = _ssd_specs(lambda t, g: (t, g))
    gh = HEADS_PER_GROUP * SSM_HEADDIM
    cs = jnp.pad(conv_state, ((0, 0), (SUBLANES - (CONV_WIDTH - 1), 0), (0, 0)))
    csx, csb, csc = cs[:, :, :SSM_INNER], cs[:, :, SSM_INNER:SSM_INNER + gn], cs[:, :, SSM_INNER + gn:]
    pred = lambda width: pl.BlockSpec((SAMPLE_PER_TILE, SUBLANES, width), lambda t, g: (t, 0, g))
    state_spec = pl.BlockSpec((SAMPLE_PER_TILE, gh, SSM_DSTATE), lambda t, g: (t, g, 0))
    specs += [
        pl.BlockSpec((ROWS, ROWS), lambda t, g: (0, 0)),
        pl.BlockSpec((ROWS, SAMPLE_PER_TILE * LANES), lambda t, g: (0, 0)),
        pred(GROUP_INNER), pred(SSM_DSTATE), pred(SSM_DSTATE),
        state_spec,
    ]
    state2 = state.reshape(bsz, SSM_HEADS * SSM_HEADDIM, SSM_DSTATE)
    assert seq == SUBLANES
    ssd_operands = (proj, proj, proj, proj, proj, *params, _ssd_tables(seq), _block_mask(BF16), csx, csb, csc,
                    state2)
    r_specs, r_operands, r_shapes, r_out_specs = _retention_sample_call(proj, 0, bsz, seq, ret_state, ret_norm)
    y, hs, csx, csb, csc, a, s = pl.pallas_call(
        functools.partial(_mixer_sample_kernel, seq, len(ssd_operands), len(r_operands)),
        out_shape=(jax.ShapeDtypeStruct((bsz * seq, SSM_INNER), BF16), jax.ShapeDtypeStruct(state2.shape, F32),
                   jax.ShapeDtypeStruct((bsz, seq, SSM_INNER), F32),
                   jax.ShapeDtypeStruct((bsz, seq, gn), F32),
                   jax.ShapeDtypeStruct((bsz, seq, gn), F32), *r_shapes),
        grid=(ntile, SSM_GROUPS),
        in_specs=specs + r_specs,
        out_specs=(pl.BlockSpec((ROWS, GROUP_INNER), lambda t, g: (t, g)), state_spec,
                   pred(GROUP_INNER), pred(SSM_DSTATE), pred(SSM_DSTATE), *r_out_specs),
        compiler_params=_params(("parallel", "parallel")),
        name="mixer_sample",
    )(*ssd_operands, *r_operands)
    return y, hs, _conv_tail(csx, csb, csc), a, s


def _post_kernel(n_prompt_tiles, ap_ref, as_ref, yp_ref, ys_ref, gap_ref, gas_ref, gbp_ref, gbs_ref, xp_ref, xs_ref,
                 wr_ref, ws_ref, wo_ref, nw_ref, wrt_ref, brt_ref, x1_ref, h2_ref, lg_ref):
    is_prompt = pl.program_id(0) < n_prompt_tiles
    half = x1_ref.shape[0] // 2
    for r in (pl.ds(0, half), pl.ds(half, half)):
        branch_a = _dot(jnp.where(is_prompt, ap_ref[r, :], as_ref[r, :]), wr_ref[...])
        branch_b = _dot(jnp.where(is_prompt, yp_ref[r, :], ys_ref[r, :]), ws_ref[...])
        g_a = jnp.where(is_prompt, gap_ref[r, :], gas_ref[r, :])
        g_b = jnp.where(is_prompt, gbp_ref[r, :], gbs_ref[r, :])
        merged = jax.nn.sigmoid(g_a) * branch_a + jax.nn.sigmoid(g_b) * branch_b
        x1 = jnp.where(is_prompt, xp_ref[r, :], xs_ref[r, :]) + _dot(merged.astype(BF16), wo_ref[...])
        x1_ref[r, :] = x1
        ms = jnp.mean(x1 * x1, axis=-1, keepdims=True)
        h2 = x1 * lax.rsqrt(ms + EPS) * nw_ref[...]
        _store_row_tiles(h2_ref.at[pl.ds(r.start * ROW_TILES, half * ROW_TILES), :], h2)
        lg_ref[r, :] = _dot(h2.astype(BF16), wrt_ref[...]) + brt_ref[...]


def _post(a_p, a_s, y_p, y_s, ga_p, gb_p, proj_s, xp, xs, w_ret, w_ssm, w_out, norm_ffn, w_router, b_router):
    t = xp.shape[0] + xs.shape[0]
    tm = _pick(math.gcd(xp.shape[0], xs.shape[0]), 512)
    npt = xp.shape[0] // tm
    row = lambda width, blk: pl.BlockSpec((tm, width), lambda i: (i, blk))
    full = lambda *shape: pl.BlockSpec(shape, lambda i: (0,) * len(shape))
    return pl.pallas_call(
        functools.partial(_post_kernel, npt),
        out_shape=(jax.ShapeDtypeStruct((t, D_MODEL), F32), jax.ShapeDtypeStruct((t * ROW_TILES, LANES), F32),
                   jax.ShapeDtypeStruct((t, LANES), F32)),
        grid=(t // tm,),
        in_specs=[*_split_rows(tm, npt, RET_V), *_split_rows(tm, npt, SSM_INNER),
                  *_split_rows(tm, npt, D_MODEL, _ColsB.GA // D_MODEL),
                  *_split_rows(tm, npt, D_MODEL, _ColsB.GB // D_MODEL),
                  *_split_rows(tm, npt, D_MODEL),
                  full(RET_V, D_MODEL), full(SSM_INNER, D_MODEL), full(D_MODEL, D_MODEL), full(1, D_MODEL),
                  full(D_MODEL, LANES), full(1, LANES)],
        out_specs=(row(D_MODEL, 0), pl.BlockSpec((tm * ROW_TILES, LANES), lambda i: (i, 0)), row(LANES, 0)),
        compiler_params=_params(("parallel",)),
        name="post_mixer",
    )(a_p, a_s, y_p, y_s, ga_p, proj_s, gb_p, proj_s, xp, xs, w_ret, w_ssm, w_out, norm_ffn, w_router, b_router)


def _route_kernel(lg_ref, lstrict_ref, srow_ref, gate_ref, tmeta_ref, cnt_ref):
    @pl.when(pl.program_id(0) == 0)
    def _():
        cnt_ref[...] = jnp.zeros_like(cnt_ref)

    tm = lg_ref.shape[0]
    lane = lax.broadcasted_iota(jnp.int32, (tm, LANES), 1)
    lane_f = lane.astype(F32)
    cur = jnp.where(lane < N_EXPERTS, lg_ref[...], -jnp.inf)
    vals, hots = [], []
    for _ in range(TOP_K):
        m = jnp.max(cur, axis=1, keepdims=True)
        idx = jnp.min(jnp.where(cur == m, lane_f, float(LANES)), axis=1, keepdims=True)
        hot = lane_f == idx
        vals.append(m)
        hots.append(hot)
        cur = jnp.where(hot, -jnp.inf, cur)
    exps = [jnp.exp(v - vals[0]) for v in vals]
    denom = exps[0] + exps[1] + exps[2] + exps[3]
    sel = hots[0] | hots[1] | hots[2] | hots[3]
    self32 = sel.astype(F32)
    base = cnt_ref[...]
    num = jnp.sum(self32, axis=0, keepdims=True)
    rank = _dot(lstrict_ref[...], self32.astype(BF16))
    cnt_ref[...] = base + num
    lane8 = lax.broadcasted_iota(jnp.int32, (SUBLANES, LANES), 1)
    incl = jnp.broadcast_to(num, (SUBLANES, LANES))
    shift = 1
    while shift < N_EXPERTS:
        incl = incl + jnp.where(lane8 >= shift, pltpu.roll(incl, shift, 1), 0.0)
        shift *= 2
    off = incl[:1] - num
    slot = rank + off
    meta = jnp.zeros((tm, LANES), F32)
    for k in range(TOP_K):
        s_k = jnp.sum(jnp.where(hots[k], slot, 0.0), axis=1, keepdims=True)
        meta = jnp.where(lane == k, s_k, meta)
        meta = jnp.where(lane == SUBLANES + k, exps[k] / denom, meta)
    meta_t = meta.T
    srow_ref[0] = (meta_t[:SUBLANES] * float(ROW_TILES)).astype(jnp.int32)
    gate_ref[0] = meta_t[SUBLANES:2 * SUBLANES]
    row8 = lax.broadcasted_iota(jnp.int32, (SUBLANES, LANES), 0)
    tmeta_ref[0] = jnp.where(row8 == 0, base, jnp.where(row8 == 1, num, jnp.where(row8 == 2, off, 0.0)))


def _route(logits, tm):
    t = logits.shape[0]
    r = jnp.arange(tm)
    lstrict = (r[:, None] > r[None, :]).astype(BF16)
    return pl.pallas_call(
        _route_kernel,
        out_shape=(jax.ShapeDtypeStruct((t // tm, SUBLANES, tm), jnp.int32),
                   jax.ShapeDtypeStruct((t // tm, SUBLANES, tm), F32),
                   jax.ShapeDtypeStruct((t // tm, SUBLANES, LANES), F32),
                   jax.ShapeDtypeStruct((1, LANES), F32)),
        grid=(t // tm,),
        in_specs=[pl.BlockSpec((tm, LANES), lambda i: (i, 0)), pl.BlockSpec((tm, tm), lambda i: (0, 0))],
        out_specs=(pl.BlockSpec((1, SUBLANES, tm), lambda i: (i, 0, 0)),
                   pl.BlockSpec((1, SUBLANES, tm), lambda i: (i, 0, 0)),
                   pl.BlockSpec((1, SUBLANES, LANES), lambda i: (i, 0, 0)),
                   pl.BlockSpec((1, LANES), lambda i: (0, 0))),
        compiler_params=_params(("arbitrary",)),
        name="route",
    )(logits, lstrict)


def _for_each_strip(tmeta_ref, tm, make_copy, action):
    del tm

    def per_expert(e, carry):
        n, hbm0, tile0 = tmeta_ref[0, 1, e], tmeta_ref[0, 0, e], tmeta_ref[0, 2, e]
        n_chunks = lax.shift_right_logical(n, STRIP_BITS)

        def chunk(c, inner):
            action(make_copy(hbm0 + c * STRIP, tile0 + c * STRIP, STRIP))
            return inner

        lax.fori_loop(0, n_chunks, chunk, 0)
        for bit in reversed(range(STRIP_BITS)):
            size = 1 << bit

            @pl.when(jnp.bitwise_and(n, size) != 0)
            def _():
                done = jnp.bitwise_and(n, ~(2 * size - 1))
                action(make_copy(hbm0 + done, tile0 + done, size))
        return carry

    lax.fori_loop(0, N_EXPERTS, per_expert, 0)


def _rows(ref, row, n):
    return ref.at[pl.ds(pl.multiple_of(row * ROW_TILES, ROW_TILES), n * ROW_TILES), :]


def _dispatch_kernel(pstart_ref, pend_ref, srow_ref, tmeta_ref, h2_ref, xs_ref, stage_ref, zero_ref, sem):
    tm = h2_ref.shape[0] // ROW_TILES
    i = pl.program_id(0)
    slot = i % 2

    used = pend_ref[N_EXPERTS - 1] // EXPERT_BLOCK
    total = xs_ref.shape[0] // (EXPERT_BLOCK * ROW_TILES)

    def clear(action):
        def block(row):
            return pltpu.make_async_copy(zero_ref, _rows(xs_ref, row, EXPERT_BLOCK), sem.at[2])

        def last_of_segment(e, carry):
            @pl.when(pend_ref[e] > pstart_ref[e])
            def _():
                action(block(pend_ref[e] - EXPERT_BLOCK))
            return carry

        lax.fori_loop(0, N_EXPERTS, last_of_segment, 0)
        lax.fori_loop(used, total, lambda j, c: (action(block(j * EXPERT_BLOCK)), c)[1], 0)

    @pl.when(i == 0)
    def _():
        zero_ref[...] = jnp.zeros_like(zero_ref)
        clear(lambda copy: copy.start())

    def place(t, carry):
        row = h2_ref[pl.ds(pl.multiple_of(t * ROW_TILES, ROW_TILES), ROW_TILES), :]
        for k in range(TOP_K):
            stage_ref[slot, pl.ds(pl.multiple_of(srow_ref[k * tm + t], ROW_TILES), ROW_TILES), :] = row
        return carry

    lax.fori_loop(0, tm, place, 0, unroll=ISSUE_UNROLL)

    @pl.when(i == 0)
    def _():
        clear(lambda copy: copy.wait())

    def strips(meta_ref, s):
        return functools.partial(
            _for_each_strip, meta_ref, tm,
            lambda hbm_row, tile_row, n: pltpu.make_async_copy(_rows(stage_ref.at[s], tile_row, n),
                                                               _rows(xs_ref, hbm_row, n), sem.at[s]))

    strips(tmeta_ref, slot)(lambda copy: copy.start())

    def wait_all(s):
        pltpu.make_async_copy(stage_ref.at[s], _rows(xs_ref, 0, TOP_K * tm), sem.at[s]).wait()

    @pl.when(i > 0)
    def _():
        wait_all(1 - slot)

    @pl.when(i == pl.num_programs(0) - 1)
    def _():
        wait_all(slot)


def _dispatch(pstart, pends, srow, tmeta, h2, n_rows, tm):
    t = h2.shape[0] // ROW_TILES
    smem = lambda shape, imap: pl.BlockSpec(shape, imap, memory_space=pltpu.SMEM)
    return pl.pallas_call(
        _dispatch_kernel,
        out_shape=jax.ShapeDtypeStruct((n_rows * ROW_TILES, LANES), F32),
        grid_spec=pltpu.PrefetchScalarGridSpec(
            num_scalar_prefetch=2,
            grid=(t // tm,),
            in_specs=[smem((SUBLANES * tm,), lambda i, ps, pe: (i,)),
                      smem((1, SUBLANES, LANES), lambda i, ps, pe: (i, 0, 0)),
                      pl.BlockSpec((tm * ROW_TILES, LANES), lambda i, ps, pe: (i, 0))],
            out_specs=pl.BlockSpec(memory_space=pl.ANY),
            scratch_shapes=[pltpu.VMEM((2, TOP_K * tm * ROW_TILES, LANES), F32),
                            pltpu.VMEM((EXPERT_BLOCK * ROW_TILES, LANES), F32),
                            pltpu.SemaphoreType.DMA((3,))],
        ),
        compiler_params=_params(("arbitrary",)),
        name="dispatch",
    )(pstart, pends, srow, tmeta, h2)


def _combine_kernel(n_prompt_tiles, srow_ref, gate_ref, tmeta_ref, tmeta_next_ref, x1_ref, nw_ref, yb_ref,
                    op_ref, os_ref, buf_ref, acc_ref, sem):
    tm = x1_ref.shape[0]
    i = pl.program_id(0)
    slot = i % 2

    def strips(meta_ref, s):
        return functools.partial(
            _for_each_strip, meta_ref, tm,
            lambda hbm_row, tile_row, n: pltpu.make_async_copy(_rows(yb_ref, hbm_row, n),
                                                               _rows(buf_ref.at[s], tile_row, n), sem.at[s]))

    @pl.when(i == 0)
    def _():
        strips(tmeta_ref, 0)(lambda copy: copy.start())

    @pl.when(i + 1 < pl.num_programs(0))
    def _():
        strips(tmeta_next_ref, 1 - slot)(lambda copy: copy.start())

    pltpu.make_async_copy(_rows(yb_ref, 0, TOP_K * tm), buf_ref.at[slot], sem.at[slot]).wait()

    def gather(t, carry):
        def term(k):
            row = buf_ref[slot, pl.ds(pl.multiple_of(srow_ref[k * tm + t], ROW_TILES), ROW_TILES), :]
            return row * gate_ref[k * tm + t]

        moe = term(0)
        for k in range(1, TOP_K):
            moe = moe + term(k)
        acc_ref[pl.ds(pl.multiple_of(t * ROW_TILES, ROW_TILES), ROW_TILES), :] = moe
        return carry

    lax.fori_loop(0, tm, gather, 0, unroll=ISSUE_UNROLL)
    x2 = x1_ref[...] + _load_row_tiles(acc_ref, tm)
    ms = jnp.mean(x2 * x2, axis=-1, keepdims=True)
    out = x2 * lax.rsqrt(ms + EPS) * nw_ref[...]

    @pl.when(i < n_prompt_tiles)
    def _():
        op_ref[...] = out

    @pl.when(i >= n_prompt_tiles)
    def _():
        os_ref[...] = out


def _combine(srow, gates, tmeta, x1, norm_final, yb, n_prompt, tm):
    t = x1.shape[0]
    nt = t // tm
    npt = n_prompt // tm
    smem = lambda shape, imap: pl.BlockSpec(shape, imap, memory_space=pltpu.SMEM)
    return pl.pallas_call(
        functools.partial(_combine_kernel, npt),
        out_shape=(jax.ShapeDtypeStruct((n_prompt, D_MODEL), F32),
                   jax.ShapeDtypeStruct((t - n_prompt, D_MODEL), F32)),
        grid=(nt,),
        in_specs=[smem((SUBLANES * tm,), lambda i: (i,)),
                  smem((SUBLANES * tm,), lambda i: (i,)),
                  smem((1, SUBLANES, LANES), lambda i: (i, 0, 0)),
                  smem((1, SUBLANES, LANES), lambda i: (jnp.minimum(i + 1, nt - 1), 0, 0)),
                  pl.BlockSpec((tm, D_MODEL), lambda i: (i, 0)),
                  pl.BlockSpec((1, D_MODEL), lambda i: (0, 0)),
                  pl.BlockSpec(memory_space=pl.ANY)],
        out_specs=_split_rows(tm, npt, D_MODEL),
        scratch_shapes=[pltpu.VMEM((2, TOP_K * tm * ROW_TILES, LANES), F32),
                        pltpu.VMEM((tm * ROW_TILES, LANES), F32),
                        pltpu.SemaphoreType.DMA((2,))],
        compiler_params=_params(("arbitrary",)),
        name="combine",
    )(srow, gates, tmeta, tmeta, x1, norm_final, yb)


def _expert_kernel(be_ref, nv_ref, next_ref, xs_ref, bgu_ref, bd_ref, wgu_hbm, wd_hbm, yb_ref,
                   wgu_f, wd_f, wgu_b, wd_b, sem):
    i = pl.program_id(0)

    def fetch(e):
        return (pltpu.make_async_copy(wgu_hbm.at[e], wgu_f, sem.at[0]),
                pltpu.make_async_copy(wd_hbm.at[e], wd_f, sem.at[1]))

    @pl.when(i == 0)
    def _():
        for copy in fetch(be_ref[0]):
            copy.start()

    first = jnp.logical_or(i == 0, be_ref[i] != be_ref[jnp.maximum(i - 1, 0)])

    @pl.when(jnp.logical_and(first, i < nv_ref[0]))
    def _():
        for copy in fetch(be_ref[i]):
            copy.wait()
        wgu_b[...] = wgu_f[...].astype(BF16)
        wd_b[...] = wd_f[...].astype(BF16)

        @pl.when(next_ref[i] >= 0)
        def _():
            for copy in fetch(next_ref[i]):
                copy.start()

    @pl.when(i < nv_ref[0])
    def _():
        gu = _dot(_load_row_tiles(xs_ref, EXPERT_BLOCK).astype(BF16), wgu_b[...]) + bgu_ref[0]
        glu = jnp.minimum(gu[:, :D_FF], SWIGLU_LIMIT)
        lin = jnp.clip(gu[:, D_FF:], -SWIGLU_LIMIT, SWIGLU_LIMIT)
        act = glu * jax.nn.sigmoid(SWIGLU_ALPHA * glu) * (lin + 1.0)
        _store_row_tiles(yb_ref, _dot(act.astype(BF16), wd_b[...]) + bd_ref[0])

    @pl.when(i >= nv_ref[0])
    def _():
        yb_ref[...] = jnp.zeros_like(yb_ref)


def _experts(block_expert, n_valid, next_expert, xs, w_gu, b_gu, w_d, b_d):
    rows = EXPERT_BLOCK * ROW_TILES
    nb = xs.shape[0] // rows
    return pl.pallas_call(
        _expert_kernel,
        out_shape=jax.ShapeDtypeStruct(xs.shape, F32),
        grid_spec=pltpu.PrefetchScalarGridSpec(
            num_scalar_prefetch=3,
            grid=(nb,),
            in_specs=[pl.BlockSpec((rows, LANES), lambda i, be, nv, nx: (jnp.minimum(i, nv[0] - 1), 0)),
                      pl.BlockSpec((1, 1, 2 * D_FF), lambda i, be, nv, nx: (be[i], 0, 0)),
                      pl.BlockSpec((1, 1, D_MODEL), lambda i, be, nv, nx: (be[i], 0, 0)),
                      pl.BlockSpec(memory_space=pl.ANY),
                      pl.BlockSpec(memory_space=pl.ANY)],
            out_specs=pl.BlockSpec((rows, LANES), lambda i, be, nv, nx: (i, 0)),
            scratch_shapes=[pltpu.VMEM((D_MODEL, 2 * D_FF), F32), pltpu.VMEM((D_FF, D_MODEL), F32),
                            pltpu.VMEM((D_MODEL, 2 * D_FF), BF16), pltpu.VMEM((D_FF, D_MODEL), BF16),
                            pltpu.SemaphoreType.DMA((2,))],
        ),
        compiler_params=_params(("arbitrary",)),
        name="experts",
    )(block_expert, n_valid, next_expert, xs, b_gu, b_d, w_gu, w_d)


def _expert_layout(counts, n_blocks):
    counts = counts.astype(jnp.int32)
    padded = (counts + EXPERT_BLOCK - 1) // EXPERT_BLOCK * EXPERT_BLOCK
    pends = jnp.cumsum(padded)
    pstart = pends - padded
    block_start = jnp.arange(n_blocks, dtype=jnp.int32) * EXPERT_BLOCK
    block_expert = jnp.minimum(jnp.sum(pends[None, :] <= block_start[:, None], axis=1), N_EXPERTS - 1)
    n_valid = (pends[-1:] // EXPERT_BLOCK).astype(jnp.int32)
    seg_end = jnp.sum(jnp.where(block_expert[:, None] == jnp.arange(N_EXPERTS), pends[None, :], 0), axis=1)
    after = seg_end // EXPERT_BLOCK
    next_expert = jnp.where(after < n_valid[0], block_expert[jnp.minimum(after, n_blocks - 1)], -1)
    return (pstart.astype(jnp.int32), pends.astype(jnp.int32), block_expert.astype(jnp.int32), n_valid,
            next_expert.astype(jnp.int32))


def _permute_w_in(w_in):
    sizes = [RET_QK, RET_QK, RET_V, RET_V, SSM_INNER, CONV_DIM, SSM_HEADS, D_MODEL, D_MODEL]
    offs = [0]
    for s in sizes:
        offs.append(offs[-1] + s)
    q, k, v, g_ret, z, xbc, dt, g_a, g_b = [w_in[:, offs[i]:offs[i + 1]] for i in range(len(sizes))]
    gn = SSM_GROUPS * SSM_DSTATE
    xs, bm, cm = xbc[:, :SSM_INNER], xbc[:, SSM_INNER:SSM_INNER + gn], xbc[:, SSM_INNER + gn:]
    dt_pad = jnp.pad(dt, ((0, 0), (0, _ColsB.V - _ColsB.DT - SSM_HEADS)))
    tail = jnp.zeros((D_MODEL, _ColsB.PADDED - _ColsB.TOTAL), w_in.dtype)
    return jnp.concatenate([g_a, g_b, z, xs, bm, cm, dt_pad, v, g_ret, q, k, tail], axis=1).astype(BF16)


def _forward(x_prompt, x_sample, state_ret, state_ssm, state_conv, norm_mix, w_in, ret_norm, w_out_ret,
             conv_w, conv_b, dt_bias, a_log, d_skip, ssm_norm, w_out_ssm, w_out, norm_ffn,
             w_router, b_router, w_gate_up, b_gate_up, w_down, b_down, norm_final):
    bp, lp, _ = x_prompt.shape
    bs, ls, _ = x_sample.shape
    assert lp % ROWS == 0 and bs % SAMPLE_PER_TILE == 0 and ls * SAMPLE_PER_TILE == ROWS
    tp, ts = bp * lp, bs * ls
    t_all = tp + ts
    xp, xs_in = x_prompt.reshape(tp, D_MODEL), x_sample.reshape(ts, D_MODEL)

    w_b = _permute_w_in(w_in[0])
    norm_w = norm_mix[0].reshape(1, D_MODEL)
    proj_s = _inproj(xs_in, norm_w, w_b)

    sp = _ssd_params(conv_w[0], conv_b[0], dt_bias[0], a_log[0], d_skip[0], ssm_norm[0])
    y_p, ssm_p, conv_p, a_p, ret_p, ga_p, gb_p = _mixer_prompt(xp, norm_w, w_b, bp, lp, sp, ret_norm[0])
    y_s, ssm_s, conv_s, a_s, ret_s = _mixer_sample(proj_s, bs, ls, sp, state_ssm[0], state_conv[0],
                                                   state_ret[0], ret_norm[0])

    w_router_pad = jnp.pad(w_router[0], ((0, 0), (0, LANES - N_EXPERTS))).astype(BF16)
    b_router_pad = jnp.pad(b_router[0], (0, LANES - N_EXPERTS)).reshape(1, LANES)
    x1, h2, logits = _post(a_p, a_s, y_p, y_s, ga_p, gb_p, proj_s, xp, xs_in, w_out_ret[0].astype(BF16),
                           w_out_ssm[0].astype(BF16), w_out[0].astype(BF16), norm_ffn[0].reshape(1, D_MODEL), w_router_pad, b_router_pad)

    tm_moe = _pick(math.gcd(tp, ts), 512)
    srow, gates, tmeta, counts = _route(logits, tm_moe)
    n_blocks = -(-(t_all * TOP_K + N_EXPERTS * (EXPERT_BLOCK - 1)) // EXPERT_BLOCK)
    pstart, pends, block_expert, n_valid, next_expert = _expert_layout(counts[0, :N_EXPERTS], n_blocks)
    tmeta = tmeta.astype(jnp.int32)
    run_start = tmeta[:, :1, :] + jnp.pad(pstart, (0, LANES - N_EXPERTS))
    tmeta = jnp.concatenate([run_start, tmeta[:, 1:, :]], axis=1)
    srow, gates = srow.reshape(-1), gates.reshape(-1)
    xs = _dispatch(pstart, pends, srow, tmeta, h2, n_blocks * EXPERT_BLOCK, tm_moe)
    yb = _experts(block_expert, n_valid, next_expert, xs, w_gate_up[0],
                  b_gate_up[0].reshape(N_EXPERTS, 1, 2 * D_FF), w_down[0], b_down[0].reshape(N_EXPERTS, 1, D_MODEL))
    out_p, out_s = _combine(srow, gates, tmeta, x1, norm_final.reshape(1, D_MODEL), yb, tp, tm_moe)

    shape_s = (1, bs, SSM_HEADS, SSM_HEADDIM, SSM_DSTATE)
    shape_p = (1, bp, SSM_HEADS, SSM_HEADDIM, SSM_DSTATE)
    return (out_p.reshape(bp, lp, D_MODEL), out_s.reshape(bs, ls, D_MODEL),
            ret_p[None], ret_s[None], ssm_p.reshape(shape_p), ssm_s.reshape(shape_s),
            conv_p[None], conv_s[None])


def kernel(x_prompt, x_sample, state_ret, state_ssm, state_conv, norm_mix, w_in, ret_norm, w_out_ret, conv_w, conv_b, dt_bias, a_log, d_skip, ssm_norm, w_out_ssm, w_out, norm_ffn, w_router, b_router, w_gate_up, b_gate_up, w_down, b_down, norm_final):
    return _forward(x_prompt, x_sample, state_ret, state_ssm, state_conv, norm_mix, w_in, ret_norm, w_out_ret,
                    conv_w, conv_b, dt_bias, a_log, d_skip, ssm_norm, w_out_ssm, w_out, norm_ffn,
                    w_router, b_router, w_gate_up, b_gate_up, w_down, b_down, norm_final)
```
